```python
import math
import jax, jax.numpy as jnp
from jax import lax
import numpy as np

D_MODEL = 1024
BATCH = 1
SEQ = 16384
DEPTH = 1

D_MIX = D_MODEL
ATTN_WIDTH = D_MIX // 2
HEAD_DIM = 64
N_HEADS = ATTN_WIDTH // HEAD_DIM
WINDOWS = (128, 512, 2048)
DILATIONS = (1, 4, 16)
BLOCK = 128
PAD_UNIT = max(DILATIONS) * BLOCK
SSM_WIDTH = D_MIX - ATTN_WIDTH
SSM_GROUP = 16
SSM_GROUPS = SSM_WIDTH // SSM_GROUP
SSM_STATE = 64
D_FF = 4 * D_MODEL
PROJ_WIDTH = 3 * ATTN_WIDTH + SSM_WIDTH
EPS = 1e-6
NEG_INF = -1e30
DT_MIN, DT_MAX = 1e-3, 1e-1

kernel_name = "hymba_longnet_s5_hybrid"


def _rmsnorm(x, g):
    xf = x.astype(jnp.float32)
    y = xf * lax.rsqrt(jnp.mean(xf * xf, axis=-1, keepdims=True) + EPS)
    return (y * g.astype(jnp.float32)).astype(x.dtype)


def _dilated_window(q, k, v, dilation, steps):
    b, sp, h, e = q.shape
    n = sp // dilation
    nb = n // BLOCK

    def to_blocks(t):
        t = t.reshape(b, n, dilation, h, e).transpose(0, 2, 3, 1, 4)
        return t.reshape(b, dilation, h, nb, BLOCK, e)

    qb, kb, vb = to_blocks(q), to_blocks(k), to_blocks(v)

    def with_prev(t):
        prev = jnp.pad(t[:, :, :, :-1], ((0, 0), (0, 0), (0, 0), (1, 0), (0, 0), (0, 0)))
        return jnp.concatenate([prev, t], axis=4)

    kw, vw = with_prev(kb), with_prev(vb)
    s = jnp.einsum('brhnqe,brhnke->brhnqk', qb, kw) * (HEAD_DIM ** -0.5)
    qi = jnp.arange(BLOCK)[:, None] + BLOCK
    ki = jnp.arange(2 * BLOCK)[None, :]
    dist = qi - ki
    band = (dist >= 0) & (dist <= steps)
    valid = (jnp.arange(nb)[:, None, None] * BLOCK - BLOCK + ki[None]) >= 0
    mask = band[None] & valid
    s = jnp.where(mask, s, NEG_INF)
    m = jnp.max(s, axis=-1)
    p = jnp.exp(s - m[..., None])
    l = jnp.sum(p, axis=-1)
    acc = jnp.einsum('brhnqk,brhnke->brhnqe', p, vw)

    def to_seq(t):
        t = t.reshape((b, dilation, h, n) + t.shape[5:])
        t = jnp.moveaxis(t, 3, 1)
        return t.reshape((b, sp, h) + t.shape[4:])

    return to_seq(acc), to_seq(m), to_seq(l)


def _dilated_attention(q, k, v):
    b, s, h, e = q.shape
    pad = (-s) % PAD_UNIT
    qf, kf, vf = [jnp.pad(t.astype(jnp.float32), ((0, 0), (0, pad), (0, 0), (0, 0))) for t in (q, k, v)]
    accs, ms, ls = [], [], []
    for w, d in zip(WINDOWS, DILATIONS):
        acc, m, l = _dilated_window(qf, kf, vf, d, w // d)
        accs.append(acc)
        ms.append(m)
        ls.append(l)
    m_all = jnp.stack(ms)
    wts = jnp.exp(m_all - jnp.max(m_all, axis=0, keepdims=True))
    num = jnp.sum(jnp.stack(accs) * wts[..., None], axis=0)
    den = jnp.sum(jnp.stack(ls) * wts, axis=0)
    out = num / den[..., None]
    return out[:, :s].astype(v.dtype)


def _complex_linear_combine(e1, e2):
    a1r, a1i, b1r, b1i = e1
    a2r, a2i, b2r, b2i = e2
    return (a2r * a1r - a2i * a1i,
            a2r * a1i + a2i * a1r,
            a2r * b1r - a2i * b1i + b2r,
            a2r * b1i + a2i * b1r + b2i)


def _s5_mixer(u, a_re, a_im, log_dt, b_re, b_im, c_re, c_im, d_skip, glu_w, glu_b):
    b, s, _ = u.shape
    f32 = jnp.float32
    uf = u.astype(f32).reshape(b, s, SSM_GROUPS, SSM_GROUP)
    lr, li = a_re.astype(f32), a_im.astype(f32)
    dt = jnp.exp(log_dt.astype(f32))[:, None]
    mag = jnp.exp(lr * dt)
    ab_r, ab_i = mag * jnp.cos(li * dt), mag * jnp.sin(li * dt)
    den = lr * lr + li * li
    nr, ni = ab_r - 1.0, ab_i
    cr = (nr * lr + ni * li) / den
    ci = (ni * lr - nr * li) / den
    br, bi = b_re.astype(f32), b_im.astype(f32)
    bb_r = cr[..., None] * br - ci[..., None] * bi
    bb_i = cr[..., None] * bi + ci[..., None] * br
    bu_r = jnp.einsum('bsgc,gpc->bsgp', uf, bb_r)
    bu_i = jnp.einsum('bsgc,gpc->bsgp', uf, bb_i)
    a_r = jnp.broadcast_to(ab_r, bu_r.shape)
    a_i = jnp.broadcast_to(ab_i, bu_i.shape)
    _, _, xr, xi = lax.associative_scan(_complex_linear_combine, (a_r, a_i, bu_r, bu_i), axis=1)
    y = (jnp.einsum('bsgp,gcp->bsgc', xr, c_re.astype(f32))
         - jnp.einsum('bsgp,gcp->bsgc', xi, c_im.astype(f32))
         + d_skip.astype(f32) * uf)
    y = y.reshape(b, s, SSM_WIDTH)
    z = jax.nn.gelu(y)
    out = z * jax.nn.sigmoid(z @ glu_w.astype(f32) + glu_b.astype(f32))
    return out.astype(u.dtype)


def _hybrid_layer(x, norm1_g, w_in, q_norm_g, k_norm_g, ssm_a_re, ssm_a_im, ssm_log_dt,
                  ssm_b_re, ssm_b_im, ssm_c_re, ssm_c_im, ssm_d, glu_w, glu_b,
                  attn_out_norm_g, ssm_out_norm_g, w_out, norm2_g, w_mlp_up, w_mlp_down):
    b, s, _ = x.shape
    xn = _rmsnorm(x, norm1_g)
    proj = xn @ w_in
    q = proj[..., :ATTN_WIDTH].reshape(b, s, N_HEADS, HEAD_DIM)
    k = proj[..., ATTN_WIDTH:2 * ATTN_WIDTH].reshape(b, s, N_HEADS, HEAD_DIM)
    v = proj[..., 2 * ATTN_WIDTH:3 * ATTN_WIDTH].reshape(b, s, N_HEADS, HEAD_DIM)
    u = proj[..., 3 * ATTN_WIDTH:]
    q = _rmsnorm(q, q_norm_g)
    k = _rmsnorm(k, k_norm_g)
    attn = _dilated_attention(q, k, v).reshape(b, s, ATTN_WIDTH)
    ssm = _s5_mixer(u, ssm_a_re, ssm_a_im, ssm_log_dt, ssm_b_re, ssm_b_im,
                    ssm_c_re, ssm_c_im, ssm_d, glu_w, glu_b)
    mix = jnp.concatenate([_rmsnorm(attn, attn_out_norm_g), _rmsnorm(ssm, ssm_out_norm_g)], axis=-1)
    x = x + mix @ w_out
    hdn = jnp.square(jax.nn.relu(_rmsnorm(x, norm2_g) @ w_mlp_up))
    return x + hdn @ w_mlp_down


def setup_inputs(seed: int = 0) -> dict:
    key = jax.random.key(seed)
    ks = jax.random.split(key, 20)
    L, G, P, C = DEPTH, SSM_GROUPS, SSM_STATE, SSM_GROUP
    nrm = lambda k, shape, scale: jax.random.normal(k, shape, jnp.float32) * scale
    x = nrm(ks[0], (BATCH, SEQ, D_MODEL), 1.0)
    norm1_g = 1.0 + nrm(ks[1], (L, D_MODEL), 0.02)
    w_in = nrm(ks[2], (L, D_MODEL, PROJ_WIDTH), D_MODEL ** -0.5)
    q_norm_g = 1.0 + nrm(ks[3], (L, HEAD_DIM), 0.02)
    k_norm_g = 1.0 + nrm(ks[4], (L, HEAD_DIM), 0.02)
    ssm_a_re = -0.5 + nrm(ks[5], (L, G, P), 0.01)
    ssm_a_im = math.pi * jnp.arange(P, dtype=jnp.float32)[None, None, :] + nrm(ks[6], (L, G, P), 0.01)
    ssm_log_dt = jax.random.uniform(ks[7], (L, G), jnp.float32, math.log(DT_MIN), math.log(DT_MAX))
    ssm_b_re = nrm(ks[8], (L, G, P, C), (2 * C) ** -0.5)
    ssm_b_im = nrm(ks[9], (L, G, P, C), (2 * C) ** -0.5)
    ssm_c_re = nrm(ks[10], (L, G, C, P), (2 * P) ** -0.5)
    ssm_c_im = nrm(ks[11], (L, G, C, P), (2 * P) ** -0.5)
    ssm_d = nrm(ks[12], (L, G, C), 1.0)
    glu_w = nrm(ks[13], (L, SSM_WIDTH, SSM_WIDTH), SSM_WIDTH ** -0.5)
    glu_b = nrm(ks[14], (L, SSM_WIDTH), 0.01)
    attn_out_norm_g = 1.0 + nrm(ks[15], (L, ATTN_WIDTH), 0.02)
    ssm_out_norm_g = 1.0 + nrm(ks[16], (L, SSM_WIDTH), 0.02)
    w_out = nrm(ks[17], (L, D_MIX, D_MODEL), D_MIX ** -0.5)
    norm2_g = 1.0 + nrm(ks[18], (L, D_MODEL), 0.02)
    k_up, k_down = jax.random.split(ks[19])
    w_mlp_up = nrm(k_up, (L, D_MODEL, D_FF), D_MODEL ** -0.5)
    w_mlp_down = nrm(k_down, (L, D_FF, D_MODEL), D_FF ** -0.5)
    return {"x": x, "norm1_g": norm1_g, "w_in": w_in, "q_norm_g": q_norm_g, "k_norm_g": k_norm_g,
            "ssm_a_re": ssm_a_re, "ssm_a_im": ssm_a_im, "ssm_log_dt": ssm_log_dt,
            "ssm_b_re": ssm_b_re, "ssm_b_im": ssm_b_im, "ssm_c_re": ssm_c_re, "ssm_c_im": ssm_c_im,
            "ssm_d": ssm_d, "glu_w": glu_w, "glu_b": glu_b,
            "attn_out_norm_g": attn_out_norm_g, "ssm_out_norm_g": ssm_out_norm_g,
            "w_out": w_out, "norm2_g": norm2_g, "w_mlp_up": w_mlp_up, "w_mlp_down": w_mlp_down}


def reference(x, norm1_g, w_in, q_norm_g, k_norm_g, ssm_a_re, ssm_a_im, ssm_log_dt,
              ssm_b_re, ssm_b_im, ssm_c_re, ssm_c_im, ssm_d, glu_w, glu_b,
              attn_out_norm_g, ssm_out_norm_g, w_out, norm2_g, w_mlp_up, w_mlp_down):
    h = x
    for l in range(DEPTH):
        h = _hybrid_layer(h, norm1_g[l], w_in[l], q_norm_g[l], k_norm_g[l], ssm_a_re[l], ssm_a_im[l],
                          ssm_log_dt[l], ssm_b_re[l], ssm_b_im[l], ssm_c_re[l], ssm_c_im[l], ssm_d[l],
                          glu_w[l], glu_b[l], attn_out_norm_g[l], ssm_out_norm_g[l], w_out[l],
                          norm2_g[l], w_mlp_up[l], w_mlp_down[l])
    return h
```

```python
import functools
import math

import jax
import jax.numpy as jnp
from jax import lax
from jax.experimental import pallas as pl
from jax.experimental.pallas import tpu as pltpu

F32 = jnp.float32
BF16 = jnp.bfloat16

D_MODEL = 1024
ATTN_WIDTH = 512
HEAD_DIM = 64
SSM_WIDTH = 512
SSM_GROUP = 16
SSM_GROUPS = 32
SSM_STATE = 64
D_FF = 4096
EPS = 1e-6
NEG_INF = -1e30
DILATIONS = (1, 4, 16)
BLOCK = 128

LANE_TILE = 256
HEADS_PER_TILE = LANE_TILE // HEAD_DIM
N_HEAD_TILES = ATTN_WIDTH // LANE_TILE
CHUNK = 16
SUB = 4
HALF_STATE = (SSM_GROUPS // 2) * SSM_STATE
STATE_COLS = SSM_GROUPS * SSM_STATE
VMEM_LIMIT = 56 * 1024 * 1024

PROJ_ROWS = 512
ATTN_ROWS = 1024
S5_ROWS = 256
MLP_ROWS = 512
FF_CHUNK = 1024


def _const_spec(shape):
    nd = len(shape)
    return pl.BlockSpec(shape, lambda *_: (0,) * nd, pipeline_mode=pl.Buffered(1))


def _params(**kw):
    return pltpu.CompilerParams(vmem_limit_bytes=VMEM_LIMIT, **kw)


def _proj_kernel(x_ref, g1_ref, w_ref, hmat_ref, gq_ref, gk_ref, q_ref, k_ref, v_ref, u_ref):
    x = x_ref[...]
    ms = jnp.mean(x * x, axis=-1, keepdims=True)
    xn = (x * lax.rsqrt(ms + EPS) * g1_ref[...]).astype(BF16)
    proj = jnp.dot(xn, w_ref[...], preferred_element_type=F32)

    def head_norm(t, g):
        ms_h = jnp.dot((t * t).astype(BF16), hmat_ref[...], preferred_element_type=F32)
        return t * lax.rsqrt(ms_h + EPS) * g

    q = head_norm(proj[:, 0:ATTN_WIDTH], gq_ref[...]) * (HEAD_DIM ** -0.5)
    k = head_norm(proj[:, ATTN_WIDTH:2 * ATTN_WIDTH], gk_ref[...])
    v = proj[:, 2 * ATTN_WIDTH:3 * ATTN_WIDTH]
    for t in range(N_HEAD_TILES):
        sl = slice(t * LANE_TILE, (t + 1) * LANE_TILE)
        q_ref[t] = q[:, sl].astype(BF16)
        k_ref[t] = k[:, sl].astype(BF16)
        v_ref[t] = v[:, sl].astype(BF16)
    u_ref[...] = proj[:, 3 * ATTN_WIDTH:].astype(BF16)


def _proj(x, g1, w_in, hmat, gq, gk):
    s = x.shape[0]
    rows = PROJ_ROWS
    qkv_shape = jax.ShapeDtypeStruct((N_HEAD_TILES, s, LANE_TILE), BF16)
    qkv_spec = pl.BlockSpec((N_HEAD_TILES, rows, LANE_TILE), lambda i: (0, i, 0))
    return pl.pallas_call(
        _proj_kernel,
        grid=(s // rows,),
        in_specs=[
            pl.BlockSpec((rows, D_MODEL), lambda i: (i, 0)),
            _const_spec((1, D_MODEL)),
            _const_spec(w_in.shape),
            _const_spec(hmat.shape),
            _const_spec((1, ATTN_WIDTH)),
            _const_spec((1, ATTN_WIDTH)),
        ],
        out_specs=[qkv_spec, qkv_spec, qkv_spec, pl.BlockSpec((rows, SSM_WIDTH), lambda i: (i, 0))],
        out_shape=[qkv_shape, qkv_shape, qkv_shape, jax.ShapeDtypeStruct((s, SSM_WIDTH), BF16)],
        compiler_params=_params(dimension_semantics=("arbitrary",)),
        name="proj",
    )(x, g1, w_in, hmat, gq, gk)


def _attn_kernel(q_ref, kc_ref, kp_ref, vc_ref, vp_ref, o_ref, lse_ref, kbuf, vbuf):
    rows = q_ref.shape[0]
    kbuf[0:BLOCK, :] = kp_ref[...]
    kbuf[BLOCK:, :] = kc_ref[...]
    vbuf[0:BLOCK, :] = vp_ref[...]
    vbuf[BLOCK:, :] = vc_ref[...]

    lane = lax.broadcasted_iota(jnp.int32, (BLOCK, LANE_TILE), 1)
    head_masks = [(lane >= h * HEAD_DIM) & (lane < (h + 1) * HEAD_DIM) for h in range(HEADS_PER_TILE)]
    qi = lax.broadcasted_iota(jnp.int32, (BLOCK, 2 * BLOCK), 0)
    ki = lax.broadcasted_iota(jnp.int32, (BLOCK, 2 * BLOCK), 1)
    band = (ki >= qi) & (ki <= qi + BLOCK)
    bias_band = jnp.where(band, 0.0, NEG_INF).astype(F32)
    no_prev = pl.program_id(2) == 0
    bias_first = jnp.where(band & ((ki >= BLOCK) | jnp.logical_not(no_prev)), 0.0, NEG_INF).astype(F32)

    for b in range(rows // BLOCK):
        qb = q_ref[b * BLOCK:(b + 1) * BLOCK, :]
        zero = jnp.zeros_like(qb)
        q_stack = jnp.concatenate([jnp.where(mk, qb, zero) for mk in head_masks], axis=0)
        kw = kbuf[b * BLOCK:(b + 2) * BLOCK, :]
        vw = vbuf[b * BLOCK:(b + 2) * BLOCK, :]
        s = lax.dot_general(q_stack, kw, (((1,), (1,)), ((), ())), preferred_element_type=F32)
        bias = bias_first if b == 0 else bias_band
        ps, ms, ls = [], [], []
        for h in range(HEADS_PER_TILE):
            sh = s[h * BLOCK:(h + 1) * BLOCK, :] + bias
            m = jnp.max(sh, axis=-1, keepdims=True)
            p = jnp.exp(sh - m)
            ls.append(jnp.sum(p, axis=-1, keepdims=True))
            ms.append(m)
            ps.append(p.astype(BF16))
        pv = jnp.dot(jnp.concatenate(ps, axis=0), vw, preferred_element_type=F32)
        o = jnp.zeros((BLOCK, LANE_TILE), F32)
        lse = jnp.zeros((BLOCK, LANE_TILE), F32)
        for h in range(HEADS_PER_TILE):
            o = jnp.where(head_masks[h], pv[h * BLOCK:(h + 1) * BLOCK, :] / ls[h], o)
            lse = jnp.where(head_masks[h], ms[h] + jnp.log(ls[h]), lse)
        o_ref[b * BLOCK:(b + 1) * BLOCK, :] = o.astype(BF16)
        lse_ref[b * BLOCK:(b + 1) * BLOCK, :] = lse


def _attn_pattern(q, k, v, dilation):
    nt, s, _ = q.shape
    n = s // dilation
    rows = min(ATTN_ROWS, n)
    blocks_per_tile = rows // BLOCK
    view = lambda t: t.reshape(nt, n, dilation * LANE_TILE)
    cur = pl.BlockSpec((None, rows, LANE_TILE), lambda t, r, j: (t, j, r))
    prev = pl.BlockSpec((None, BLOCK, LANE_TILE),
                        lambda t, r, j: (t, jnp.maximum(j * blocks_per_tile - 1, 0), r))
    o, lse = pl.pallas_call(
        _attn_kernel,
        grid=(nt, dilation, n // rows),
        in_specs=[cur, cur, prev, cur, prev],
        out_specs=[cur, cur],
        out_shape=[jax.ShapeDtypeStruct((nt, n, dilation * LANE_TILE), BF16),
                   jax.ShapeDtypeStruct((nt, n, dilation * LANE_TILE), F32)],
        scratch_shapes=[pltpu.VMEM((rows + BLOCK, LANE_TILE), BF16),
                        pltpu.VMEM((rows + BLOCK, LANE_TILE), BF16)],
        compiler_params=_params(dimension_semantics=("arbitrary", "arbitrary", "arbitrary")),
        name=f"attn_d{dilation}",
    )(view(q), view(k), view(k), view(v), view(v))
    return o.reshape(nt, s, LANE_TILE), lse.reshape(nt, s, LANE_TILE)


def _s5_weights(a_re, a_im, log_dt, b_re, b_im, c_re, c_im, d_skip):
    hp = lax.Precision.HIGHEST
    g, p, c = SSM_GROUPS, SSM_STATE, SSM_GROUP
    gl = g // 2
    lr, li = a_re.astype(F32), a_im.astype(F32)
    dt = jnp.exp(log_dt.astype(F32))[:, None]

    def apow(j):
        mag = jnp.exp(lr * dt * j)
        return mag * jnp.cos(li * dt * j), mag * jnp.sin(li * dt * j)

    ab_r, ab_i = apow(1.0)
    den = lr * lr + li * li
    nr, ni = ab_r - 1.0, ab_i
    cr = (nr * lr + ni * li) / den
    ci = (ni * lr - nr * li) / den
    br, bi = b_re.astype(F32), b_im.astype(F32)
    bb_r = cr[..., None] * br - ci[..., None] * bi
    bb_i = cr[..., None] * bi + ci[..., None] * br
    cre, cim = c_re.astype(F32), c_im.astype(F32)
    eye = jnp.eye(gl, dtype=F32)

    def c_times_apow(j):
        pr, pi = apow(j)
        return cre * pr[:, None, :] - cim * pi[:, None, :], cre * pi[:, None, :] + cim * pr[:, None, :]

    lags = []
    for j in range(CHUNK):
        gr, gi = c_times_apow(float(j))
        kj = (jnp.einsum('gop,gpi->goi', gr, bb_r, precision=hp)
              - jnp.einsum('gop,gpi->goi', gi, bb_i, precision=hp))
        if j == 0:
            kj = kj + d_skip.astype(F32)[:, :, None] * jnp.eye(c, dtype=F32)[None]
        lags.append(kj)
    kt = jnp.stack(lags).transpose(0, 1, 3, 2).reshape(CHUNK, 2, gl, c, c)
    m_t = jnp.einsum('jhgic,gk->jhgikc', kt, eye).reshape(CHUNK, 2, gl * c, gl * c)

    def expand_in(w):
        w = w.reshape(2, gl, p, c)
        return jnp.einsum('hgnc,gk->hgckn', w, eye).reshape(2, gl * c, gl * p)

    def expand_out(w):
        w = w.reshape(2, gl, c, p)
        return jnp.einsum('hgcn,gk->hgnkc', w, eye).reshape(2, gl * p, gl * c)

    w_r, w_i, e_r, e_i = [], [], [], []
    for b in range(SUB):
        pr, pi = apow(float(SUB - 1 - b))
        w_r.append(expand_in(pr[..., None] * bb_r - pi[..., None] * bb_i))
        w_i.append(expand_in(pr[..., None] * bb_i + pi[..., None] * bb_r))
        gr, gi = c_times_apow(float(b + 1))
        e_r.append(expand_out(gr))
        e_i.append(expand_out(-gi))

    flat = lambda pair: jnp.stack([pair[0].reshape(-1), pair[1].reshape(-1)])
    levels = [flat(apow(float(CHUNK)))]
    n_levels = int(math.log2(S5_ROWS))
    for _ in range(n_levels - 1):
        lr_, li_ = levels[-1][0], levels[-1][1]
        levels.append(jnp.stack([lr_ * lr_ - li_ * li_, 2.0 * lr_ * li_]))
    return dict(
        m_t=m_t.astype(BF16),
        w_r=jnp.stack(w_r).astype(BF16), w_i=jnp.stack(w_i).astype(BF16),
        e_r=jnp.stack(e_r).astype(BF16), e_i=jnp.stack(e_i).astype(BF16),
        a_sub=flat(apow(float(SUB))),
        levels=jnp.stack(levels),
    )


def _cmul(ar, ai, br, bi):
    return ar * br - ai * bi, ar * bi + ai * br


def _s5_kernel(u_ref, mt_ref, wr_ref, wi_ref, er_ref, ei_ref, asub_ref, lvl_ref,
               gluw_ref, glub_ref, gout_ref, out_ref, carry_ref, zr_ref, zi_ref):
    rows = u_ref.shape[0]
    width = SSM_WIDTH

    @pl.when(pl.program_id(0) == 0)
    def _():
        carry_ref[...] = jnp.zeros_like(carry_ref)

    def u_tile(t, h):
        return u_ref[:, t * width + h * LANE_TILE:t * width + (h + 1) * LANE_TILE]

    a_r, a_i = asub_ref[0:1, :], asub_ref[1:2, :]

    for a in range(SUB):
        for h in range(2):
            cols = slice(h * HALF_STATE, (h + 1) * HALF_STATE)
            pr = pi = None
            for b in range(SUB):
                ut = u_tile(a * SUB + b, h)
                dr = jnp.dot(ut, wr_ref[b, h], preferred_element_type=F32)
                di = jnp.dot(ut, wi_ref[b, h], preferred_element_type=F32)
                pr = dr if pr is None else pr + dr
                pi = di if pi is None else pi + di
            if a == 0:
                zr_ref[:, cols] = pr
                zi_ref[:, cols] = pi
            else:
                hr, hi = _cmul(zr_ref[:, cols], zi_ref[:, cols], a_r[:, cols], a_i[:, cols])
                zr_ref[:, cols] = hr + pr
                zi_ref[:, cols] = hi + pi

    row = lax.broadcasted_iota(jnp.int32, (rows, 1), 0)
    c_r, c_i = carry_ref[0:1, :], carry_ref[1:2, :]
    lam_r, lam_i = lvl_ref[0, 0:1, :], lvl_ref[0, 1:2, :]
    in_r, in_i = _cmul(c_r, c_i, lam_r, lam_i)
    sr = zr_ref[...] + jnp.where(row == 0, in_r, 0.0)
    si = zi_ref[...] + jnp.where(row == 0, in_i, 0.0)
    for lev in range(lvl_ref.shape[0]):
        sh = 1 << lev
        keep = row >= sh
        pr_, pi_ = lvl_ref[lev, 0:1, :], lvl_ref[lev, 1:2, :]
        tr = jnp.where(keep, pltpu.roll(sr, sh, axis=0), 0.0)
        ti = jnp.where(keep, pltpu.roll(si, sh, axis=0), 0.0)
        mr, mi = _cmul(tr, ti, pr_, pi_)
        sr, si = sr + mr, si + mi
    xr = jnp.where(row == 0, c_r, pltpu.roll(sr, 1, axis=0))
    xi = jnp.where(row == 0, c_i, pltpu.roll(si, 1, axis=0))
    carry_ref[0:1, :] = sr[rows - 1:rows, :]
    carry_ref[1:2, :] = si[rows - 1:rows, :]
    zr_ref[...] = xr
    zi_ref[...] = xi

    for a in range(SUB):
        if a > 0:
            nr, ni = _cmul(zr_ref[...], zi_ref[...], a_r, a_i)
            zr_ref[...] = nr
            zi_ref[...] = ni
        xr_b = zr_ref[...].astype(BF16)
        xi_b = zi_ref[...].astype(BF16)
        for b in range(SUB):
            t = a * SUB + b
            halves = []
            for h in range(2):
                cols = slice(h * HALF_STATE, (h + 1) * HALF_STATE)
                acc = jnp.dot(xr_b[:, cols], er_ref[b, h], preferred_element_type=F32)
                acc = acc + jnp.dot(xi_b[:, cols], ei_ref[b, h], preferred_element_type=F32)
                for t_in in range(t + 1):
                    acc = acc + jnp.dot(u_tile(t_in, h), mt_ref[t - t_in, h], preferred_element_type=F32)
                halves.append(acc)
            y = jnp.concatenate(halves, axis=1)
            z = 0.5 * y * (1.0 + jnp.tanh(math.sqrt(2.0 / math.pi) * (y + 0.044715 * (y * y * y))))
            gate = jnp.dot(z.astype(BF16), gluw_ref[...], preferred_element_type=F32) + glub_ref[...]
            o = z * (1.0 / (1.0 + jnp.exp(-gate)))
            ms = jnp.mean(o * o, axis=-1, keepdims=True)
            out_ref[:, t * width:(t + 1) * width] = (o * lax.rsqrt(ms + EPS) * gout_ref[...]).astype(BF16)


def _s5(u, w, glu_w, glu_b, g_out):
    s = u.shape[0]
    n = s // CHUNK
    rows = S5_ROWS
    uv = u.reshape(n, CHUNK * SSM_WIDTH)
    tile = pl.BlockSpec((rows, CHUNK * SSM_WIDTH), lambda i: (i, 0))
    out = pl.pallas_call(
        _s5_kernel,
        grid=(n // rows,),
        in_specs=[tile] + [_const_spec(a.shape) for a in
                           (w["m_t"], w["w_r"], w["w_i"], w["e_r"], w["e_i"], w["a_sub"], w["levels"],
                            glu_w, glu_b, g_out)],
        out_specs=tile,
        out_shape=jax.ShapeDtypeStruct((n, CHUNK * SSM_WIDTH), BF16),
        scratch_shapes=[pltpu.VMEM((2, STATE_COLS), F32),
                        pltpu.VMEM((rows, STATE_COLS), F32),
                        pltpu.VMEM((rows, STATE_COLS), F32)],
        compiler_params=_params(dimension_semantics=("arbitrary",)),
        name="s5",
    )(uv, w["m_t"], w["w_r"], w["w_i"], w["e_r"], w["e_i"], w["a_sub"], w["levels"], glu_w, glu_b, g_out)
    return out.reshape(s, SSM_WIDTH)


def _mlp_kernel(x_ref, o1_ref, o2_ref, o3_ref, l1_ref, l2_ref, l3_ref, ssm_ref, ga_ref, wout_ref,
                g2_ref, wup_ref, wdn_ref, out_ref):
    def cat(ref):
        return jnp.concatenate([ref[t] for t in range(N_HEAD_TILES)], axis=1)

    l1, l2, l3 = cat(l1_ref), cat(l2_ref), cat(l3_ref)
    m = jnp.maximum(jnp.maximum(l1, l2), l3)
    w1, w2, w3 = jnp.exp(l1 - m), jnp.exp(l2 - m), jnp.exp(l3 - m)
    num = w1 * cat(o1_ref).astype(F32) + w2 * cat(o2_ref).astype(F32) + w3 * cat(o3_ref).astype(F32)
    attn = num / (w1 + w2 + w3)
    ms = jnp.mean(attn * attn, axis=-1, keepdims=True)
    attn_n = (attn * lax.rsqrt(ms + EPS) * ga_ref[...]).astype(BF16)
    mix = jnp.concatenate([attn_n, ssm_ref[...]], axis=1)
    x1 = x_ref[...] + jnp.dot(mix, wout_ref[...], preferred_element_type=F32)
    ms2 = jnp.mean(x1 * x1, axis=-1, keepdims=True)
    xn = (x1 * lax.rsqrt(ms2 + EPS) * g2_ref[...]).astype(BF16)
    acc = x1
    for c in range(D_FF // FF_CHUNK):
        h = jnp.dot(xn, wup_ref[:, c * FF_CHUNK:(c + 1) * FF_CHUNK], preferred_element_type=F32)
        h = jnp.square(jnp.maximum(h, 0.0)).astype(BF16)
        acc = acc + jnp.dot(h, wdn_ref[c * FF_CHUNK:(c + 1) * FF_CHUNK, :], preferred_element_type=F32)
    out_ref[...] = acc


def _mlp(x, os_, ls_, ssm, ga, w_out, g2, w_up, w_dn):
    s = x.shape[0]
    rows = MLP_ROWS
    xt = pl.BlockSpec((rows, D_MODEL), lambda i: (i, 0))
    ht = pl.BlockSpec((N_HEAD_TILES, rows, LANE_TILE), lambda i: (0, i, 0))
    return pl.pallas_call(
        _mlp_kernel,
        grid=(s // rows,),
        in_specs=[xt, ht, ht, ht, ht, ht, ht, pl.BlockSpec((rows, SSM_WIDTH), lambda i: (i, 0)),
                  _const_spec(ga.shape), _const_spec(w_out.shape), _const_spec(g2.shape),
                  _const_spec(w_up.shape), _const_spec(w_dn.shape)],
        out_specs=xt,
        out_shape=jax.ShapeDtypeStruct((s, D_MODEL), F32),
        compiler_params=_params(dimension_semantics=("arbitrary",)),
        name="mlp",
    )(x, *os_, *ls_, ssm, ga, w_out, g2, w_up, w_dn)


def _layer(x, norm1_g, w_in, q_norm_g, k_norm_g, ssm_a_re, ssm_a_im, ssm_log_dt, ssm_b_re, ssm_b_im,
           ssm_c_re, ssm_c_im, ssm_d, glu_w, glu_b, attn_out_norm_g, ssm_out_norm_g, w_out, norm2_g,
           w_mlp_up, w_mlp_down):
    row = lambda g: g.astype(F32).reshape(1, -1)
    heads = ATTN_WIDTH // HEAD_DIM
    head_id = jnp.arange(ATTN_WIDTH) // HEAD_DIM
    hmat = jnp.where(head_id[:, None] == head_id[None, :], 1.0 / HEAD_DIM, 0.0).astype(BF16)
    q, k, v, u = _proj(x, row(norm1_g), w_in.astype(BF16), hmat,
                       row(jnp.tile(q_norm_g, heads)), row(jnp.tile(k_norm_g, heads)))
    outs, lses = [], []
    for d in DILATIONS:
        o, lse = _attn_pattern(q, k, v, d)
        outs.append(o)
        lses.append(lse)
    w = _s5_weights(ssm_a_re, ssm_a_im, ssm_log_dt, ssm_b_re, ssm_b_im, ssm_c_re, ssm_c_im, ssm_d)
    ssm = _s5(u, w, glu_w.astype(BF16), row(glu_b), row(ssm_out_norm_g))
    return _mlp(x, outs, lses, ssm, row(attn_out_norm_g), w_out.astype(BF16), row(norm2_g),
                w_mlp_up.astype(BF16), w_mlp_down.astype(BF16))


def kernel(x, norm1_g, w_in, q_norm_g, k_norm_g, ssm_a_re, ssm_a_im, ssm_log_dt, ssm_b_re, ssm_b_im,
           ssm_c_re, ssm_c_im, ssm_d, glu_w, glu_b, attn_out_norm_g, ssm_out_norm_g, w_out, norm2_g,
           w_mlp_up, w_mlp_down):
    params = (norm1_g, w_in, q_norm_g, k_norm_g, ssm_a_re, ssm_a_im, ssm_log_dt, ssm_b_re, ssm_b_im,
              ssm_c_re, ssm_c_im, ssm_d, glu_w, glu_b, attn_out_norm_g, ssm_out_norm_g, w_out, norm2_g,
              w_mlp_up, w_mlp_down)
    batch = x.shape[0]
    outs = []
    for bi in range(batch):
        h = x[bi]
        for layer in range(norm1_g.shape[0]):
            h = _layer(h, *[p[layer] for p in params])
        outs.append(h)
    return jnp.stack(outs)
```

```python
import functools
import math

import jax
import jax.numpy as jnp
from jax import lax
from jax.experimental import pallas as pl
from jax.experimental.pallas import tpu as pltpu

F32 = jnp.float32
BF16 = jnp.bfloat16

D_MODEL = 1024
ATTN_WIDTH = 512
HEAD_DIM = 64
SSM_WIDTH = 512
SSM_GROUP = 16
SSM_GROUPS = 32
SSM_STATE = 64
D_FF = 4096
EPS = 1e-6
NEG_INF = -1e30
DILATIONS = (1, 4, 16)
BLOCK = 128

SLAB = 128
LANE_TILE = 256
HEADS_PER_TILE = LANE_TILE // HEAD_DIM
N_HEAD_TILES = ATTN_WIDTH // LANE_TILE
CHUNK = 16
SUB = 4
HALF_STATE = (SSM_GROUPS // 2) * SSM_STATE
STATE_COLS = SSM_GROUPS * SSM_STATE
VMEM_LIMIT = 56 * 1024 * 1024

PROJ_ROWS = 512
ATTN_ROWS = 1024
S5_ROWS = 256
MLP_ROWS = 512
FF_CHUNK = 1024


def _const_spec(shape):
    nd = len(shape)
    return pl.BlockSpec(shape, lambda *_: (0,) * nd, pipeline_mode=pl.Buffered(1))


def _params(**kw):
    return pltpu.CompilerParams(vmem_limit_bytes=VMEM_LIMIT, **kw)


def _proj_kernel(x_ref, g1_ref, w_ref, hmat_ref, gq_ref, gk_ref,
                 q1_ref, k1_ref, v1_ref, q4_ref, k4_ref, v4_ref, q16_ref, k16_ref, v16_ref, u_ref, slab_ref):
    rows = x_ref.shape[0]
    x = x_ref[...]
    ms = jnp.mean(x * x, axis=-1, keepdims=True)
    xn = (x * lax.rsqrt(ms + EPS) * g1_ref[...]).astype(BF16)
    proj = jnp.dot(xn, w_ref[...], preferred_element_type=F32)

    def head_norm(t, g):
        ms_h = jnp.dot((t * t).astype(BF16), hmat_ref[...], preferred_element_type=F32)
        return t * lax.rsqrt(ms_h + EPS) * g

    q = head_norm(proj[:, 0:ATTN_WIDTH], gq_ref[...]) * (HEAD_DIM ** -0.5)
    k = head_norm(proj[:, ATTN_WIDTH:2 * ATTN_WIDTH], gk_ref[...])
    v = proj[:, 2 * ATTN_WIDTH:3 * ATTN_WIDTH]
    u = proj[:, 3 * ATTN_WIDTH:]
    slabs_per_tensor = ATTN_WIDTH // SLAB
    for ti, val in enumerate((q, k, v, u)):
        for s in range(slabs_per_tensor):
            slab_ref[ti * slabs_per_tensor + s] = val[:, s * SLAB:(s + 1) * SLAB]

    outs = ((q1_ref, q4_ref, q16_ref), (k1_ref, k4_ref, k16_ref), (v1_ref, v4_ref, v16_ref))
    for ti, val in enumerate((q, k, v)):
        for t in range(N_HEAD_TILES):
            outs[ti][0][t, 0] = val[:, t * LANE_TILE:(t + 1) * LANE_TILE].astype(BF16)
        for s in range(slabs_per_tensor):
            t, lanes = _slab_home(s)
            for ref, d in zip(outs[ti][1:], DILATIONS[1:]):
                for r in range(d):
                    ref[t, r, :, lanes] = slab_ref[ti * slabs_per_tensor + s,
                                                   pl.ds(r, rows // d, stride=d), :].astype(BF16)
    for s in range(slabs_per_tensor):
        for t in range(CHUNK):
            u_ref[:, t * SSM_WIDTH + s * SLAB:t * SSM_WIDTH + (s + 1) * SLAB] = (
                slab_ref[3 * slabs_per_tensor + s, pl.ds(t, rows // CHUNK, stride=CHUNK), :].astype(BF16))


def _slab_home(s):
    t, half = divmod(s, LANE_TILE // SLAB)
    return t, slice(half * SLAB, (half + 1) * SLAB)


def _proj(x, g1, w_in, hmat, gq, gk):
    s = x.shape[0]
    rows = PROJ_ROWS
    out_specs, out_shape = [], []
    for d in DILATIONS:
        for _ in range(3):
            out_specs.append(pl.BlockSpec((N_HEAD_TILES, d, rows // d, LANE_TILE), lambda i: (0, 0, i, 0)))
            out_shape.append(jax.ShapeDtypeStruct((N_HEAD_TILES, d, s // d, LANE_TILE), BF16))
    out_specs.append(pl.BlockSpec((rows // CHUNK, CHUNK * SSM_WIDTH), lambda i: (i, 0)))
    out_shape.append(jax.ShapeDtypeStruct((s // CHUNK, CHUNK * SSM_WIDTH), BF16))
    return pl.pallas_call(
        _proj_kernel,
        grid=(s // rows,),
        in_specs=[
            pl.BlockSpec((rows, D_MODEL), lambda i: (i, 0)),
            _const_spec((1, D_MODEL)),
            _const_spec(w_in.shape),
            _const_spec(hmat.shape),
            _const_spec((1, ATTN_WIDTH)),
            _const_spec((1, ATTN_WIDTH)),
        ],
        out_specs=out_specs,
        out_shape=out_shape,
        scratch_shapes=[pltpu.VMEM((4 * ATTN_WIDTH // SLAB, rows, SLAB), F32)],
        compiler_params=_params(dimension_semantics=("arbitrary",)),
        name="proj",
    )(x, g1, w_in, hmat, gq, gk)


def _attn_kernel(q_ref, kc_ref, kp_ref, vc_ref, vp_ref, o_ref, lse_ref, kbuf, vbuf):
    rows = q_ref.shape[0]
    kbuf[0:BLOCK, :] = kp_ref[...]
    kbuf[BLOCK:, :] = kc_ref[...]
    vbuf[0:BLOCK, :] = vp_ref[...]
    vbuf[BLOCK:, :] = vc_ref[...]

    lane = lax.broadcasted_iota(jnp.int32, (BLOCK, LANE_TILE), 1)
    head_masks = [(lane >= h * HEAD_DIM) & (lane < (h + 1) * HEAD_DIM) for h in range(HEADS_PER_TILE)]
    qi = lax.broadcasted_iota(jnp.int32, (BLOCK, 2 * BLOCK), 0)
    ki = lax.broadcasted_iota(jnp.int32, (BLOCK, 2 * BLOCK), 1)
    band = (ki >= qi) & (ki <= qi + BLOCK)
    bias_band = jnp.where(band, 0.0, NEG_INF).astype(F32)
    no_prev = pl.program_id(2) == 0
    bias_first = jnp.where(band & ((ki >= BLOCK) | jnp.logical_not(no_prev)), 0.0, NEG_INF).astype(F32)

    for b in range(rows // BLOCK):
        qb = q_ref[b * BLOCK:(b + 1) * BLOCK, :]
        zero = jnp.zeros_like(qb)
        q_stack = jnp.concatenate([jnp.where(mk, qb, zero) for mk in head_masks], axis=0)
        kw = kbuf[b * BLOCK:(b + 2) * BLOCK, :]
        vw = vbuf[b * BLOCK:(b + 2) * BLOCK, :]
        s = lax.dot_general(q_stack, kw, (((1,), (1,)), ((), ())), preferred_element_type=F32)
        bias = bias_first if b == 0 else bias_band
        ps, ms, ls = [], [], []
        for h in range(HEADS_PER_TILE):
            sh = s[h * BLOCK:(h + 1) * BLOCK, :] + bias
            m = jnp.max(sh, axis=-1, keepdims=True)
            p = jnp.exp(sh - m)
            ls.append(jnp.sum(p, axis=-1, keepdims=True))
            ms.append(m)
            ps.append(p.astype(BF16))
        pv = jnp.dot(jnp.concatenate(ps, axis=0), vw, preferred_element_type=F32)
        o = jnp.zeros((BLOCK, LANE_TILE), F32)
        lse = jnp.zeros((BLOCK, LANE_TILE), F32)
        for h in range(HEADS_PER_TILE):
            o = jnp.where(head_masks[h], pv[h * BLOCK:(h + 1) * BLOCK, :] / ls[h], o)
            lse = jnp.where(head_masks[h], ms[h] + jnp.log(ls[h]), lse)
        o_ref[b * BLOCK:(b + 1) * BLOCK, :] = o.astype(BF16)
        lse_ref[b * BLOCK:(b + 1) * BLOCK, :] = lse


def _attn_pattern(q, k, v):
    nt, dilation, n, _ = q.shape
    rows = min(ATTN_ROWS, n)
    blocks_per_tile = rows // BLOCK
    cur = pl.BlockSpec((None, None, rows, LANE_TILE), lambda t, r, j: (t, r, j, 0))
    prev = pl.BlockSpec((None, None, BLOCK, LANE_TILE),
                        lambda t, r, j: (t, r, jnp.maximum(j * blocks_per_tile - 1, 0), 0))
    return pl.pallas_call(
        _attn_kernel,
        grid=(nt, dilation, n // rows),
        in_specs=[cur, cur, prev, cur, prev],
        out_specs=[cur, cur],
        out_shape=[jax.ShapeDtypeStruct(q.shape, BF16), jax.ShapeDtypeStruct(q.shape, F32)],
        scratch_shapes=[pltpu.VMEM((rows + BLOCK, LANE_TILE), BF16),
                        pltpu.VMEM((rows + BLOCK, LANE_TILE), BF16)],
        compiler_params=_params(dimension_semantics=("arbitrary", "arbitrary", "arbitrary")),
        name=f"attn_d{dilation}",
    )(q, k, k, v, v)


def _s5_weights(a_re, a_im, log_dt, b_re, b_im, c_re, c_im, d_skip):
    hp = lax.Precision.HIGHEST
    g, p, c = SSM_GROUPS, SSM_STATE, SSM_GROUP
    gl = g // 2
    lr, li = a_re.astype(F32), a_im.astype(F32)
    dt = jnp.exp(log_dt.astype(F32))[:, None]

    def apow(j):
        mag = jnp.exp(lr * dt * j)
        return mag * jnp.cos(li * dt * j), mag * jnp.sin(li * dt * j)

    ab_r, ab_i = apow(1.0)
    den = lr * lr + li * li
    nr, ni = ab_r - 1.0, ab_i
    cr = (nr * lr + ni * li) / den
    ci = (ni * lr - nr * li) / den
    br, bi = b_re.astype(F32), b_im.astype(F32)
    bb_r = cr[..., None] * br - ci[..., None] * bi
    bb_i = cr[..., None] * bi + ci[..., None] * br
    cre, cim = c_re.astype(F32), c_im.astype(F32)

    def apow_many(js):
        j = jnp.asarray(js, F32)[:, None, None]
        mag = jnp.exp(lr * dt * j)
        return mag * jnp.cos(li * dt * j), mag * jnp.sin(li * dt * j)

    def block_diag(w, inner_rows, inner_cols):
        tiled = jnp.tile(w, (1, 1, 1, gl))
        rg = lax.broadcasted_iota(jnp.int32, tiled.shape, 2) // inner_rows
        cg = lax.broadcasted_iota(jnp.int32, tiled.shape, 3) // inner_cols
        return jnp.where(rg == cg, tiled, 0.0).astype(BF16)

    pr, pi = apow_many(range(CHUNK))
    ca_r = cre[None] * pr[:, :, None, :] - cim[None] * pi[:, :, None, :]
    ca_i = cre[None] * pi[:, :, None, :] + cim[None] * pr[:, :, None, :]

    lags = (jnp.einsum('jgop,gpi->jgoi', ca_r, bb_r, precision=hp)
            - jnp.einsum('jgop,gpi->jgoi', ca_i, bb_i, precision=hp))
    skip = d_skip.astype(F32)[:, :, None] * jnp.eye(c, dtype=F32)[None]
    lags = lags.at[0].add(skip)
    m_t = block_diag(lags.transpose(0, 1, 3, 2).reshape(CHUNK, 2, gl * c, c), c, c)

    wp_r, wp_i = pr[SUB - 1::-1], pi[SUB - 1::-1]
    win_r = wp_r[..., None] * bb_r[None] - wp_i[..., None] * bb_i[None]
    win_i = wp_r[..., None] * bb_i[None] + wp_i[..., None] * bb_r[None]
    to_in = lambda w: block_diag(w.transpose(0, 1, 3, 2).reshape(SUB, 2, gl * c, p), c, p)
    to_out = lambda w: block_diag(w.transpose(0, 1, 3, 2).reshape(SUB, 2, gl * p, c), p, c)

    flat = lambda re, im: jnp.stack([re.reshape(-1), im.reshape(-1)])
    top_r, top_i = apow_many([SUB, CHUNK])
    levels = [flat(top_r[1], top_i[1])]
    n_levels = int(math.log2(S5_ROWS))
    for _ in range(n_levels - 1):
        lr_, li_ = levels[-1][0], levels[-1][1]
        levels.append(jnp.stack([lr_ * lr_ - li_ * li_, 2.0 * lr_ * li_]))
    return dict(
        m_t=m_t,
        w_r=to_in(win_r), w_i=to_in(win_i),
        e_r=to_out(ca_r[1:SUB + 1]), e_i=to_out(-ca_i[1:SUB + 1]),
        a_sub=flat(top_r[0], top_i[0]),
        levels=jnp.stack(levels),
    )


def _cmul(ar, ai, br, bi):
    return ar * br - ai * bi, ar * bi + ai * br


def _s5_kernel(u_ref, mt_ref, wr_ref, wi_ref, er_ref, ei_ref, asub_ref, lvl_ref,
               gluw_ref, glub_ref, gout_ref, out_ref, carry_ref, zr_ref, zi_ref):
    rows = u_ref.shape[0]
    width = SSM_WIDTH

    @pl.when(pl.program_id(0) == 0)
    def _():
        carry_ref[...] = jnp.zeros_like(carry_ref)

    def u_tile(t, h):
        return u_ref[:, t * width + h * LANE_TILE:t * width + (h + 1) * LANE_TILE]

    a_r, a_i = asub_ref[0:1, :], asub_ref[1:2, :]

    for a in range(SUB):
        for h in range(2):
            cols = slice(h * HALF_STATE, (h + 1) * HALF_STATE)
            pr = pi = None
            for b in range(SUB):
                ut = u_tile(a * SUB + b, h)
                dr = jnp.dot(ut, wr_ref[b, h], preferred_element_type=F32)
                di = jnp.dot(ut, wi_ref[b, h], preferred_element_type=F32)
                pr = dr if pr is None else pr + dr
                pi = di if pi is None else pi + di
            if a == 0:
                zr_ref[:, cols] = pr
                zi_ref[:, cols] = pi
            else:
                hr, hi = _cmul(zr_ref[:, cols], zi_ref[:, cols], a_r[:, cols], a_i[:, cols])
                zr_ref[:, cols] = hr + pr
                zi_ref[:, cols] = hi + pi

    row = lax.broadcasted_iota(jnp.int32, (rows, 1), 0)
    c_r, c_i = carry_ref[0:1, :], carry_ref[1:2, :]
    lam_r, lam_i = lvl_ref[0, 0:1, :], lvl_ref[0, 1:2, :]
    in_r, in_i = _cmul(c_r, c_i, lam_r, lam_i)
    sr = zr_ref[...] + jnp.where(row == 0, in_r, 0.0)
    si = zi_ref[...] + jnp.where(row == 0, in_i, 0.0)
    for lev in range(lvl_ref.shape[0]):
        sh = 1 << lev
        keep = row >= sh
        pr_, pi_ = lvl_ref[lev, 0:1, :], lvl_ref[lev, 1:2, :]
        tr = jnp.where(keep, pltpu.roll(sr, sh, axis=0), 0.0)
        ti = jnp.where(keep, pltpu.roll(si, sh, axis=0), 0.0)
        mr, mi = _cmul(tr, ti, pr_, pi_)
        sr, si = sr + mr, si + mi
    xr = jnp.where(row == 0, c_r, pltpu.roll(sr, 1, axis=0))
    xi = jnp.where(row == 0, c_i, pltpu.roll(si, 1, axis=0))
    carry_ref[0:1, :] = sr[rows - 1:rows, :]
    carry_ref[1:2, :] = si[rows - 1:rows, :]
    zr_ref[...] = xr
    zi_ref[...] = xi

    for a in range(SUB):
        if a > 0:
            nr, ni = _cmul(zr_ref[...], zi_ref[...], a_r, a_i)
            zr_ref[...] = nr
            zi_ref[...] = ni
        xr_b = zr_ref[...].astype(BF16)
        xi_b = zi_ref[...].astype(BF16)
        for b in range(SUB):
            t = a * SUB + b
            halves = []
            for h in range(2):
                cols = slice(h * HALF_STATE, (h + 1) * HALF_STATE)
                acc = jnp.dot(xr_b[:, cols], er_ref[b, h], preferred_element_type=F32)
                acc = acc + jnp.dot(xi_b[:, cols], ei_ref[b, h], preferred_element_type=F32)
                for t_in in range(t + 1):
                    acc = acc + jnp.dot(u_tile(t_in, h), mt_ref[t - t_in, h], preferred_element_type=F32)
                halves.append(acc)
            y = jnp.concatenate(halves, axis=1)
            z = 0.5 * y * (1.0 + jnp.tanh(math.sqrt(2.0 / math.pi) * (y + 0.044715 * (y * y * y))))
            gate = jnp.dot(z.astype(BF16), gluw_ref[...], preferred_element_type=F32) + glub_ref[...]
            o = z * (1.0 / (1.0 + jnp.exp(-gate)))
            ms = jnp.mean(o * o, axis=-1, keepdims=True)
            out_ref[:, t * width:(t + 1) * width] = (o * lax.rsqrt(ms + EPS) * gout_ref[...]).astype(BF16)


def _s5(uv, w, glu_w, glu_b, g_out):
    n = uv.shape[0]
    rows = S5_ROWS
    tile = pl.BlockSpec((rows, CHUNK * SSM_WIDTH), lambda i: (i, 0))
    return pl.pallas_call(
        _s5_kernel,
        grid=(n // rows,),
        in_specs=[tile] + [_const_spec(a.shape) for a in
                           (w["m_t"], w["w_r"], w["w_i"], w["e_r"], w["e_i"], w["a_sub"], w["levels"],
                            glu_w, glu_b, g_out)],
        out_specs=tile,
        out_shape=jax.ShapeDtypeStruct((n, CHUNK * SSM_WIDTH), BF16),
        scratch_shapes=[pltpu.VMEM((2, STATE_COLS), F32),
                        pltpu.VMEM((rows, STATE_COLS), F32),
                        pltpu.VMEM((rows, STATE_COLS), F32)],
        compiler_params=_params(dimension_semantics=("arbitrary",)),
        name="s5",
    )(uv, w["m_t"], w["w_r"], w["w_i"], w["e_r"], w["e_i"], w["a_sub"], w["levels"], glu_w, glu_b, g_out)


def _mlp_kernel(x_ref, o1_ref, o2_ref, o3_ref, l1_ref, l2_ref, l3_ref, ssm_ref, ga_ref, wout_ref,
                g2_ref, wup_ref, wdn_ref, out_ref, slab_ref):
    rows = x_ref.shape[0]
    n_slabs = ATTN_WIDTH // SLAB

    def natural(ref, d, buf):
        if d == 1:
            return jnp.concatenate([ref[t, 0] for t in range(N_HEAD_TILES)], axis=1).astype(F32)
        for s in range(n_slabs):
            t, lanes = _slab_home(s)
            for r in range(d):
                slab_ref[buf, s, pl.ds(r, rows // d, stride=d), :] = ref[t, r, :, lanes].astype(F32)
        return jnp.concatenate([slab_ref[buf, s] for s in range(n_slabs)], axis=1)

    o1, l1 = natural(o1_ref, 1, 0), natural(l1_ref, 1, 0)
    o2, l2 = natural(o2_ref, DILATIONS[1], 0), natural(l2_ref, DILATIONS[1], 1)
    o3, l3 = natural(o3_ref, DILATIONS[2], 2), natural(l3_ref, DILATIONS[2], 3)
    for s in range(n_slabs):
        for t in range(CHUNK):
            slab_ref[4, s, pl.ds(t, rows // CHUNK, stride=CHUNK), :] = (
                ssm_ref[:, t * SSM_WIDTH + s * SLAB:t * SSM_WIDTH + (s + 1) * SLAB].astype(F32))
    ssm = jnp.concatenate([slab_ref[4, s] for s in range(n_slabs)], axis=1).astype(BF16)

    m = jnp.maximum(jnp.maximum(l1, l2), l3)
    w1, w2, w3 = jnp.exp(l1 - m), jnp.exp(l2 - m), jnp.exp(l3 - m)
    attn = (w1 * o1 + w2 * o2 + w3 * o3) / (w1 + w2 + w3)
    ms = jnp.mean(attn * attn, axis=-1, keepdims=True)
    attn_n = (attn * lax.rsqrt(ms + EPS) * ga_ref[...]).astype(BF16)
    mix = jnp.concatenate([attn_n, ssm], axis=1)
    x1 = x_ref[...] + jnp.dot(mix, wout_ref[...], preferred_element_type=F32)
    ms2 = jnp.mean(x1 * x1, axis=-1, keepdims=True)
    xn = (x1 * lax.rsqrt(ms2 + EPS) * g2_ref[...]).astype(BF16)
    acc = x1
    for c in range(D_FF // FF_CHUNK):
        h = jnp.dot(xn, wup_ref[:, c * FF_CHUNK:(c + 1) * FF_CHUNK], preferred_element_type=F32)
        h = jnp.square(jnp.maximum(h, 0.0)).astype(BF16)
        acc = acc + jnp.dot(h, wdn_ref[c * FF_CHUNK:(c + 1) * FF_CHUNK, :], preferred_element_type=F32)
    out_ref[...] = acc


def _mlp(x, os_, ls_, ssm, ga, w_out, g2, w_up, w_dn):
    s = x.shape[0]
    rows = MLP_ROWS
    xt = pl.BlockSpec((rows, D_MODEL), lambda i: (i, 0))
    ht = [pl.BlockSpec((N_HEAD_TILES, d, rows // d, LANE_TILE), lambda i: (0, 0, i, 0)) for d in DILATIONS]
    return pl.pallas_call(
        _mlp_kernel,
        grid=(s // rows,),
        in_specs=[xt, *ht, *ht, pl.BlockSpec((rows // CHUNK, CHUNK * SSM_WIDTH), lambda i: (i, 0)),
                  _const_spec(ga.shape), _const_spec(w_out.shape), _const_spec(g2.shape),
                  _const_spec(w_up.shape), _const_spec(w_dn.shape)],
        out_specs=xt,
        out_shape=jax.ShapeDtypeStruct((s, D_MODEL), F32),
        scratch_shapes=[pltpu.VMEM((5, ATTN_WIDTH // SLAB, rows, SLAB), F32)],
        compiler_params=_params(dimension_semantics=("arbitrary",)),
        name="mlp",
    )(x, *os_, *ls_, ssm, ga, w_out, g2, w_up, w_dn)


def _layer(x, norm1_g, w_in, q_norm_g, k_norm_g, ssm_a_re, ssm_a_im, ssm_log_dt, ssm_b_re, ssm_b_im,
           ssm_c_re, ssm_c_im, ssm_d, glu_w, glu_b, attn_out_norm_g, ssm_out_norm_g, w_out, norm2_g,
           w_mlp_up, w_mlp_down):
    row = lambda g: g.astype(F32).reshape(1, -1)
    heads = ATTN_WIDTH // HEAD_DIM
    head_id = jnp.arange(ATTN_WIDTH) // HEAD_DIM
    hmat = jnp.where(head_id[:, None] == head_id[None, :], 1.0 / HEAD_DIM, 0.0).astype(BF16)
    *qkv, u = _proj(x, row(norm1_g), w_in.astype(BF16), hmat,
                    row(jnp.tile(q_norm_g, heads)), row(jnp.tile(k_norm_g, heads)))
    outs, lses = [], []
    for i in range(len(DILATIONS)):
        o, lse = _attn_pattern(*qkv[3 * i:3 * i + 3])
        outs.append(o)
        lses.append(lse)
    w = _s5_weights(ssm_a_re, ssm_a_im, ssm_log_dt, ssm_b_re, ssm_b_im, ssm_c_re, ssm_c_im, ssm_d)
    ssm = _s5(u, w, glu_w.astype(BF16), row(glu_b), row(ssm_out_norm_g))
    return _mlp(x, outs, lses, ssm, row(attn_out_norm_g), w_out.astype(BF16), row(norm2_g),
                w_mlp_up.astype(BF16), w_mlp_down.astype(BF16))


def kernel(x, norm1_g, w_in, q_norm_g, k_norm_g, ssm_a_re, ssm_a_im, ssm_log_dt, ssm_b_re, ssm_b_im,
           ssm_c_re, ssm_c_im, ssm_d, glu_w, glu_b, attn_out_norm_g, ssm_out_norm_g, w_out, norm2_g,
           w_mlp_up, w_mlp_down):
    params = (norm1_g, w_in, q_norm_g, k_norm_g, ssm_a_re, ssm_a_im, ssm_log_dt, ssm_b_re, ssm_b_im,
              ssm_c_re, ssm_c_im, ssm_d, glu_w, glu_b, attn_out_norm_g, ssm_out_norm_g, w_out, norm2_g,
              w_mlp_up, w_mlp_down)
    batch = x.shape[0]
    outs = []
    for bi in range(batch):
        h = x[bi]
        for layer in range(norm1_g.shape[0]):
            h = _layer(h, *[p[layer] for p in params])
        outs.append(h)
    return jnp.stack(outs)
```

```python
import functools
import math

import jax
import jax.numpy as jnp
from jax import lax
from jax.experimental import pallas as pl
from jax.experimental.pallas import tpu as pltpu

F32 = jnp.float32
BF16 = jnp.bfloat16

D_MODEL = 1024
ATTN_WIDTH = 512
HEAD_DIM = 64
SSM_WIDTH = 512
SSM_GROUP = 16
SSM_GROUPS = 32
SSM_STATE = 64
D_FF = 4096
EPS = 1e-6
NEG_INF = -1e30
DILATIONS = (1, 4, 16)
BLOCK = 128

SLAB = 128
LANE_TILE = 256
HEADS_PER_TILE = LANE_TILE // HEAD_DIM
N_HEAD_TILES = ATTN_WIDTH // LANE_TILE
CHUNK = 16
SUB = 4
HALF_STATE = (SSM_GROUPS // 2) * SSM_STATE
STATE_COLS = SSM_GROUPS * SSM_STATE
VMEM_LIMIT = 56 * 1024 * 1024

PROJ_ROWS = 512
ATTN_ROWS = 1024
S5_ROWS = 256
MLP_ROWS = 512
FF_CHUNK = 1024


def _const_spec(shape):
    nd = len(shape)
    return pl.BlockSpec(shape, lambda *_: (0,) * nd, pipeline_mode=pl.Buffered(1))


def _params(**kw):
    return pltpu.CompilerParams(vmem_limit_bytes=VMEM_LIMIT, **kw)


def _proj_kernel(x_ref, g1_ref, w_ref, hmat_ref, gq_ref, gk_ref,
                 q1_ref, k1_ref, v1_ref, q4_ref, k4_ref, v4_ref, q16_ref, k16_ref, v16_ref, u_ref,
                 slab_ref, slab4_ref):
    rows = x_ref.shape[0]
    x = x_ref[...]
    ms = jnp.mean(x * x, axis=-1, keepdims=True)
    xn = (x * lax.rsqrt(ms + EPS) * g1_ref[...]).astype(BF16)
    proj = jnp.dot(xn, w_ref[...], preferred_element_type=F32)

    def head_norm(t, g):
        ms_h = jnp.dot((t * t).astype(BF16), hmat_ref[...], preferred_element_type=F32)
        return t * lax.rsqrt(ms_h + EPS) * g

    q = head_norm(proj[:, 0:ATTN_WIDTH], gq_ref[...]) * (HEAD_DIM ** -0.5)
    k = head_norm(proj[:, ATTN_WIDTH:2 * ATTN_WIDTH], gk_ref[...])
    v = proj[:, 2 * ATTN_WIDTH:3 * ATTN_WIDTH]
    u = proj[:, 3 * ATTN_WIDTH:]
    slabs_per_tensor = ATTN_WIDTH // SLAB
    for ti, val in enumerate((q, k, v, u)):
        for s in range(slabs_per_tensor):
            slab_ref[ti * slabs_per_tensor + s] = val[:, s * SLAB:(s + 1) * SLAB]

    outs = ((q1_ref, q4_ref, q16_ref), (k1_ref, k4_ref, k16_ref), (v1_ref, v4_ref, v16_ref))
    for ti, val in enumerate((q, k, v)):
        for t in range(N_HEAD_TILES):
            outs[ti][0][t, 0] = val[:, t * LANE_TILE:(t + 1) * LANE_TILE].astype(BF16)
    step = DILATIONS[1]
    for ti in range(4):
        for s in range(slabs_per_tensor):
            t, lanes = _slab_home(s)
            for r4 in range(step):
                part = slab_ref[ti * slabs_per_tensor + s, pl.ds(r4, rows // step, stride=step), :]
                if ti < 3:
                    outs[ti][1][t, r4, :, lanes] = part.astype(BF16)
                slab4_ref[r4] = part
            for r4 in range(step):
                for j4 in range(step):
                    r16 = step * j4 + r4
                    part = slab4_ref[r4, pl.ds(j4, rows // CHUNK, stride=step), :].astype(BF16)
                    if ti < 3:
                        outs[ti][2][t, r16, :, lanes] = part
                    else:
                        u_ref[:, r16 * SSM_WIDTH + s * SLAB:r16 * SSM_WIDTH + (s + 1) * SLAB] = part


def _slab_home(s):
    t, half = divmod(s, LANE_TILE // SLAB)
    return t, slice(half * SLAB, (half + 1) * SLAB)


def _proj(x, g1, w_in, hmat, gq, gk):
    s = x.shape[0]
    rows = PROJ_ROWS
    out_specs, out_shape = [], []
    for d in DILATIONS:
        for _ in range(3):
            out_specs.append(pl.BlockSpec((N_HEAD_TILES, d, rows // d, LANE_TILE), lambda i: (0, 0, i, 0)))
            out_shape.append(jax.ShapeDtypeStruct((N_HEAD_TILES, d, s // d, LANE_TILE), BF16))
    out_specs.append(pl.BlockSpec((rows // CHUNK, CHUNK * SSM_WIDTH), lambda i: (i, 0)))
    out_shape.append(jax.ShapeDtypeStruct((s // CHUNK, CHUNK * SSM_WIDTH), BF16))
    return pl.pallas_call(
        _proj_kernel,
        grid=(s // rows,),
        in_specs=[
            pl.BlockSpec((rows, D_MODEL), lambda i: (i, 0)),
            _const_spec((1, D_MODEL)),
            _const_spec(w_in.shape),
            _const_spec(hmat.shape),
            _const_spec((1, ATTN_WIDTH)),
            _const_spec((1, ATTN_WIDTH)),
        ],
        out_specs=out_specs,
        out_shape=out_shape,
        scratch_shapes=[pltpu.VMEM((4 * ATTN_WIDTH // SLAB, rows, SLAB), F32),
                        pltpu.VMEM((DILATIONS[1], rows // DILATIONS[1], SLAB), F32)],
        compiler_params=_params(dimension_semantics=("arbitrary",)),
        name="proj",
    )(x, g1, w_in, hmat, gq, gk)


def _attn_kernel(q_ref, kc_ref, kp_ref, vc_ref, vp_ref, o_ref, lse_ref, kbuf, vbuf):
    rows = q_ref.shape[0]
    kbuf[0:BLOCK, :] = kp_ref[...]
    kbuf[BLOCK:, :] = kc_ref[...]
    vbuf[0:BLOCK, :] = vp_ref[...]
    vbuf[BLOCK:, :] = vc_ref[...]

    lane = lax.broadcasted_iota(jnp.int32, (BLOCK, LANE_TILE), 1)
    head_masks = [(lane >= h * HEAD_DIM) & (lane < (h + 1) * HEAD_DIM) for h in range(HEADS_PER_TILE)]
    qi = lax.broadcasted_iota(jnp.int32, (BLOCK, 2 * BLOCK), 0)
    ki = lax.broadcasted_iota(jnp.int32, (BLOCK, 2 * BLOCK), 1)
    band = (ki >= qi) & (ki <= qi + BLOCK)
    bias_band = jnp.where(band, 0.0, NEG_INF).astype(F32)
    no_prev = pl.program_id(2) == 0
    bias_first = jnp.where(band & ((ki >= BLOCK) | jnp.logical_not(no_prev)), 0.0, NEG_INF).astype(F32)

    for b in range(rows // BLOCK):
        qb = q_ref[b * BLOCK:(b + 1) * BLOCK, :]
        zero = jnp.zeros_like(qb)
        q_stack = jnp.concatenate([jnp.where(mk, qb, zero) for mk in head_masks], axis=0)
        kw = kbuf[b * BLOCK:(b + 2) * BLOCK, :]
        vw = vbuf[b * BLOCK:(b + 2) * BLOCK, :]
        s = lax.dot_general(q_stack, kw, (((1,), (1,)), ((), ())), preferred_element_type=F32)
        bias = bias_first if b == 0 else bias_band
        ps, ms, ls = [], [], []
        for h in range(HEADS_PER_TILE):
            sh = s[h * BLOCK:(h + 1) * BLOCK, :] + bias
            m = jnp.max(sh, axis=-1, keepdims=True)
            p = jnp.exp(sh - m)
            ls.append(jnp.sum(p, axis=-1, keepdims=True))
            ms.append(m)
            ps.append(p.astype(BF16))
        pv = jnp.dot(jnp.concatenate(ps, axis=0), vw, preferred_element_type=F32)
        o = jnp.zeros((BLOCK, LANE_TILE), F32)
        lse = jnp.zeros((BLOCK, LANE_TILE), F32)
        for h in range(HEADS_PER_TILE):
            o = jnp.where(head_masks[h], pv[h * BLOCK:(h + 1) * BLOCK, :] / ls[h], o)
            lse = jnp.where(head_masks[h], ms[h] + jnp.log(ls[h]), lse)
        o_ref[b * BLOCK:(b + 1) * BLOCK, :] = o.astype(BF16)
        lse_ref[b * BLOCK:(b + 1) * BLOCK, :] = lse


def _attn_pattern(q, k, v):
    nt, dilation, n, _ = q.shape
    rows = min(ATTN_ROWS, n)
    blocks_per_tile = rows // BLOCK
    cur = pl.BlockSpec((None, None, rows, LANE_TILE), lambda t, r, j: (t, r, j, 0))
    prev = pl.BlockSpec((None, None, BLOCK, LANE_TILE),
                        lambda t, r, j: (t, r, jnp.maximum(j * blocks_per_tile - 1, 0), 0))
    return pl.pallas_call(
        _attn_kernel,
        grid=(nt, dilation, n // rows),
        in_specs=[cur, cur, prev, cur, prev],
        out_specs=[cur, cur],
        out_shape=[jax.ShapeDtypeStruct(q.shape, BF16), jax.ShapeDtypeStruct(q.shape, F32)],
        scratch_shapes=[pltpu.VMEM((rows + BLOCK, LANE_TILE), BF16),
                        pltpu.VMEM((rows + BLOCK, LANE_TILE), BF16)],
        compiler_params=_params(dimension_semantics=("arbitrary", "arbitrary", "arbitrary")),
        name=f"attn_d{dilation}",
    )(q, k, k, v, v)


def _s5_weights(a_re, a_im, log_dt, b_re, b_im, c_re, c_im, d_skip):
    g, p, c = SSM_GROUPS, SSM_STATE, SSM_GROUP
    gl = g // 2
    lr, li = a_re.astype(F32), a_im.astype(F32)
    dt = jnp.exp(log_dt.astype(F32))[:, None]

    def apow(j):
        mag = jnp.exp(lr * dt * j)
        return mag * jnp.cos(li * dt * j), mag * jnp.sin(li * dt * j)

    ab_r, ab_i = apow(1.0)
    den = lr * lr + li * li
    nr, ni = ab_r - 1.0, ab_i
    cr = (nr * lr + ni * li) / den
    ci = (ni * lr - nr * li) / den
    br, bi = b_re.astype(F32), b_im.astype(F32)
    bb_r = cr[..., None] * br - ci[..., None] * bi
    bb_i = cr[..., None] * bi + ci[..., None] * br
    cre, cim = c_re.astype(F32), c_im.astype(F32)

    def apow_many(js):
        j = jnp.asarray(js, F32)[:, None, None]
        mag = jnp.exp(lr * dt * j)
        return mag * jnp.cos(li * dt * j), mag * jnp.sin(li * dt * j)

    pr, pi = apow_many(range(SUB + 1))
    wp_r, wp_i = pr[SUB - 1::-1], pi[SUB - 1::-1]
    win_r = wp_r[..., None] * bb_r[None] - wp_i[..., None] * bb_i[None]
    win_i = wp_r[..., None] * bb_i[None] + wp_i[..., None] * bb_r[None]
    pad_lanes = lambda w: jnp.pad(w, ((0, 0), (0, 0), (0, 0), (0, SLAB - w.shape[-1])))
    to_in = lambda w: pad_lanes(w.transpose(0, 1, 3, 2).reshape(SUB, 2, gl * c, p))
    out_r = cre[None] * pr[:, :, None, :] - cim[None] * pi[:, :, None, :]
    out_i = -(cre[None] * pi[:, :, None, :] + cim[None] * pr[:, :, None, :])
    to_out = lambda w: pad_lanes(w.transpose(0, 1, 3, 2).reshape(SUB + 1, 2, gl * p, c))

    flat = lambda re, im: jnp.stack([re.reshape(-1), im.reshape(-1)])
    top_r, top_i = apow_many([CHUNK])
    levels = [flat(top_r[0], top_i[0])]
    n_levels = int(math.log2(S5_ROWS))
    for _ in range(n_levels - 1):
        lr_, li_ = levels[-1][0], levels[-1][1]
        levels.append(jnp.stack([lr_ * lr_ - li_ * li_, 2.0 * lr_ * li_]))
    m_t, w_r, w_i, e_r, e_i = _s5_expand(to_in(win_r), to_in(win_i), to_out(out_r), to_out(out_i),
                                         d_skip.astype(F32).reshape(2, 1, gl * c))
    return dict(m_t=m_t, w_r=w_r, w_i=w_i, e_r=e_r, e_i=e_i,
                a_sub=flat(pr[SUB], pi[SUB]), levels=jnp.stack(levels))


def _s5_expand_kernel(wr_ref, wi_ref, or_ref, oi_ref, d_ref, mt_ref, wfr_ref, wfi_ref, efr_ref, efi_ref):
    n_in, n_out = wr_ref.shape[2], or_ref.shape[2]
    p, c = SSM_STATE, SSM_GROUP

    def rep(inner, total):
        r = lax.broadcasted_iota(jnp.int32, (SLAB, total), 0)
        col = lax.broadcasted_iota(jnp.int32, (SLAB, total), 1)
        return jnp.where((col & (inner - 1)) == r, 1.0, 0.0).astype(BF16)

    def group_mask(n_rows, rows_per_group, n_cols, cols_per_group):
        r = lax.broadcasted_iota(jnp.int32, (n_rows, n_cols), 0) >> int(math.log2(rows_per_group))
        col = lax.broadcasted_iota(jnp.int32, (n_rows, n_cols), 1) >> int(math.log2(cols_per_group))
        return r == col

    rep_in, rep_out = rep(p, n_out), rep(c, n_in)
    mask_in = group_mask(n_in, c, n_out, p)
    mask_out = group_mask(n_out, p, n_in, c)

    def split(x):
        hi = x.astype(BF16)
        return hi, (x - hi.astype(F32)).astype(BF16)

    def expand(x_b, rep_m, mask):
        full = jnp.dot(x_b, rep_m, preferred_element_type=F32)
        return jnp.where(mask, full, 0.0).astype(BF16)

    diag = (lax.broadcasted_iota(jnp.int32, (n_in, n_in), 0) == lax.broadcasted_iota(jnp.int32, (n_in, n_in), 1))
    for h in range(2):
        in0 = []
        for b in range(SUB):
            for src, dst in ((wr_ref, wfr_ref), (wi_ref, wfi_ref)):
                hi, lo = split(src[b, h])
                full = expand(hi, rep_in, mask_in)
                dst[b, h] = full
                if b == SUB - 1:
                    in0.append((full, expand(lo, rep_in, mask_in)))
        for j in range(SUB + 1):
            lag = None
            for (src, dst), (in_hi, in_lo) in zip(((or_ref, efr_ref), (oi_ref, efi_ref)), in0):
                hi, lo = split(src[j, h])
                out_hi = expand(hi, rep_out, mask_out)
                if j > 0:
                    dst[j - 1, h] = out_hi
                if j < SUB:
                    out_lo = expand(lo, rep_out, mask_out)
                    part = (jnp.dot(in_hi, out_hi, preferred_element_type=F32)
                            + jnp.dot(in_hi, out_lo, preferred_element_type=F32)
                            + jnp.dot(in_lo, out_hi, preferred_element_type=F32))
                    lag = part if lag is None else lag + part
            if j == 0:
                lag = lag + jnp.where(diag, d_ref[h], 0.0)
            if j < SUB:
                mt_ref[j, h] = lag.astype(BF16)


def _s5_expand(wc_r, wc_i, oc_r, oc_i, d_row):
    n_in, n_out = wc_r.shape[2], oc_r.shape[2]
    shapes = [jax.ShapeDtypeStruct((SUB, 2, n_in, n_in), BF16),
              jax.ShapeDtypeStruct((SUB, 2, n_in, n_out), BF16), jax.ShapeDtypeStruct((SUB, 2, n_in, n_out), BF16),
              jax.ShapeDtypeStruct((SUB, 2, n_out, n_in), BF16), jax.ShapeDtypeStruct((SUB, 2, n_out, n_in), BF16)]
    return pl.pallas_call(
        _s5_expand_kernel,
        out_shape=shapes,
        compiler_params=_params(),
        name="s5_expand",
    )(wc_r, wc_i, oc_r, oc_i, d_row)


def _cmul(ar, ai, br, bi):
    return ar * br - ai * bi, ar * bi + ai * br


def _s5_kernel(u_ref, mt_ref, wr_ref, wi_ref, er_ref, ei_ref, asub_ref, lvl_ref,
               gluw_ref, glub_ref, gout_ref, out_ref, carry_ref, zr_ref, zi_ref, locr_ref, loci_ref):
    rows = u_ref.shape[0]
    width = SSM_WIDTH

    @pl.when(pl.program_id(0) == 0)
    def _():
        carry_ref[...] = jnp.zeros_like(carry_ref)

    def u_tile(t, h):
        return u_ref[:, t * width + h * LANE_TILE:t * width + (h + 1) * LANE_TILE]

    a_r, a_i = asub_ref[0:1, :], asub_ref[1:2, :]

    for a in range(SUB):
        for h in range(2):
            cols = slice(h * HALF_STATE, (h + 1) * HALF_STATE)
            pr = pi = None
            for b in range(SUB):
                ut = u_tile(a * SUB + b, h)
                dr = jnp.dot(ut, wr_ref[b, h], preferred_element_type=F32)
                di = jnp.dot(ut, wi_ref[b, h], preferred_element_type=F32)
                pr = dr if pr is None else pr + dr
                pi = di if pi is None else pi + di
            if a == 0:
                zr_ref[:, cols] = pr
                zi_ref[:, cols] = pi
            else:
                hr, hi = _cmul(zr_ref[:, cols], zi_ref[:, cols], a_r[:, cols], a_i[:, cols])
                pr, pi = hr + pr, hi + pi
                zr_ref[:, cols] = pr
                zi_ref[:, cols] = pi
            if a < SUB - 1:
                locr_ref[a, :, cols] = pr.astype(BF16)
                loci_ref[a, :, cols] = pi.astype(BF16)

    row = lax.broadcasted_iota(jnp.int32, (rows, 1), 0)
    c_r, c_i = carry_ref[0:1, :], carry_ref[1:2, :]
    lam_r, lam_i = lvl_ref[0, 0:1, :], lvl_ref[0, 1:2, :]
    in_r, in_i = _cmul(c_r, c_i, lam_r, lam_i)
    sr = zr_ref[...] + jnp.where(row == 0, in_r, 0.0)
    si = zi_ref[...] + jnp.where(row == 0, in_i, 0.0)
    for lev in range(lvl_ref.shape[0]):
        sh = 1 << lev
        keep = row >= sh
        pr_, pi_ = lvl_ref[lev, 0:1, :], lvl_ref[lev, 1:2, :]
        tr = jnp.where(keep, pltpu.roll(sr, sh, axis=0), 0.0)
        ti = jnp.where(keep, pltpu.roll(si, sh, axis=0), 0.0)
        mr, mi = _cmul(tr, ti, pr_, pi_)
        sr, si = sr + mr, si + mi
    xr = jnp.where(row == 0, c_r, pltpu.roll(sr, 1, axis=0))
    xi = jnp.where(row == 0, c_i, pltpu.roll(si, 1, axis=0))
    carry_ref[0:1, :] = sr[rows - 1:rows, :]
    carry_ref[1:2, :] = si[rows - 1:rows, :]
    zr_ref[...] = xr
    zi_ref[...] = xi

    for a in range(SUB):
        if a > 0:
            nr, ni = _cmul(zr_ref[...], zi_ref[...], a_r, a_i)
            zr_ref[...] = nr
            zi_ref[...] = ni
        if a == 0:
            xr_b = zr_ref[...].astype(BF16)
            xi_b = zi_ref[...].astype(BF16)
        else:
            xr_b = (zr_ref[...] + locr_ref[a - 1].astype(F32)).astype(BF16)
            xi_b = (zi_ref[...] + loci_ref[a - 1].astype(F32)).astype(BF16)
        for b in range(SUB):
            t = a * SUB + b
            halves = []
            for h in range(2):
                cols = slice(h * HALF_STATE, (h + 1) * HALF_STATE)
                acc = jnp.dot(xr_b[:, cols], er_ref[b, h], preferred_element_type=F32)
                acc = acc + jnp.dot(xi_b[:, cols], ei_ref[b, h], preferred_element_type=F32)
                for b_in in range(b + 1):
                    acc = acc + jnp.dot(u_tile(a * SUB + b_in, h), mt_ref[b - b_in, h],
                                        preferred_element_type=F32)
                halves.append(acc)
            y = jnp.concatenate(halves, axis=1)
            z = 0.5 * y * (1.0 + jnp.tanh(math.sqrt(2.0 / math.pi) * (y + 0.044715 * (y * y * y))))
            gate = jnp.dot(z.astype(BF16), gluw_ref[...], preferred_element_type=F32) + glub_ref[...]
            o = z * (1.0 / (1.0 + jnp.exp(-gate)))
            ms = jnp.mean(o * o, axis=-1, keepdims=True)
            out_ref[:, t * width:(t + 1) * width] = (o * lax.rsqrt(ms + EPS) * gout_ref[...]).astype(BF16)


def _s5(uv, w, glu_w, glu_b, g_out):
    n = uv.shape[0]
    rows = S5_ROWS
    tile = pl.BlockSpec((rows, CHUNK * SSM_WIDTH), lambda i: (i, 0))
    return pl.pallas_call(
        _s5_kernel,
        grid=(n // rows,),
        in_specs=[tile] + [_const_spec(a.shape) for a in
                           (w["m_t"], w["w_r"], w["w_i"], w["e_r"], w["e_i"], w["a_sub"], w["levels"],
                            glu_w, glu_b, g_out)],
        out_specs=tile,
        out_shape=jax.ShapeDtypeStruct((n, CHUNK * SSM_WIDTH), BF16),
        scratch_shapes=[pltpu.VMEM((2, STATE_COLS), F32),
                        pltpu.VMEM((rows, STATE_COLS), F32),
                        pltpu.VMEM((rows, STATE_COLS), F32),
                        pltpu.VMEM((SUB - 1, rows, STATE_COLS), BF16),
                        pltpu.VMEM((SUB - 1, rows, STATE_COLS), BF16)],
        compiler_params=_params(dimension_semantics=("arbitrary",)),
        name="s5",
    )(uv, w["m_t"], w["w_r"], w["w_i"], w["e_r"], w["e_i"], w["a_sub"], w["levels"], glu_w, glu_b, g_out)


def _mlp_kernel(x_ref, o1_ref, o2_ref, o3_ref, l1_ref, l2_ref, l3_ref, ssm_ref, ga_ref, wout_ref,
                g2_ref, wup_ref, wdn_ref, out_ref, slab_ref):
    rows = x_ref.shape[0]
    n_slabs = ATTN_WIDTH // SLAB

    def natural(ref, d, buf):
        if d == 1:
            return jnp.concatenate([ref[t, 0] for t in range(N_HEAD_TILES)], axis=1).astype(F32)
        for s in range(n_slabs):
            t, lanes = _slab_home(s)
            for r in range(d):
                slab_ref[buf, s, pl.ds(r, rows // d, stride=d), :] = ref[t, r, :, lanes].astype(F32)
        return jnp.concatenate([slab_ref[buf, s] for s in range(n_slabs)], axis=1)

    o1, l1 = natural(o1_ref, 1, 0), natural(l1_ref, 1, 0)
    o2, l2 = natural(o2_ref, DILATIONS[1], 0), natural(l2_ref, DILATIONS[1], 1)
    o3, l3 = natural(o3_ref, DILATIONS[2], 2), natural(l3_ref, DILATIONS[2], 3)
    for s in range(n_slabs):
        for t in range(CHUNK):
            slab_ref[4, s, pl.ds(t, rows // CHUNK, stride=CHUNK), :] = (
                ssm_ref[:, t * SSM_WIDTH + s * SLAB:t * SSM_WIDTH + (s + 1) * SLAB].astype(F32))
    ssm = jnp.concatenate([slab_ref[4, s] for s in range(n_slabs)], axis=1).astype(BF16)

    m = jnp.maximum(jnp.maximum(l1, l2), l3)
    w1, w2, w3 = jnp.exp(l1 - m), jnp.exp(l2 - m), jnp.exp(l3 - m)
    attn = (w1 * o1 + w2 * o2 + w3 * o3) / (w1 + w2 + w3)
    ms = jnp.mean(attn * attn, axis=-1, keepdims=True)
    attn_n = (attn * lax.rsqrt(ms + EPS) * ga_ref[...]).astype(BF16)
    mix = jnp.concatenate([attn_n, ssm], axis=1)
    x1 = x_ref[...] + jnp.dot(mix, wout_ref[...], preferred_element_type=F32)
    ms2 = jnp.mean(x1 * x1, axis=-1, keepdims=True)
    xn = (x1 * lax.rsqrt(ms2 + EPS) * g2_ref[...]).astype(BF16)
    acc = x1
    for c in range(D_FF // FF_CHUNK):
        h = jnp.dot(xn, wup_ref[:, c * FF_CHUNK:(c + 1) * FF_CHUNK], preferred_element_type=F32)
        h = jnp.square(jnp.maximum(h, 0.0)).astype(BF16)
        acc = acc + jnp.dot(h, wdn_ref[c * FF_CHUNK:(c + 1) * FF_CHUNK, :], preferred_element_type=F32)
    out_ref[...] = acc


def _mlp(x, os_, ls_, ssm, ga, w_out, g2, w_up, w_dn):
    s = x.shape[0]
    rows = MLP_ROWS
    xt = pl.BlockSpec((rows, D_MODEL), lambda i: (i, 0))
    ht = [pl.BlockSpec((N_HEAD_TILES, d, rows // d, LANE_TILE), lambda i: (0, 0, i, 0)) for d in DILATIONS]
    return pl.pallas_call(
        _mlp_kernel,
        grid=(s // rows,),
        in_specs=[xt, *ht, *ht, pl.BlockSpec((rows // CHUNK, CHUNK * SSM_WIDTH), lambda i: (i, 0)),
                  _const_spec(ga.shape), _const_spec(w_out.shape), _const_spec(g2.shape),
                  _const_spec(w_up.shape), _const_spec(w_dn.shape)],
        out_specs=xt,
        out_shape=jax.ShapeDtypeStruct((s, D_MODEL), F32),
        scratch_shapes=[pltpu.VMEM((5, ATTN_WIDTH // SLAB, rows, SLAB), F32)],
        compiler_params=_params(dimension_semantics=("arbitrary",)),
        name="mlp",
    )(x, *os_, *ls_, ssm, ga, w_out, g2, w_up, w_dn)


def _layer(x, norm1_g, w_in, q_norm_g, k_norm_g, ssm_a_re, ssm_a_im, ssm_log_dt, ssm_b_re, ssm_b_im,
           ssm_c_re, ssm_c_im, ssm_d, glu_w, glu_b, attn_out_norm_g, ssm_out_norm_g, w_out, norm2_g,
           w_mlp_up, w_mlp_down):
    row = lambda g: g.astype(F32).reshape(1, -1)
    heads = ATTN_WIDTH // HEAD_DIM
    head_id = jnp.arange(ATTN_WIDTH) // HEAD_DIM
    hmat = jnp.where(head_id[:, None] == head_id[None, :], 1.0 / HEAD_DIM, 0.0).astype(BF16)
    *qkv, u = _proj(x, row(norm1_g), w_in.astype(BF16), hmat,
                    row(jnp.tile(q_norm_g, heads)), row(jnp.tile(k_norm_g, heads)))
    outs, lses = [], []
    for i in range(len(DILATIONS)):
        o, lse = _attn_pattern(*qkv[3 * i:3 * i + 3])
        outs.append(o)
        lses.append(lse)
    w = _s5_weights(ssm_a_re, ssm_a_im, ssm_log_dt, ssm_b_re, ssm_b_im, ssm_c_re, ssm_c_im, ssm_d)
    ssm = _s5(u, w, glu_w.astype(BF16), row(glu_b), row(ssm_out_norm_g))
    return _mlp(x, outs, lses, ssm, row(attn_out_norm_g), w_out.astype(BF16), row(norm2_g),
                w_mlp_up.astype(BF16), w_mlp_down.astype(BF16))


def kernel(x, norm1_g, w_in, q_norm_g, k_norm_g, ssm_a_re, ssm_a_im, ssm_log_dt, ssm_b_re, ssm_b_im,
           ssm_c_re, ssm_c_im, ssm_d, glu_w, glu_b, attn_out_norm_g, ssm_out_norm_g, w_out, norm2_g,
           w_mlp_up, w_mlp_down):
    params = (norm1_g, w_in, q_norm_g, k_norm_g, ssm_a_re, ssm_a_im, ssm_log_dt, ssm_b_re, ssm_b_im,
              ssm_c_re, ssm_c_im, ssm_d, glu_w, glu_b, attn_out_norm_g, ssm_out_norm_g, w_out, norm2_g,
              w_mlp_up, w_mlp_down)
    batch = x.shape[0]
    outs = []
    for bi in range(batch):
        h = x[bi]
        for layer in range(norm1_g.shape[0]):
            h = _layer(h, *[p[layer] for p in params])
        outs.append(h)
    return jnp.stack(outs)
```

```python
import functools
import math

import jax
import jax.numpy as jnp
from jax import lax
from jax.experimental import pallas as pl
from jax.experimental.pallas import tpu as pltpu

F32 = jnp.float32
BF16 = jnp.bfloat16

D_MODEL = 1024
ATTN_WIDTH = 512
HEAD_DIM = 64
SSM_WIDTH = 512
SSM_GROUP = 16
SSM_GROUPS = 32
SSM_STATE = 64
D_FF = 4096
EPS = 1e-6
NEG_INF = -1e30
DILATIONS = (1, 4, 16)
BLOCK = 128

SLAB = 128
LANE_TILE = 256
HEADS_PER_TILE = LANE_TILE // HEAD_DIM
N_HEAD_TILES = ATTN_WIDTH // LANE_TILE
CHUNK = 16
SUB = 4
HALF_STATE = (SSM_GROUPS // 2) * SSM_STATE
STATE_COLS = SSM_GROUPS * SSM_STATE
VMEM_LIMIT = 56 * 1024 * 1024

LOG2E = math.log2(math.e)
Q_SCALE = HEAD_DIM ** -0.5 * LOG2E

PROJ_ROWS = 512
PROJ_SUB = 256
ATTN_ROWS = 2048
S5_ROWS = 256
MLP_ROWS = 512
FF_CHUNK = 1024


def _const_spec(shape):
    nd = len(shape)
    return pl.BlockSpec(shape, lambda *_: (0,) * nd, pipeline_mode=pl.Buffered(1))


def _params(**kw):
    return pltpu.CompilerParams(vmem_limit_bytes=VMEM_LIMIT, **kw)


def _proj_kernel(x_ref, g1_ref, w_ref, hmat_ref, gq_ref, gk_ref,
                 q1_ref, k1_ref, v1_ref, q4_ref, k4_ref, v4_ref, q16_ref, k16_ref, v16_ref, u_ref,
                 slab_ref, slab4_ref):
    outs = ((q1_ref, q4_ref, q16_ref), (k1_ref, k4_ref, k16_ref), (v1_ref, v4_ref, v16_ref))
    slabs_per_tensor = ATTN_WIDTH // SLAB
    step = DILATIONS[1]
    rows = PROJ_SUB
    for sub in range(x_ref.shape[0] // rows):
        x = x_ref[sub * rows:(sub + 1) * rows, :]
        ms = jnp.mean(x * x, axis=-1, keepdims=True)
        xn = (x * lax.rsqrt(ms + EPS) * g1_ref[...]).astype(BF16)
        proj = jnp.dot(xn, w_ref[...], preferred_element_type=F32)

        def head_norm(t, g):
            sq = (t * t).astype(BF16)
            ms_h = jnp.concatenate(
                [jnp.dot(sq[:, c * LANE_TILE:(c + 1) * LANE_TILE], hmat_ref[...], preferred_element_type=F32)
                 for c in range(N_HEAD_TILES)], axis=1)
            return t * lax.rsqrt(ms_h + EPS) * g

        q = head_norm(proj[:, 0:ATTN_WIDTH], gq_ref[...]) * Q_SCALE
        k = head_norm(proj[:, ATTN_WIDTH:2 * ATTN_WIDTH], gk_ref[...])
        v = proj[:, 2 * ATTN_WIDTH:3 * ATTN_WIDTH]
        u = proj[:, 3 * ATTN_WIDTH:]
        for ti, val in enumerate((q, k, v, u)):
            for s in range(slabs_per_tensor):
                slab_ref[sub, ti * slabs_per_tensor + s] = val[:, s * SLAB:(s + 1) * SLAB]
        for ti, val in enumerate((q, k, v)):
            for t in range(N_HEAD_TILES):
                outs[ti][0][t, 0, sub * rows:(sub + 1) * rows, :] = (
                    val[:, t * LANE_TILE:(t + 1) * LANE_TILE].astype(BF16))
        r4_rows = slice(sub * (rows // step), (sub + 1) * (rows // step))
        r16_rows = slice(sub * (rows // CHUNK), (sub + 1) * (rows // CHUNK))
        for ti in range(4):
            for s in range(slabs_per_tensor):
                t, lanes = _slab_home(s)
                for r4 in range(step):
                    part = slab_ref[sub, ti * slabs_per_tensor + s, pl.ds(r4, rows // step, stride=step), :]
                    if ti < 3:
                        outs[ti][1][t, r4, r4_rows, lanes] = part.astype(BF16)
                    slab4_ref[sub, ti * slabs_per_tensor + s, r4] = part
                for r4 in range(step):
                    for j4 in range(step):
                        r16 = step * j4 + r4
                        part = slab4_ref[sub, ti * slabs_per_tensor + s, r4,
                                         pl.ds(j4, rows // CHUNK, stride=step), :].astype(BF16)
                        if ti < 3:
                            outs[ti][2][t, r16, r16_rows, lanes] = part
                        else:
                            u_ref[r16_rows, r16 * SSM_WIDTH + s * SLAB:r16 * SSM_WIDTH + (s + 1) * SLAB] = part


def _slab_home(s):
    t, half = divmod(s, LANE_TILE // SLAB)
    return t, slice(half * SLAB, (half + 1) * SLAB)


def _proj(x, g1, w_in, hmat, gq, gk):
    s = x.shape[0]
    rows = PROJ_ROWS
    out_specs, out_shape = [], []
    for d in DILATIONS:
        for _ in range(3):
            out_specs.append(pl.BlockSpec((N_HEAD_TILES, d, rows // d, LANE_TILE), lambda i: (0, 0, i, 0)))
            out_shape.append(jax.ShapeDtypeStruct((N_HEAD_TILES, d, s // d, LANE_TILE), BF16))
    out_specs.append(pl.BlockSpec((rows // CHUNK, CHUNK * SSM_WIDTH), lambda i: (i, 0)))
    out_shape.append(jax.ShapeDtypeStruct((s // CHUNK, CHUNK * SSM_WIDTH), BF16))
    return pl.pallas_call(
        _proj_kernel,
        grid=(s // rows,),
        in_specs=[
            pl.BlockSpec((rows, D_MODEL), lambda i: (i, 0)),
            _const_spec((1, D_MODEL)),
            _const_spec(w_in.shape),
            _const_spec(hmat.shape),
            _const_spec((1, ATTN_WIDTH)),
            _const_spec((1, ATTN_WIDTH)),
        ],
        out_specs=out_specs,
        out_shape=out_shape,
        scratch_shapes=[pltpu.VMEM((rows // PROJ_SUB, 4 * ATTN_WIDTH // SLAB, PROJ_SUB, SLAB), F32),
                        pltpu.VMEM((rows // PROJ_SUB, 4 * ATTN_WIDTH // SLAB, DILATIONS[1],
                                    PROJ_SUB // DILATIONS[1], SLAB), F32)],
        compiler_params=_params(dimension_semantics=("arbitrary",)),
        name="proj",
    )(x, g1, w_in, hmat, gq, gk)


def _attn_kernel(q_ref, kc_ref, kp_ref, vc_ref, vp_ref, o_ref, lse_ref, kbuf, vbuf):
    n_res, rows = q_ref.shape[0], q_ref.shape[1]
    kbuf[:, 0:BLOCK, :] = kp_ref[...]
    kbuf[:, BLOCK:, :] = kc_ref[...]
    vbuf[:, 0:BLOCK, :] = vp_ref[...]
    vbuf[:, BLOCK:, :] = vc_ref[...]

    lane = lax.broadcasted_iota(jnp.int32, (BLOCK, LANE_TILE), 1)
    head_masks = [(lane >= h * HEAD_DIM) & (lane < (h + 1) * HEAD_DIM) for h in range(HEADS_PER_TILE)]
    qi = lax.broadcasted_iota(jnp.int32, (BLOCK, 2 * BLOCK), 0)
    ki = lax.broadcasted_iota(jnp.int32, (BLOCK, 2 * BLOCK), 1)
    band = (ki >= qi) & (ki <= qi + BLOCK)
    bias_band = jnp.where(band, 0.0, NEG_INF).astype(F32)
    no_prev = pl.program_id(2) == 0
    bias_first = jnp.where(band & ((ki >= BLOCK) | jnp.logical_not(no_prev)), 0.0, NEG_INF).astype(F32)

    for r, b in [(r, b) for r in range(n_res) for b in range(rows // BLOCK)]:
        qb = q_ref[r, b * BLOCK:(b + 1) * BLOCK, :]
        zero = jnp.zeros_like(qb)
        q_stack = jnp.concatenate([jnp.where(mk, qb, zero) for mk in head_masks], axis=0)
        kw = kbuf[r, b * BLOCK:(b + 2) * BLOCK, :]
        vw = vbuf[r, b * BLOCK:(b + 2) * BLOCK, :]
        s = lax.dot_general(q_stack, kw, (((1,), (1,)), ((), ())), preferred_element_type=F32)
        bias = bias_first if b == 0 else bias_band
        ps, ms, ls = [], [], []
        for h in range(HEADS_PER_TILE):
            sh = s[h * BLOCK:(h + 1) * BLOCK, :] + bias
            m = jnp.max(sh, axis=-1, keepdims=True)
            p = jnp.exp2(sh - m)
            ls.append(jnp.sum(p, axis=-1, keepdims=True))
            ms.append(m)
            ps.append(p.astype(BF16))
        pv = jnp.dot(jnp.concatenate(ps, axis=0), vw, preferred_element_type=F32)
        o = jnp.zeros((BLOCK, LANE_TILE), F32)
        lse = jnp.zeros((BLOCK, LANE_TILE), F32)
        for h in range(HEADS_PER_TILE):
            o = jnp.where(head_masks[h], pv[h * BLOCK:(h + 1) * BLOCK, :] / ls[h], o)
            lse = jnp.where(head_masks[h], (ms[h] + jnp.log2(ls[h])) * (1.0 / LOG2E), lse)
        o_ref[r, b * BLOCK:(b + 1) * BLOCK, :] = o.astype(BF16)
        lse_ref[r, b * BLOCK:(b + 1) * BLOCK, :] = lse


def _attn_pattern(q, k, v):
    nt, dilation, n, _ = q.shape
    rows = min(ATTN_ROWS, n)
    n_res = min(ATTN_ROWS // rows, dilation)
    blocks_per_tile = rows // BLOCK
    cur = pl.BlockSpec((None, n_res, rows, LANE_TILE), lambda t, r, j: (t, r, j, 0))
    prev = pl.BlockSpec((None, n_res, BLOCK, LANE_TILE),
                        lambda t, r, j: (t, r, jnp.maximum(j * blocks_per_tile - 1, 0), 0))
    return pl.pallas_call(
        _attn_kernel,
        grid=(nt, dilation // n_res, n // rows),
        in_specs=[cur, cur, prev, cur, prev],
        out_specs=[cur, cur],
        out_shape=[jax.ShapeDtypeStruct(q.shape, BF16), jax.ShapeDtypeStruct(q.shape, F32)],
        scratch_shapes=[pltpu.VMEM((n_res, rows + BLOCK, LANE_TILE), BF16),
                        pltpu.VMEM((n_res, rows + BLOCK, LANE_TILE), BF16)],
        compiler_params=_params(dimension_semantics=("arbitrary", "arbitrary", "arbitrary")),
        name=f"attn_d{dilation}",
    )(q, k, k, v, v)


def _s5_weights(a_re, a_im, log_dt, b_re, b_im, c_re, c_im, d_skip):
    g, p, c = SSM_GROUPS, SSM_STATE, SSM_GROUP
    gl = g // 2
    lr, li = a_re.astype(F32), a_im.astype(F32)
    dt = jnp.exp(log_dt.astype(F32))[:, None]

    def apow(j):
        mag = jnp.exp(lr * dt * j)
        return mag * jnp.cos(li * dt * j), mag * jnp.sin(li * dt * j)

    ab_r, ab_i = apow(1.0)
    den = lr * lr + li * li
    nr, ni = ab_r - 1.0, ab_i
    cr = (nr * lr + ni * li) / den
    ci = (ni * lr - nr * li) / den
    br, bi = b_re.astype(F32), b_im.astype(F32)
    bb_r = cr[..., None] * br - ci[..., None] * bi
    bb_i = cr[..., None] * bi + ci[..., None] * br
    cre, cim = c_re.astype(F32), c_im.astype(F32)

    def apow_many(js):
        j = jnp.asarray(js, F32)[:, None, None]
        mag = jnp.exp(lr * dt * j)
        return mag * jnp.cos(li * dt * j), mag * jnp.sin(li * dt * j)

    pr, pi = apow_many(range(SUB + 1))
    wp_r, wp_i = pr[SUB - 1::-1], pi[SUB - 1::-1]
    win_r = wp_r[..., None] * bb_r[None] - wp_i[..., None] * bb_i[None]
    win_i = wp_r[..., None] * bb_i[None] + wp_i[..., None] * bb_r[None]
    pad_lanes = lambda w: jnp.pad(w, ((0, 0), (0, 0), (0, 0), (0, SLAB - w.shape[-1])))
    to_in = lambda w: pad_lanes(w.transpose(0, 1, 3, 2).reshape(SUB, 2, gl * c, p))
    out_r = cre[None] * pr[:, :, None, :] - cim[None] * pi[:, :, None, :]
    out_i = -(cre[None] * pi[:, :, None, :] + cim[None] * pr[:, :, None, :])
    to_out = lambda w: pad_lanes(w.transpose(0, 1, 3, 2).reshape(SUB + 1, 2, gl * p, c))

    flat = lambda re, im: jnp.stack([re.reshape(-1), im.reshape(-1)])
    top_r, top_i = apow_many([CHUNK])
    levels = [flat(top_r[0], top_i[0])]
    n_levels = int(math.log2(S5_ROWS))
    for _ in range(n_levels - 1):
        lr_, li_ = levels[-1][0], levels[-1][1]
        levels.append(jnp.stack([lr_ * lr_ - li_ * li_, 2.0 * lr_ * li_]))
    m_t, w_r, w_i, e_r, e_i = _s5_expand(to_in(win_r), to_in(win_i), to_out(out_r), to_out(out_i),
                                         d_skip.astype(F32).reshape(2, 1, gl * c))
    return dict(m_t=m_t, w_r=w_r, w_i=w_i, e_r=e_r, e_i=e_i,
                a_sub=flat(pr[SUB], pi[SUB]), levels=jnp.stack(levels))


def _s5_expand_kernel(wr_ref, wi_ref, or_ref, oi_ref, d_ref, mt_ref, wfr_ref, wfi_ref, efr_ref, efi_ref):
    n_in, n_out = wr_ref.shape[2], or_ref.shape[2]
    p, c = SSM_STATE, SSM_GROUP

    def rep(inner, total):
        r = lax.broadcasted_iota(jnp.int32, (SLAB, total), 0)
        col = lax.broadcasted_iota(jnp.int32, (SLAB, total), 1)
        return jnp.where((col & (inner - 1)) == r, 1.0, 0.0).astype(BF16)

    def group_mask(n_rows, rows_per_group, n_cols, cols_per_group):
        r = lax.broadcasted_iota(jnp.int32, (n_rows, n_cols), 0) >> int(math.log2(rows_per_group))
        col = lax.broadcasted_iota(jnp.int32, (n_rows, n_cols), 1) >> int(math.log2(cols_per_group))
        return r == col

    rep_in, rep_out = rep(p, n_out), rep(c, n_in)
    mask_in = group_mask(n_in, c, n_out, p)
    mask_out = group_mask(n_out, p, n_in, c)

    def split(x):
        hi = x.astype(BF16)
        return hi, (x - hi.astype(F32)).astype(BF16)

    def expand(x_b, rep_m, mask):
        full = jnp.dot(x_b, rep_m, preferred_element_type=F32)
        return jnp.where(mask, full, 0.0).astype(BF16)

    diag = (lax.broadcasted_iota(jnp.int32, (n_in, n_in), 0) == lax.broadcasted_iota(jnp.int32, (n_in, n_in), 1))
    for h in range(2):
        in0 = []
        for b in range(SUB):
            for src, dst in ((wr_ref, wfr_ref), (wi_ref, wfi_ref)):
                hi, lo = split(src[b, h])
                full = expand(hi, rep_in, mask_in)
                dst[b, h] = full
                if b == SUB - 1:
                    in0.append((full, expand(lo, rep_in, mask_in)))
        for j in range(SUB + 1):
            lag = None
            for (src, dst), (in_hi, in_lo) in zip(((or_ref, efr_ref), (oi_ref, efi_ref)), in0):
                hi, lo = split(src[j, h])
                out_hi = expand(hi, rep_out, mask_out)
                if j > 0:
                    dst[j - 1, h] = out_hi
                if j < SUB:
                    out_lo = expand(lo, rep_out, mask_out)
                    part = (jnp.dot(in_hi, out_hi, preferred_element_type=F32)
                            + jnp.dot(in_hi, out_lo, preferred_element_type=F32)
                            + jnp.dot(in_lo, out_hi, preferred_element_type=F32))
                    lag = part if lag is None else lag + part
            if j == 0:
                lag = lag + jnp.where(diag, d_ref[h], 0.0)
            if j < SUB:
                mt_ref[j, h] = lag.astype(BF16)


def _s5_expand(wc_r, wc_i, oc_r, oc_i, d_row):
    n_in, n_out = wc_r.shape[2], oc_r.shape[2]
    shapes = [jax.ShapeDtypeStruct((SUB, 2, n_in, n_in), BF16),
              jax.ShapeDtypeStruct((SUB, 2, n_in, n_out), BF16), jax.ShapeDtypeStruct((SUB, 2, n_in, n_out), BF16),
              jax.ShapeDtypeStruct((SUB, 2, n_out, n_in), BF16), jax.ShapeDtypeStruct((SUB, 2, n_out, n_in), BF16)]
    return pl.pallas_call(
        _s5_expand_kernel,
        out_shape=shapes,
        compiler_params=_params(),
        name="s5_expand",
    )(wc_r, wc_i, oc_r, oc_i, d_row)


def _cmul(ar, ai, br, bi):
    return ar * br - ai * bi, ar * bi + ai * br


def _s5_kernel(u_ref, mt_ref, wr_ref, wi_ref, er_ref, ei_ref, asub_ref, lvl_ref,
               gluw_ref, glub_ref, gout_ref, out_ref, carry_ref, zr_ref, zi_ref, locr_ref, loci_ref):
    rows = u_ref.shape[0]
    width = SSM_WIDTH

    @pl.when(pl.program_id(0) == 0)
    def _():
        carry_ref[...] = jnp.zeros_like(carry_ref)

    def u_tile(t, h):
        return u_ref[:, t * width + h * LANE_TILE:t * width + (h + 1) * LANE_TILE]

    a_r, a_i = asub_ref[0:1, :], asub_ref[1:2, :]

    for a in range(SUB):
        for h in range(2):
            cols = slice(h * HALF_STATE, (h + 1) * HALF_STATE)
            pr = pi = None
            for b in range(SUB):
                ut = u_tile(a * SUB + b, h)
                dr = jnp.dot(ut, wr_ref[b, h], preferred_element_type=F32)
                di = jnp.dot(ut, wi_ref[b, h], preferred_element_type=F32)
                pr = dr if pr is None else pr + dr
                pi = di if pi is None else pi + di
            if a == 0:
                zr_ref[:, cols] = pr
                zi_ref[:, cols] = pi
            else:
                hr, hi = _cmul(zr_ref[:, cols], zi_ref[:, cols], a_r[:, cols], a_i[:, cols])
                pr, pi = hr + pr, hi + pi
                zr_ref[:, cols] = pr
                zi_ref[:, cols] = pi
            if a < SUB - 1:
                locr_ref[a, :, cols] = pr.astype(BF16)
                loci_ref[a, :, cols] = pi.astype(BF16)

    row = lax.broadcasted_iota(jnp.int32, (rows, 1), 0)
    c_r, c_i = carry_ref[0:1, :], carry_ref[1:2, :]
    lam_r, lam_i = lvl_ref[0, 0:1, :], lvl_ref[0, 1:2, :]
    in_r, in_i = _cmul(c_r, c_i, lam_r, lam_i)
    sr = zr_ref[...] + jnp.where(row == 0, in_r, 0.0)
    si = zi_ref[...] + jnp.where(row == 0, in_i, 0.0)
    for lev in range(lvl_ref.shape[0]):
        sh = 1 << lev
        keep = row >= sh
        pr_, pi_ = lvl_ref[lev, 0:1, :], lvl_ref[lev, 1:2, :]
        tr = jnp.where(keep, pltpu.roll(sr, sh, axis=0), 0.0)
        ti = jnp.where(keep, pltpu.roll(si, sh, axis=0), 0.0)
        mr, mi = _cmul(tr, ti, pr_, pi_)
        sr, si = sr + mr, si + mi
    xr = jnp.where(row == 0, c_r, pltpu.roll(sr, 1, axis=0))
    xi = jnp.where(row == 0, c_i, pltpu.roll(si, 1, axis=0))
    carry_ref[0:1, :] = sr[rows - 1:rows, :]
    carry_ref[1:2, :] = si[rows - 1:rows, :]
    zr_ref[...] = xr
    zi_ref[...] = xi

    for a in range(SUB):
        if a > 0:
            nr, ni = _cmul(zr_ref[...], zi_ref[...], a_r, a_i)
            zr_ref[...] = nr
            zi_ref[...] = ni
        if a == 0:
            xr_b = zr_ref[...].astype(BF16)
            xi_b = zi_ref[...].astype(BF16)
        else:
            xr_b = (zr_ref[...] + locr_ref[a - 1].astype(F32)).astype(BF16)
            xi_b = (zi_ref[...] + loci_ref[a - 1].astype(F32)).astype(BF16)
        for b in range(SUB):
            t = a * SUB + b
            halves = []
            for h in range(2):
                cols = slice(h * HALF_STATE, (h + 1) * HALF_STATE)
                acc = jnp.dot(xr_b[:, cols], er_ref[b, h], preferred_element_type=F32)
                acc = acc + jnp.dot(xi_b[:, cols], ei_ref[b, h], preferred_element_type=F32)
                for b_in in range(b + 1):
                    acc = acc + jnp.dot(u_tile(a * SUB + b_in, h), mt_ref[b - b_in, h],
                                        preferred_element_type=F32)
                halves.append(acc)
            y = jnp.concatenate(halves, axis=1)
            z = 0.5 * y * (1.0 + jnp.tanh(math.sqrt(2.0 / math.pi) * (y + 0.044715 * (y * y * y))))
            gate = jnp.dot(z.astype(BF16), gluw_ref[...], preferred_element_type=F32) + glub_ref[...]
            o = z * (1.0 / (1.0 + jnp.exp(-gate)))
            ms = jnp.mean(o * o, axis=-1, keepdims=True)
            out_ref[:, t * width:(t + 1) * width] = (o * lax.rsqrt(ms + EPS) * gout_ref[...]).astype(BF16)


def _s5(uv, w, glu_w, glu_b, g_out):
    n = uv.shape[0]
    rows = S5_ROWS
    tile = pl.BlockSpec((rows, CHUNK * SSM_WIDTH), lambda i: (i, 0))
    return pl.pallas_call(
        _s5_kernel,
        grid=(n // rows,),
        in_specs=[tile] + [_const_spec(a.shape) for a in
                           (w["m_t"], w["w_r"], w["w_i"], w["e_r"], w["e_i"], w["a_sub"], w["levels"],
                            glu_w, glu_b, g_out)],
        out_specs=tile,
        out_shape=jax.ShapeDtypeStruct((n, CHUNK * SSM_WIDTH), BF16),
        scratch_shapes=[pltpu.VMEM((2, STATE_COLS), F32),
                        pltpu.VMEM((rows, STATE_COLS), F32),
                        pltpu.VMEM((rows, STATE_COLS), F32),
                        pltpu.VMEM((SUB - 1, rows, STATE_COLS), BF16),
                        pltpu.VMEM((SUB - 1, rows, STATE_COLS), BF16)],
        compiler_params=_params(dimension_semantics=("arbitrary",)),
        name="s5",
    )(uv, w["m_t"], w["w_r"], w["w_i"], w["e_r"], w["e_i"], w["a_sub"], w["levels"], glu_w, glu_b, g_out)


def _mlp_kernel(x_ref, o1_ref, o2_ref, o3_ref, l1_ref, l2_ref, l3_ref, ssm_ref, ga_ref, wout_ref,
                g2_ref, wup_ref, wdn_ref, out_ref, slab_ref):
    rows = PROJ_SUB
    n_slabs = ATTN_WIDTH // SLAB

    for sub in range(x_ref.shape[0] // rows):
        def natural(ref, d, buf):
            part = slice(sub * (rows // d), (sub + 1) * (rows // d))
            if d == 1:
                return jnp.concatenate([ref[t, 0, part, :] for t in range(N_HEAD_TILES)], axis=1).astype(F32)
            for s in range(n_slabs):
                t, lanes = _slab_home(s)
                for r in range(d):
                    slab_ref[sub, buf, s, pl.ds(r, rows // d, stride=d), :] = ref[t, r, part, lanes].astype(F32)
            return jnp.concatenate([slab_ref[sub, buf, s] for s in range(n_slabs)], axis=1)

        o1, l1 = natural(o1_ref, 1, 0), natural(l1_ref, 1, 0)
        o2, l2 = natural(o2_ref, DILATIONS[1], 0), natural(l2_ref, DILATIONS[1], 1)
        o3, l3 = natural(o3_ref, DILATIONS[2], 2), natural(l3_ref, DILATIONS[2], 3)
        part = slice(sub * (rows // CHUNK), (sub + 1) * (rows // CHUNK))
        for s in range(n_slabs):
            for t in range(CHUNK):
                slab_ref[sub, 4, s, pl.ds(t, rows // CHUNK, stride=CHUNK), :] = (
                    ssm_ref[part, t * SSM_WIDTH + s * SLAB:t * SSM_WIDTH + (s + 1) * SLAB].astype(F32))
        ssm = jnp.concatenate([slab_ref[sub, 4, s] for s in range(n_slabs)], axis=1).astype(BF16)

        m = jnp.maximum(jnp.maximum(l1, l2), l3)
        w1, w2, w3 = jnp.exp(l1 - m), jnp.exp(l2 - m), jnp.exp(l3 - m)
        attn = (w1 * o1 + w2 * o2 + w3 * o3) / (w1 + w2 + w3)
        ms = jnp.mean(attn * attn, axis=-1, keepdims=True)
        attn_n = (attn * lax.rsqrt(ms + EPS) * ga_ref[...]).astype(BF16)
        mix = jnp.concatenate([attn_n, ssm], axis=1)
        x1 = x_ref[sub * rows:(sub + 1) * rows, :] + jnp.dot(mix, wout_ref[...], preferred_element_type=F32)
        ms2 = jnp.mean(x1 * x1, axis=-1, keepdims=True)
        xn = (x1 * lax.rsqrt(ms2 + EPS) * g2_ref[...]).astype(BF16)
        acc = x1
        for c in range(D_FF // FF_CHUNK):
            h = jnp.dot(xn, wup_ref[:, c * FF_CHUNK:(c + 1) * FF_CHUNK], preferred_element_type=F32)
            h = jnp.square(jnp.maximum(h, 0.0)).astype(BF16)
            acc = acc + jnp.dot(h, wdn_ref[c * FF_CHUNK:(c + 1) * FF_CHUNK, :], preferred_element_type=F32)
        out_ref[sub * rows:(sub + 1) * rows, :] = acc


def _mlp(x, os_, ls_, ssm, ga, w_out, g2, w_up, w_dn):
    s = x.shape[0]
    rows = MLP_ROWS
    xt = pl.BlockSpec((rows, D_MODEL), lambda i: (i, 0))
    ht = [pl.BlockSpec((N_HEAD_TILES, d, rows // d, LANE_TILE), lambda i: (0, 0, i, 0)) for d in DILATIONS]
    return pl.pallas_call(
        _mlp_kernel,
        grid=(s // rows,),
        in_specs=[xt, *ht, *ht, pl.BlockSpec((rows // CHUNK, CHUNK * SSM_WIDTH), lambda i: (i, 0)),
                  _const_spec(ga.shape), _const_spec(w_out.shape), _const_spec(g2.shape),
                  _const_spec(w_up.shape), _const_spec(w_dn.shape)],
        out_specs=xt,
        out_shape=jax.ShapeDtypeStruct((s, D_MODEL), F32),
        scratch_shapes=[pltpu.VMEM((rows // PROJ_SUB, 5, ATTN_WIDTH // SLAB, PROJ_SUB, SLAB), F32)],
        compiler_params=_params(dimension_semantics=("arbitrary",)),
        name="mlp",
    )(x, *os_, *ls_, ssm, ga, w_out, g2, w_up, w_dn)


def _layer(x, norm1_g, w_in, q_norm_g, k_norm_g, ssm_a_re, ssm_a_im, ssm_log_dt, ssm_b_re, ssm_b_im,
           ssm_c_re, ssm_c_im, ssm_d, glu_w, glu_b, attn_out_norm_g, ssm_out_norm_g, w_out, norm2_g,
           w_mlp_up, w_mlp_down):
    row = lambda g: g.astype(F32).reshape(1, -1)
    heads = ATTN_WIDTH // HEAD_DIM
    head_id = jnp.arange(LANE_TILE) // HEAD_DIM
    hmat = jnp.where(head_id[:, None] == head_id[None, :], 1.0 / HEAD_DIM, 0.0).astype(BF16)
    *qkv, u = _proj(x, row(norm1_g), w_in.astype(BF16), hmat,
                    row(jnp.tile(q_norm_g, heads)), row(jnp.tile(k_norm_g, heads)))
    outs, lses = [], []
    for i in range(len(DILATIONS)):
        o, lse = _attn_pattern(*qkv[3 * i:3 * i + 3])
        outs.append(o)
        lses.append(lse)
    w = _s5_weights(ssm_a_re, ssm_a_im, ssm_log_dt, ssm_b_re, ssm_b_im, ssm_c_re, ssm_c_im, ssm_d)
    ssm = _s5(u, w, glu_w.astype(BF16), row(glu_b), row(ssm_out_norm_g))
    return _mlp(x, outs, lses, ssm, row(attn_out_norm_g), w_out.astype(BF16), row(norm2_g),
                w_mlp_up.astype(BF16), w_mlp_down.astype(BF16))


def kernel(x, norm1_g, w_in, q_norm_g, k_norm_g, ssm_a_re, ssm_a_im, ssm_log_dt, ssm_b_re, ssm_b_im,
           ssm_c_re, ssm_c_im, ssm_d, glu_w, glu_b, attn_out_norm_g, ssm_out_norm_g, w_out, norm2_g,
           w_mlp_up, w_mlp_down):
    params = (norm1_g, w_in, q_norm_g, k_norm_g, ssm_a_re, ssm_a_im, ssm_log_dt, ssm_b_re, ssm_b_im,
              ssm_c_re, ssm_c_im, ssm_d, glu_w, glu_b, attn_out_norm_g, ssm_out_norm_g, w_out, norm2_g,
              w_mlp_up, w_mlp_down)
    batch = x.shape[0]
    outs = []
    for bi in range(batch):
        h = x[bi]
        for layer in range(norm1_g.shape[0]):
            h = _layer(h, *[p[layer] for p in params])
        outs.append(h)
    return jnp.stack(outs)
```

```python
import functools
import math

import jax
import jax.numpy as jnp
from jax import lax
from jax.experimental import pallas as pl
from jax.experimental.pallas import tpu as pltpu

F32 = jnp.float32
BF16 = jnp.bfloat16

D_MODEL = 1024
ATTN_WIDTH = 512
HEAD_DIM = 64
SSM_WIDTH = 512
SSM_GROUP = 16
SSM_GROUPS = 32
SSM_STATE = 64
D_FF = 4096
EPS = 1e-6
NEG_INF = -1e30
DILATIONS = (1, 4, 16)
BLOCK = 128

SLAB = 128
LANE_TILE = 256
HEADS_PER_TILE = LANE_TILE // HEAD_DIM
N_HEAD_TILES = ATTN_WIDTH // LANE_TILE
CHUNK = 16
SUB = 4
HALF_STATE = (SSM_GROUPS // 2) * SSM_STATE
STATE_COLS = SSM_GROUPS * SSM_STATE
VMEM_LIMIT = 56 * 1024 * 1024

LOG2E = math.log2(math.e)
Q_SCALE = HEAD_DIM ** -0.5 * LOG2E

PROJ_ROWS = 512
PROJ_SUB = 256
ATTN_ROWS = 2048
S5_ROWS = 256
MLP_ROWS = 512
FF_CHUNK = 512


def _const_spec(shape):
    nd = len(shape)
    return pl.BlockSpec(shape, lambda *_: (0,) * nd, pipeline_mode=pl.Buffered(1))


def _params(**kw):
    return pltpu.CompilerParams(vmem_limit_bytes=VMEM_LIMIT, **kw)


def _proj_kernel(x_ref, g1_ref, w_ref, hmat_ref, gq_ref, gk_ref,
                 q1_ref, k1_ref, v1_ref, q4_ref, k4_ref, v4_ref, q16_ref, k16_ref, v16_ref, u_ref,
                 slab_ref, slab4_ref):
    outs = ((q1_ref, q4_ref, q16_ref), (k1_ref, k4_ref, k16_ref), (v1_ref, v4_ref, v16_ref))
    slabs_per_tensor = ATTN_WIDTH // SLAB
    step = DILATIONS[1]
    rows = PROJ_SUB
    n_sub = x_ref.shape[0] // rows
    n_chunks = w_ref.shape[1] // LANE_TILE
    chunks_per_tensor = ATTN_WIDTH // LANE_TILE
    gains = (gq_ref, gk_ref)
    xn = [None] * n_sub

    def prologue(sub):
        x = x_ref[sub * rows:(sub + 1) * rows, :]
        ms = jnp.mean(x * x, axis=-1, keepdims=True)
        xn[sub] = (x * lax.rsqrt(ms + EPS) * g1_ref[...]).astype(BF16)

    def matmul(sub, c):
        return jnp.dot(xn[sub], w_ref[:, c * LANE_TILE:(c + 1) * LANE_TILE], preferred_element_type=F32)

    def epilogue(sub, c, val):
        ti, t = divmod(c, chunks_per_tensor)
        lane_tile = slice(t * LANE_TILE, (t + 1) * LANE_TILE)
        if ti < 2:
            ms_h = jnp.dot((val * val).astype(BF16), hmat_ref[...], preferred_element_type=F32)
            val = val * lax.rsqrt(ms_h + EPS) * gains[ti][:, lane_tile]
            if ti == 0:
                val = val * Q_SCALE
        if ti < 3:
            outs[ti][0][t, 0, sub * rows:(sub + 1) * rows, :] = val.astype(BF16)
        r4_rows = slice(sub * (rows // step), (sub + 1) * (rows // step))
        r16_rows = slice(sub * (rows // CHUNK), (sub + 1) * (rows // CHUNK))
        for half in range(LANE_TILE // SLAB):
            s = t * (LANE_TILE // SLAB) + half
            lanes = slice(half * SLAB, (half + 1) * SLAB)
            slab = ti * slabs_per_tensor + s
            slab_ref[sub, slab] = val[:, lanes]
            for r4 in range(step):
                part = slab_ref[sub, slab, pl.ds(r4, rows // step, stride=step), :]
                if ti < 3:
                    outs[ti][1][t, r4, r4_rows, lanes] = part.astype(BF16)
                slab4_ref[sub, slab, r4] = part
            for r4 in range(step):
                for j4 in range(step):
                    r16 = step * j4 + r4
                    part = slab4_ref[sub, slab, r4, pl.ds(j4, rows // CHUNK, stride=step), :].astype(BF16)
                    if ti < 3:
                        outs[ti][2][t, r16, r16_rows, lanes] = part
                    else:
                        u_ref[r16_rows, r16 * SSM_WIDTH + s * SLAB:r16 * SSM_WIDTH + (s + 1) * SLAB] = part

    work = [(sub, c) for sub in range(n_sub) for c in range(n_chunks)]
    prologue(0)
    pending = None
    for sub, c in work:
        val = matmul(sub, c)
        if pending is not None:
            epilogue(*pending)
        pending = (sub, c, val)
        if c == n_chunks // 2 and sub + 1 < n_sub:
            prologue(sub + 1)
    epilogue(*pending)


def _slab_home(s):
    t, half = divmod(s, LANE_TILE // SLAB)
    return t, slice(half * SLAB, (half + 1) * SLAB)


def _proj(x, g1, w_in, hmat, gq, gk):
    s = x.shape[0]
    rows = PROJ_ROWS
    out_specs, out_shape = [], []
    for d in DILATIONS:
        for _ in range(3):
            out_specs.append(pl.BlockSpec((N_HEAD_TILES, d, rows // d, LANE_TILE), lambda i: (0, 0, i, 0)))
            out_shape.append(jax.ShapeDtypeStruct((N_HEAD_TILES, d, s // d, LANE_TILE), BF16))
    out_specs.append(pl.BlockSpec((rows // CHUNK, CHUNK * SSM_WIDTH), lambda i: (i, 0)))
    out_shape.append(jax.ShapeDtypeStruct((s // CHUNK, CHUNK * SSM_WIDTH), BF16))
    return pl.pallas_call(
        _proj_kernel,
        grid=(s // rows,),
        in_specs=[
            pl.BlockSpec((rows, D_MODEL), lambda i: (i, 0)),
            _const_spec((1, D_MODEL)),
            _const_spec(w_in.shape),
            _const_spec(hmat.shape),
            _const_spec((1, ATTN_WIDTH)),
            _const_spec((1, ATTN_WIDTH)),
        ],
        out_specs=out_specs,
        out_shape=out_shape,
        scratch_shapes=[pltpu.VMEM((rows // PROJ_SUB, 4 * ATTN_WIDTH // SLAB, PROJ_SUB, SLAB), F32),
                        pltpu.VMEM((rows // PROJ_SUB, 4 * ATTN_WIDTH // SLAB, DILATIONS[1],
                                    PROJ_SUB // DILATIONS[1], SLAB), F32)],
        compiler_params=_params(dimension_semantics=("arbitrary",)),
        name="proj",
    )(x, g1, w_in, hmat, gq, gk)


def _attn_kernel(q_ref, kc_ref, kp_ref, vc_ref, vp_ref, o_ref, lse_ref, kbuf, vbuf):
    n_res, rows = q_ref.shape[0], q_ref.shape[1]
    kbuf[:, 0:BLOCK, :] = kp_ref[...]
    kbuf[:, BLOCK:, :] = kc_ref[...]
    vbuf[:, 0:BLOCK, :] = vp_ref[...]
    vbuf[:, BLOCK:, :] = vc_ref[...]

    lane = lax.broadcasted_iota(jnp.int32, (BLOCK, LANE_TILE), 1)
    head_masks = [(lane >= h * HEAD_DIM) & (lane < (h + 1) * HEAD_DIM) for h in range(HEADS_PER_TILE)]
    qi = lax.broadcasted_iota(jnp.int32, (BLOCK, 2 * BLOCK), 0)
    ki = lax.broadcasted_iota(jnp.int32, (BLOCK, 2 * BLOCK), 1)
    band = (ki >= qi) & (ki <= qi + BLOCK)
    bias_band = jnp.where(band, 0.0, NEG_INF).astype(F32)
    no_prev = pl.program_id(2) == 0
    bias_first = jnp.where(band & ((ki >= BLOCK) | jnp.logical_not(no_prev)), 0.0, NEG_INF).astype(F32)

    for r, b in [(r, b) for r in range(n_res) for b in range(rows // BLOCK)]:
        qb = q_ref[r, b * BLOCK:(b + 1) * BLOCK, :]
        zero = jnp.zeros_like(qb)
        q_stack = jnp.concatenate([jnp.where(mk, qb, zero) for mk in head_masks], axis=0)
        kw = kbuf[r, b * BLOCK:(b + 2) * BLOCK, :]
        vw = vbuf[r, b * BLOCK:(b + 2) * BLOCK, :]
        s = lax.dot_general(q_stack, kw, (((1,), (1,)), ((), ())), preferred_element_type=F32)
        bias = bias_first if b == 0 else bias_band
        ps, ms, ls = [], [], []
        for h in range(HEADS_PER_TILE):
            sh = s[h * BLOCK:(h + 1) * BLOCK, :] + bias
            m = jnp.max(sh, axis=-1, keepdims=True)
            p = jnp.exp2(sh - m)
            ls.append(jnp.sum(p, axis=-1, keepdims=True))
            ms.append(m)
            ps.append(p.astype(BF16))
        pv = jnp.dot(jnp.concatenate(ps, axis=0), vw, preferred_element_type=F32)
        o = jnp.zeros((BLOCK, LANE_TILE), F32)
        lse = jnp.zeros((BLOCK, LANE_TILE), F32)
        for h in range(HEADS_PER_TILE):
            o = jnp.where(head_masks[h], pv[h * BLOCK:(h + 1) * BLOCK, :] / ls[h], o)
            lse = jnp.where(head_masks[h], (ms[h] + jnp.log2(ls[h])) * (1.0 / LOG2E), lse)
        o_ref[r, b * BLOCK:(b + 1) * BLOCK, :] = o.astype(BF16)
        lse_ref[r, b * BLOCK:(b + 1) * BLOCK, :] = lse


def _attn_pattern(q, k, v):
    nt, dilation, n, _ = q.shape
    rows = min(ATTN_ROWS, n)
    n_res = min(ATTN_ROWS // rows, dilation)
    blocks_per_tile = rows // BLOCK
    cur = pl.BlockSpec((None, n_res, rows, LANE_TILE), lambda t, r, j: (t, r, j, 0))
    prev = pl.BlockSpec((None, n_res, BLOCK, LANE_TILE),
                        lambda t, r, j: (t, r, jnp.maximum(j * blocks_per_tile - 1, 0), 0))
    return pl.pallas_call(
        _attn_kernel,
        grid=(nt, dilation // n_res, n // rows),
        in_specs=[cur, cur, prev, cur, prev],
        out_specs=[cur, cur],
        out_shape=[jax.ShapeDtypeStruct(q.shape, BF16), jax.ShapeDtypeStruct(q.shape, F32)],
        scratch_shapes=[pltpu.VMEM((n_res, rows + BLOCK, LANE_TILE), BF16),
                        pltpu.VMEM((n_res, rows + BLOCK, LANE_TILE), BF16)],
        compiler_params=_params(dimension_semantics=("arbitrary", "arbitrary", "arbitrary")),
        name=f"attn_d{dilation}",
    )(q, k, k, v, v)


def _s5_weights(a_re, a_im, log_dt, b_re, b_im, c_re, c_im, d_skip):
    g, p, c = SSM_GROUPS, SSM_STATE, SSM_GROUP
    gl = g // 2
    lr, li = a_re.astype(F32), a_im.astype(F32)
    dt = jnp.exp(log_dt.astype(F32))[:, None]

    def apow(j):
        mag = jnp.exp(lr * dt * j)
        return mag * jnp.cos(li * dt * j), mag * jnp.sin(li * dt * j)

    ab_r, ab_i = apow(1.0)
    den = lr * lr + li * li
    nr, ni = ab_r - 1.0, ab_i
    cr = (nr * lr + ni * li) / den
    ci = (ni * lr - nr * li) / den
    br, bi = b_re.astype(F32), b_im.astype(F32)
    bb_r = cr[..., None] * br - ci[..., None] * bi
    bb_i = cr[..., None] * bi + ci[..., None] * br
    cre, cim = c_re.astype(F32), c_im.astype(F32)

    def apow_many(js):
        j = jnp.asarray(js, F32)[:, None, None]
        mag = jnp.exp(lr * dt * j)
        return mag * jnp.cos(li * dt * j), mag * jnp.sin(li * dt * j)

    pr, pi = apow_many(range(SUB + 1))
    wp_r, wp_i = pr[SUB - 1::-1], pi[SUB - 1::-1]
    win_r = wp_r[..., None] * bb_r[None] - wp_i[..., None] * bb_i[None]
    win_i = wp_r[..., None] * bb_i[None] + wp_i[..., None] * bb_r[None]
    pad_lanes = lambda w: jnp.pad(w, ((0, 0), (0, 0), (0, 0), (0, SLAB - w.shape[-1])))
    to_in = lambda w: pad_lanes(w.transpose(0, 1, 3, 2).reshape(SUB, 2, gl * c, p))
    out_r = cre[None] * pr[:, :, None, :] - cim[None] * pi[:, :, None, :]
    out_i = -(cre[None] * pi[:, :, None, :] + cim[None] * pr[:, :, None, :])
    to_out = lambda w: pad_lanes(w.transpose(0, 1, 3, 2).reshape(SUB + 1, 2, gl * p, c))

    flat = lambda re, im: jnp.stack([re.reshape(-1), im.reshape(-1)])
    top_r, top_i = apow_many([CHUNK])
    levels = [flat(top_r[0], top_i[0])]
    n_levels = int(math.log2(S5_ROWS))
    for _ in range(n_levels - 1):
        lr_, li_ = levels[-1][0], levels[-1][1]
        levels.append(jnp.stack([lr_ * lr_ - li_ * li_, 2.0 * lr_ * li_]))
    m_t, w_r, w_i, e_r, e_i = _s5_expand(to_in(win_r), to_in(win_i), to_out(out_r), to_out(out_i),
                                         d_skip.astype(F32).reshape(2, 1, gl * c))
    return dict(m_t=m_t, w_r=w_r, w_i=w_i, e_r=e_r, e_i=e_i,
                a_sub=flat(pr[SUB], pi[SUB]), levels=jnp.stack(levels))


def _s5_expand_kernel(wr_ref, wi_ref, or_ref, oi_ref, d_ref, mt_ref, wfr_ref, wfi_ref, efr_ref, efi_ref):
    n_in, n_out = wr_ref.shape[2], or_ref.shape[2]
    p, c = SSM_STATE, SSM_GROUP

    def rep(inner, total):
        r = lax.broadcasted_iota(jnp.int32, (SLAB, total), 0)
        col = lax.broadcasted_iota(jnp.int32, (SLAB, total), 1)
        return jnp.where((col & (inner - 1)) == r, 1.0, 0.0).astype(BF16)

    def group_mask(n_rows, rows_per_group, n_cols, cols_per_group):
        r = lax.broadcasted_iota(jnp.int32, (n_rows, n_cols), 0) >> int(math.log2(rows_per_group))
        col = lax.broadcasted_iota(jnp.int32, (n_rows, n_cols), 1) >> int(math.log2(cols_per_group))
        return r == col

    rep_in, rep_out = rep(p, n_out), rep(c, n_in)
    mask_in = group_mask(n_in, c, n_out, p)
    mask_out = group_mask(n_out, p, n_in, c)

    def split(x):
        hi = x.astype(BF16)
        return hi, (x - hi.astype(F32)).astype(BF16)

    def expand(x_b, rep_m, mask):
        full = jnp.dot(x_b, rep_m, preferred_element_type=F32)
        return jnp.where(mask, full, 0.0).astype(BF16)

    diag = (lax.broadcasted_iota(jnp.int32, (n_in, n_in), 0) == lax.broadcasted_iota(jnp.int32, (n_in, n_in), 1))
    for h in range(2):
        in0 = []
        for b in range(SUB):
            for src, dst in ((wr_ref, wfr_ref), (wi_ref, wfi_ref)):
                hi, lo = split(src[b, h])
                full = expand(hi, rep_in, mask_in)
                dst[b, h] = full
                if b == SUB - 1:
                    in0.append((full, expand(lo, rep_in, mask_in)))
        for j in range(SUB + 1):
            lag = None
            for (src, dst), (in_hi, in_lo) in zip(((or_ref, efr_ref), (oi_ref, efi_ref)), in0):
                hi, lo = split(src[j, h])
                out_hi = expand(hi, rep_out, mask_out)
                if j > 0:
                    dst[j - 1, h] = out_hi
                if j < SUB:
                    out_lo = expand(lo, rep_out, mask_out)
                    part = (jnp.dot(in_hi, out_hi, preferred_element_type=F32)
                            + jnp.dot(in_hi, out_lo, preferred_element_type=F32)
                            + jnp.dot(in_lo, out_hi, preferred_element_type=F32))
                    lag = part if lag is None else lag + part
            if j == 0:
                lag = lag + jnp.where(diag, d_ref[h], 0.0)
            if j < SUB:
                mt_ref[j, h] = lag.astype(BF16)


def _s5_expand(wc_r, wc_i, oc_r, oc_i, d_row):
    n_in, n_out = wc_r.shape[2], oc_r.shape[2]
    shapes = [jax.ShapeDtypeStruct((SUB, 2, n_in, n_in), BF16),
              jax.ShapeDtypeStruct((SUB, 2, n_in, n_out), BF16), jax.ShapeDtypeStruct((SUB, 2, n_in, n_out), BF16),
              jax.ShapeDtypeStruct((SUB, 2, n_out, n_in), BF16), jax.ShapeDtypeStruct((SUB, 2, n_out, n_in), BF16)]
    return pl.pallas_call(
        _s5_expand_kernel,
        out_shape=shapes,
        compiler_params=_params(),
        name="s5_expand",
    )(wc_r, wc_i, oc_r, oc_i, d_row)


def _cmul(ar, ai, br, bi):
    return ar * br - ai * bi, ar * bi + ai * br


def _s5_kernel(u_ref, mt_ref, wr_ref, wi_ref, er_ref, ei_ref, asub_ref, lvl_ref,
               gluw_ref, glub_ref, gout_ref, out_ref, carry_ref, zr_ref, zi_ref, locr_ref, loci_ref):
    rows = u_ref.shape[0]
    width = SSM_WIDTH

    @pl.when(pl.program_id(0) == 0)
    def _():
        carry_ref[...] = jnp.zeros_like(carry_ref)

    def u_tile(t, h):
        return u_ref[:, t * width + h * LANE_TILE:t * width + (h + 1) * LANE_TILE]

    a_r, a_i = asub_ref[0:1, :], asub_ref[1:2, :]

    for a in range(SUB):
        for h in range(2):
            cols = slice(h * HALF_STATE, (h + 1) * HALF_STATE)
            pr = pi = None
            for b in range(SUB):
                ut = u_tile(a * SUB + b, h)
                dr = jnp.dot(ut, wr_ref[b, h], preferred_element_type=F32)
                di = jnp.dot(ut, wi_ref[b, h], preferred_element_type=F32)
                pr = dr if pr is None else pr + dr
                pi = di if pi is None else pi + di
            if a == 0:
                zr_ref[:, cols] = pr
                zi_ref[:, cols] = pi
            else:
                hr, hi = _cmul(zr_ref[:, cols], zi_ref[:, cols], a_r[:, cols], a_i[:, cols])
                pr, pi = hr + pr, hi + pi
                zr_ref[:, cols] = pr
                zi_ref[:, cols] = pi
            if a < SUB - 1:
                locr_ref[a, :, cols] = pr.astype(BF16)
                loci_ref[a, :, cols] = pi.astype(BF16)

    row = lax.broadcasted_iota(jnp.int32, (rows, 1), 0)
    c_r, c_i = carry_ref[0:1, :], carry_ref[1:2, :]
    lam_r, lam_i = lvl_ref[0, 0:1, :], lvl_ref[0, 1:2, :]
    in_r, in_i = _cmul(c_r, c_i, lam_r, lam_i)
    sr = zr_ref[...] + jnp.where(row == 0, in_r, 0.0)
    si = zi_ref[...] + jnp.where(row == 0, in_i, 0.0)
    for lev in range(lvl_ref.shape[0]):
        sh = 1 << lev
        keep = row >= sh
        pr_, pi_ = lvl_ref[lev, 0:1, :], lvl_ref[lev, 1:2, :]
        tr = jnp.where(keep, pltpu.roll(sr, sh, axis=0), 0.0)
        ti = jnp.where(keep, pltpu.roll(si, sh, axis=0), 0.0)
        mr, mi = _cmul(tr, ti, pr_, pi_)
        sr, si = sr + mr, si + mi
    xr = jnp.where(row == 0, c_r, pltpu.roll(sr, 1, axis=0))
    xi = jnp.where(row == 0, c_i, pltpu.roll(si, 1, axis=0))
    carry_ref[0:1, :] = sr[rows - 1:rows, :]
    carry_ref[1:2, :] = si[rows - 1:rows, :]
    zr_ref[...] = xr
    zi_ref[...] = xi

    for a in range(SUB):
        if a > 0:
            nr, ni = _cmul(zr_ref[...], zi_ref[...], a_r, a_i)
            zr_ref[...] = nr
            zi_ref[...] = ni
        if a == 0:
            xr_b = zr_ref[...].astype(BF16)
            xi_b = zi_ref[...].astype(BF16)
        else:
            xr_b = (zr_ref[...] + locr_ref[a - 1].astype(F32)).astype(BF16)
            xi_b = (zi_ref[...] + loci_ref[a - 1].astype(F32)).astype(BF16)
        for b in range(SUB):
            t = a * SUB + b
            halves = []
            for h in range(2):
                cols = slice(h * HALF_STATE, (h + 1) * HALF_STATE)
                acc = jnp.dot(xr_b[:, cols], er_ref[b, h], preferred_element_type=F32)
                acc = acc + jnp.dot(xi_b[:, cols], ei_ref[b, h], preferred_element_type=F32)
                for b_in in range(b + 1):
                    acc = acc + jnp.dot(u_tile(a * SUB + b_in, h), mt_ref[b - b_in, h],
                                        preferred_element_type=F32)
                halves.append(acc)
            y = jnp.concatenate(halves, axis=1)
            z = 0.5 * y * (1.0 + jnp.tanh(math.sqrt(2.0 / math.pi) * (y + 0.044715 * (y * y * y))))
            gate = jnp.dot(z.astype(BF16), gluw_ref[...], preferred_element_type=F32) + glub_ref[...]
            o = z * (1.0 / (1.0 + jnp.exp(-gate)))
            ms = jnp.mean(o * o, axis=-1, keepdims=True)
            out_ref[:, t * width:(t + 1) * width] = (o * lax.rsqrt(ms + EPS) * gout_ref[...]).astype(BF16)


def _s5(uv, w, glu_w, glu_b, g_out):
    n = uv.shape[0]
    rows = S5_ROWS
    tile = pl.BlockSpec((rows, CHUNK * SSM_WIDTH), lambda i: (i, 0))
    return pl.pallas_call(
        _s5_kernel,
        grid=(n // rows,),
        in_specs=[tile] + [_const_spec(a.shape) for a in
                           (w["m_t"], w["w_r"], w["w_i"], w["e_r"], w["e_i"], w["a_sub"], w["levels"],
                            glu_w, glu_b, g_out)],
        out_specs=tile,
        out_shape=jax.ShapeDtypeStruct((n, CHUNK * SSM_WIDTH), BF16),
        scratch_shapes=[pltpu.VMEM((2, STATE_COLS), F32),
                        pltpu.VMEM((rows, STATE_COLS), F32),
                        pltpu.VMEM((rows, STATE_COLS), F32),
                        pltpu.VMEM((SUB - 1, rows, STATE_COLS), BF16),
                        pltpu.VMEM((SUB - 1, rows, STATE_COLS), BF16)],
        compiler_params=_params(dimension_semantics=("arbitrary",)),
        name="s5",
    )(uv, w["m_t"], w["w_r"], w["w_i"], w["e_r"], w["e_i"], w["a_sub"], w["levels"], glu_w, glu_b, g_out)


def _mlp_kernel(x_ref, o1_ref, o2_ref, o3_ref, l1_ref, l2_ref, l3_ref, ssm_ref, ga_ref, wout_ref,
                g2_ref, wup_ref, wdn_ref, out_ref, slab_ref, slab4_ref, mix_ref):
    rows = PROJ_SUB
    n_sub = x_ref.shape[0] // rows
    n_slabs = ATTN_WIDTH // SLAB
    step = DILATIONS[1]
    n_ff = D_FF // FF_CHUNK

    def regroup(ref, d, sub, buf):
        part = slice(sub * (rows // d), (sub + 1) * (rows // d))
        for s in range(n_slabs):
            t, lanes = _slab_home(s)
            if ref is ssm_ref:
                piece = lambda r: ref[part, r * SSM_WIDTH + s * SLAB:r * SSM_WIDTH + (s + 1) * SLAB]
            else:
                piece = lambda r: ref[t, r, part, lanes]
            if d == step:
                for r in range(d):
                    slab_ref[sub, buf, s, pl.ds(r, rows // d, stride=d), :] = piece(r).astype(F32)
            else:
                for r4 in range(step):
                    for j4 in range(step):
                        slab4_ref[sub, buf - 2, s, r4, pl.ds(j4, rows // d, stride=step), :] = (
                            piece(step * j4 + r4).astype(F32))
                for r4 in range(step):
                    slab_ref[sub, buf, s, pl.ds(r4, rows // step, stride=step), :] = slab4_ref[sub, buf - 2, s, r4]

    def in_order(sub, buf):
        return jnp.concatenate([slab_ref[sub, buf, s] for s in range(n_slabs)], axis=1)

    def merge_slab(sub, s):
        t, lanes = _slab_home(s)
        part = slice(sub * rows, (sub + 1) * rows)
        l1, l2, l3 = l1_ref[t, 0, part, lanes], slab_ref[sub, 1, s], slab_ref[sub, 3, s]
        m = jnp.maximum(jnp.maximum(l1, l2), l3)
        w1, w2, w3 = jnp.exp(l1 - m), jnp.exp(l2 - m), jnp.exp(l3 - m)
        num = w1 * o1_ref[t, 0, part, lanes].astype(F32) + w2 * slab_ref[sub, 0, s] + w3 * slab_ref[sub, 2, s]
        slab_ref[sub, 0, s] = num / (w1 + w2 + w3)

    def merge_finish(sub):
        attn = in_order(sub, 0)
        ms = jnp.mean(attn * attn, axis=-1, keepdims=True)
        mix_ref[sub, :, 0:ATTN_WIDTH] = (attn * lax.rsqrt(ms + EPS) * ga_ref[...]).astype(BF16)
        mix_ref[sub, :, ATTN_WIDTH:] = in_order(sub, 4).astype(BF16)

    def prologue(sub):
        return ([lambda: regroup(o2_ref, step, sub, 0), lambda: regroup(l2_ref, step, sub, 1),
                 lambda: regroup(o3_ref, CHUNK, sub, 2), lambda: regroup(l3_ref, CHUNK, sub, 3),
                 lambda: regroup(ssm_ref, CHUNK, sub, 4)]
                + [lambda s=s: merge_slab(sub, s) for s in range(n_slabs)] + [lambda: merge_finish(sub)])

    def main(sub):
        part = slice(sub * rows, (sub + 1) * rows)
        st = {}

        def out_proj():
            x1 = x_ref[part, :] + jnp.dot(mix_ref[sub], wout_ref[...], preferred_element_type=F32)
            ms2 = jnp.mean(x1 * x1, axis=-1, keepdims=True)
            st["xn"] = (x1 * lax.rsqrt(ms2 + EPS) * g2_ref[...]).astype(BF16)
            st["acc"] = x1

        def up(c):
            st[c] = jnp.dot(st["xn"], wup_ref[:, c * FF_CHUNK:(c + 1) * FF_CHUNK], preferred_element_type=F32)

        def down(c):
            h = jnp.square(jnp.maximum(st.pop(c), 0.0)).astype(BF16)
            st["acc"] = st["acc"] + jnp.dot(h, wdn_ref[c * FF_CHUNK:(c + 1) * FF_CHUNK, :],
                                            preferred_element_type=F32)

        def finish():
            out_ref[part, :] = st["acc"]

        stages = [out_proj, lambda: up(0)]
        for c in range(n_ff):
            if c + 1 < n_ff:
                stages.append(lambda c=c: up(c + 1))
            stages.append(lambda c=c: down(c))
        return stages + [finish]

    tail = 3
    program = prologue(0)
    all_stages = [main(sub) for sub in range(n_sub)]
    for sub in range(n_sub):
        stages = all_stages[sub]
        head = stages if sub == 0 else stages[1:]
        if sub + 1 < n_sub:
            side = prologue(sub + 1)
            body = []
            for stage in head[:-tail]:
                body.append(stage)
                if side:
                    body.append(side.pop(0))
            program += body + side + [all_stages[sub + 1][0]] + head[-tail:]
        else:
            program += head
    for piece in program:
        piece()


def _mlp(x, os_, ls_, ssm, ga, w_out, g2, w_up, w_dn):
    s = x.shape[0]
    rows = MLP_ROWS
    xt = pl.BlockSpec((rows, D_MODEL), lambda i: (i, 0))
    ht = [pl.BlockSpec((N_HEAD_TILES, d, rows // d, LANE_TILE), lambda i: (0, 0, i, 0)) for d in DILATIONS]
    return pl.pallas_call(
        _mlp_kernel,
        grid=(s // rows,),
        in_specs=[xt, *ht, *ht, pl.BlockSpec((rows // CHUNK, CHUNK * SSM_WIDTH), lambda i: (i, 0)),
                  _const_spec(ga.shape), _const_spec(w_out.shape), _const_spec(g2.shape),
                  _const_spec(w_up.shape), _const_spec(w_dn.shape)],
        out_specs=xt,
        out_shape=jax.ShapeDtypeStruct((s, D_MODEL), F32),
        scratch_shapes=[pltpu.VMEM((rows // PROJ_SUB, 5, ATTN_WIDTH // SLAB, PROJ_SUB, SLAB), F32),
                        pltpu.VMEM((rows // PROJ_SUB, 3, ATTN_WIDTH // SLAB, DILATIONS[1],
                                    PROJ_SUB // DILATIONS[1], SLAB), F32),
                        pltpu.VMEM((rows // PROJ_SUB, PROJ_SUB, D_MODEL), BF16)],
        compiler_params=_params(dimension_semantics=("arbitrary",)),
        name="mlp",
    )(x, *os_, *ls_, ssm, ga, w_out, g2, w_up, w_dn)


def _layer(x, norm1_g, w_in, q_norm_g, k_norm_g, ssm_a_re, ssm_a_im, ssm_log_dt, ssm_b_re, ssm_b_im,
           ssm_c_re, ssm_c_im, ssm_d, glu_w, glu_b, attn_out_norm_g, ssm_out_norm_g, w_out, norm2_g,
           w_mlp_up, w_mlp_down):
    row = lambda g: g.astype(F32).reshape(1, -1)
    heads = ATTN_WIDTH // HEAD_DIM
    head_id = jnp.arange(LANE_TILE) // HEAD_DIM
    hmat = jnp.where(head_id[:, None] == head_id[None, :], 1.0 / HEAD_DIM, 0.0).astype(BF16)
    *qkv, u = _proj(x, row(norm1_g), w_in.astype(BF16), hmat,
                    row(jnp.tile(q_norm_g, heads)), row(jnp.tile(k_norm_g, heads)))
    outs, lses = [], []
    for i in range(len(DILATIONS)):
        o, lse = _attn_pattern(*qkv[3 * i:3 * i + 3])
        outs.append(o)
        lses.append(lse)
    w = _s5_weights(ssm_a_re, ssm_a_im, ssm_log_dt, ssm_b_re, ssm_b_im, ssm_c_re, ssm_c_im, ssm_d)
    ssm = _s5(u, w, glu_w.astype(BF16), row(glu_b), row(ssm_out_norm_g))
    return _mlp(x, outs, lses, ssm, row(attn_out_norm_g), w_out.astype(BF16), row(norm2_g),
                w_mlp_up.astype(BF16), w_mlp_down.astype(BF16))


def kernel(x, norm1_g, w_in, q_norm_g, k_norm_g, ssm_a_re, ssm_a_im, ssm_log_dt, ssm_b_re, ssm_b_im,
           ssm_c_re, ssm_c_im, ssm_d, glu_w, glu_b, attn_out_norm_g, ssm_out_norm_g, w_out, norm2_g,
           w_mlp_up, w_mlp_down):
    params = (norm1_g, w_in, q_norm_g, k_norm_g, ssm_a_re, ssm_a_im, ssm_log_dt, ssm_b_re, ssm_b_im,
              ssm_c_re, ssm_c_im, ssm_d, glu_w, glu_b, attn_out_norm_g, ssm_out_norm_g, w_out, norm2_g,
              w_mlp_up, w_mlp_down)
    batch = x.shape[0]
    outs = []
    for bi in range(batch):
        h = x[bi]
        for layer in range(norm1_g.shape[0]):
            h = _layer(h, *[p[layer] for p in params])
        outs.append(h)
    return jnp.stack(outs)
```

```python
import functools
import math

import jax
import jax.numpy as jnp
from jax import lax
from jax.experimental import pallas as pl
from jax.experimental.pallas import tpu as pltpu

F32 = jnp.float32
BF16 = jnp.bfloat16

D_MODEL = 1024
ATTN_WIDTH = 512
HEAD_DIM = 64
SSM_WIDTH = 512
SSM_GROUP = 16
SSM_GROUPS = 32
SSM_STATE = 64
D_FF = 4096
EPS = 1e-6
NEG_INF = -1e30
DILATIONS = (1, 4, 16)
BLOCK = 128

SLAB = 128
LANE_TILE = 256
HEADS_PER_TILE = LANE_TILE // HEAD_DIM
N_HEAD_TILES = ATTN_WIDTH // LANE_TILE
CHUNK = 16
SUB = 4
SCAN_GROUP = 8
HALF_STATE = (SSM_GROUPS // 2) * SSM_STATE
STATE_COLS = SSM_GROUPS * SSM_STATE
VMEM_LIMIT = 56 * 1024 * 1024

LOG2E = math.log2(math.e)
Q_SCALE = HEAD_DIM ** -0.5 * LOG2E

PROJ_ROWS = 1024
PROJ_SUB = 256
SLAB_BUFFERS = 2
ATTN_ROWS = 2048
S5_ROWS = 256
MLP_ROWS = 512
FF_CHUNK = 512


def _const_spec(shape):
    nd = len(shape)
    return pl.BlockSpec(shape, lambda *_: (0,) * nd, pipeline_mode=pl.Buffered(1))


def _params(**kw):
    return pltpu.CompilerParams(vmem_limit_bytes=VMEM_LIMIT, **kw)


def _proj_kernel(x_ref, g1_ref, w_ref, hmat_ref, gq_ref, gk_ref,
                 q1_ref, k1_ref, v1_ref, q4_ref, k4_ref, v4_ref, q16_ref, k16_ref, v16_ref, u_ref,
                 slab_ref, slab4_ref, wb_ref):
    @pl.when(pl.program_id(0) == 0)
    def _():
        wb_ref[...] = w_ref[...].astype(BF16)

    outs = ((q1_ref, q4_ref, q16_ref), (k1_ref, k4_ref, k16_ref), (v1_ref, v4_ref, v16_ref))
    slabs_per_tensor = ATTN_WIDTH // SLAB
    step = DILATIONS[1]
    rows = PROJ_SUB
    n_sub = x_ref.shape[0] // rows
    n_chunks = w_ref.shape[1] // LANE_TILE
    chunks_per_tensor = ATTN_WIDTH // LANE_TILE
    gains = (gq_ref, gk_ref)
    xn = [None] * n_sub

    def prologue(sub):
        x = x_ref[sub * rows:(sub + 1) * rows, :]
        ms = jnp.mean(x * x, axis=-1, keepdims=True)
        xn[sub] = (x * lax.rsqrt(ms + EPS) * g1_ref[...]).astype(BF16)

    def matmul(sub, c):
        return jnp.dot(xn[sub], wb_ref[:, c * LANE_TILE:(c + 1) * LANE_TILE], preferred_element_type=F32)

    def epilogue(sub, c, val):
        ti, t = divmod(c, chunks_per_tensor)
        lane_tile = slice(t * LANE_TILE, (t + 1) * LANE_TILE)
        if ti < 2:
            ms_h = jnp.dot((val * val).astype(BF16), hmat_ref[...], preferred_element_type=F32)
            val = val * lax.rsqrt(ms_h + EPS) * gains[ti][:, lane_tile]
            if ti == 0:
                val = val * Q_SCALE
        if ti < 3:
            outs[ti][0][t, 0, sub * rows:(sub + 1) * rows, :] = val.astype(BF16)
        r4_rows = slice(sub * (rows // step), (sub + 1) * (rows // step))
        r16_rows = slice(sub * (rows // CHUNK), (sub + 1) * (rows // CHUNK))
        for half in range(LANE_TILE // SLAB):
            s = t * (LANE_TILE // SLAB) + half
            lanes = slice(half * SLAB, (half + 1) * SLAB)
            slab = ti * slabs_per_tensor + s
            buf = sub % SLAB_BUFFERS
            slab_ref[buf, slab] = val[:, lanes]
            for r4 in range(step):
                part = slab_ref[buf, slab, pl.ds(r4, rows // step, stride=step), :]
                if ti < 3:
                    outs[ti][1][t, r4, r4_rows, lanes] = part.astype(BF16)
                slab4_ref[buf, slab, r4] = part
            for r4 in range(step):
                for j4 in range(step):
                    r16 = step * j4 + r4
                    part = slab4_ref[buf, slab, r4, pl.ds(j4, rows // CHUNK, stride=step), :].astype(BF16)
                    if ti < 3:
                        outs[ti][2][t, r16, r16_rows, lanes] = part
                    else:
                        u_ref[r16_rows, r16 * SSM_WIDTH + s * SLAB:r16 * SSM_WIDTH + (s + 1) * SLAB] = part

    work = [(sub, c) for sub in range(n_sub) for c in range(n_chunks)]
    prologue(0)
    pending = None
    for sub, c in work:
        val = matmul(sub, c)
        if pending is not None:
            epilogue(*pending)
        pending = (sub, c, val)
        if c == n_chunks // 2 and sub + 1 < n_sub:
            prologue(sub + 1)
    epilogue(*pending)


def _slab_home(s):
    t, half = divmod(s, LANE_TILE // SLAB)
    return t, slice(half * SLAB, (half + 1) * SLAB)


def _proj(x, g1, w_in, hmat, gq, gk):
    s = x.shape[0]
    rows = PROJ_ROWS
    out_specs, out_shape = [], []
    for d in DILATIONS:
        for _ in range(3):
            out_specs.append(pl.BlockSpec((N_HEAD_TILES, d, rows // d, LANE_TILE), lambda i: (0, 0, i, 0)))
            out_shape.append(jax.ShapeDtypeStruct((N_HEAD_TILES, d, s // d, LANE_TILE), BF16))
    out_specs.append(pl.BlockSpec((rows // CHUNK, CHUNK * SSM_WIDTH), lambda i: (i, 0)))
    out_shape.append(jax.ShapeDtypeStruct((s // CHUNK, CHUNK * SSM_WIDTH), BF16))
    return pl.pallas_call(
        _proj_kernel,
        grid=(s // rows,),
        in_specs=[
            pl.BlockSpec((rows, D_MODEL), lambda i: (i, 0)),
            _const_spec((1, D_MODEL)),
            _const_spec(w_in.shape),
            _const_spec(hmat.shape),
            _const_spec((1, ATTN_WIDTH)),
            _const_spec((1, ATTN_WIDTH)),
        ],
        out_specs=out_specs,
        out_shape=out_shape,
        scratch_shapes=[pltpu.VMEM((SLAB_BUFFERS, 4 * ATTN_WIDTH // SLAB, PROJ_SUB, SLAB), F32),
                        pltpu.VMEM((SLAB_BUFFERS, 4 * ATTN_WIDTH // SLAB, DILATIONS[1],
                                    PROJ_SUB // DILATIONS[1], SLAB), F32),
                        pltpu.VMEM(w_in.shape, BF16)],
        compiler_params=_params(dimension_semantics=("arbitrary",)),
        name="proj",
    )(x, g1, w_in, hmat, gq, gk)


def _attn_kernel(q_ref, kc_ref, kp_ref, vc_ref, vp_ref, o_ref, lse_ref, kbuf, vbuf):
    n_res, rows = q_ref.shape[0], q_ref.shape[1]
    kbuf[:, 0:BLOCK, :] = kp_ref[...]
    kbuf[:, BLOCK:, :] = kc_ref[...]
    vbuf[:, 0:BLOCK, :] = vp_ref[...]
    vbuf[:, BLOCK:, :] = vc_ref[...]

    lane = lax.broadcasted_iota(jnp.int32, (BLOCK, LANE_TILE), 1)
    head_masks = [(lane >= h * HEAD_DIM) & (lane < (h + 1) * HEAD_DIM) for h in range(HEADS_PER_TILE)]
    qi = lax.broadcasted_iota(jnp.int32, (BLOCK, 2 * BLOCK), 0)
    ki = lax.broadcasted_iota(jnp.int32, (BLOCK, 2 * BLOCK), 1)
    band = (ki >= qi) & (ki <= qi + BLOCK)
    bias_band = jnp.where(band, 0.0, NEG_INF).astype(F32)
    no_prev = pl.program_id(2) == 0
    bias_first = jnp.where(band & ((ki >= BLOCK) | jnp.logical_not(no_prev)), 0.0, NEG_INF).astype(F32)

    for r, b in [(r, b) for r in range(n_res) for b in range(rows // BLOCK)]:
        qb = q_ref[r, b * BLOCK:(b + 1) * BLOCK, :]
        zero = jnp.zeros_like(qb)
        q_stack = jnp.concatenate([jnp.where(mk, qb, zero) for mk in head_masks], axis=0)
        kw = kbuf[r, b * BLOCK:(b + 2) * BLOCK, :]
        vw = vbuf[r, b * BLOCK:(b + 2) * BLOCK, :]
        s = lax.dot_general(q_stack, kw, (((1,), (1,)), ((), ())), preferred_element_type=F32)
        bias = bias_first if b == 0 else bias_band
        ps, ms, ls = [], [], []
        for h in range(HEADS_PER_TILE):
            sh = s[h * BLOCK:(h + 1) * BLOCK, :] + bias
            m = jnp.max(sh, axis=-1, keepdims=True)
            p = jnp.exp2(sh - m)
            ls.append(jnp.sum(p, axis=-1, keepdims=True))
            ms.append(m)
            ps.append(p.astype(BF16))
        pv = jnp.dot(jnp.concatenate(ps, axis=0), vw, preferred_element_type=F32)
        o = jnp.zeros((BLOCK, LANE_TILE), F32)
        lse = jnp.zeros((BLOCK, LANE_TILE), F32)
        for h in range(HEADS_PER_TILE):
            o = jnp.where(head_masks[h], pv[h * BLOCK:(h + 1) * BLOCK, :] / ls[h], o)
            lse = jnp.where(head_masks[h], (ms[h] + jnp.log2(ls[h])) * (1.0 / LOG2E), lse)
        o_ref[r, b * BLOCK:(b + 1) * BLOCK, :] = o.astype(BF16)
        lse_ref[r, b * BLOCK:(b + 1) * BLOCK, :] = lse


def _attn_pattern(q, k, v):
    nt, dilation, n, _ = q.shape
    rows = min(ATTN_ROWS, n)
    n_res = min(ATTN_ROWS // rows, dilation)
    blocks_per_tile = rows // BLOCK
    cur = pl.BlockSpec((None, n_res, rows, LANE_TILE), lambda t, r, j: (t, r, j, 0))
    prev = pl.BlockSpec((None, n_res, BLOCK, LANE_TILE),
                        lambda t, r, j: (t, r, jnp.maximum(j * blocks_per_tile - 1, 0), 0))
    return pl.pallas_call(
        _attn_kernel,
        grid=(nt, dilation // n_res, n // rows),
        in_specs=[cur, cur, prev, cur, prev],
        out_specs=[cur, cur],
        out_shape=[jax.ShapeDtypeStruct(q.shape, BF16), jax.ShapeDtypeStruct(q.shape, F32)],
        scratch_shapes=[pltpu.VMEM((n_res, rows + BLOCK, LANE_TILE), BF16),
                        pltpu.VMEM((n_res, rows + BLOCK, LANE_TILE), BF16)],
        compiler_params=_params(dimension_semantics=("arbitrary", "arbitrary", "arbitrary")),
        name=f"attn_d{dilation}",
    )(q, k, k, v, v)


def _s5_weights(a_re, a_im, log_dt, b_re, b_im, c_re, c_im, d_skip):
    g, p, c = SSM_GROUPS, SSM_STATE, SSM_GROUP
    gl = g // 2
    lr, li = a_re.astype(F32), a_im.astype(F32)
    dt = jnp.exp(log_dt.astype(F32))[:, None]

    def apow(j):
        mag = jnp.exp(lr * dt * j)
        return mag * jnp.cos(li * dt * j), mag * jnp.sin(li * dt * j)

    ab_r, ab_i = apow(1.0)
    den = lr * lr + li * li
    nr, ni = ab_r - 1.0, ab_i
    cr = (nr * lr + ni * li) / den
    ci = (ni * lr - nr * li) / den
    br, bi = b_re.astype(F32), b_im.astype(F32)
    bb_r = cr[..., None] * br - ci[..., None] * bi
    bb_i = cr[..., None] * bi + ci[..., None] * br
    cre, cim = c_re.astype(F32), c_im.astype(F32)

    def apow_many(js):
        j = jnp.asarray(js, F32)[:, None, None]
        mag = jnp.exp(lr * dt * j)
        return mag * jnp.cos(li * dt * j), mag * jnp.sin(li * dt * j)

    pr, pi = apow_many(range(SUB + 1))
    wp_r, wp_i = pr[SUB - 1::-1], pi[SUB - 1::-1]
    win_r = wp_r[..., None] * bb_r[None] - wp_i[..., None] * bb_i[None]
    win_i = wp_r[..., None] * bb_i[None] + wp_i[..., None] * bb_r[None]
    pad_lanes = lambda w: jnp.pad(w, ((0, 0), (0, 0), (0, 0), (0, SLAB - w.shape[-1])))
    to_in = lambda w: pad_lanes(w.transpose(0, 1, 3, 2).reshape(SUB, 2, gl * c, p))
    out_r = cre[None] * pr[:, :, None, :] - cim[None] * pi[:, :, None, :]
    out_i = -(cre[None] * pi[:, :, None, :] + cim[None] * pr[:, :, None, :])
    to_out = lambda w: pad_lanes(w.transpose(0, 1, 3, 2).reshape(SUB + 1, 2, gl * p, c))

    flat = lambda re, im: jnp.stack([re.reshape(-1), im.reshape(-1)])
    top_r, top_i = apow_many([CHUNK])
    lam = flat(top_r[0], top_i[0])
    levels = [lam]
    for _ in range(SCAN_GROUP - 1):
        lr_, li_ = levels[-1][0], levels[-1][1]
        levels.append(jnp.stack([lr_ * lam[0] - li_ * lam[1], lr_ * lam[1] + li_ * lam[0]]))
    m_t, w_r, w_i, e_r, e_i = _s5_expand(to_in(win_r), to_in(win_i), to_out(out_r), to_out(out_i),
                                         d_skip.astype(F32).reshape(2, 1, gl * c))
    return dict(m_t=m_t, w_r=w_r, w_i=w_i, e_r=e_r, e_i=e_i,
                a_sub=flat(pr[SUB], pi[SUB]), levels=jnp.stack(levels, axis=1))


def _s5_expand_kernel(wr_ref, wi_ref, or_ref, oi_ref, d_ref, mt_ref, wfr_ref, wfi_ref, efr_ref, efi_ref):
    n_in, n_out = wr_ref.shape[2], or_ref.shape[2]
    p, c = SSM_STATE, SSM_GROUP

    def rep(inner, total):
        r = lax.broadcasted_iota(jnp.int32, (SLAB, total), 0)
        col = lax.broadcasted_iota(jnp.int32, (SLAB, total), 1)
        return jnp.where((col & (inner - 1)) == r, 1.0, 0.0).astype(BF16)

    def group_mask(n_rows, rows_per_group, n_cols, cols_per_group):
        r = lax.broadcasted_iota(jnp.int32, (n_rows, n_cols), 0) >> int(math.log2(rows_per_group))
        col = lax.broadcasted_iota(jnp.int32, (n_rows, n_cols), 1) >> int(math.log2(cols_per_group))
        return r == col

    rep_in, rep_out = rep(p, n_out), rep(c, n_in)
    mask_in = group_mask(n_in, c, n_out, p)
    mask_out = group_mask(n_out, p, n_in, c)

    def split(x):
        hi = x.astype(BF16)
        return hi, (x - hi.astype(F32)).astype(BF16)

    def expand(x_b, rep_m, mask):
        full = jnp.dot(x_b, rep_m, preferred_element_type=F32)
        return jnp.where(mask, full, 0.0).astype(BF16)

    diag = (lax.broadcasted_iota(jnp.int32, (n_in, n_in), 0) == lax.broadcasted_iota(jnp.int32, (n_in, n_in), 1))
    for h in range(2):
        in0 = []
        for b in range(SUB):
            for src, dst in ((wr_ref, wfr_ref), (wi_ref, wfi_ref)):
                hi, lo = split(src[b, h])
                full = expand(hi, rep_in, mask_in)
                dst[b, h] = full
                if b == SUB - 1:
                    in0.append((full, expand(lo, rep_in, mask_in)))
        for j in range(SUB + 1):
            lag = None
            for (src, dst), (in_hi, in_lo) in zip(((or_ref, efr_ref), (oi_ref, efi_ref)), in0):
                hi, lo = split(src[j, h])
                out_hi = expand(hi, rep_out, mask_out)
                if j > 0:
                    dst[j - 1, h] = out_hi
                if j < SUB:
                    out_lo = expand(lo, rep_out, mask_out)
                    part = (jnp.dot(in_hi, out_hi, preferred_element_type=F32)
                            + jnp.dot(in_hi, out_lo, preferred_element_type=F32)
                            + jnp.dot(in_lo, out_hi, preferred_element_type=F32))
                    lag = part if lag is None else lag + part
            if j == 0:
                lag = lag + jnp.where(diag, d_ref[h], 0.0)
            if j < SUB:
                mt_ref[j, h] = lag.astype(BF16)


def _s5_expand(wc_r, wc_i, oc_r, oc_i, d_row):
    n_in, n_out = wc_r.shape[2], oc_r.shape[2]
    shapes = [jax.ShapeDtypeStruct((SUB, 2, n_in, n_in), BF16),
              jax.ShapeDtypeStruct((SUB, 2, n_in, n_out), BF16), jax.ShapeDtypeStruct((SUB, 2, n_in, n_out), BF16),
              jax.ShapeDtypeStruct((SUB, 2, n_out, n_in), BF16), jax.ShapeDtypeStruct((SUB, 2, n_out, n_in), BF16)]
    return pl.pallas_call(
        _s5_expand_kernel,
        out_shape=shapes,
        compiler_params=_params(),
        name="s5_expand",
    )(wc_r, wc_i, oc_r, oc_i, d_row)


def _cmul(ar, ai, br, bi):
    return ar * br - ai * bi, ar * bi + ai * br


def _s5_kernel(u_ref, mt_ref, wr_ref, wi_ref, er_ref, ei_ref, asub_ref, lvl_ref,
               gluw_ref, glub_ref, gout_ref, out_ref, carry_ref, zr_ref, zi_ref, locr_ref, loci_ref):
    rows = u_ref.shape[0]
    width = SSM_WIDTH

    @pl.when(pl.program_id(0) == 0)
    def _():
        carry_ref[...] = jnp.zeros_like(carry_ref)

    def u_tile(t, h):
        return u_ref[:, t * width + h * LANE_TILE:t * width + (h + 1) * LANE_TILE]

    a_r, a_i = asub_ref[0:1, :], asub_ref[1:2, :]

    for a in range(SUB):
        for h in range(2):
            cols = slice(h * HALF_STATE, (h + 1) * HALF_STATE)
            pr = pi = None
            for b in range(SUB):
                ut = u_tile(a * SUB + b, h)
                dr = jnp.dot(ut, wr_ref[b, h], preferred_element_type=F32)
                di = jnp.dot(ut, wi_ref[b, h], preferred_element_type=F32)
                pr = dr if pr is None else pr + dr
                pi = di if pi is None else pi + di
            if a == 0:
                zr_ref[:, cols] = pr
                zi_ref[:, cols] = pi
            else:
                hr, hi = _cmul(zr_ref[:, cols], zi_ref[:, cols], a_r[:, cols], a_i[:, cols])
                pr, pi = hr + pr, hi + pi
                zr_ref[:, cols] = pr
                zi_ref[:, cols] = pi
            if a < SUB - 1:
                locr_ref[a, :, cols] = pr.astype(BF16)
                loci_ref[a, :, cols] = pi.astype(BF16)

    row = lax.broadcasted_iota(jnp.int32, (rows, 1), 0)
    in_group = row & (SCAN_GROUP - 1)
    sr, si = zr_ref[...], zi_ref[...]
    sh = 1
    while sh < SCAN_GROUP:
        keep = in_group >= sh
        pr_, pi_ = lvl_ref[0, sh - 1:sh, :], lvl_ref[1, sh - 1:sh, :]
        tr = jnp.where(keep, pltpu.roll(sr, sh, axis=0), 0.0)
        ti = jnp.where(keep, pltpu.roll(si, sh, axis=0), 0.0)
        mr, mi = _cmul(tr, ti, pr_, pi_)
        sr, si = sr + mr, si + mi
        sh *= 2
    c_r, c_i = carry_ref[0:1, :], carry_ref[1:2, :]
    pw_r, pw_i = lvl_ref[0], lvl_ref[1]
    last_r, last_i = c_r, c_i
    for g in range(rows // SCAN_GROUP):
        grp = slice(g * SCAN_GROUP, (g + 1) * SCAN_GROUP)
        ar, ai = _cmul(jnp.broadcast_to(last_r, pw_r.shape), jnp.broadcast_to(last_i, pw_i.shape), pw_r, pw_i)
        gr, gi = sr[grp, :] + ar, si[grp, :] + ai
        zr_ref[grp, :] = gr
        zi_ref[grp, :] = gi
        last_r, last_i = gr[SCAN_GROUP - 1:SCAN_GROUP, :], gi[SCAN_GROUP - 1:SCAN_GROUP, :]
    carry_ref[0:1, :] = last_r
    carry_ref[1:2, :] = last_i
    xr = jnp.where(row == 0, c_r, pltpu.roll(zr_ref[...], 1, axis=0))
    xi = jnp.where(row == 0, c_i, pltpu.roll(zi_ref[...], 1, axis=0))
    zr_ref[...] = xr
    zi_ref[...] = xi

    for a in range(SUB):
        if a > 0:
            nr, ni = _cmul(zr_ref[...], zi_ref[...], a_r, a_i)
            zr_ref[...] = nr
            zi_ref[...] = ni
        if a == 0:
            xr_b = zr_ref[...].astype(BF16)
            xi_b = zi_ref[...].astype(BF16)
        else:
            xr_b = (zr_ref[...] + locr_ref[a - 1].astype(F32)).astype(BF16)
            xi_b = (zi_ref[...] + loci_ref[a - 1].astype(F32)).astype(BF16)
        for b in range(SUB):
            t = a * SUB + b
            halves = []
            for h in range(2):
                cols = slice(h * HALF_STATE, (h + 1) * HALF_STATE)
                acc = jnp.dot(xr_b[:, cols], er_ref[b, h], preferred_element_type=F32)
                acc = acc + jnp.dot(xi_b[:, cols], ei_ref[b, h], preferred_element_type=F32)
                for b_in in range(b + 1):
                    acc = acc + jnp.dot(u_tile(a * SUB + b_in, h), mt_ref[b - b_in, h],
                                        preferred_element_type=F32)
                halves.append(acc)
            y = jnp.concatenate(halves, axis=1)
            z = 0.5 * y * (1.0 + jnp.tanh(math.sqrt(2.0 / math.pi) * (y + 0.044715 * (y * y * y))))
            gate = jnp.dot(z.astype(BF16), gluw_ref[...], preferred_element_type=F32) + glub_ref[...]
            o = z * (1.0 / (1.0 + jnp.exp(-gate)))
            ms = jnp.mean(o * o, axis=-1, keepdims=True)
            out_ref[:, t * width:(t + 1) * width] = (o * lax.rsqrt(ms + EPS) * gout_ref[...]).astype(BF16)


def _s5(uv, w, glu_w, glu_b, g_out):
    n = uv.shape[0]
    rows = S5_ROWS
    tile = pl.BlockSpec((rows, CHUNK * SSM_WIDTH), lambda i: (i, 0))
    return pl.pallas_call(
        _s5_kernel,
        grid=(n // rows,),
        in_specs=[tile] + [_const_spec(a.shape) for a in
                           (w["m_t"], w["w_r"], w["w_i"], w["e_r"], w["e_i"], w["a_sub"], w["levels"],
                            glu_w, glu_b, g_out)],
        out_specs=tile,
        out_shape=jax.ShapeDtypeStruct((n, CHUNK * SSM_WIDTH), BF16),
        scratch_shapes=[pltpu.VMEM((2, STATE_COLS), F32),
                        pltpu.VMEM((rows, STATE_COLS), F32),
                        pltpu.VMEM((rows, STATE_COLS), F32),
                        pltpu.VMEM((SUB - 1, rows, STATE_COLS), BF16),
                        pltpu.VMEM((SUB - 1, rows, STATE_COLS), BF16)],
        compiler_params=_params(dimension_semantics=("arbitrary",)),
        name="s5",
    )(uv, w["m_t"], w["w_r"], w["w_i"], w["e_r"], w["e_i"], w["a_sub"], w["levels"], glu_w, glu_b, g_out)


def _mlp_kernel(x_ref, o1_ref, o2_ref, o3_ref, l1_ref, l2_ref, l3_ref, ssm_ref, ga_ref, wout_ref,
                g2_ref, wup_ref, wdn_ref, out_ref, slab_ref, slab4_ref, mix_ref):
    rows = PROJ_SUB
    n_sub = x_ref.shape[0] // rows
    n_slabs = ATTN_WIDTH // SLAB
    step = DILATIONS[1]
    n_ff = D_FF // FF_CHUNK

    def regroup(ref, d, sub, buf):
        part = slice(sub * (rows // d), (sub + 1) * (rows // d))
        for s in range(n_slabs):
            t, lanes = _slab_home(s)
            if ref is ssm_ref:
                piece = lambda r: ref[part, r * SSM_WIDTH + s * SLAB:r * SSM_WIDTH + (s + 1) * SLAB]
            else:
                piece = lambda r: ref[t, r, part, lanes]
            if d == step:
                for r in range(d):
                    slab_ref[sub, buf, s, pl.ds(r, rows // d, stride=d), :] = piece(r).astype(F32)
            else:
                for r4 in range(step):
                    for j4 in range(step):
                        slab4_ref[sub, buf - 2, s, r4, pl.ds(j4, rows // d, stride=step), :] = (
                            piece(step * j4 + r4).astype(F32))
                for r4 in range(step):
                    slab_ref[sub, buf, s, pl.ds(r4, rows // step, stride=step), :] = slab4_ref[sub, buf - 2, s, r4]

    def in_order(sub, buf):
        return jnp.concatenate([slab_ref[sub, buf, s] for s in range(n_slabs)], axis=1)

    def merge_slab(sub, s):
        t, lanes = _slab_home(s)
        part = slice(sub * rows, (sub + 1) * rows)
        l1, l2, l3 = l1_ref[t, 0, part, lanes], slab_ref[sub, 1, s], slab_ref[sub, 3, s]
        m = jnp.maximum(jnp.maximum(l1, l2), l3)
        w1, w2, w3 = jnp.exp(l1 - m), jnp.exp(l2 - m), jnp.exp(l3 - m)
        num = w1 * o1_ref[t, 0, part, lanes].astype(F32) + w2 * slab_ref[sub, 0, s] + w3 * slab_ref[sub, 2, s]
        slab_ref[sub, 0, s] = num / (w1 + w2 + w3)

    def merge_finish(sub):
        attn = in_order(sub, 0)
        ms = jnp.mean(attn * attn, axis=-1, keepdims=True)
        mix_ref[sub, :, 0:ATTN_WIDTH] = (attn * lax.rsqrt(ms + EPS) * ga_ref[...]).astype(BF16)
        mix_ref[sub, :, ATTN_WIDTH:] = in_order(sub, 4).astype(BF16)

    def prologue(sub):
        return ([lambda: regroup(o2_ref, step, sub, 0), lambda: regroup(l2_ref, step, sub, 1),
                 lambda: regroup(o3_ref, CHUNK, sub, 2), lambda: regroup(l3_ref, CHUNK, sub, 3),
                 lambda: regroup(ssm_ref, CHUNK, sub, 4)]
                + [lambda s=s: merge_slab(sub, s) for s in range(n_slabs)] + [lambda: merge_finish(sub)])

    def main(sub):
        part = slice(sub * rows, (sub + 1) * rows)
        st = {}

        def out_proj():
            x1 = x_ref[part, :] + jnp.dot(mix_ref[sub], wout_ref[...], preferred_element_type=F32)
            ms2 = jnp.mean(x1 * x1, axis=-1, keepdims=True)
            st["xn"] = (x1 * lax.rsqrt(ms2 + EPS) * g2_ref[...]).astype(BF16)
            st["acc"] = x1

        def up(c):
            st[c] = jnp.dot(st["xn"], wup_ref[:, c * FF_CHUNK:(c + 1) * FF_CHUNK], preferred_element_type=F32)

        def down(c):
            h = jnp.square(jnp.maximum(st.pop(c), 0.0)).astype(BF16)
            st["acc"] = st["acc"] + jnp.dot(h, wdn_ref[c * FF_CHUNK:(c + 1) * FF_CHUNK, :],
                                            preferred_element_type=F32)

        def finish():
            out_ref[part, :] = st["acc"]

        stages = [out_proj, lambda: up(0)]
        for c in range(n_ff):
            if c + 1 < n_ff:
                stages.append(lambda c=c: up(c + 1))
            stages.append(lambda c=c: down(c))
        return stages + [finish]

    tail = 3
    program = prologue(0)
    all_stages = [main(sub) for sub in range(n_sub)]
    for sub in range(n_sub):
        stages = all_stages[sub]
        head = stages if sub == 0 else stages[1:]
        if sub + 1 < n_sub:
            side = prologue(sub + 1)
            body = []
            for stage in head[:-tail]:
                body.append(stage)
                if side:
                    body.append(side.pop(0))
            program += body + side + [all_stages[sub + 1][0]] + head[-tail:]
        else:
            program += head
    for piece in program:
        piece()


def _mlp(x, os_, ls_, ssm, ga, w_out, g2, w_up, w_dn):
    s = x.shape[0]
    rows = MLP_ROWS
    xt = pl.BlockSpec((rows, D_MODEL), lambda i: (i, 0))
    ht = [pl.BlockSpec((N_HEAD_TILES, d, rows // d, LANE_TILE), lambda i: (0, 0, i, 0)) for d in DILATIONS]
    return pl.pallas_call(
        _mlp_kernel,
        grid=(s // rows,),
        in_specs=[xt, *ht, *ht, pl.BlockSpec((rows // CHUNK, CHUNK * SSM_WIDTH), lambda i: (i, 0)),
                  _const_spec(ga.shape), _const_spec(w_out.shape), _const_spec(g2.shape),
                  _const_spec(w_up.shape), _const_spec(w_dn.shape)],
        out_specs=xt,
        out_shape=jax.ShapeDtypeStruct((s, D_MODEL), F32),
        scratch_shapes=[pltpu.VMEM((rows // PROJ_SUB, 5, ATTN_WIDTH // SLAB, PROJ_SUB, SLAB), F32),
                        pltpu.VMEM((rows // PROJ_SUB, 3, ATTN_WIDTH // SLAB, DILATIONS[1],
                                    PROJ_SUB // DILATIONS[1], SLAB), F32),
                        pltpu.VMEM((rows // PROJ_SUB, PROJ_SUB, D_MODEL), BF16)],
        compiler_params=_params(dimension_semantics=("arbitrary",)),
        name="mlp",
    )(x, *os_, *ls_, ssm, ga, w_out, g2, w_up, w_dn)


def _layer(x, norm1_g, w_in, q_norm_g, k_norm_g, ssm_a_re, ssm_a_im, ssm_log_dt, ssm_b_re, ssm_b_im,
           ssm_c_re, ssm_c_im, ssm_d, glu_w, glu_b, attn_out_norm_g, ssm_out_norm_g, w_out, norm2_g,
           w_mlp_up, w_mlp_down):
    row = lambda g: g.astype(F32).reshape(1, -1)
    heads = ATTN_WIDTH // HEAD_DIM
    head_id = jnp.arange(LANE_TILE) // HEAD_DIM
    hmat = jnp.where(head_id[:, None] == head_id[None, :], 1.0 / HEAD_DIM, 0.0).astype(BF16)
    *qkv, u = _proj(x, row(norm1_g), w_in.astype(F32), hmat,
                    row(jnp.tile(q_norm_g, heads)), row(jnp.tile(k_norm_g, heads)))
    outs, lses = [], []
    for i in range(len(DILATIONS)):
        o, lse = _attn_pattern(*qkv[3 * i:3 * i + 3])
        outs.append(o)
        lses.append(lse)
    w = _s5_weights(ssm_a_re, ssm_a_im, ssm_log_dt, ssm_b_re, ssm_b_im, ssm_c_re, ssm_c_im, ssm_d)
    ssm = _s5(u, w, glu_w.astype(BF16), row(glu_b), row(ssm_out_norm_g))
    return _mlp(x, outs, lses, ssm, row(attn_out_norm_g), w_out.astype(BF16), row(norm2_g),
                w_mlp_up.astype(BF16), w_mlp_down.astype(BF16))


def kernel(x, norm1_g, w_in, q_norm_g, k_norm_g, ssm_a_re, ssm_a_im, ssm_log_dt, ssm_b_re, ssm_b_im,
           ssm_c_re, ssm_c_im, ssm_d, glu_w, glu_b, attn_out_norm_g, ssm_out_norm_g, w_out, norm2_g,
           w_mlp_up, w_mlp_down):
    params = (norm1_g, w_in, q_norm_g, k_norm_g, ssm_a_re, ssm_a_im, ssm_log_dt, ssm_b_re, ssm_b_im,
              ssm_c_re, ssm_c_im, ssm_d, glu_w, glu_b, attn_out_norm_g, ssm_out_norm_g, w_out, norm2_g,
              w_mlp_up, w_mlp_down)
    batch = x.shape[0]
    outs = []
    for bi in range(batch):
        h = x[bi]
        for layer in range(norm1_g.shape[0]):
            h = _layer(h, *[p[layer] for p in params])
        outs.append(h)
    return jnp.stack(outs)
```

```python
import functools
import math

import jax
import jax.numpy as jnp
from jax import lax
from jax.experimental import pallas as pl
from jax.experimental.pallas import tpu as pltpu

F32 = jnp.float32
BF16 = jnp.bfloat16

D_MODEL = 1024
ATTN_WIDTH = 512
HEAD_DIM = 64
SSM_WIDTH = 512
SSM_GROUP = 16
SSM_GROUPS = 32
SSM_STATE = 64
D_FF = 4096
EPS = 1e-6
NEG_INF = -1e30
DILATIONS = (1, 4, 16)
BLOCK = 128

SLAB = 128
LANE_TILE = 256
HEADS_PER_TILE = LANE_TILE // HEAD_DIM
N_HEAD_TILES = ATTN_WIDTH // LANE_TILE
CHUNK = 16
SUB = 4
SCAN_GROUP = 8
HALF_STATE = (SSM_GROUPS // 2) * SSM_STATE
STATE_COLS = SSM_GROUPS * SSM_STATE
VMEM_LIMIT = 56 * 1024 * 1024

LOG2E = math.log2(math.e)
Q_SCALE = HEAD_DIM ** -0.5 * LOG2E

PROJ_ROWS = 1024
PROJ_SUB = 256
SLAB_BUFFERS = 2
ATTN_ROWS = 4096
S5_ROWS = 256
MLP_ROWS = 512
FF_CHUNK = 512


def _const_spec(shape):
    nd = len(shape)
    return pl.BlockSpec(shape, lambda *_: (0,) * nd, pipeline_mode=pl.Buffered(1))


def _params(**kw):
    return pltpu.CompilerParams(vmem_limit_bytes=VMEM_LIMIT, **kw)


def _proj_kernel(x_ref, g1_ref, w_ref, hmat_ref, gq_ref, gk_ref,
                 q1_ref, k1_ref, v1_ref, q4_ref, k4_ref, v4_ref, q16_ref, k16_ref, v16_ref, u_ref,
                 slab_ref, slab4_ref, wb_ref):
    @pl.when(pl.program_id(0) == 0)
    def _():
        wb_ref[...] = w_ref[...].astype(BF16)

    outs = ((q1_ref, q4_ref, q16_ref), (k1_ref, k4_ref, k16_ref), (v1_ref, v4_ref, v16_ref))
    slabs_per_tensor = ATTN_WIDTH // SLAB
    step = DILATIONS[1]
    rows = PROJ_SUB
    n_sub = x_ref.shape[0] // rows
    n_chunks = w_ref.shape[1] // LANE_TILE
    chunks_per_tensor = ATTN_WIDTH // LANE_TILE
    gains = (gq_ref, gk_ref)
    xn = [None] * n_sub

    def prologue(sub):
        x = x_ref[sub * rows:(sub + 1) * rows, :]
        ms = jnp.mean(x * x, axis=-1, keepdims=True)
        xn[sub] = (x * lax.rsqrt(ms + EPS) * g1_ref[...]).astype(BF16)

    def matmul(sub, c):
        return jnp.dot(xn[sub], wb_ref[:, c * LANE_TILE:(c + 1) * LANE_TILE], preferred_element_type=F32)

    def epilogue(sub, c, val):
        ti, t = divmod(c, chunks_per_tensor)
        lane_tile = slice(t * LANE_TILE, (t + 1) * LANE_TILE)
        if ti < 2:
            ms_h = jnp.dot((val * val).astype(BF16), hmat_ref[...], preferred_element_type=F32)
            val = val * lax.rsqrt(ms_h + EPS) * gains[ti][:, lane_tile]
            if ti == 0:
                val = val * Q_SCALE
        if ti < 3:
            outs[ti][0][t, 0, sub * rows:(sub + 1) * rows, :] = val.astype(BF16)
        r4_rows = slice(sub * (rows // step), (sub + 1) * (rows // step))
        r16_rows = slice(sub * (rows // CHUNK), (sub + 1) * (rows // CHUNK))
        for half in range(LANE_TILE // SLAB):
            s = t * (LANE_TILE // SLAB) + half
            lanes = slice(half * SLAB, (half + 1) * SLAB)
            slab = ti * slabs_per_tensor + s
            buf = sub % SLAB_BUFFERS
            slab_ref[buf, slab] = val[:, lanes]
            for r4 in range(step):
                part = slab_ref[buf, slab, pl.ds(r4, rows // step, stride=step), :]
                if ti < 3:
                    outs[ti][1][t, r4, r4_rows, lanes] = part.astype(BF16)
                slab4_ref[buf, slab, r4] = part
            for r4 in range(step):
                for j4 in range(step):
                    r16 = step * j4 + r4
                    part = slab4_ref[buf, slab, r4, pl.ds(j4, rows // CHUNK, stride=step), :].astype(BF16)
                    if ti < 3:
                        outs[ti][2][t, r16, r16_rows, lanes] = part
                    else:
                        u_ref[r16_rows, r16 * SSM_WIDTH + s * SLAB:r16 * SSM_WIDTH + (s + 1) * SLAB] = part

    work = [(sub, c) for sub in range(n_sub) for c in range(n_chunks)]
    prologue(0)
    pending = None
    for sub, c in work:
        val = matmul(sub, c)
        if pending is not None:
            epilogue(*pending)
        pending = (sub, c, val)
        if c == n_chunks // 2 and sub + 1 < n_sub:
            prologue(sub + 1)
    epilogue(*pending)


def _slab_home(s):
    t, half = divmod(s, LANE_TILE // SLAB)
    return t, slice(half * SLAB, (half + 1) * SLAB)


def _proj(x, g1, w_in, hmat, gq, gk):
    s = x.shape[0]
    rows = PROJ_ROWS
    out_specs, out_shape = [], []
    for d in DILATIONS:
        for _ in range(3):
            out_specs.append(pl.BlockSpec((N_HEAD_TILES, d, rows // d, LANE_TILE), lambda i: (0, 0, i, 0)))
            out_shape.append(jax.ShapeDtypeStruct((N_HEAD_TILES, d, s // d, LANE_TILE), BF16))
    out_specs.append(pl.BlockSpec((rows // CHUNK, CHUNK * SSM_WIDTH), lambda i: (i, 0)))
    out_shape.append(jax.ShapeDtypeStruct((s // CHUNK, CHUNK * SSM_WIDTH), BF16))
    return pl.pallas_call(
        _proj_kernel,
        grid=(s // rows,),
        in_specs=[
            pl.BlockSpec((rows, D_MODEL), lambda i: (i, 0)),
            _const_spec((1, D_MODEL)),
            _const_spec(w_in.shape),
            _const_spec(hmat.shape),
            _const_spec((1, ATTN_WIDTH)),
            _const_spec((1, ATTN_WIDTH)),
        ],
        out_specs=out_specs,
        out_shape=out_shape,
        scratch_shapes=[pltpu.VMEM((SLAB_BUFFERS, 4 * ATTN_WIDTH // SLAB, PROJ_SUB, SLAB), F32),
                        pltpu.VMEM((SLAB_BUFFERS, 4 * ATTN_WIDTH // SLAB, DILATIONS[1],
                                    PROJ_SUB // DILATIONS[1], SLAB), F32),
                        pltpu.VMEM(w_in.shape, BF16)],
        compiler_params=_params(dimension_semantics=("arbitrary",)),
        name="proj",
    )(x, g1, w_in, hmat, gq, gk)


def _attn_kernel(q_ref, kc_ref, kp_ref, vc_ref, vp_ref, *rest):
    n_cast = (len(rest) - 4) // 2
    cast_in, (o_ref, lse_ref), cast_out = rest[:n_cast], rest[n_cast:n_cast + 2], rest[n_cast + 2:2 * n_cast + 2]
    kbuf, vbuf = rest[2 * n_cast + 2:]
    for src, dst in zip(cast_in, cast_out):
        dst[...] = src[...].astype(BF16)
    n_res, rows = q_ref.shape[0], q_ref.shape[1]
    kbuf[:, 0:BLOCK, :] = kp_ref[...]
    kbuf[:, BLOCK:, :] = kc_ref[...]
    vbuf[:, 0:BLOCK, :] = vp_ref[...]
    vbuf[:, BLOCK:, :] = vc_ref[...]

    lane = lax.broadcasted_iota(jnp.int32, (BLOCK, LANE_TILE), 1)
    head_masks = [(lane >= h * HEAD_DIM) & (lane < (h + 1) * HEAD_DIM) for h in range(HEADS_PER_TILE)]
    qi = lax.broadcasted_iota(jnp.int32, (BLOCK, 2 * BLOCK), 0)
    ki = lax.broadcasted_iota(jnp.int32, (BLOCK, 2 * BLOCK), 1)
    band = (ki >= qi) & (ki <= qi + BLOCK)
    bias_band = jnp.where(band, 0.0, NEG_INF).astype(F32)
    no_prev = pl.program_id(2) == 0
    bias_first = jnp.where(band & ((ki >= BLOCK) | jnp.logical_not(no_prev)), 0.0, NEG_INF).astype(F32)

    for r, b in [(r, b) for r in range(n_res) for b in range(rows // BLOCK)]:
        qb = q_ref[r, b * BLOCK:(b + 1) * BLOCK, :]
        zero = jnp.zeros_like(qb)
        q_stack = jnp.concatenate([jnp.where(mk, qb, zero) for mk in head_masks], axis=0)
        kw = kbuf[r, b * BLOCK:(b + 2) * BLOCK, :]
        vw = vbuf[r, b * BLOCK:(b + 2) * BLOCK, :]
        s = lax.dot_general(q_stack, kw, (((1,), (1,)), ((), ())), preferred_element_type=F32)
        bias = bias_first if b == 0 else bias_band
        ps, ms, ls = [], [], []
        for h in range(HEADS_PER_TILE):
            sh = s[h * BLOCK:(h + 1) * BLOCK, :] + bias
            m = jnp.max(sh, axis=-1, keepdims=True)
            p = jnp.exp2(sh - m)
            ls.append(jnp.sum(p, axis=-1, keepdims=True))
            ms.append(m)
            ps.append(p.astype(BF16))
        pv = jnp.dot(jnp.concatenate(ps, axis=0), vw, preferred_element_type=F32)
        o = jnp.zeros((BLOCK, LANE_TILE), F32)
        lse = jnp.zeros((BLOCK, LANE_TILE), F32)
        for h in range(HEADS_PER_TILE):
            o = jnp.where(head_masks[h], pv[h * BLOCK:(h + 1) * BLOCK, :] / ls[h], o)
            lse = jnp.where(head_masks[h], (ms[h] + jnp.log2(ls[h])) * (1.0 / LOG2E), lse)
        o_ref[r, b * BLOCK:(b + 1) * BLOCK, :] = o.astype(BF16)
        lse_ref[r, b * BLOCK:(b + 1) * BLOCK, :] = lse


def _attn_pattern(q, k, v, cast=()):
    nt, dilation, n, _ = q.shape
    rows = min(ATTN_ROWS, n)
    n_res = min(ATTN_ROWS // rows, dilation)
    blocks_per_tile = rows // BLOCK
    grid = (nt, dilation // n_res, n // rows)
    n_steps = grid[0] * grid[1] * grid[2]
    cur = pl.BlockSpec((None, n_res, rows, LANE_TILE), lambda t, r, j: (t, r, j, 0))
    prev = pl.BlockSpec((None, n_res, BLOCK, LANE_TILE),
                        lambda t, r, j: (t, r, jnp.maximum(j * blocks_per_tile - 1, 0), 0))
    step_id = lambda t, r, j: ((t * grid[1] + r) * grid[2] + j, 0)
    cast_specs = [pl.BlockSpec((w.shape[0] // n_steps, w.shape[1]), step_id) for w in cast]
    outs = pl.pallas_call(
        _attn_kernel,
        grid=grid,
        in_specs=[cur, cur, prev, cur, prev] + cast_specs,
        out_specs=[cur, cur] + cast_specs,
        out_shape=[jax.ShapeDtypeStruct(q.shape, BF16), jax.ShapeDtypeStruct(q.shape, F32)]
        + [jax.ShapeDtypeStruct(w.shape, BF16) for w in cast],
        scratch_shapes=[pltpu.VMEM((n_res, rows + BLOCK, LANE_TILE), BF16),
                        pltpu.VMEM((n_res, rows + BLOCK, LANE_TILE), BF16)],
        compiler_params=_params(dimension_semantics=("arbitrary", "arbitrary", "arbitrary")),
        name=f"attn_d{dilation}",
    )(q, k, k, v, v, *cast)
    return outs[0], outs[1], outs[2:]


def _s5_weights(a_re, a_im, log_dt, b_re, b_im, c_re, c_im, d_skip):
    g, p, c = SSM_GROUPS, SSM_STATE, SSM_GROUP
    gl = g // 2
    lr, li = a_re.astype(F32), a_im.astype(F32)
    dt = jnp.exp(log_dt.astype(F32))[:, None]

    def apow(j):
        mag = jnp.exp(lr * dt * j)
        return mag * jnp.cos(li * dt * j), mag * jnp.sin(li * dt * j)

    ab_r, ab_i = apow(1.0)
    den = lr * lr + li * li
    nr, ni = ab_r - 1.0, ab_i
    cr = (nr * lr + ni * li) / den
    ci = (ni * lr - nr * li) / den
    br, bi = b_re.astype(F32), b_im.astype(F32)
    bb_r = cr[..., None] * br - ci[..., None] * bi
    bb_i = cr[..., None] * bi + ci[..., None] * br
    cre, cim = c_re.astype(F32), c_im.astype(F32)

    def apow_many(js):
        j = jnp.asarray(js, F32)[:, None, None]
        mag = jnp.exp(lr * dt * j)
        return mag * jnp.cos(li * dt * j), mag * jnp.sin(li * dt * j)

    pr, pi = apow_many(range(SUB + 1))
    wp_r, wp_i = pr[SUB - 1::-1], pi[SUB - 1::-1]
    win_r = wp_r[..., None] * bb_r[None] - wp_i[..., None] * bb_i[None]
    win_i = wp_r[..., None] * bb_i[None] + wp_i[..., None] * bb_r[None]
    pad_lanes = lambda w: jnp.pad(w, ((0, 0), (0, 0), (0, 0), (0, SLAB - w.shape[-1])))
    to_in = lambda w: pad_lanes(w.transpose(0, 1, 3, 2).reshape(SUB, 2, gl * c, p))
    out_r = cre[None] * pr[:, :, None, :] - cim[None] * pi[:, :, None, :]
    out_i = -(cre[None] * pi[:, :, None, :] + cim[None] * pr[:, :, None, :])
    to_out = lambda w: pad_lanes(w.transpose(0, 1, 3, 2).reshape(SUB + 1, 2, gl * p, c))

    flat = lambda re, im: jnp.stack([re.reshape(-1), im.reshape(-1)])
    top_r, top_i = apow_many([CHUNK])
    lam = flat(top_r[0], top_i[0])
    levels = [lam]
    for _ in range(SCAN_GROUP - 1):
        lr_, li_ = levels[-1][0], levels[-1][1]
        levels.append(jnp.stack([lr_ * lam[0] - li_ * lam[1], lr_ * lam[1] + li_ * lam[0]]))
    m_t, w_r, w_i, e_r, e_i = _s5_expand(to_in(win_r), to_in(win_i), to_out(out_r), to_out(out_i),
                                         d_skip.astype(F32).reshape(2, 1, gl * c))
    return dict(m_t=m_t, w_r=w_r, w_i=w_i, e_r=e_r, e_i=e_i,
                a_sub=flat(pr[SUB], pi[SUB]), levels=jnp.stack(levels, axis=1))


def _s5_expand_kernel(wr_ref, wi_ref, or_ref, oi_ref, d_ref, mt_ref, wfr_ref, wfi_ref, efr_ref, efi_ref):
    n_in, n_out = wr_ref.shape[2], or_ref.shape[2]
    p, c = SSM_STATE, SSM_GROUP

    def rep(inner, total):
        r = lax.broadcasted_iota(jnp.int32, (SLAB, total), 0)
        col = lax.broadcasted_iota(jnp.int32, (SLAB, total), 1)
        return jnp.where((col & (inner - 1)) == r, 1.0, 0.0).astype(BF16)

    def group_mask(n_rows, rows_per_group, n_cols, cols_per_group):
        r = lax.broadcasted_iota(jnp.int32, (n_rows, n_cols), 0) >> int(math.log2(rows_per_group))
        col = lax.broadcasted_iota(jnp.int32, (n_rows, n_cols), 1) >> int(math.log2(cols_per_group))
        return r == col

    rep_in, rep_out = rep(p, n_out), rep(c, n_in)
    mask_in = group_mask(n_in, c, n_out, p)
    mask_out = group_mask(n_out, p, n_in, c)

    def split(x):
        hi = x.astype(BF16)
        return hi, (x - hi.astype(F32)).astype(BF16)

    def expand(x_b, rep_m, mask):
        full = jnp.dot(x_b, rep_m, preferred_element_type=F32)
        return jnp.where(mask, full, 0.0).astype(BF16)

    diag = (lax.broadcasted_iota(jnp.int32, (n_in, n_in), 0) == lax.broadcasted_iota(jnp.int32, (n_in, n_in), 1))
    for h in range(2):
        in0 = []
        for b in range(SUB):
            for src, dst in ((wr_ref, wfr_ref), (wi_ref, wfi_ref)):
                hi, lo = split(src[b, h])
                full = expand(hi, rep_in, mask_in)
                dst[b, h] = full
                if b == SUB - 1:
                    in0.append((full, expand(lo, rep_in, mask_in)))
        for j in range(SUB + 1):
            lag = None
            for (src, dst), (in_hi, in_lo) in zip(((or_ref, efr_ref), (oi_ref, efi_ref)), in0):
                hi, lo = split(src[j, h])
                out_hi = expand(hi, rep_out, mask_out)
                if j > 0:
                    dst[j - 1, h] = out_hi
                if j < SUB:
                    out_lo = expand(lo, rep_out, mask_out)
                    part = (jnp.dot(in_hi, out_hi, preferred_element_type=F32)
                            + jnp.dot(in_hi, out_lo, preferred_element_type=F32)
                            + jnp.dot(in_lo, out_hi, preferred_element_type=F32))
                    lag = part if lag is None else lag + part
            if j == 0:
                lag = lag + jnp.where(diag, d_ref[h], 0.0)
            if j < SUB:
                mt_ref[j, h] = lag.astype(BF16)


def _s5_expand(wc_r, wc_i, oc_r, oc_i, d_row):
    n_in, n_out = wc_r.shape[2], oc_r.shape[2]
    shapes = [jax.ShapeDtypeStruct((SUB, 2, n_in, n_in), BF16),
              jax.ShapeDtypeStruct((SUB, 2, n_in, n_out), BF16), jax.ShapeDtypeStruct((SUB, 2, n_in, n_out), BF16),
              jax.ShapeDtypeStruct((SUB, 2, n_out, n_in), BF16), jax.ShapeDtypeStruct((SUB, 2, n_out, n_in), BF16)]
    return pl.pallas_call(
        _s5_expand_kernel,
        out_shape=shapes,
        compiler_params=_params(),
        name="s5_expand",
    )(wc_r, wc_i, oc_r, oc_i, d_row)


def _cmul(ar, ai, br, bi):
    return ar * br - ai * bi, ar * bi + ai * br


def _s5_kernel(u_ref, mt_ref, wr_ref, wi_ref, er_ref, ei_ref, asub_ref, lvl_ref,
               gluw_ref, glub_ref, gout_ref, out_ref, carry_ref, zr_ref, zi_ref, locr_ref, loci_ref):
    rows = u_ref.shape[0]
    width = SSM_WIDTH

    @pl.when(pl.program_id(0) == 0)
    def _():
        carry_ref[...] = jnp.zeros_like(carry_ref)

    def u_tile(t, h):
        return u_ref[:, t * width + h * LANE_TILE:t * width + (h + 1) * LANE_TILE]

    a_r, a_i = asub_ref[0:1, :], asub_ref[1:2, :]

    for a in range(SUB):
        for h in range(2):
            cols = slice(h * HALF_STATE, (h + 1) * HALF_STATE)
            pr = pi = None
            for b in range(SUB):
                ut = u_tile(a * SUB + b, h)
                dr = jnp.dot(ut, wr_ref[b, h], preferred_element_type=F32)
                di = jnp.dot(ut, wi_ref[b, h], preferred_element_type=F32)
                pr = dr if pr is None else pr + dr
                pi = di if pi is None else pi + di
            if a == 0:
                zr_ref[:, cols] = pr
                zi_ref[:, cols] = pi
            else:
                hr, hi = _cmul(zr_ref[:, cols], zi_ref[:, cols], a_r[:, cols], a_i[:, cols])
                pr, pi = hr + pr, hi + pi
                zr_ref[:, cols] = pr
                zi_ref[:, cols] = pi
            if a < SUB - 1:
                locr_ref[a, :, cols] = pr.astype(BF16)
                loci_ref[a, :, cols] = pi.astype(BF16)

    row = lax.broadcasted_iota(jnp.int32, (rows, 1), 0)
    in_group = row & (SCAN_GROUP - 1)
    sr, si = zr_ref[...], zi_ref[...]
    sh = 1
    while sh < SCAN_GROUP:
        keep = in_group >= sh
        pr_, pi_ = lvl_ref[0, sh - 1:sh, :], lvl_ref[1, sh - 1:sh, :]
        tr = jnp.where(keep, pltpu.roll(sr, sh, axis=0), 0.0)
        ti = jnp.where(keep, pltpu.roll(si, sh, axis=0), 0.0)
        mr, mi = _cmul(tr, ti, pr_, pi_)
        sr, si = sr + mr, si + mi
        sh *= 2
    c_r, c_i = carry_ref[0:1, :], carry_ref[1:2, :]
    pw_r, pw_i = lvl_ref[0], lvl_ref[1]
    last_r, last_i = c_r, c_i
    for g in range(rows // SCAN_GROUP):
        grp = slice(g * SCAN_GROUP, (g + 1) * SCAN_GROUP)
        ar, ai = _cmul(jnp.broadcast_to(last_r, pw_r.shape), jnp.broadcast_to(last_i, pw_i.shape), pw_r, pw_i)
        gr, gi = sr[grp, :] + ar, si[grp, :] + ai
        zr_ref[grp, :] = gr
        zi_ref[grp, :] = gi
        last_r, last_i = gr[SCAN_GROUP - 1:SCAN_GROUP, :], gi[SCAN_GROUP - 1:SCAN_GROUP, :]
    carry_ref[0:1, :] = last_r
    carry_ref[1:2, :] = last_i
    xr = jnp.where(row == 0, c_r, pltpu.roll(zr_ref[...], 1, axis=0))
    xi = jnp.where(row == 0, c_i, pltpu.roll(zi_ref[...], 1, axis=0))
    zr_ref[...] = xr
    zi_ref[...] = xi

    for a in range(SUB):
        if a > 0:
            nr, ni = _cmul(zr_ref[...], zi_ref[...], a_r, a_i)
            zr_ref[...] = nr
            zi_ref[...] = ni
        if a == 0:
            xr_b = zr_ref[...].astype(BF16)
            xi_b = zi_ref[...].astype(BF16)
        else:
            xr_b = (zr_ref[...] + locr_ref[a - 1].astype(F32)).astype(BF16)
            xi_b = (zi_ref[...] + loci_ref[a - 1].astype(F32)).astype(BF16)
        for b in range(SUB):
            t = a * SUB + b
            halves = []
            for h in range(2):
                cols = slice(h * HALF_STATE, (h + 1) * HALF_STATE)
                acc = jnp.dot(xr_b[:, cols], er_ref[b, h], preferred_element_type=F32)
                acc = acc + jnp.dot(xi_b[:, cols], ei_ref[b, h], preferred_element_type=F32)
                for b_in in range(b + 1):
                    acc = acc + jnp.dot(u_tile(a * SUB + b_in, h), mt_ref[b - b_in, h],
                                        preferred_element_type=F32)
                halves.append(acc)
            y = jnp.concatenate(halves, axis=1)
            z = 0.5 * y * (1.0 + jnp.tanh(math.sqrt(2.0 / math.pi) * (y + 0.044715 * (y * y * y))))
            gate = jnp.dot(z.astype(BF16), gluw_ref[...], preferred_element_type=F32) + glub_ref[...]
            o = z * (1.0 / (1.0 + jnp.exp(-gate)))
            ms = jnp.mean(o * o, axis=-1, keepdims=True)
            out_ref[:, t * width:(t + 1) * width] = (o * lax.rsqrt(ms + EPS) * gout_ref[...]).astype(BF16)


def _s5(uv, w, glu_w, glu_b, g_out):
    n = uv.shape[0]
    rows = S5_ROWS
    tile = pl.BlockSpec((rows, CHUNK * SSM_WIDTH), lambda i: (i, 0))
    return pl.pallas_call(
        _s5_kernel,
        grid=(n // rows,),
        in_specs=[tile] + [_const_spec(a.shape) for a in
                           (w["m_t"], w["w_r"], w["w_i"], w["e_r"], w["e_i"], w["a_sub"], w["levels"],
                            glu_w, glu_b, g_out)],
        out_specs=tile,
        out_shape=jax.ShapeDtypeStruct((n, CHUNK * SSM_WIDTH), BF16),
        scratch_shapes=[pltpu.VMEM((2, STATE_COLS), F32),
                        pltpu.VMEM((rows, STATE_COLS), F32),
                        pltpu.VMEM((rows, STATE_COLS), F32),
                        pltpu.VMEM((SUB - 1, rows, STATE_COLS), BF16),
                        pltpu.VMEM((SUB - 1, rows, STATE_COLS), BF16)],
        compiler_params=_params(dimension_semantics=("arbitrary",)),
        name="s5",
    )(uv, w["m_t"], w["w_r"], w["w_i"], w["e_r"], w["e_i"], w["a_sub"], w["levels"], glu_w, glu_b, g_out)


def _mlp_kernel(x_ref, o1_ref, o2_ref, o3_ref, l1_ref, l2_ref, l3_ref, ssm_ref, ga_ref, wout_ref,
                g2_ref, wup_ref, wdn_ref, out_ref, slab_ref, slab4_ref, mix_ref):
    rows = PROJ_SUB
    n_sub = x_ref.shape[0] // rows
    n_slabs = ATTN_WIDTH // SLAB
    step = DILATIONS[1]
    n_ff = D_FF // FF_CHUNK

    def regroup(ref, d, sub, buf):
        part = slice(sub * (rows // d), (sub + 1) * (rows // d))
        for s in range(n_slabs):
            t, lanes = _slab_home(s)
            if ref is ssm_ref:
                piece = lambda r: ref[part, r * SSM_WIDTH + s * SLAB:r * SSM_WIDTH + (s + 1) * SLAB]
            else:
                piece = lambda r: ref[t, r, part, lanes]
            if d == step:
                for r in range(d):
                    slab_ref[sub, buf, s, pl.ds(r, rows // d, stride=d), :] = piece(r).astype(F32)
            else:
                for r4 in range(step):
                    for j4 in range(step):
                        slab4_ref[sub, buf - 2, s, r4, pl.ds(j4, rows // d, stride=step), :] = (
                            piece(step * j4 + r4).astype(F32))
                for r4 in range(step):
                    slab_ref[sub, buf, s, pl.ds(r4, rows // step, stride=step), :] = slab4_ref[sub, buf - 2, s, r4]

    def in_order(sub, buf):
        return jnp.concatenate([slab_ref[sub, buf, s] for s in range(n_slabs)], axis=1)

    def merge_slab(sub, s):
        t, lanes = _slab_home(s)
        part = slice(sub * rows, (sub + 1) * rows)
        l1, l2, l3 = l1_ref[t, 0, part, lanes], slab_ref[sub, 1, s], slab_ref[sub, 3, s]
        m = jnp.maximum(jnp.maximum(l1, l2), l3)
        w1, w2, w3 = jnp.exp(l1 - m), jnp.exp(l2 - m), jnp.exp(l3 - m)
        num = w1 * o1_ref[t, 0, part, lanes].astype(F32) + w2 * slab_ref[sub, 0, s] + w3 * slab_ref[sub, 2, s]
        slab_ref[sub, 0, s] = num / (w1 + w2 + w3)

    def merge_finish(sub):
        attn = in_order(sub, 0)
        ms = jnp.mean(attn * attn, axis=-1, keepdims=True)
        mix_ref[sub, :, 0:ATTN_WIDTH] = (attn * lax.rsqrt(ms + EPS) * ga_ref[...]).astype(BF16)
        mix_ref[sub, :, ATTN_WIDTH:] = in_order(sub, 4).astype(BF16)

    def prologue(sub):
        return ([lambda: regroup(o2_ref, step, sub, 0), lambda: regroup(l2_ref, step, sub, 1),
                 lambda: regroup(o3_ref, CHUNK, sub, 2), lambda: regroup(l3_ref, CHUNK, sub, 3),
                 lambda: regroup(ssm_ref, CHUNK, sub, 4)]
                + [lambda s=s: merge_slab(sub, s) for s in range(n_slabs)] + [lambda: merge_finish(sub)])

    def main(sub):
        part = slice(sub * rows, (sub + 1) * rows)
        st = {}

        def out_proj():
            x1 = x_ref[part, :] + jnp.dot(mix_ref[sub], wout_ref[...], preferred_element_type=F32)
            ms2 = jnp.mean(x1 * x1, axis=-1, keepdims=True)
            st["xn"] = (x1 * lax.rsqrt(ms2 + EPS) * g2_ref[...]).astype(BF16)
            st["acc"] = x1

        def up(c):
            st[c] = jnp.dot(st["xn"], wup_ref[:, c * FF_CHUNK:(c + 1) * FF_CHUNK], preferred_element_type=F32)

        def down(c):
            h = jnp.square(jnp.maximum(st.pop(c), 0.0)).astype(BF16)
            st["acc"] = st["acc"] + jnp.dot(h, wdn_ref[c * FF_CHUNK:(c + 1) * FF_CHUNK, :],
                                            preferred_element_type=F32)

        def finish():
            out_ref[part, :] = st["acc"]

        stages = [out_proj, lambda: up(0)]
        for c in range(n_ff):
            if c + 1 < n_ff:
                stages.append(lambda c=c: up(c + 1))
            stages.append(lambda c=c: down(c))
        return stages + [finish]

    tail = 3
    program = prologue(0)
    all_stages = [main(sub) for sub in range(n_sub)]
    for sub in range(n_sub):
        stages = all_stages[sub]
        head = stages if sub == 0 else stages[1:]
        if sub + 1 < n_sub:
            side = prologue(sub + 1)
            body = []
            for stage in head[:-tail]:
                body.append(stage)
                if side:
                    body.append(side.pop(0))
            program += body + side + [all_stages[sub + 1][0]] + head[-tail:]
        else:
            program += head
    for piece in program:
        piece()


def _mlp(x, os_, ls_, ssm, ga, w_out, g2, w_up, w_dn):
    s = x.shape[0]
    rows = MLP_ROWS
    xt = pl.BlockSpec((rows, D_MODEL), lambda i: (i, 0))
    ht = [pl.BlockSpec((N_HEAD_TILES, d, rows // d, LANE_TILE), lambda i: (0, 0, i, 0)) for d in DILATIONS]
    return pl.pallas_call(
        _mlp_kernel,
        grid=(s // rows,),
        in_specs=[xt, *ht, *ht, pl.BlockSpec((rows // CHUNK, CHUNK * SSM_WIDTH), lambda i: (i, 0)),
                  _const_spec(ga.shape), _const_spec(w_out.shape), _const_spec(g2.shape),
                  _const_spec(w_up.shape), _const_spec(w_dn.shape)],
        out_specs=xt,
        out_shape=jax.ShapeDtypeStruct((s, D_MODEL), F32),
        scratch_shapes=[pltpu.VMEM((rows // PROJ_SUB, 5, ATTN_WIDTH // SLAB, PROJ_SUB, SLAB), F32),
                        pltpu.VMEM((rows // PROJ_SUB, 3, ATTN_WIDTH // SLAB, DILATIONS[1],
                                    PROJ_SUB // DILATIONS[1], SLAB), F32),
                        pltpu.VMEM((rows // PROJ_SUB, PROJ_SUB, D_MODEL), BF16)],
        compiler_params=_params(dimension_semantics=("arbitrary",)),
        name="mlp",
    )(x, *os_, *ls_, ssm, ga, w_out, g2, w_up, w_dn)


def _layer(x, norm1_g, w_in, q_norm_g, k_norm_g, ssm_a_re, ssm_a_im, ssm_log_dt, ssm_b_re, ssm_b_im,
           ssm_c_re, ssm_c_im, ssm_d, glu_w, glu_b, attn_out_norm_g, ssm_out_norm_g, w_out, norm2_g,
           w_mlp_up, w_mlp_down):
    row = lambda g: g.astype(F32).reshape(1, -1)
    heads = ATTN_WIDTH // HEAD_DIM
    head_id = jnp.arange(LANE_TILE) // HEAD_DIM
    hmat = jnp.where(head_id[:, None] == head_id[None, :], 1.0 / HEAD_DIM, 0.0).astype(BF16)
    *qkv, u = _proj(x, row(norm1_g), w_in.astype(F32), hmat,
                    row(jnp.tile(q_norm_g, heads)), row(jnp.tile(k_norm_g, heads)))
    outs, lses = [], []
    mlp_weights = (w_out.astype(F32), w_mlp_up.astype(F32), w_mlp_down.astype(F32))
    for i in range(len(DILATIONS)):
        o, lse, cast = _attn_pattern(*qkv[3 * i:3 * i + 3], cast=mlp_weights if i == 0 else ())
        outs.append(o)
        lses.append(lse)
        if i == 0:
            w_out_b, w_up_b, w_dn_b = cast
    w = _s5_weights(ssm_a_re, ssm_a_im, ssm_log_dt, ssm_b_re, ssm_b_im, ssm_c_re, ssm_c_im, ssm_d)
    ssm = _s5(u, w, glu_w.astype(BF16), row(glu_b), row(ssm_out_norm_g))
    return _mlp(x, outs, lses, ssm, row(attn_out_norm_g), w_out_b, row(norm2_g), w_up_b, w_dn_b)


def kernel(x, norm1_g, w_in, q_norm_g, k_norm_g, ssm_a_re, ssm_a_im, ssm_log_dt, ssm_b_re, ssm_b_im,
           ssm_c_re, ssm_c_im, ssm_d, glu_w, glu_b, attn_out_norm_g, ssm_out_norm_g, w_out, norm2_g,
           w_mlp_up, w_mlp_down):
    params = (norm1_g, w_in, q_norm_g, k_norm_g, ssm_a_re, ssm_a_im, ssm_log_dt, ssm_b_re, ssm_b_im,
              ssm_c_re, ssm_c_im, ssm_d, glu_w, glu_b, attn_out_norm_g, ssm_out_norm_g, w_out, norm2_g,
              w_mlp_up, w_mlp_down)
    batch = x.shape[0]
    outs = []
    for bi in range(batch):
        h = x[bi]
        for layer in range(norm1_g.shape[0]):
            h = _layer(h, *[p[layer] for p in params])
        outs.append(h)
    return jnp.stack(outs)
```

```python
import functools
import math

import jax
import jax.numpy as jnp
from jax import lax
from jax.experimental import pallas as pl
from jax.experimental.pallas import tpu as pltpu

F32 = jnp.float32
BF16 = jnp.bfloat16

D_MODEL = 1024
ATTN_WIDTH = 512
HEAD_DIM = 64
SSM_WIDTH = 512
SSM_GROUP = 16
SSM_GROUPS = 32
SSM_STATE = 64
D_FF = 4096
EPS = 1e-6
NEG_INF = -1e30
DILATIONS = (1, 4, 16)
BLOCK = 128

SLAB = 128
LANE_TILE = 256
HEADS_PER_TILE = LANE_TILE // HEAD_DIM
N_HEAD_TILES = ATTN_WIDTH // LANE_TILE
CHUNK = 16
SUB = 4
SCAN_GROUP = 8
HALF_STATE = (SSM_GROUPS // 2) * SSM_STATE
STATE_COLS = SSM_GROUPS * SSM_STATE
VMEM_LIMIT = 56 * 1024 * 1024

LOG2E = math.log2(math.e)
Q_SCALE = HEAD_DIM ** -0.5 * LOG2E

PROJ_ROWS = 1024
PROJ_SUB = 256
SLAB_BUFFERS = 2
PROJ_AHEAD = 2
ATTN_ROWS = 4096
S5_ROWS = 128
MLP_ROWS = 512
FF_CHUNK = 512


def _const_spec(shape):
    nd = len(shape)
    return pl.BlockSpec(shape, lambda *_: (0,) * nd, pipeline_mode=pl.Buffered(1))


def _params(**kw):
    return pltpu.CompilerParams(vmem_limit_bytes=VMEM_LIMIT, **kw)


def _proj_kernel(x_ref, g1_ref, w_ref, hmat_ref, gq_ref, gk_ref,
                 q1_ref, k1_ref, v1_ref, q4_ref, k4_ref, v4_ref, q16_ref, k16_ref, v16_ref, u_ref,
                 slab_ref, slab4_ref, wb_ref):
    @pl.when(pl.program_id(0) == 0)
    def _():
        wb_ref[...] = w_ref[...].astype(BF16)

    outs = ((q1_ref, q4_ref, q16_ref), (k1_ref, k4_ref, k16_ref), (v1_ref, v4_ref, v16_ref))
    slabs_per_tensor = ATTN_WIDTH // SLAB
    step = DILATIONS[1]
    rows = PROJ_SUB
    n_sub = x_ref.shape[0] // rows
    n_chunks = w_ref.shape[1] // LANE_TILE
    chunks_per_tensor = ATTN_WIDTH // LANE_TILE
    gains = (gq_ref, gk_ref)
    xn = [None] * n_sub

    def prologue(sub):
        x = x_ref[sub * rows:(sub + 1) * rows, :]
        ms = jnp.mean(x * x, axis=-1, keepdims=True)
        xn[sub] = (x * lax.rsqrt(ms + EPS) * g1_ref[...]).astype(BF16)

    def matmul(sub, c):
        return jnp.dot(xn[sub], wb_ref[:, c * LANE_TILE:(c + 1) * LANE_TILE], preferred_element_type=F32)

    def epilogue(sub, c, val):
        ti, t = divmod(c, chunks_per_tensor)
        lane_tile = slice(t * LANE_TILE, (t + 1) * LANE_TILE)
        if ti < 2:
            ms_h = jnp.dot((val * val).astype(BF16), hmat_ref[...], preferred_element_type=F32)
            val = val * lax.rsqrt(ms_h + EPS) * gains[ti][:, lane_tile]
            if ti == 0:
                val = val * Q_SCALE
        if ti < 3:
            outs[ti][0][t, 0, sub * rows:(sub + 1) * rows, :] = val.astype(BF16)
        r4_rows = slice(sub * (rows // step), (sub + 1) * (rows // step))
        r16_rows = slice(sub * (rows // CHUNK), (sub + 1) * (rows // CHUNK))
        for half in range(LANE_TILE // SLAB):
            s = t * (LANE_TILE // SLAB) + half
            lanes = slice(half * SLAB, (half + 1) * SLAB)
            slab = ti * slabs_per_tensor + s
            buf = sub % SLAB_BUFFERS
            slab_ref[buf, slab] = val[:, lanes]
            for r4 in range(step):
                part = slab_ref[buf, slab, pl.ds(r4, rows // step, stride=step), :]
                if ti < 3:
                    outs[ti][1][t, r4, r4_rows, lanes] = part.astype(BF16)
                slab4_ref[buf, slab, r4] = part
            for r4 in range(step):
                for j4 in range(step):
                    r16 = step * j4 + r4
                    part = slab4_ref[buf, slab, r4, pl.ds(j4, rows // CHUNK, stride=step), :].astype(BF16)
                    if ti < 3:
                        outs[ti][2][t, r16, r16_rows, lanes] = part
                    else:
                        u_ref[r16_rows, r16 * SSM_WIDTH + s * SLAB:r16 * SSM_WIDTH + (s + 1) * SLAB] = part

    work = [(sub, c) for sub in range(n_sub) for c in range(n_chunks)]
    prologue(0)
    pending = []
    for sub, c in work:
        pending.append((sub, c, matmul(sub, c)))
        if len(pending) > PROJ_AHEAD:
            epilogue(*pending.pop(0))
        if c == n_chunks // 2 and sub + 1 < n_sub:
            prologue(sub + 1)
    for item in pending:
        epilogue(*item)


def _slab_home(s):
    t, half = divmod(s, LANE_TILE // SLAB)
    return t, slice(half * SLAB, (half + 1) * SLAB)


def _proj(x, g1, w_in, hmat, gq, gk):
    s = x.shape[0]
    rows = PROJ_ROWS
    out_specs, out_shape = [], []
    for d in DILATIONS:
        for _ in range(3):
            out_specs.append(pl.BlockSpec((N_HEAD_TILES, d, rows // d, LANE_TILE), lambda i: (0, 0, i, 0)))
            out_shape.append(jax.ShapeDtypeStruct((N_HEAD_TILES, d, s // d, LANE_TILE), BF16))
    out_specs.append(pl.BlockSpec((rows // CHUNK, CHUNK * SSM_WIDTH), lambda i: (i, 0)))
    out_shape.append(jax.ShapeDtypeStruct((s // CHUNK, CHUNK * SSM_WIDTH), BF16))
    return pl.pallas_call(
        _proj_kernel,
        grid=(s // rows,),
        in_specs=[
            pl.BlockSpec((rows, D_MODEL), lambda i: (i, 0)),
            _const_spec((1, D_MODEL)),
            _const_spec(w_in.shape),
            _const_spec(hmat.shape),
            _const_spec((1, ATTN_WIDTH)),
            _const_spec((1, ATTN_WIDTH)),
        ],
        out_specs=out_specs,
        out_shape=out_shape,
        scratch_shapes=[pltpu.VMEM((SLAB_BUFFERS, 4 * ATTN_WIDTH // SLAB, PROJ_SUB, SLAB), F32),
                        pltpu.VMEM((SLAB_BUFFERS, 4 * ATTN_WIDTH // SLAB, DILATIONS[1],
                                    PROJ_SUB // DILATIONS[1], SLAB), F32),
                        pltpu.VMEM(w_in.shape, BF16)],
        compiler_params=_params(dimension_semantics=("arbitrary",)),
        name="proj",
    )(x, g1, w_in, hmat, gq, gk)


def _attn_kernel(q_ref, kc_ref, kp_ref, vc_ref, vp_ref, *rest):
    n_cast = (len(rest) - 4) // 2
    cast_in, (o_ref, lse_ref), cast_out = rest[:n_cast], rest[n_cast:n_cast + 2], rest[n_cast + 2:2 * n_cast + 2]
    kbuf, vbuf = rest[2 * n_cast + 2:]
    for src, dst in zip(cast_in, cast_out):
        dst[...] = src[...].astype(BF16)
    n_res, rows = q_ref.shape[0], q_ref.shape[1]
    kbuf[:, 0:BLOCK, :] = kp_ref[...]
    kbuf[:, BLOCK:, :] = kc_ref[...]
    vbuf[:, 0:BLOCK, :] = vp_ref[...]
    vbuf[:, BLOCK:, :] = vc_ref[...]

    lane = lax.broadcasted_iota(jnp.int32, (BLOCK, LANE_TILE), 1)
    head_masks = [(lane >= h * HEAD_DIM) & (lane < (h + 1) * HEAD_DIM) for h in range(HEADS_PER_TILE)]
    qi = lax.broadcasted_iota(jnp.int32, (BLOCK, 2 * BLOCK), 0)
    ki = lax.broadcasted_iota(jnp.int32, (BLOCK, 2 * BLOCK), 1)
    band = (ki >= qi) & (ki <= qi + BLOCK)
    bias_band = jnp.where(band, 0.0, NEG_INF).astype(F32)
    no_prev = pl.program_id(2) == 0
    bias_first = jnp.where(band & ((ki >= BLOCK) | jnp.logical_not(no_prev)), 0.0, NEG_INF).astype(F32)

    for r, b in [(r, b) for r in range(n_res) for b in range(rows // BLOCK)]:
        qb = q_ref[r, b * BLOCK:(b + 1) * BLOCK, :]
        zero = jnp.zeros_like(qb)
        q_stack = jnp.concatenate([jnp.where(mk, qb, zero) for mk in head_masks], axis=0)
        kw = kbuf[r, b * BLOCK:(b + 2) * BLOCK, :]
        vw = vbuf[r, b * BLOCK:(b + 2) * BLOCK, :]
        s = lax.dot_general(q_stack, kw, (((1,), (1,)), ((), ())), preferred_element_type=F32)
        bias = bias_first if b == 0 else bias_band
        ps, ms, ls = [], [], []
        for h in range(HEADS_PER_TILE):
            sh = s[h * BLOCK:(h + 1) * BLOCK, :] + bias
            m = jnp.max(sh, axis=-1, keepdims=True)
            p = jnp.exp2(sh - m)
            ls.append(jnp.sum(p, axis=-1, keepdims=True))
            ms.append(m)
            ps.append(p.astype(BF16))
        pv = jnp.dot(jnp.concatenate(ps, axis=0), vw, preferred_element_type=F32)
        o = jnp.zeros((BLOCK, LANE_TILE), F32)
        lse = jnp.zeros((BLOCK, LANE_TILE), F32)
        for h in range(HEADS_PER_TILE):
            o = jnp.where(head_masks[h], pv[h * BLOCK:(h + 1) * BLOCK, :] / ls[h], o)
            lse = jnp.where(head_masks[h], (ms[h] + jnp.log2(ls[h])) * (1.0 / LOG2E), lse)
        o_ref[r, b * BLOCK:(b + 1) * BLOCK, :] = o.astype(BF16)
        lse_ref[r, b * BLOCK:(b + 1) * BLOCK, :] = lse


def _attn_pattern(q, k, v, cast=()):
    nt, dilation, n, _ = q.shape
    rows = min(ATTN_ROWS, n)
    n_res = min(ATTN_ROWS // rows, dilation)
    blocks_per_tile = rows // BLOCK
    grid = (nt, dilation // n_res, n // rows)
    n_steps = grid[0] * grid[1] * grid[2]
    cur = pl.BlockSpec((None, n_res, rows, LANE_TILE), lambda t, r, j: (t, r, j, 0))
    prev = pl.BlockSpec((None, n_res, BLOCK, LANE_TILE),
                        lambda t, r, j: (t, r, jnp.maximum(j * blocks_per_tile - 1, 0), 0))
    step_id = lambda t, r, j: ((t * grid[1] + r) * grid[2] + j, 0)
    cast_specs = [pl.BlockSpec((w.shape[0] // n_steps, w.shape[1]), step_id) for w in cast]
    outs = pl.pallas_call(
        _attn_kernel,
        grid=grid,
        in_specs=[cur, cur, prev, cur, prev] + cast_specs,
        out_specs=[cur, cur] + cast_specs,
        out_shape=[jax.ShapeDtypeStruct(q.shape, BF16), jax.ShapeDtypeStruct(q.shape, F32)]
        + [jax.ShapeDtypeStruct(w.shape, BF16) for w in cast],
        scratch_shapes=[pltpu.VMEM((n_res, rows + BLOCK, LANE_TILE), BF16),
                        pltpu.VMEM((n_res, rows + BLOCK, LANE_TILE), BF16)],
        compiler_params=_params(dimension_semantics=("arbitrary", "arbitrary", "arbitrary")),
        name=f"attn_d{dilation}",
    )(q, k, k, v, v, *cast)
    return outs[0], outs[1], outs[2:]


def _s5_weights(a_re, a_im, log_dt, b_re, b_im, c_re, c_im, d_skip):
    g, p, c = SSM_GROUPS, SSM_STATE, SSM_GROUP
    gl = g // 2
    lr, li = a_re.astype(F32), a_im.astype(F32)
    dt = jnp.exp(log_dt.astype(F32))[:, None]

    def apow(j):
        mag = jnp.exp(lr * dt * j)
        return mag * jnp.cos(li * dt * j), mag * jnp.sin(li * dt * j)

    ab_r, ab_i = apow(1.0)
    den = lr * lr + li * li
    nr, ni = ab_r - 1.0, ab_i
    cr = (nr * lr + ni * li) / den
    ci = (ni * lr - nr * li) / den
    br, bi = b_re.astype(F32), b_im.astype(F32)
    bb_r = cr[..., None] * br - ci[..., None] * bi
    bb_i = cr[..., None] * bi + ci[..., None] * br
    cre, cim = c_re.astype(F32), c_im.astype(F32)

    def apow_many(js):
        j = jnp.asarray(js, F32)[:, None, None]
        mag = jnp.exp(lr * dt * j)
        return mag * jnp.cos(li * dt * j), mag * jnp.sin(li * dt * j)

    pr, pi = apow_many(range(SUB + 1))
    wp_r, wp_i = pr[SUB - 1::-1], pi[SUB - 1::-1]
    win_r = wp_r[..., None] * bb_r[None] - wp_i[..., None] * bb_i[None]
    win_i = wp_r[..., None] * bb_i[None] + wp_i[..., None] * bb_r[None]
    pad_lanes = lambda w: jnp.pad(w, ((0, 0), (0, 0), (0, 0), (0, SLAB - w.shape[-1])))
    to_in = lambda w: pad_lanes(w.transpose(0, 1, 3, 2).reshape(SUB, 2, gl * c, p))
    out_r = cre[None] * pr[:, :, None, :] - cim[None] * pi[:, :, None, :]
    out_i = -(cre[None] * pi[:, :, None, :] + cim[None] * pr[:, :, None, :])
    to_out = lambda w: pad_lanes(w.transpose(0, 1, 3, 2).reshape(SUB + 1, 2, gl * p, c))

    flat = lambda re, im: jnp.stack([re.reshape(-1), im.reshape(-1)])
    top_r, top_i = apow_many([CHUNK])
    lam = flat(top_r[0], top_i[0])
    levels = [lam]
    for _ in range(SCAN_GROUP - 1):
        lr_, li_ = levels[-1][0], levels[-1][1]
        levels.append(jnp.stack([lr_ * lam[0] - li_ * lam[1], lr_ * lam[1] + li_ * lam[0]]))
    m_t, w_r, w_i, e_r, e_i = _s5_expand(to_in(win_r), to_in(win_i), to_out(out_r), to_out(out_i),
                                         d_skip.astype(F32).reshape(2, 1, gl * c))
    return dict(m_t=m_t, w_r=w_r, w_i=w_i, e_r=e_r, e_i=e_i,
                a_sub=flat(pr[SUB], pi[SUB]), levels=jnp.stack(levels, axis=1))


def _s5_expand_kernel(wr_ref, wi_ref, or_ref, oi_ref, d_ref, mt_ref, wfr_ref, wfi_ref, efr_ref, efi_ref):
    n_in, n_out = wr_ref.shape[2], or_ref.shape[2]
    p, c = SSM_STATE, SSM_GROUP

    def rep(inner, total):
        r = lax.broadcasted_iota(jnp.int32, (SLAB, total), 0)
        col = lax.broadcasted_iota(jnp.int32, (SLAB, total), 1)
        return jnp.where((col & (inner - 1)) == r, 1.0, 0.0).astype(BF16)

    def group_mask(n_rows, rows_per_group, n_cols, cols_per_group):
        r = lax.broadcasted_iota(jnp.int32, (n_rows, n_cols), 0) >> int(math.log2(rows_per_group))
        col = lax.broadcasted_iota(jnp.int32, (n_rows, n_cols), 1) >> int(math.log2(cols_per_group))
        return r == col

    rep_in, rep_out = rep(p, n_out), rep(c, n_in)
    mask_in = group_mask(n_in, c, n_out, p)
    mask_out = group_mask(n_out, p, n_in, c)

    def split(x):
        hi = x.astype(BF16)
        return hi, (x - hi.astype(F32)).astype(BF16)

    def expand(x_b, rep_m, mask):
        full = jnp.dot(x_b, rep_m, preferred_element_type=F32)
        return jnp.where(mask, full, 0.0).astype(BF16)

    diag = (lax.broadcasted_iota(jnp.int32, (n_in, n_in), 0) == lax.broadcasted_iota(jnp.int32, (n_in, n_in), 1))
    for h in range(2):
        in0 = []
        for b in range(SUB):
            for src, dst in ((wr_ref, wfr_ref), (wi_ref, wfi_ref)):
                hi, lo = split(src[b, h])
                full = expand(hi, rep_in, mask_in)
                dst[b, h] = full
                if b == SUB - 1:
                    in0.append((full, expand(lo, rep_in, mask_in)))
        for j in range(SUB + 1):
            lag = None
            for (src, dst), (in_hi, in_lo) in zip(((or_ref, efr_ref), (oi_ref, efi_ref)), in0):
                hi, lo = split(src[j, h])
                out_hi = expand(hi, rep_out, mask_out)
                if j > 0:
                    dst[j - 1, h] = out_hi
                if j < SUB:
                    out_lo = expand(lo, rep_out, mask_out)
                    part = (jnp.dot(in_hi, out_hi, preferred_element_type=F32)
                            + jnp.dot(in_hi, out_lo, preferred_element_type=F32)
                            + jnp.dot(in_lo, out_hi, preferred_element_type=F32))
                    lag = part if lag is None else lag + part
            if j == 0:
                lag = lag + jnp.where(diag, d_ref[h], 0.0)
            if j < SUB:
                mt_ref[j, h] = lag.astype(BF16)


def _s5_expand(wc_r, wc_i, oc_r, oc_i, d_row):
    n_in, n_out = wc_r.shape[2], oc_r.shape[2]
    shapes = [jax.ShapeDtypeStruct((SUB, 2, n_in, n_in), BF16),
              jax.ShapeDtypeStruct((SUB, 2, n_in, n_out), BF16), jax.ShapeDtypeStruct((SUB, 2, n_in, n_out), BF16),
              jax.ShapeDtypeStruct((SUB, 2, n_out, n_in), BF16), jax.ShapeDtypeStruct((SUB, 2, n_out, n_in), BF16)]
    return pl.pallas_call(
        _s5_expand_kernel,
        out_shape=shapes,
        compiler_params=_params(),
        name="s5_expand",
    )(wc_r, wc_i, oc_r, oc_i, d_row)


def _cmul(ar, ai, br, bi):
    return ar * br - ai * bi, ar * bi + ai * br


def _s5_kernel(u_ref, mt_ref, wr_ref, wi_ref, er_ref, ei_ref, asub_ref, lvl_ref,
               gluw_ref, glub_ref, gout_ref, out_ref, carry_ref, zr_ref, zi_ref, locr_ref, loci_ref):
    rows = u_ref.shape[0]
    width = SSM_WIDTH

    @pl.when(pl.program_id(0) == 0)
    def _():
        carry_ref[...] = jnp.zeros_like(carry_ref)

    def u_tile(t, h):
        return u_ref[:, t * width + h * LANE_TILE:t * width + (h + 1) * LANE_TILE]

    a_r, a_i = asub_ref[0:1, :], asub_ref[1:2, :]

    for a in range(SUB):
        for h in range(2):
            cols = slice(h * HALF_STATE, (h + 1) * HALF_STATE)
            pr = pi = None
            for b in range(SUB):
                ut = u_tile(a * SUB + b, h)
                dr = jnp.dot(ut, wr_ref[b, h], preferred_element_type=F32)
                di = jnp.dot(ut, wi_ref[b, h], preferred_element_type=F32)
                pr = dr if pr is None else pr + dr
                pi = di if pi is None else pi + di
            if a == 0:
                zr_ref[:, cols] = pr
                zi_ref[:, cols] = pi
            else:
                hr, hi = _cmul(zr_ref[:, cols], zi_ref[:, cols], a_r[:, cols], a_i[:, cols])
                pr, pi = hr + pr, hi + pi
                zr_ref[:, cols] = pr
                zi_ref[:, cols] = pi
            if a < SUB - 1:
                locr_ref[a, :, cols] = pr.astype(BF16)
                loci_ref[a, :, cols] = pi.astype(BF16)

    row = lax.broadcasted_iota(jnp.int32, (rows, 1), 0)
    in_group = row & (SCAN_GROUP - 1)
    sr, si = zr_ref[...], zi_ref[...]
    sh = 1
    while sh < SCAN_GROUP:
        keep = in_group >= sh
        pr_, pi_ = lvl_ref[0, sh - 1:sh, :], lvl_ref[1, sh - 1:sh, :]
        tr = jnp.where(keep, pltpu.roll(sr, sh, axis=0), 0.0)
        ti = jnp.where(keep, pltpu.roll(si, sh, axis=0), 0.0)
        mr, mi = _cmul(tr, ti, pr_, pi_)
        sr, si = sr + mr, si + mi
        sh *= 2
    c_r, c_i = carry_ref[0:1, :], carry_ref[1:2, :]
    pw_r, pw_i = lvl_ref[0], lvl_ref[1]
    last_r, last_i = c_r, c_i
    for g in range(rows // SCAN_GROUP):
        grp = slice(g * SCAN_GROUP, (g + 1) * SCAN_GROUP)
        ar, ai = _cmul(jnp.broadcast_to(last_r, pw_r.shape), jnp.broadcast_to(last_i, pw_i.shape), pw_r, pw_i)
        gr, gi = sr[grp, :] + ar, si[grp, :] + ai
        zr_ref[grp, :] = gr
        zi_ref[grp, :] = gi
        last_r, last_i = gr[SCAN_GROUP - 1:SCAN_GROUP, :], gi[SCAN_GROUP - 1:SCAN_GROUP, :]
    carry_ref[0:1, :] = last_r
    carry_ref[1:2, :] = last_i
    xr = jnp.where(row == 0, c_r, pltpu.roll(zr_ref[...], 1, axis=0))
    xi = jnp.where(row == 0, c_i, pltpu.roll(zi_ref[...], 1, axis=0))
    zr_ref[...] = xr
    zi_ref[...] = xi

    for a in range(SUB):
        if a > 0:
            nr, ni = _cmul(zr_ref[...], zi_ref[...], a_r, a_i)
            zr_ref[...] = nr
            zi_ref[...] = ni
        if a == 0:
            xr_b = zr_ref[...].astype(BF16)
            xi_b = zi_ref[...].astype(BF16)
        else:
            xr_b = (zr_ref[...] + locr_ref[a - 1].astype(F32)).astype(BF16)
            xi_b = (zi_ref[...] + loci_ref[a - 1].astype(F32)).astype(BF16)
        for b in range(SUB):
            t = a * SUB + b
            halves = []
            for h in range(2):
                cols = slice(h * HALF_STATE, (h + 1) * HALF_STATE)
                acc = jnp.dot(xr_b[:, cols], er_ref[b, h], preferred_element_type=F32)
                acc = acc + jnp.dot(xi_b[:, cols], ei_ref[b, h], preferred_element_type=F32)
                for b_in in range(b + 1):
                    acc = acc + jnp.dot(u_tile(a * SUB + b_in, h), mt_ref[b - b_in, h],
                                        preferred_element_type=F32)
                halves.append(acc)
            y = jnp.concatenate(halves, axis=1)
            z = 0.5 * y * (1.0 + jnp.tanh(math.sqrt(2.0 / math.pi) * (y + 0.044715 * (y * y * y))))
            gate = jnp.dot(z.astype(BF16), gluw_ref[...], preferred_element_type=F32) + glub_ref[...]
            o = z * (1.0 / (1.0 + jnp.exp(-gate)))
            ms = jnp.mean(o * o, axis=-1, keepdims=True)
            out_ref[:, t * width:(t + 1) * width] = (o * lax.rsqrt(ms + EPS) * gout_ref[...]).astype(BF16)


def _s5(uv, w, glu_w, glu_b, g_out):
    n = uv.shape[0]
    rows = S5_ROWS
    tile = pl.BlockSpec((rows, CHUNK * SSM_WIDTH), lambda i: (i, 0))
    return pl.pallas_call(
        _s5_kernel,
        grid=(n // rows,),
        in_specs=[tile] + [_const_spec(a.shape) for a in
                           (w["m_t"], w["w_r"], w["w_i"], w["e_r"], w["e_i"], w["a_sub"], w["levels"],
                            glu_w, glu_b, g_out)],
        out_specs=tile,
        out_shape=jax.ShapeDtypeStruct((n, CHUNK * SSM_WIDTH), BF16),
        scratch_shapes=[pltpu.VMEM((2, STATE_COLS), F32),
                        pltpu.VMEM((rows, STATE_COLS), F32),
                        pltpu.VMEM((rows, STATE_COLS), F32),
                        pltpu.VMEM((SUB - 1, rows, STATE_COLS), BF16),
                        pltpu.VMEM((SUB - 1, rows, STATE_COLS), BF16)],
        compiler_params=_params(dimension_semantics=("arbitrary",)),
        name="s5",
    )(uv, w["m_t"], w["w_r"], w["w_i"], w["e_r"], w["e_i"], w["a_sub"], w["levels"], glu_w, glu_b, g_out)


def _mlp_kernel(x_ref, o1_ref, o2_ref, o3_ref, l1_ref, l2_ref, l3_ref, ssm_ref, ga_ref, wout_ref,
                g2_ref, wup_ref, wdn_ref, out_ref, slab_ref, slab4_ref, mix_ref):
    rows = PROJ_SUB
    n_sub = x_ref.shape[0] // rows
    n_slabs = ATTN_WIDTH // SLAB
    step = DILATIONS[1]
    n_ff = D_FF // FF_CHUNK

    def regroup(ref, d, sub, buf):
        part = slice(sub * (rows // d), (sub + 1) * (rows // d))
        for s in range(n_slabs):
            t, lanes = _slab_home(s)
            if ref is ssm_ref:
                piece = lambda r: ref[part, r * SSM_WIDTH + s * SLAB:r * SSM_WIDTH + (s + 1) * SLAB]
            else:
                piece = lambda r: ref[t, r, part, lanes]
            if d == step:
                for r in range(d):
                    slab_ref[sub, buf, s, pl.ds(r, rows // d, stride=d), :] = piece(r).astype(F32)
            else:
                for r4 in range(step):
                    for j4 in range(step):
                        slab4_ref[sub, buf - 2, s, r4, pl.ds(j4, rows // d, stride=step), :] = (
                            piece(step * j4 + r4).astype(F32))
                for r4 in range(step):
                    slab_ref[sub, buf, s, pl.ds(r4, rows // step, stride=step), :] = slab4_ref[sub, buf - 2, s, r4]

    def in_order(sub, buf):
        return jnp.concatenate([slab_ref[sub, buf, s] for s in range(n_slabs)], axis=1)

    def merge_slab(sub, s):
        t, lanes = _slab_home(s)
        part = slice(sub * rows, (sub + 1) * rows)
        l1, l2, l3 = l1_ref[t, 0, part, lanes], slab_ref[sub, 1, s], slab_ref[sub, 3, s]
        m = jnp.maximum(jnp.maximum(l1, l2), l3)
        w1, w2, w3 = jnp.exp(l1 - m), jnp.exp(l2 - m), jnp.exp(l3 - m)
        num = w1 * o1_ref[t, 0, part, lanes].astype(F32) + w2 * slab_ref[sub, 0, s] + w3 * slab_ref[sub, 2, s]
        slab_ref[sub, 0, s] = num / (w1 + w2 + w3)

    def merge_finish(sub):
        attn = in_order(sub, 0)
        ms = jnp.mean(attn * attn, axis=-1, keepdims=True)
        mix_ref[sub, :, 0:ATTN_WIDTH] = (attn * lax.rsqrt(ms + EPS) * ga_ref[...]).astype(BF16)
        mix_ref[sub, :, ATTN_WIDTH:] = in_order(sub, 4).astype(BF16)

    def prologue(sub):
        return ([lambda: regroup(o2_ref, step, sub, 0), lambda: regroup(l2_ref, step, sub, 1),
                 lambda: regroup(o3_ref, CHUNK, sub, 2), lambda: regroup(l3_ref, CHUNK, sub, 3),
                 lambda: regroup(ssm_ref, CHUNK, sub, 4)]
                + [lambda s=s: merge_slab(sub, s) for s in range(n_slabs)] + [lambda: merge_finish(sub)])

    def main(sub):
        part = slice(sub * rows, (sub + 1) * rows)
        st = {}

        def out_proj():
            x1 = x_ref[part, :] + jnp.dot(mix_ref[sub], wout_ref[...], preferred_element_type=F32)
            ms2 = jnp.mean(x1 * x1, axis=-1, keepdims=True)
            st["xn"] = (x1 * lax.rsqrt(ms2 + EPS) * g2_ref[...]).astype(BF16)
            st["acc"] = x1

        def up(c):
            st[c] = jnp.dot(st["xn"], wup_ref[:, c * FF_CHUNK:(c + 1) * FF_CHUNK], preferred_element_type=F32)

        def down(c):
            h = jnp.square(jnp.maximum(st.pop(c), 0.0)).astype(BF16)
            st["acc"] = st["acc"] + jnp.dot(h, wdn_ref[c * FF_CHUNK:(c + 1) * FF_CHUNK, :],
                                            preferred_element_type=F32)

        def finish():
            out_ref[part, :] = st["acc"]

        stages = [out_proj, lambda: up(0)]
        for c in range(n_ff):
            if c + 1 < n_ff:
                stages.append(lambda c=c: up(c + 1))
            stages.append(lambda c=c: down(c))
        return stages + [finish]

    tail = 3
    program = prologue(0)
    all_stages = [main(sub) for sub in range(n_sub)]
    for sub in range(n_sub):
        stages = all_stages[sub]
        head = stages if sub == 0 else stages[1:]
        if sub + 1 < n_sub:
            side = prologue(sub + 1)
            body = []
            for stage in head[:-tail]:
                body.append(stage)
                if side:
                    body.append(side.pop(0))
            program += body + side + [all_stages[sub + 1][0]] + head[-tail:]
        else:
            program += head
    for piece in program:
        piece()


def _mlp(x, os_, ls_, ssm, ga, w_out, g2, w_up, w_dn):
    s = x.shape[0]
    rows = MLP_ROWS
    xt = pl.BlockSpec((rows, D_MODEL), lambda i: (i, 0))
    ht = [pl.BlockSpec((N_HEAD_TILES, d, rows // d, LANE_TILE), lambda i: (0, 0, i, 0)) for d in DILATIONS]
    return pl.pallas_call(
        _mlp_kernel,
        grid=(s // rows,),
        in_specs=[xt, *ht, *ht, pl.BlockSpec((rows // CHUNK, CHUNK * SSM_WIDTH), lambda i: (i, 0)),
                  _const_spec(ga.shape), _const_spec(w_out.shape), _const_spec(g2.shape),
                  _const_spec(w_up.shape), _const_spec(w_dn.shape)],
        out_specs=xt,
        out_shape=jax.ShapeDtypeStruct((s, D_MODEL), F32),
        scratch_shapes=[pltpu.VMEM((rows // PROJ_SUB, 5, ATTN_WIDTH // SLAB, PROJ_SUB, SLAB), F32),
                        pltpu.VMEM((rows // PROJ_SUB, 3, ATTN_WIDTH // SLAB, DILATIONS[1],
                                    PROJ_SUB // DILATIONS[1], SLAB), F32),
                        pltpu.VMEM((rows // PROJ_SUB, PROJ_SUB, D_MODEL), BF16)],
        compiler_params=_params(dimension_semantics=("arbitrary",)),
        name="mlp",
    )(x, *os_, *ls_, ssm, ga, w_out, g2, w_up, w_dn)


def _layer(x, norm1_g, w_in, q_norm_g, k_norm_g, ssm_a_re, ssm_a_im, ssm_log_dt, ssm_b_re, ssm_b_im,
           ssm_c_re, ssm_c_im, ssm_d, glu_w, glu_b, attn_out_norm_g, ssm_out_norm_g, w_out, norm2_g,
           w_mlp_up, w_mlp_down):
    row = lambda g: g.astype(F32).reshape(1, -1)
    heads = ATTN_WIDTH // HEAD_DIM
    head_id = jnp.arange(LANE_TILE) // HEAD_DIM
    hmat = jnp.where(head_id[:, None] == head_id[None, :], 1.0 / HEAD_DIM, 0.0).astype(BF16)
    *qkv, u = _proj(x, row(norm1_g), w_in.astype(F32), hmat,
                    row(jnp.tile(q_norm_g, heads)), row(jnp.tile(k_norm_g, heads)))
    outs, lses = [], []
    mlp_weights = (w_out.astype(F32), w_mlp_up.astype(F32), w_mlp_down.astype(F32))
    for i in range(len(DILATIONS)):
        o, lse, cast = _attn_pattern(*qkv[3 * i:3 * i + 3], cast=mlp_weights if i == 0 else ())
        outs.append(o)
        lses.append(lse)
        if i == 0:
            w_out_b, w_up_b, w_dn_b = cast
    w = _s5_weights(ssm_a_re, ssm_a_im, ssm_log_dt, ssm_b_re, ssm_b_im, ssm_c_re, ssm_c_im, ssm_d)
    ssm = _s5(u, w, glu_w.astype(BF16), row(glu_b), row(ssm_out_norm_g))
    return _mlp(x, outs, lses, ssm, row(attn_out_norm_g), w_out_b, row(norm2_g), w_up_b, w_dn_b)


def kernel(x, norm1_g, w_in, q_norm_g, k_norm_g, ssm_a_re, ssm_a_im, ssm_log_dt, ssm_b_re, ssm_b_im,
           ssm_c_re, ssm_c_im, ssm_d, glu_w, glu_b, attn_out_norm_g, ssm_out_norm_g, w_out, norm2_g,
           w_mlp_up, w_mlp_down):
    params = (norm1_g, w_in, q_norm_g, k_norm_g, ssm_a_re, ssm_a_im, ssm_log_dt, ssm_b_re, ssm_b_im,
              ssm_c_re, ssm_c_im, ssm_d, glu_w, glu_b, attn_out_norm_g, ssm_out_norm_g, w_out, norm2_g,
              w_mlp_up, w_mlp_down)
    batch = x.shape[0]
    outs = []
    for bi in range(batch):
        h = x[bi]
        for layer in range(norm1_g.shape[0]):
            h = _layer(h, *[p[layer] for p in params])
        outs.append(h)
    return jnp.stack(outs)
```

```python
import functools
import math

import jax
import jax.numpy as jnp
from jax import lax
from jax.experimental import pallas as pl
from jax.experimental.pallas import tpu as pltpu

F32 = jnp.float32
BF16 = jnp.bfloat16

D_MODEL = 1024
ATTN_WIDTH = 512
HEAD_DIM = 64
SSM_WIDTH = 512
SSM_GROUP = 16
SSM_GROUPS = 32
SSM_STATE = 64
D_FF = 4096
EPS = 1e-6
NEG_INF = -1e30
DILATIONS = (1, 4, 16)
BLOCK = 128

SLAB = 128
LANE_TILE = 256
HEADS_PER_TILE = LANE_TILE // HEAD_DIM
N_HEAD_TILES = ATTN_WIDTH // LANE_TILE
CHUNK = 16
SUB = 2
N_SUB = CHUNK // SUB
OCT = 8
N_OCT = SSM_GROUPS // OCT
SCAN_GROUP = 8
STATE_COLS = SSM_GROUPS * SSM_STATE
VMEM_LIMIT = 56 * 1024 * 1024

LOG2E = math.log2(math.e)
Q_SCALE = HEAD_DIM ** -0.5 * LOG2E

PROJ_ROWS = 1024
PROJ_SUB = 256
SLAB_BUFFERS = 2
PROJ_AHEAD = 2
ATTN_ROWS = 4096
S5_ROWS = 256
MLP_ROWS = 512
FF_CHUNK = 512


def _const_spec(shape):
    nd = len(shape)
    return pl.BlockSpec(shape, lambda *_: (0,) * nd, pipeline_mode=pl.Buffered(1))


def _params(**kw):
    return pltpu.CompilerParams(vmem_limit_bytes=VMEM_LIMIT, **kw)


def _proj_kernel(x_ref, g1_ref, w_ref, hmat_ref, gq_ref, gk_ref,
                 q1_ref, k1_ref, v1_ref, q4_ref, k4_ref, v4_ref, q16_ref, k16_ref, v16_ref, u_ref,
                 slab_ref, slab4_ref, wb_ref):
    @pl.when(pl.program_id(0) == 0)
    def _():
        wb_ref[...] = w_ref[...].astype(BF16)

    outs = ((q1_ref, q4_ref, q16_ref), (k1_ref, k4_ref, k16_ref), (v1_ref, v4_ref, v16_ref))
    slabs_per_tensor = ATTN_WIDTH // SLAB
    step = DILATIONS[1]
    rows = PROJ_SUB
    n_sub = x_ref.shape[0] // rows
    n_chunks = w_ref.shape[1] // LANE_TILE
    chunks_per_tensor = ATTN_WIDTH // LANE_TILE
    gains = (gq_ref, gk_ref)
    xn = [None] * n_sub

    def prologue(sub):
        x = x_ref[sub * rows:(sub + 1) * rows, :]
        ms = jnp.mean(x * x, axis=-1, keepdims=True)
        xn[sub] = (x * lax.rsqrt(ms + EPS) * g1_ref[...]).astype(BF16)

    def matmul(sub, c):
        return jnp.dot(xn[sub], wb_ref[:, c * LANE_TILE:(c + 1) * LANE_TILE], preferred_element_type=F32)

    def epilogue(sub, c, val):
        ti, t = divmod(c, chunks_per_tensor)
        lane_tile = slice(t * LANE_TILE, (t + 1) * LANE_TILE)
        if ti < 2:
            ms_h = jnp.dot((val * val).astype(BF16), hmat_ref[...], preferred_element_type=F32)
            val = val * lax.rsqrt(ms_h + EPS) * gains[ti][:, lane_tile]
            if ti == 0:
                val = val * Q_SCALE
        if ti < 3:
            outs[ti][0][t, 0, sub * rows:(sub + 1) * rows, :] = val.astype(BF16)
        r4_rows = slice(sub * (rows // step), (sub + 1) * (rows // step))
        r16_rows = slice(sub * (rows // CHUNK), (sub + 1) * (rows // CHUNK))
        for half in range(LANE_TILE // SLAB):
            s = t * (LANE_TILE // SLAB) + half
            lanes = slice(half * SLAB, (half + 1) * SLAB)
            slab = ti * slabs_per_tensor + s
            buf = sub % SLAB_BUFFERS
            slab_ref[buf, slab] = val[:, lanes]
            for r4 in range(step):
                part = slab_ref[buf, slab, pl.ds(r4, rows // step, stride=step), :]
                if ti < 3:
                    outs[ti][1][t, r4, r4_rows, lanes] = part.astype(BF16)
                slab4_ref[buf, slab, r4] = part
            for r4 in range(step):
                for j4 in range(step):
                    r16 = step * j4 + r4
                    part = slab4_ref[buf, slab, r4, pl.ds(j4, rows // CHUNK, stride=step), :].astype(BF16)
                    if ti < 3:
                        outs[ti][2][t, r16, r16_rows, lanes] = part
                    else:
                        u_ref[r16_rows, r16 * SSM_WIDTH + s * SLAB:r16 * SSM_WIDTH + (s + 1) * SLAB] = part

    work = [(sub, c) for sub in range(n_sub) for c in range(n_chunks)]
    prologue(0)
    pending = []
    for sub, c in work:
        pending.append((sub, c, matmul(sub, c)))
        if len(pending) > PROJ_AHEAD:
            epilogue(*pending.pop(0))
        if c == n_chunks // 2 and sub + 1 < n_sub:
            prologue(sub + 1)
    for item in pending:
        epilogue(*item)


def _slab_home(s):
    t, half = divmod(s, LANE_TILE // SLAB)
    return t, slice(half * SLAB, (half + 1) * SLAB)


def _proj(x, g1, w_in, hmat, gq, gk):
    s = x.shape[0]
    rows = PROJ_ROWS
    out_specs, out_shape = [], []
    for d in DILATIONS:
        for _ in range(3):
            out_specs.append(pl.BlockSpec((N_HEAD_TILES, d, rows // d, LANE_TILE), lambda i: (0, 0, i, 0)))
            out_shape.append(jax.ShapeDtypeStruct((N_HEAD_TILES, d, s // d, LANE_TILE), BF16))
    out_specs.append(pl.BlockSpec((rows // CHUNK, CHUNK * SSM_WIDTH), lambda i: (i, 0)))
    out_shape.append(jax.ShapeDtypeStruct((s // CHUNK, CHUNK * SSM_WIDTH), BF16))
    return pl.pallas_call(
        _proj_kernel,
        grid=(s // rows,),
        in_specs=[
            pl.BlockSpec((rows, D_MODEL), lambda i: (i, 0)),
            _const_spec((1, D_MODEL)),
            _const_spec(w_in.shape),
            _const_spec(hmat.shape),
            _const_spec((1, ATTN_WIDTH)),
            _const_spec((1, ATTN_WIDTH)),
        ],
        out_specs=out_specs,
        out_shape=out_shape,
        scratch_shapes=[pltpu.VMEM((SLAB_BUFFERS, 4 * ATTN_WIDTH // SLAB, PROJ_SUB, SLAB), F32),
                        pltpu.VMEM((SLAB_BUFFERS, 4 * ATTN_WIDTH // SLAB, DILATIONS[1],
                                    PROJ_SUB // DILATIONS[1], SLAB), F32),
                        pltpu.VMEM(w_in.shape, BF16)],
        compiler_params=_params(dimension_semantics=("arbitrary",)),
        name="proj",
    )(x, g1, w_in, hmat, gq, gk)


def _attn_kernel(q_ref, kc_ref, kp_ref, vc_ref, vp_ref, *rest):
    n_cast = (len(rest) - 4) // 2
    cast_in, (o_ref, lse_ref), cast_out = rest[:n_cast], rest[n_cast:n_cast + 2], rest[n_cast + 2:2 * n_cast + 2]
    kbuf, vbuf = rest[2 * n_cast + 2:]
    for src, dst in zip(cast_in, cast_out):
        dst[...] = src[...].astype(BF16)
    n_res, rows = q_ref.shape[0], q_ref.shape[1]
    kbuf[:, 0:BLOCK, :] = kp_ref[...]
    kbuf[:, BLOCK:, :] = kc_ref[...]
    vbuf[:, 0:BLOCK, :] = vp_ref[...]
    vbuf[:, BLOCK:, :] = vc_ref[...]

    lane = lax.broadcasted_iota(jnp.int32, (BLOCK, LANE_TILE), 1)
    head_masks = [(lane >= h * HEAD_DIM) & (lane < (h + 1) * HEAD_DIM) for h in range(HEADS_PER_TILE)]
    qi = lax.broadcasted_iota(jnp.int32, (BLOCK, 2 * BLOCK), 0)
    ki = lax.broadcasted_iota(jnp.int32, (BLOCK, 2 * BLOCK), 1)
    band = (ki >= qi) & (ki <= qi + BLOCK)
    bias_band = jnp.where(band, 0.0, NEG_INF).astype(F32)
    no_prev = pl.program_id(2) == 0
    bias_first = jnp.where(band & ((ki >= BLOCK) | jnp.logical_not(no_prev)), 0.0, NEG_INF).astype(F32)

    for r, b in [(r, b) for r in range(n_res) for b in range(rows // BLOCK)]:
        qb = q_ref[r, b * BLOCK:(b + 1) * BLOCK, :]
        zero = jnp.zeros_like(qb)
        q_stack = jnp.concatenate([jnp.where(mk, qb, zero) for mk in head_masks], axis=0)
        kw = kbuf[r, b * BLOCK:(b + 2) * BLOCK, :]
        vw = vbuf[r, b * BLOCK:(b + 2) * BLOCK, :]
        s = lax.dot_general(q_stack, kw, (((1,), (1,)), ((), ())), preferred_element_type=F32)
        bias = bias_first if b == 0 else bias_band
        ps, ms, ls = [], [], []
        for h in range(HEADS_PER_TILE):
            sh = s[h * BLOCK:(h + 1) * BLOCK, :] + bias
            m = jnp.max(sh, axis=-1, keepdims=True)
            p = jnp.exp2(sh - m)
            ls.append(jnp.sum(p, axis=-1, keepdims=True))
            ms.append(m)
            ps.append(p.astype(BF16))
        pv = jnp.dot(jnp.concatenate(ps, axis=0), vw, preferred_element_type=F32)
        o = jnp.zeros((BLOCK, LANE_TILE), F32)
        lse = jnp.zeros((BLOCK, LANE_TILE), F32)
        for h in range(HEADS_PER_TILE):
            o = jnp.where(head_masks[h], pv[h * BLOCK:(h + 1) * BLOCK, :] / ls[h], o)
            lse = jnp.where(head_masks[h], (ms[h] + jnp.log2(ls[h])) * (1.0 / LOG2E), lse)
        o_ref[r, b * BLOCK:(b + 1) * BLOCK, :] = o.astype(BF16)
        lse_ref[r, b * BLOCK:(b + 1) * BLOCK, :] = lse


def _attn_pattern(q, k, v, cast=()):
    nt, dilation, n, _ = q.shape
    rows = min(ATTN_ROWS, n)
    n_res = min(ATTN_ROWS // rows, dilation)
    blocks_per_tile = rows // BLOCK
    grid = (nt, dilation // n_res, n // rows)
    n_steps = grid[0] * grid[1] * grid[2]
    cur = pl.BlockSpec((None, n_res, rows, LANE_TILE), lambda t, r, j: (t, r, j, 0))
    prev = pl.BlockSpec((None, n_res, BLOCK, LANE_TILE),
                        lambda t, r, j: (t, r, jnp.maximum(j * blocks_per_tile - 1, 0), 0))
    step_id = lambda t, r, j: ((t * grid[1] + r) * grid[2] + j, 0)
    cast_specs = [pl.BlockSpec((w.shape[0] // n_steps, w.shape[1]), step_id) for w in cast]
    outs = pl.pallas_call(
        _attn_kernel,
        grid=grid,
        in_specs=[cur, cur, prev, cur, prev] + cast_specs,
        out_specs=[cur, cur] + cast_specs,
        out_shape=[jax.ShapeDtypeStruct(q.shape, BF16), jax.ShapeDtypeStruct(q.shape, F32)]
        + [jax.ShapeDtypeStruct(w.shape, BF16) for w in cast],
        scratch_shapes=[pltpu.VMEM((n_res, rows + BLOCK, LANE_TILE), BF16),
                        pltpu.VMEM((n_res, rows + BLOCK, LANE_TILE), BF16)],
        compiler_params=_params(dimension_semantics=("arbitrary", "arbitrary", "arbitrary")),
        name=f"attn_d{dilation}",
    )(q, k, k, v, v, *cast)
    return outs[0], outs[1], outs[2:]


def _s5_weights(a_re, a_im, log_dt, b_re, b_im, c_re, c_im, d_skip):
    p, c = SSM_STATE, SSM_GROUP
    lr, li = a_re.astype(F32), a_im.astype(F32)
    dt = jnp.exp(log_dt.astype(F32))[:, None]

    def apow(j):
        mag = jnp.exp(lr * dt * j)
        return mag * jnp.cos(li * dt * j), mag * jnp.sin(li * dt * j)

    ab_r, ab_i = apow(1.0)
    den = lr * lr + li * li
    nr, ni = ab_r - 1.0, ab_i
    cr = (nr * lr + ni * li) / den
    ci = (ni * lr - nr * li) / den
    br, bi = b_re.astype(F32), b_im.astype(F32)
    bb_r = cr[..., None] * br - ci[..., None] * bi
    bb_i = cr[..., None] * bi + ci[..., None] * br
    cre, cim = c_re.astype(F32), c_im.astype(F32)

    def apow_many(js):
        j = jnp.asarray(js, F32)[:, None, None]
        mag = jnp.exp(lr * dt * j)
        return mag * jnp.cos(li * dt * j), mag * jnp.sin(li * dt * j)

    pr, pi = apow_many(range(SUB + 1))
    wp_r = jnp.stack([pr[SUB - 1 - t] for t in range(SUB)])
    wp_i = jnp.stack([pi[SUB - 1 - t] for t in range(SUB)])
    win_r = wp_r[..., None] * bb_r[None] - wp_i[..., None] * bb_i[None]
    win_i = wp_r[..., None] * bb_i[None] + wp_i[..., None] * bb_r[None]
    pad_lanes = lambda w: jnp.pad(w, [(0, 0)] * (w.ndim - 1) + [(0, SLAB - w.shape[-1])])
    to_in = lambda w: pad_lanes(w.reshape(SUB, N_OCT, OCT, p, c).transpose(1, 0, 2, 4, 3)
                                .reshape(N_OCT, SUB * OCT * c, p))
    out_r = cre[None] * pr[:, :, None, :] - cim[None] * pi[:, :, None, :]
    out_i = -(cre[None] * pi[:, :, None, :] + cim[None] * pr[:, :, None, :])
    to_out = lambda w: pad_lanes(w.reshape(SUB + 1, N_OCT, OCT, c, p).transpose(0, 1, 2, 4, 3)
                                 .reshape(SUB + 1, N_OCT, OCT * p, c))

    flat = lambda re, im: jnp.stack([re.reshape(-1), im.reshape(-1)])
    top_r, top_i = apow_many([CHUNK])
    lam = flat(top_r[0], top_i[0])
    levels = [lam]
    for _ in range(SCAN_GROUP - 1):
        lr_, li_ = levels[-1][0], levels[-1][1]
        levels.append(jnp.stack([lr_ * lam[0] - li_ * lam[1], lr_ * lam[1] + li_ * lam[0]]))
    m_t, w_r, w_i, e_r, e_i = _s5_expand(to_in(win_r), to_in(win_i), to_out(out_r), to_out(out_i),
                                         d_skip.astype(F32).reshape(N_OCT, 1, OCT * c))
    return dict(m_t=m_t, w_r=w_r, w_i=w_i, e_r=e_r, e_i=e_i,
                a_sub=flat(pr[SUB], pi[SUB]), levels=jnp.stack(levels, axis=1))


def _s5_expand_kernel(wr_ref, wi_ref, or_ref, oi_ref, d_ref, mt_ref, wfr_ref, wfi_ref, efr_ref, efi_ref):
    p, c = SSM_STATE, SSM_GROUP
    n_state, n_lane = OCT * p, OCT * c

    def rep(inner, total):
        r = lax.broadcasted_iota(jnp.int32, (SLAB, total), 0)
        col = lax.broadcasted_iota(jnp.int32, (SLAB, total), 1)
        return jnp.where((col & (inner - 1)) == r, 1.0, 0.0).astype(BF16)

    shift = lambda n: int(math.log2(n))
    rep_in, rep_out = rep(p, n_state), rep(c, n_lane)
    in_row = lax.broadcasted_iota(jnp.int32, (SUB * n_lane, n_state), 0)
    in_col = lax.broadcasted_iota(jnp.int32, (SUB * n_lane, n_state), 1)
    mask_in = ((in_row >> shift(c)) & (OCT - 1)) == (in_col >> shift(p))
    out_row = lax.broadcasted_iota(jnp.int32, (n_state, n_lane), 0)
    out_col = lax.broadcasted_iota(jnp.int32, (n_state, n_lane), 1)
    mask_out = (out_row >> shift(p)) == (out_col >> shift(c))
    diag = (lax.broadcasted_iota(jnp.int32, (n_lane, n_lane), 0)
            == lax.broadcasted_iota(jnp.int32, (n_lane, n_lane), 1))

    def split(x):
        hi = x.astype(BF16)
        return hi, (x - hi.astype(F32)).astype(BF16)

    def expand(x_b, rep_m, mask):
        full = jnp.dot(x_b, rep_m, preferred_element_type=F32)
        return jnp.where(mask, full, 0.0).astype(BF16)

    last = slice((SUB - 1) * n_lane, SUB * n_lane)
    for o in range(N_OCT):
        in0 = []
        for src, dst in ((wr_ref, wfr_ref), (wi_ref, wfi_ref)):
            hi, lo = split(src[o])
            full = expand(hi, rep_in, mask_in)
            dst[o] = full
            in0.append((full[last, :], expand(lo, rep_in, mask_in)[last, :]))
        lags = [None] * SUB
        for (src, dst), (in_hi, in_lo) in zip(((or_ref, efr_ref), (oi_ref, efi_ref)), in0):
            outs_hi = []
            for j in range(SUB + 1):
                hi, lo = split(src[j, o])
                out_hi = expand(hi, rep_out, mask_out)
                outs_hi.append(out_hi)
                if j < SUB:
                    out_lo = expand(lo, rep_out, mask_out)
                    part = (jnp.dot(in_hi, out_hi, preferred_element_type=F32)
                            + jnp.dot(in_hi, out_lo, preferred_element_type=F32)
                            + jnp.dot(in_lo, out_hi, preferred_element_type=F32))
                    lags[j] = part if lags[j] is None else lags[j] + part
            dst[o] = jnp.concatenate(outs_hi[1:], axis=1)
        lags[0] = lags[0] + jnp.where(diag, d_ref[o], 0.0)
        zero = jnp.zeros_like(lags[0])
        mt_ref[o] = jnp.concatenate(
            [jnp.concatenate([lags[t_out - t_in] if t_out >= t_in else zero for t_out in range(SUB)], axis=1)
             for t_in in range(SUB)], axis=0).astype(BF16)


def _s5_expand(wc_r, wc_i, oc_r, oc_i, d_row):
    n_in, n_state = wc_r.shape[1], oc_r.shape[2]
    shapes = [jax.ShapeDtypeStruct((N_OCT, n_in, n_in), BF16),
              jax.ShapeDtypeStruct((N_OCT, n_in, n_state), BF16), jax.ShapeDtypeStruct((N_OCT, n_in, n_state), BF16),
              jax.ShapeDtypeStruct((N_OCT, n_state, n_in), BF16), jax.ShapeDtypeStruct((N_OCT, n_state, n_in), BF16)]
    return pl.pallas_call(
        _s5_expand_kernel,
        out_shape=shapes,
        compiler_params=_params(),
        name="s5_expand",
    )(wc_r, wc_i, oc_r, oc_i, d_row)


def _cmul(ar, ai, br, bi):
    return ar * br - ai * bi, ar * bi + ai * br


def _s5_kernel(u_ref, mt_ref, wr_ref, wi_ref, er_ref, ei_ref, asub_ref, lvl_ref,
               gluw_ref, glub_ref, gout_ref, out_ref, carry_ref, zr_ref, zi_ref, locr_ref, loci_ref):
    rows = u_ref.shape[0]
    width = SSM_WIDTH
    oct_lanes = OCT * SSM_GROUP
    oct_state = OCT * SSM_STATE

    @pl.when(pl.program_id(0) == 0)
    def _():
        carry_ref[...] = jnp.zeros_like(carry_ref)

    def u_tile(a, o):
        return jnp.concatenate(
            [u_ref[:, (a * SUB + t) * width + o * oct_lanes:(a * SUB + t) * width + (o + 1) * oct_lanes]
             for t in range(SUB)], axis=1)

    a_r, a_i = asub_ref[0:1, :], asub_ref[1:2, :]

    for a in range(N_SUB):
        for o in range(N_OCT):
            cols = slice(o * oct_state, (o + 1) * oct_state)
            ut = u_tile(a, o)
            pr = jnp.dot(ut, wr_ref[o], preferred_element_type=F32)
            pi = jnp.dot(ut, wi_ref[o], preferred_element_type=F32)
            if a > 0:
                hr, hi = _cmul(zr_ref[:, cols], zi_ref[:, cols], a_r[:, cols], a_i[:, cols])
                pr, pi = hr + pr, hi + pi
            zr_ref[:, cols] = pr
            zi_ref[:, cols] = pi
            if a < N_SUB - 1:
                locr_ref[a, :, cols] = pr.astype(BF16)
                loci_ref[a, :, cols] = pi.astype(BF16)

    row = lax.broadcasted_iota(jnp.int32, (rows, 1), 0)
    in_group = row & (SCAN_GROUP - 1)
    sr, si = zr_ref[...], zi_ref[...]
    sh = 1
    while sh < SCAN_GROUP:
        keep = in_group >= sh
        pr_, pi_ = lvl_ref[0, sh - 1:sh, :], lvl_ref[1, sh - 1:sh, :]
        tr = jnp.where(keep, pltpu.roll(sr, sh, axis=0), 0.0)
        ti = jnp.where(keep, pltpu.roll(si, sh, axis=0), 0.0)
        mr, mi = _cmul(tr, ti, pr_, pi_)
        sr, si = sr + mr, si + mi
        sh *= 2
    c_r, c_i = carry_ref[0:1, :], carry_ref[1:2, :]
    pw_r, pw_i = lvl_ref[0], lvl_ref[1]
    last_r, last_i = c_r, c_i
    for g in range(rows // SCAN_GROUP):
        grp = slice(g * SCAN_GROUP, (g + 1) * SCAN_GROUP)
        ar, ai = _cmul(jnp.broadcast_to(last_r, pw_r.shape), jnp.broadcast_to(last_i, pw_i.shape), pw_r, pw_i)
        gr, gi = sr[grp, :] + ar, si[grp, :] + ai
        zr_ref[grp, :] = gr
        zi_ref[grp, :] = gi
        last_r, last_i = gr[SCAN_GROUP - 1:SCAN_GROUP, :], gi[SCAN_GROUP - 1:SCAN_GROUP, :]
    carry_ref[0:1, :] = last_r
    carry_ref[1:2, :] = last_i
    xr = jnp.where(row == 0, c_r, pltpu.roll(zr_ref[...], 1, axis=0))
    xi = jnp.where(row == 0, c_i, pltpu.roll(zi_ref[...], 1, axis=0))
    zr_ref[...] = xr
    zi_ref[...] = xi

    for a in range(N_SUB):
        if a > 0:
            nr, ni = _cmul(zr_ref[...], zi_ref[...], a_r, a_i)
            zr_ref[...] = nr
            zi_ref[...] = ni
        if a == 0:
            xr_b = zr_ref[...].astype(BF16)
            xi_b = zi_ref[...].astype(BF16)
        else:
            xr_b = (zr_ref[...] + locr_ref[a - 1].astype(F32)).astype(BF16)
            xi_b = (zi_ref[...] + loci_ref[a - 1].astype(F32)).astype(BF16)
        tiles = []
        for o in range(N_OCT):
            cols = slice(o * oct_state, (o + 1) * oct_state)
            acc = jnp.dot(xr_b[:, cols], er_ref[o], preferred_element_type=F32)
            acc = acc + jnp.dot(xi_b[:, cols], ei_ref[o], preferred_element_type=F32)
            tiles.append(acc + jnp.dot(u_tile(a, o), mt_ref[o], preferred_element_type=F32))
        for b in range(SUB):
            t = a * SUB + b
            y = jnp.concatenate([tl[:, b * oct_lanes:(b + 1) * oct_lanes] for tl in tiles], axis=1)
            z = 0.5 * y * (1.0 + jnp.tanh(math.sqrt(2.0 / math.pi) * (y + 0.044715 * (y * y * y))))
            gate = jnp.dot(z.astype(BF16), gluw_ref[...], preferred_element_type=F32) + glub_ref[...]
            o = z * (1.0 / (1.0 + jnp.exp(-gate)))
            ms = jnp.mean(o * o, axis=-1, keepdims=True)
            out_ref[:, t * width:(t + 1) * width] = (o * lax.rsqrt(ms + EPS) * gout_ref[...]).astype(BF16)


def _s5(uv, w, glu_w, glu_b, g_out):
    n = uv.shape[0]
    rows = S5_ROWS
    tile = pl.BlockSpec((rows, CHUNK * SSM_WIDTH), lambda i: (i, 0))
    return pl.pallas_call(
        _s5_kernel,
        grid=(n // rows,),
        in_specs=[tile] + [_const_spec(a.shape) for a in
                           (w["m_t"], w["w_r"], w["w_i"], w["e_r"], w["e_i"], w["a_sub"], w["levels"],
                            glu_w, glu_b, g_out)],
        out_specs=tile,
        out_shape=jax.ShapeDtypeStruct((n, CHUNK * SSM_WIDTH), BF16),
        scratch_shapes=[pltpu.VMEM((2, STATE_COLS), F32),
                        pltpu.VMEM((rows, STATE_COLS), F32),
                        pltpu.VMEM((rows, STATE_COLS), F32),
                        pltpu.VMEM((N_SUB - 1, rows, STATE_COLS), BF16),
                        pltpu.VMEM((N_SUB - 1, rows, STATE_COLS), BF16)],
        compiler_params=_params(dimension_semantics=("arbitrary",)),
        name="s5",
    )(uv, w["m_t"], w["w_r"], w["w_i"], w["e_r"], w["e_i"], w["a_sub"], w["levels"], glu_w, glu_b, g_out)


def _mlp_kernel(x_ref, o1_ref, o2_ref, o3_ref, l1_ref, l2_ref, l3_ref, ssm_ref, ga_ref, wout_ref,
                g2_ref, wup_ref, wdn_ref, out_ref, slab_ref, slab4_ref, mix_ref):
    rows = PROJ_SUB
    n_sub = x_ref.shape[0] // rows
    n_slabs = ATTN_WIDTH // SLAB
    step = DILATIONS[1]
    n_ff = D_FF // FF_CHUNK

    def regroup(ref, d, sub, buf):
        part = slice(sub * (rows // d), (sub + 1) * (rows // d))
        for s in range(n_slabs):
            t, lanes = _slab_home(s)
            if ref is ssm_ref:
                piece = lambda r: ref[part, r * SSM_WIDTH + s * SLAB:r * SSM_WIDTH + (s + 1) * SLAB]
            else:
                piece = lambda r: ref[t, r, part, lanes]
            if d == step:
                for r in range(d):
                    slab_ref[sub, buf, s, pl.ds(r, rows // d, stride=d), :] = piece(r).astype(F32)
            else:
                for r4 in range(step):
                    for j4 in range(step):
                        slab4_ref[sub, buf - 2, s, r4, pl.ds(j4, rows // d, stride=step), :] = (
                            piece(step * j4 + r4).astype(F32))
                for r4 in range(step):
                    slab_ref[sub, buf, s, pl.ds(r4, rows // step, stride=step), :] = slab4_ref[sub, buf - 2, s, r4]

    def in_order(sub, buf):
        return jnp.concatenate([slab_ref[sub, buf, s] for s in range(n_slabs)], axis=1)

    def merge_slab(sub, s):
        t, lanes = _slab_home(s)
        part = slice(sub * rows, (sub + 1) * rows)
        l1, l2, l3 = l1_ref[t, 0, part, lanes], slab_ref[sub, 1, s], slab_ref[sub, 3, s]
        m = jnp.maximum(jnp.maximum(l1, l2), l3)
        w1, w2, w3 = jnp.exp(l1 - m), jnp.exp(l2 - m), jnp.exp(l3 - m)
        num = w1 * o1_ref[t, 0, part, lanes].astype(F32) + w2 * slab_ref[sub, 0, s] + w3 * slab_ref[sub, 2, s]
        slab_ref[sub, 0, s] = num / (w1 + w2 + w3)

    def merge_finish(sub):
        attn = in_order(sub, 0)
        ms = jnp.mean(attn * attn, axis=-1, keepdims=True)
        mix_ref[sub, :, 0:ATTN_WIDTH] = (attn * lax.rsqrt(ms + EPS) * ga_ref[...]).astype(BF16)
        mix_ref[sub, :, ATTN_WIDTH:] = in_order(sub, 4).astype(BF16)

    def prologue(sub):
        return ([lambda: regroup(o2_ref, step, sub, 0), lambda: regroup(l2_ref, step, sub, 1),
                 lambda: regroup(o3_ref, CHUNK, sub, 2), lambda: regroup(l3_ref, CHUNK, sub, 3),
                 lambda: regroup(ssm_ref, CHUNK, sub, 4)]
                + [lambda s=s: merge_slab(sub, s) for s in range(n_slabs)] + [lambda: merge_finish(sub)])

    def main(sub):
        part = slice(sub * rows, (sub + 1) * rows)
        st = {}

        def out_proj():
            x1 = x_ref[part, :] + jnp.dot(mix_ref[sub], wout_ref[...], preferred_element_type=F32)
            ms2 = jnp.mean(x1 * x1, axis=-1, keepdims=True)
            st["xn"] = (x1 * lax.rsqrt(ms2 + EPS) * g2_ref[...]).astype(BF16)
            st["acc"] = x1

        def up(c):
            st[c] = jnp.dot(st["xn"], wup_ref[:, c * FF_CHUNK:(c + 1) * FF_CHUNK], preferred_element_type=F32)

        def down(c):
            h = jnp.square(jnp.maximum(st.pop(c), 0.0)).astype(BF16)
            st["acc"] = st["acc"] + jnp.dot(h, wdn_ref[c * FF_CHUNK:(c + 1) * FF_CHUNK, :],
                                            preferred_element_type=F32)

        def finish():
            out_ref[part, :] = st["acc"]

        stages = [out_proj, lambda: up(0)]
        for c in range(n_ff):
            if c + 1 < n_ff:
                stages.append(lambda c=c: up(c + 1))
            stages.append(lambda c=c: down(c))
        return stages + [finish]

    tail = 3
    program = prologue(0)
    all_stages = [main(sub) for sub in range(n_sub)]
    for sub in range(n_sub):
        stages = all_stages[sub]
        head = stages if sub == 0 else stages[1:]
        if sub + 1 < n_sub:
            side = prologue(sub + 1)
            body = []
            for stage in head[:-tail]:
                body.append(stage)
                if side:
                    body.append(side.pop(0))
            program += body + side + [all_stages[sub + 1][0]] + head[-tail:]
        else:
            program += head
    for piece in program:
        piece()


def _mlp(x, os_, ls_, ssm, ga, w_out, g2, w_up, w_dn):
    s = x.shape[0]
    rows = MLP_ROWS
    xt = pl.BlockSpec((rows, D_MODEL), lambda i: (i, 0))
    ht = [pl.BlockSpec((N_HEAD_TILES, d, rows // d, LANE_TILE), lambda i: (0, 0, i, 0)) for d in DILATIONS]
    return pl.pallas_call(
        _mlp_kernel,
        grid=(s // rows,),
        in_specs=[xt, *ht, *ht, pl.BlockSpec((rows // CHUNK, CHUNK * SSM_WIDTH), lambda i: (i, 0)),
                  _const_spec(ga.shape), _const_spec(w_out.shape), _const_spec(g2.shape),
                  _const_spec(w_up.shape), _const_spec(w_dn.shape)],
        out_specs=xt,
        out_shape=jax.ShapeDtypeStruct((s, D_MODEL), F32),
        scratch_shapes=[pltpu.VMEM((rows // PROJ_SUB, 5, ATTN_WIDTH // SLAB, PROJ_SUB, SLAB), F32),
                        pltpu.VMEM((rows // PROJ_SUB, 3, ATTN_WIDTH // SLAB, DILATIONS[1],
                                    PROJ_SUB // DILATIONS[1], SLAB), F32),
                        pltpu.VMEM((rows // PROJ_SUB, PROJ_SUB, D_MODEL), BF16)],
        compiler_params=_params(dimension_semantics=("arbitrary",)),
        name="mlp",
    )(x, *os_, *ls_, ssm, ga, w_out, g2, w_up, w_dn)


def _layer(x, norm1_g, w_in, q_norm_g, k_norm_g, ssm_a_re, ssm_a_im, ssm_log_dt, ssm_b_re, ssm_b_im,
           ssm_c_re, ssm_c_im, ssm_d, glu_w, glu_b, attn_out_norm_g, ssm_out_norm_g, w_out, norm2_g,
           w_mlp_up, w_mlp_down):
    row = lambda g: g.astype(F32).reshape(1, -1)
    heads = ATTN_WIDTH // HEAD_DIM
    head_id = jnp.arange(LANE_TILE) // HEAD_DIM
    hmat = jnp.where(head_id[:, None] == head_id[None, :], 1.0 / HEAD_DIM, 0.0).astype(BF16)
    *qkv, u = _proj(x, row(norm1_g), w_in.astype(F32), hmat,
                    row(jnp.tile(q_norm_g, heads)), row(jnp.tile(k_norm_g, heads)))
    outs, lses = [], []
    mlp_weights = (w_out.astype(F32), w_mlp_up.astype(F32), w_mlp_down.astype(F32))
    for i in range(len(DILATIONS)):
        o, lse, cast = _attn_pattern(*qkv[3 * i:3 * i + 3], cast=mlp_weights if i == 0 else ())
        outs.append(o)
        lses.append(lse)
        if i == 0:
            w_out_b, w_up_b, w_dn_b = cast
    w = _s5_weights(ssm_a_re, ssm_a_im, ssm_log_dt, ssm_b_re, ssm_b_im, ssm_c_re, ssm_c_im, ssm_d)
    ssm = _s5(u, w, glu_w.astype(BF16), row(glu_b), row(ssm_out_norm_g))
    return _mlp(x, outs, lses, ssm, row(attn_out_norm_g), w_out_b, row(norm2_g), w_up_b, w_dn_b)


def kernel(x, norm1_g, w_in, q_norm_g, k_norm_g, ssm_a_re, ssm_a_im, ssm_log_dt, ssm_b_re, ssm_b_im,
           ssm_c_re, ssm_c_im, ssm_d, glu_w, glu_b, attn_out_norm_g, ssm_out_norm_g, w_out, norm2_g,
           w_mlp_up, w_mlp_down):
    params = (norm1_g, w_in, q_norm_g, k_norm_g, ssm_a_re, ssm_a_im, ssm_log_dt, ssm_b_re, ssm_b_im,
              ssm_c_re, ssm_c_im, ssm_d, glu_w, glu_b, attn_out_norm_g, ssm_out_norm_g, w_out, norm2_g,
              w_mlp_up, w_mlp_down)
    batch = x.shape[0]
    outs = []
    for bi in range(batch):
        h = x[bi]
        for layer in range(norm1_g.shape[0]):
            h = _layer(h, *[p[layer] for p in params])
        outs.append(h)
    return jnp.stack(outs)
```

```python
import functools
import math

import jax
import jax.numpy as jnp
from jax import lax
from jax.experimental import pallas as pl
from jax.experimental.pallas import tpu as pltpu

F32 = jnp.float32
BF16 = jnp.bfloat16

D_MODEL = 1024
ATTN_WIDTH = 512
HEAD_DIM = 64
SSM_WIDTH = 512
SSM_GROUP = 16
SSM_GROUPS = 32
SSM_STATE = 64
D_FF = 4096
EPS = 1e-6
NEG_INF = -1e30
DILATIONS = (1, 4, 16)
BLOCK = 128

SLAB = 128
LANE_TILE = 256
HEADS_PER_TILE = LANE_TILE // HEAD_DIM
N_HEAD_TILES = ATTN_WIDTH // LANE_TILE
CHUNK = 16
SUB = 2
N_SUB = CHUNK // SUB
OCT = 8
N_OCT = SSM_GROUPS // OCT
SCAN_GROUP = 8
STATE_COLS = SSM_GROUPS * SSM_STATE
VMEM_LIMIT = 56 * 1024 * 1024

LOG2E = math.log2(math.e)
Q_SCALE = HEAD_DIM ** -0.5 * LOG2E

PROJ_ROWS = 1024
PROJ_SUB = 256
SLAB_BUFFERS = 2
PROJ_AHEAD = 2
ATTN_ROWS = 4096
S5_ROWS = 128
MLP_ROWS = 512
FF_CHUNK = 512


def _const_spec(shape):
    nd = len(shape)
    return pl.BlockSpec(shape, lambda *_: (0,) * nd, pipeline_mode=pl.Buffered(1))


def _params(**kw):
    return pltpu.CompilerParams(vmem_limit_bytes=VMEM_LIMIT, **kw)


def _proj_kernel(x_ref, g1_ref, w_ref, hmat_ref, gq_ref, gk_ref,
                 q1_ref, k1_ref, v1_ref, q4_ref, k4_ref, v4_ref, q16_ref, k16_ref, v16_ref, u_ref,
                 slab_ref, slab4_ref, wb_ref):
    @pl.when(pl.program_id(0) == 0)
    def _():
        wb_ref[...] = w_ref[...].astype(BF16)

    outs = ((q1_ref, q4_ref, q16_ref), (k1_ref, k4_ref, k16_ref), (v1_ref, v4_ref, v16_ref))
    slabs_per_tensor = ATTN_WIDTH // SLAB
    step = DILATIONS[1]
    rows = PROJ_SUB
    n_sub = x_ref.shape[0] // rows
    n_chunks = w_ref.shape[1] // LANE_TILE
    chunks_per_tensor = ATTN_WIDTH // LANE_TILE
    gains = (gq_ref, gk_ref)
    xn = [None] * n_sub

    def prologue(sub):
        x = x_ref[sub * rows:(sub + 1) * rows, :]
        ms = jnp.mean(x * x, axis=-1, keepdims=True)
        xn[sub] = (x * lax.rsqrt(ms + EPS) * g1_ref[...]).astype(BF16)

    def matmul(sub, c):
        return jnp.dot(xn[sub], wb_ref[:, c * LANE_TILE:(c + 1) * LANE_TILE], preferred_element_type=F32)

    def epilogue(sub, c, val):
        ti, t = divmod(c, chunks_per_tensor)
        lane_tile = slice(t * LANE_TILE, (t + 1) * LANE_TILE)
        if ti < 2:
            ms_h = jnp.dot((val * val).astype(BF16), hmat_ref[...], preferred_element_type=F32)
            val = val * lax.rsqrt(ms_h + EPS) * gains[ti][:, lane_tile]
            if ti == 0:
                val = val * Q_SCALE
        if ti < 3:
            outs[ti][0][t, 0, sub * rows:(sub + 1) * rows, :] = val.astype(BF16)
        r4_rows = slice(sub * (rows // step), (sub + 1) * (rows // step))
        r16_rows = slice(sub * (rows // CHUNK), (sub + 1) * (rows // CHUNK))
        for half in range(LANE_TILE // SLAB):
            s = t * (LANE_TILE // SLAB) + half
            lanes = slice(half * SLAB, (half + 1) * SLAB)
            slab = ti * slabs_per_tensor + s
            buf = sub % SLAB_BUFFERS
            slab_ref[buf, slab] = val[:, lanes]
            for r4 in range(step):
                part = slab_ref[buf, slab, pl.ds(r4, rows // step, stride=step), :]
                if ti < 3:
                    outs[ti][1][t, r4, r4_rows, lanes] = part.astype(BF16)
                slab4_ref[buf, slab, r4] = part
            for r4 in range(step):
                for j4 in range(step):
                    r16 = step * j4 + r4
                    part = slab4_ref[buf, slab, r4, pl.ds(j4, rows // CHUNK, stride=step), :].astype(BF16)
                    if ti < 3:
                        outs[ti][2][t, r16, r16_rows, lanes] = part
                    else:
                        u_ref[r16_rows, r16 * SSM_WIDTH + s * SLAB:r16 * SSM_WIDTH + (s + 1) * SLAB] = part

    work = [(sub, c) for sub in range(n_sub) for c in range(n_chunks)]
    prologue(0)
    pending = []
    for sub, c in work:
        pending.append((sub, c, matmul(sub, c)))
        if len(pending) > PROJ_AHEAD:
            epilogue(*pending.pop(0))
        if c == n_chunks // 2 and sub + 1 < n_sub:
            prologue(sub + 1)
    for item in pending:
        epilogue(*item)


def _slab_home(s):
    t, half = divmod(s, LANE_TILE // SLAB)
    return t, slice(half * SLAB, (half + 1) * SLAB)


def _proj(x, g1, w_in, hmat, gq, gk):
    s = x.shape[0]
    rows = PROJ_ROWS
    out_specs, out_shape = [], []
    for d in DILATIONS:
        for _ in range(3):
            out_specs.append(pl.BlockSpec((N_HEAD_TILES, d, rows // d, LANE_TILE), lambda i: (0, 0, i, 0)))
            out_shape.append(jax.ShapeDtypeStruct((N_HEAD_TILES, d, s // d, LANE_TILE), BF16))
    out_specs.append(pl.BlockSpec((rows // CHUNK, CHUNK * SSM_WIDTH), lambda i: (i, 0)))
    out_shape.append(jax.ShapeDtypeStruct((s // CHUNK, CHUNK * SSM_WIDTH), BF16))
    return pl.pallas_call(
        _proj_kernel,
        grid=(s // rows,),
        in_specs=[
            pl.BlockSpec((rows, D_MODEL), lambda i: (i, 0)),
            _const_spec((1, D_MODEL)),
            _const_spec(w_in.shape),
            _const_spec(hmat.shape),
            _const_spec((1, ATTN_WIDTH)),
            _const_spec((1, ATTN_WIDTH)),
        ],
        out_specs=out_specs,
        out_shape=out_shape,
        scratch_shapes=[pltpu.VMEM((SLAB_BUFFERS, 4 * ATTN_WIDTH // SLAB, PROJ_SUB, SLAB), F32),
                        pltpu.VMEM((SLAB_BUFFERS, 4 * ATTN_WIDTH // SLAB, DILATIONS[1],
                                    PROJ_SUB // DILATIONS[1], SLAB), F32),
                        pltpu.VMEM(w_in.shape, BF16)],
        compiler_params=_params(dimension_semantics=("arbitrary",)),
        name="proj",
    )(x, g1, w_in, hmat, gq, gk)


def _attn_kernel(q_ref, kc_ref, kp_ref, vc_ref, vp_ref, *rest):
    n_cast = (len(rest) - 4) // 2
    cast_in, (o_ref, lse_ref), cast_out = rest[:n_cast], rest[n_cast:n_cast + 2], rest[n_cast + 2:2 * n_cast + 2]
    kbuf, vbuf = rest[2 * n_cast + 2:]
    for src, dst in zip(cast_in, cast_out):
        dst[...] = src[...].astype(BF16)
    n_res, rows = q_ref.shape[0], q_ref.shape[1]
    kbuf[:, 0:BLOCK, :] = kp_ref[...]
    kbuf[:, BLOCK:, :] = kc_ref[...]
    vbuf[:, 0:BLOCK, :] = vp_ref[...]
    vbuf[:, BLOCK:, :] = vc_ref[...]

    lane = lax.broadcasted_iota(jnp.int32, (BLOCK, LANE_TILE), 1)
    head_masks = [(lane >= h * HEAD_DIM) & (lane < (h + 1) * HEAD_DIM) for h in range(HEADS_PER_TILE)]
    qi = lax.broadcasted_iota(jnp.int32, (BLOCK, 2 * BLOCK), 0)
    ki = lax.broadcasted_iota(jnp.int32, (BLOCK, 2 * BLOCK), 1)
    band = (ki >= qi) & (ki <= qi + BLOCK)
    bias_band = jnp.where(band, 0.0, NEG_INF).astype(F32)
    no_prev = pl.program_id(2) == 0
    bias_first = jnp.where(band & ((ki >= BLOCK) | jnp.logical_not(no_prev)), 0.0, NEG_INF).astype(F32)

    for r, b in [(r, b) for r in range(n_res) for b in range(rows // BLOCK)]:
        qb = q_ref[r, b * BLOCK:(b + 1) * BLOCK, :]
        zero = jnp.zeros_like(qb)
        q_stack = jnp.concatenate([jnp.where(mk, qb, zero) for mk in head_masks], axis=0)
        kw = kbuf[r, b * BLOCK:(b + 2) * BLOCK, :]
        vw = vbuf[r, b * BLOCK:(b + 2) * BLOCK, :]
        s = lax.dot_general(q_stack, kw, (((1,), (1,)), ((), ())), preferred_element_type=F32)
        bias = bias_first if b == 0 else bias_band
        ps, ms, ls = [], [], []
        for h in range(HEADS_PER_TILE):
            sh = s[h * BLOCK:(h + 1) * BLOCK, :] + bias
            m = jnp.max(sh, axis=-1, keepdims=True)
            p = jnp.exp2(sh - m)
            ls.append(jnp.sum(p, axis=-1, keepdims=True))
            ms.append(m)
            ps.append(p.astype(BF16))
        pv = jnp.dot(jnp.concatenate(ps, axis=0), vw, preferred_element_type=F32)
        o = jnp.zeros((BLOCK, LANE_TILE), F32)
        lse = jnp.zeros((BLOCK, LANE_TILE), F32)
        for h in range(HEADS_PER_TILE):
            o = jnp.where(head_masks[h], pv[h * BLOCK:(h + 1) * BLOCK, :] / ls[h], o)
            lse = jnp.where(head_masks[h], ms[h] + jnp.log2(ls[h]), lse)
        o_ref[r, b * BLOCK:(b + 1) * BLOCK, :] = o.astype(BF16)
        lse_ref[r, b * BLOCK:(b + 1) * BLOCK, :] = lse


def _attn_pattern(q, k, v, cast=()):
    nt, dilation, n, _ = q.shape
    rows = min(ATTN_ROWS, n)
    n_res = min(ATTN_ROWS // rows, dilation)
    blocks_per_tile = rows // BLOCK
    grid = (nt, dilation // n_res, n // rows)
    n_steps = grid[0] * grid[1] * grid[2]
    cur = pl.BlockSpec((None, n_res, rows, LANE_TILE), lambda t, r, j: (t, r, j, 0))
    prev = pl.BlockSpec((None, n_res, BLOCK, LANE_TILE),
                        lambda t, r, j: (t, r, jnp.maximum(j * blocks_per_tile - 1, 0), 0))
    step_id = lambda t, r, j: ((t * grid[1] + r) * grid[2] + j, 0)
    cast_specs = [pl.BlockSpec((w.shape[0] // n_steps, w.shape[1]), step_id) for w in cast]
    outs = pl.pallas_call(
        _attn_kernel,
        grid=grid,
        in_specs=[cur, cur, prev, cur, prev] + cast_specs,
        out_specs=[cur, cur] + cast_specs,
        out_shape=[jax.ShapeDtypeStruct(q.shape, BF16), jax.ShapeDtypeStruct(q.shape, F32)]
        + [jax.ShapeDtypeStruct(w.shape, BF16) for w in cast],
        scratch_shapes=[pltpu.VMEM((n_res, rows + BLOCK, LANE_TILE), BF16),
                        pltpu.VMEM((n_res, rows + BLOCK, LANE_TILE), BF16)],
        compiler_params=_params(dimension_semantics=("arbitrary", "arbitrary", "arbitrary")),
        name=f"attn_d{dilation}",
    )(q, k, k, v, v, *cast)
    return outs[0], outs[1], outs[2:]


def _s5_weights(a_re, a_im, log_dt, b_re, b_im, c_re, c_im, d_skip):
    p, c = SSM_STATE, SSM_GROUP
    lr, li = a_re.astype(F32), a_im.astype(F32)
    dt = jnp.exp(log_dt.astype(F32))[:, None]

    def apow(j):
        mag = jnp.exp(lr * dt * j)
        return mag * jnp.cos(li * dt * j), mag * jnp.sin(li * dt * j)

    ab_r, ab_i = apow(1.0)
    den = lr * lr + li * li
    nr, ni = ab_r - 1.0, ab_i
    cr = (nr * lr + ni * li) / den
    ci = (ni * lr - nr * li) / den
    br, bi = b_re.astype(F32), b_im.astype(F32)
    bb_r = cr[..., None] * br - ci[..., None] * bi
    bb_i = cr[..., None] * bi + ci[..., None] * br
    cre, cim = c_re.astype(F32), c_im.astype(F32)

    def apow_many(js):
        j = jnp.asarray(js, F32)[:, None, None]
        mag = jnp.exp(lr * dt * j)
        return mag * jnp.cos(li * dt * j), mag * jnp.sin(li * dt * j)

    pr, pi = apow_many(range(SUB + 1))
    wp_r = jnp.stack([pr[SUB - 1 - t] for t in range(SUB)])
    wp_i = jnp.stack([pi[SUB - 1 - t] for t in range(SUB)])
    win_r = wp_r[..., None] * bb_r[None] - wp_i[..., None] * bb_i[None]
    win_i = wp_r[..., None] * bb_i[None] + wp_i[..., None] * bb_r[None]
    pad_lanes = lambda w: jnp.pad(w, [(0, 0)] * (w.ndim - 1) + [(0, SLAB - w.shape[-1])])
    to_in = lambda w: pad_lanes(w.reshape(SUB, N_OCT, OCT, p, c).transpose(1, 0, 2, 4, 3)
                                .reshape(N_OCT, SUB * OCT * c, p))
    out_r = cre[None] * pr[:, :, None, :] - cim[None] * pi[:, :, None, :]
    out_i = -(cre[None] * pi[:, :, None, :] + cim[None] * pr[:, :, None, :])
    to_out = lambda w: pad_lanes(w.reshape(SUB + 1, N_OCT, OCT, c, p).transpose(0, 1, 2, 4, 3)
                                 .reshape(SUB + 1, N_OCT, OCT * p, c))

    flat = lambda re, im: jnp.stack([re.reshape(-1), im.reshape(-1)])
    top_r, top_i = apow_many([CHUNK])
    lam = flat(top_r[0], top_i[0])
    levels = [lam]
    for _ in range(SCAN_GROUP - 1):
        lr_, li_ = levels[-1][0], levels[-1][1]
        levels.append(jnp.stack([lr_ * lam[0] - li_ * lam[1], lr_ * lam[1] + li_ * lam[0]]))
    adv_r, adv_i = apow_many([SUB * a for a in range(N_SUB)])
    to_cols = lambda w: pad_lanes(w.reshape(N_SUB, N_OCT, OCT * p).transpose(1, 2, 0))
    m_t, w_r, w_i, e_r, e_i = _s5_expand(to_in(win_r), to_in(win_i), to_out(out_r), to_out(out_i),
                                         d_skip.astype(F32).reshape(N_OCT, 1, OCT * c),
                                         to_cols(adv_r), to_cols(adv_i))
    return dict(m_t=m_t, w_r=w_r, w_i=w_i, e_r=e_r, e_i=e_i,
                a_sub=flat(pr[SUB], pi[SUB]), levels=jnp.stack(levels, axis=1))


def _s5_expand_kernel(wr_ref, wi_ref, or_ref, oi_ref, d_ref, advr_ref, advi_ref,
                      mt_ref, wfr_ref, wfi_ref, efr_ref, efi_ref):
    p, c = SSM_STATE, SSM_GROUP
    n_state, n_lane = OCT * p, OCT * c

    def rep(inner, total):
        r = lax.broadcasted_iota(jnp.int32, (SLAB, total), 0)
        col = lax.broadcasted_iota(jnp.int32, (SLAB, total), 1)
        return jnp.where((col & (inner - 1)) == r, 1.0, 0.0).astype(BF16)

    shift = lambda n: int(math.log2(n))
    rep_in, rep_out = rep(p, n_state), rep(c, n_lane)
    in_row = lax.broadcasted_iota(jnp.int32, (SUB * n_lane, n_state), 0)
    in_col = lax.broadcasted_iota(jnp.int32, (SUB * n_lane, n_state), 1)
    mask_in = ((in_row >> shift(c)) & (OCT - 1)) == (in_col >> shift(p))
    out_row = lax.broadcasted_iota(jnp.int32, (n_state, n_lane), 0)
    out_col = lax.broadcasted_iota(jnp.int32, (n_state, n_lane), 1)
    mask_out = (out_row >> shift(p)) == (out_col >> shift(c))
    diag = (lax.broadcasted_iota(jnp.int32, (n_lane, n_lane), 0)
            == lax.broadcasted_iota(jnp.int32, (n_lane, n_lane), 1))

    def split(x):
        hi = x.astype(BF16)
        return hi, (x - hi.astype(F32)).astype(BF16)

    def expand(x_b, rep_m, mask):
        full = jnp.dot(x_b, rep_m, preferred_element_type=F32)
        return jnp.where(mask, full, 0.0).astype(BF16)

    last = slice((SUB - 1) * n_lane, SUB * n_lane)
    for o in range(N_OCT):
        in0 = []
        for src, dst in ((wr_ref, wfr_ref), (wi_ref, wfi_ref)):
            hi, lo = split(src[o])
            full = expand(hi, rep_in, mask_in)
            dst[o] = full
            in0.append((full[last, :], expand(lo, rep_in, mask_in)[last, :]))
        lags = [None] * SUB
        base = []
        for src, (in_hi, in_lo) in zip((or_ref, oi_ref), in0):
            outs_hi = []
            for j in range(SUB + 1):
                hi, lo = split(src[j, o])
                out_hi = expand(hi, rep_out, mask_out)
                outs_hi.append(out_hi)
                if j < SUB:
                    out_lo = expand(lo, rep_out, mask_out)
                    part = (jnp.dot(in_hi, out_hi, preferred_element_type=F32)
                            + jnp.dot(in_hi, out_lo, preferred_element_type=F32)
                            + jnp.dot(in_lo, out_hi, preferred_element_type=F32))
                    lags[j] = part if lags[j] is None else lags[j] + part
            base.append(jnp.concatenate(outs_hi[1:], axis=1).astype(F32))
        for a in range(N_SUB):
            p_r, p_i = advr_ref[o][:, a:a + 1], advi_ref[o][:, a:a + 1]
            efr_ref[a, o] = (base[0] * p_r + base[1] * p_i).astype(BF16)
            efi_ref[a, o] = (base[1] * p_r - base[0] * p_i).astype(BF16)
        lags[0] = lags[0] + jnp.where(diag, d_ref[o], 0.0)
        zero = jnp.zeros_like(lags[0])
        mt_ref[o] = jnp.concatenate(
            [jnp.concatenate([lags[t_out - t_in] if t_out >= t_in else zero for t_out in range(SUB)], axis=1)
             for t_in in range(SUB)], axis=0).astype(BF16)


def _s5_expand(wc_r, wc_i, oc_r, oc_i, d_row, adv_r, adv_i):
    n_in, n_state = wc_r.shape[1], oc_r.shape[2]
    e_shape = jax.ShapeDtypeStruct((N_SUB, N_OCT, n_state, n_in), BF16)
    shapes = [jax.ShapeDtypeStruct((N_OCT, n_in, n_in), BF16),
              jax.ShapeDtypeStruct((N_OCT, n_in, n_state), BF16), jax.ShapeDtypeStruct((N_OCT, n_in, n_state), BF16),
              e_shape, e_shape]
    return pl.pallas_call(
        _s5_expand_kernel,
        out_shape=shapes,
        compiler_params=_params(),
        name="s5_expand",
    )(wc_r, wc_i, oc_r, oc_i, d_row, adv_r, adv_i)


def _cmul(ar, ai, br, bi):
    return ar * br - ai * bi, ar * bi + ai * br


def _s5_kernel(u_ref, mt_ref, wr_ref, wi_ref, er_ref, ei_ref, asub_ref, lvl_ref,
               gluw_ref, glub_ref, gout_ref, out_ref, carry_ref, zr_ref, zi_ref, locr_ref, loci_ref,
               xbr_ref, xbi_ref):
    rows = u_ref.shape[0]
    width = SSM_WIDTH
    oct_lanes = OCT * SSM_GROUP
    oct_state = OCT * SSM_STATE

    @pl.when(pl.program_id(0) == 0)
    def _():
        carry_ref[...] = jnp.zeros_like(carry_ref)

    def u_tile(a, o):
        return jnp.concatenate(
            [u_ref[:, (a * SUB + t) * width + o * oct_lanes:(a * SUB + t) * width + (o + 1) * oct_lanes]
             for t in range(SUB)], axis=1)

    a_r, a_i = asub_ref[0:1, :], asub_ref[1:2, :]

    for a in range(N_SUB):
        for o in range(N_OCT):
            cols = slice(o * oct_state, (o + 1) * oct_state)
            ut = u_tile(a, o)
            pr = jnp.dot(ut, wr_ref[o], preferred_element_type=F32)
            pi = jnp.dot(ut, wi_ref[o], preferred_element_type=F32)
            if a > 0:
                hr, hi = _cmul(zr_ref[:, cols], zi_ref[:, cols], a_r[:, cols], a_i[:, cols])
                pr, pi = hr + pr, hi + pi
            zr_ref[:, cols] = pr
            zi_ref[:, cols] = pi
            if a < N_SUB - 1:
                locr_ref[a, :, cols] = pr.astype(BF16)
                loci_ref[a, :, cols] = pi.astype(BF16)

    row = lax.broadcasted_iota(jnp.int32, (rows, 1), 0)
    in_group = row & (SCAN_GROUP - 1)
    sr, si = zr_ref[...], zi_ref[...]
    sh = 1
    while sh < SCAN_GROUP:
        keep = in_group >= sh
        pr_, pi_ = lvl_ref[0, sh - 1:sh, :], lvl_ref[1, sh - 1:sh, :]
        tr = jnp.where(keep, pltpu.roll(sr, sh, axis=0), 0.0)
        ti = jnp.where(keep, pltpu.roll(si, sh, axis=0), 0.0)
        mr, mi = _cmul(tr, ti, pr_, pi_)
        sr, si = sr + mr, si + mi
        sh *= 2
    c_r, c_i = carry_ref[0:1, :], carry_ref[1:2, :]
    pw_r, pw_i = lvl_ref[0], lvl_ref[1]
    last_r, last_i = c_r, c_i
    for g in range(rows // SCAN_GROUP):
        grp = slice(g * SCAN_GROUP, (g + 1) * SCAN_GROUP)
        ar, ai = _cmul(jnp.broadcast_to(last_r, pw_r.shape), jnp.broadcast_to(last_i, pw_i.shape), pw_r, pw_i)
        gr, gi = sr[grp, :] + ar, si[grp, :] + ai
        zr_ref[grp, :] = gr
        zi_ref[grp, :] = gi
        last_r, last_i = gr[SCAN_GROUP - 1:SCAN_GROUP, :], gi[SCAN_GROUP - 1:SCAN_GROUP, :]
    carry_ref[0:1, :] = last_r
    carry_ref[1:2, :] = last_i
    xr = jnp.where(row == 0, c_r, pltpu.roll(zr_ref[...], 1, axis=0))
    xi = jnp.where(row == 0, c_i, pltpu.roll(zi_ref[...], 1, axis=0))
    xbr_ref[...] = xr.astype(BF16)
    xbi_ref[...] = xi.astype(BF16)

    for a in range(N_SUB):
        tiles = []
        for o in range(N_OCT):
            cols = slice(o * oct_state, (o + 1) * oct_state)
            acc = jnp.dot(xbr_ref[:, cols], er_ref[a, o], preferred_element_type=F32)
            acc = acc + jnp.dot(xbi_ref[:, cols], ei_ref[a, o], preferred_element_type=F32)
            if a > 0:
                acc = acc + jnp.dot(locr_ref[a - 1, :, cols], er_ref[0, o], preferred_element_type=F32)
                acc = acc + jnp.dot(loci_ref[a - 1, :, cols], ei_ref[0, o], preferred_element_type=F32)
            tiles.append(acc + jnp.dot(u_tile(a, o), mt_ref[o], preferred_element_type=F32))
        for b in range(SUB):
            t = a * SUB + b
            y = jnp.concatenate([tl[:, b * oct_lanes:(b + 1) * oct_lanes] for tl in tiles], axis=1)
            z = 0.5 * y * (1.0 + jnp.tanh(math.sqrt(2.0 / math.pi) * (y + 0.044715 * (y * y * y))))
            gate = jnp.dot(z.astype(BF16), gluw_ref[...], preferred_element_type=F32) + glub_ref[...]
            o = z * (1.0 / (1.0 + jnp.exp(-gate)))
            ms = jnp.mean(o * o, axis=-1, keepdims=True)
            out_ref[:, t * width:(t + 1) * width] = (o * lax.rsqrt(ms + EPS) * gout_ref[...]).astype(BF16)


def _s5(uv, w, glu_w, glu_b, g_out):
    n = uv.shape[0]
    rows = S5_ROWS
    tile = pl.BlockSpec((rows, CHUNK * SSM_WIDTH), lambda i: (i, 0))
    return pl.pallas_call(
        _s5_kernel,
        grid=(n // rows,),
        in_specs=[tile] + [_const_spec(a.shape) for a in
                           (w["m_t"], w["w_r"], w["w_i"], w["e_r"], w["e_i"], w["a_sub"], w["levels"],
                            glu_w, glu_b, g_out)],
        out_specs=tile,
        out_shape=jax.ShapeDtypeStruct((n, CHUNK * SSM_WIDTH), BF16),
        scratch_shapes=[pltpu.VMEM((2, STATE_COLS), F32),
                        pltpu.VMEM((rows, STATE_COLS), F32),
                        pltpu.VMEM((rows, STATE_COLS), F32),
                        pltpu.VMEM((N_SUB - 1, rows, STATE_COLS), BF16),
                        pltpu.VMEM((N_SUB - 1, rows, STATE_COLS), BF16),
                        pltpu.VMEM((rows, STATE_COLS), BF16),
                        pltpu.VMEM((rows, STATE_COLS), BF16)],
        compiler_params=_params(dimension_semantics=("arbitrary",)),
        name="s5",
    )(uv, w["m_t"], w["w_r"], w["w_i"], w["e_r"], w["e_i"], w["a_sub"], w["levels"], glu_w, glu_b, g_out)


def _mlp_kernel(x_ref, o1_ref, o2_ref, o3_ref, l1_ref, l2_ref, l3_ref, ssm_ref, ga_ref, wout_ref,
                g2_ref, wup_ref, wdn_ref, out_ref, slab_ref, slab4_ref, mix_ref):
    rows = PROJ_SUB
    n_sub = x_ref.shape[0] // rows
    n_slabs = ATTN_WIDTH // SLAB
    step = DILATIONS[1]
    n_ff = D_FF // FF_CHUNK

    def regroup(ref, d, sub, buf):
        part = slice(sub * (rows // d), (sub + 1) * (rows // d))
        for s in range(n_slabs):
            t, lanes = _slab_home(s)
            if ref is ssm_ref:
                piece = lambda r: ref[part, r * SSM_WIDTH + s * SLAB:r * SSM_WIDTH + (s + 1) * SLAB]
            else:
                piece = lambda r: ref[t, r, part, lanes]
            if d == step:
                for r in range(d):
                    slab_ref[sub, buf, s, pl.ds(r, rows // d, stride=d), :] = piece(r).astype(F32)
            else:
                for r4 in range(step):
                    for j4 in range(step):
                        slab4_ref[sub, buf - 2, s, r4, pl.ds(j4, rows // d, stride=step), :] = (
                            piece(step * j4 + r4).astype(F32))
                for r4 in range(step):
                    slab_ref[sub, buf, s, pl.ds(r4, rows // step, stride=step), :] = slab4_ref[sub, buf - 2, s, r4]

    def in_order(sub, buf):
        return jnp.concatenate([slab_ref[sub, buf, s] for s in range(n_slabs)], axis=1)

    def merge_slab(sub, s):
        t, lanes = _slab_home(s)
        part = slice(sub * rows, (sub + 1) * rows)
        l1, l2, l3 = l1_ref[t, 0, part, lanes], slab_ref[sub, 1, s], slab_ref[sub, 3, s]
        m = jnp.maximum(jnp.maximum(l1, l2), l3)
        w1, w2, w3 = jnp.exp2(l1 - m), jnp.exp2(l2 - m), jnp.exp2(l3 - m)
        num = w1 * o1_ref[t, 0, part, lanes].astype(F32) + w2 * slab_ref[sub, 0, s] + w3 * slab_ref[sub, 2, s]
        slab_ref[sub, 0, s] = num / (w1 + w2 + w3)

    def merge_finish(sub):
        attn = in_order(sub, 0)
        ms = jnp.mean(attn * attn, axis=-1, keepdims=True)
        mix_ref[sub, :, 0:ATTN_WIDTH] = (attn * lax.rsqrt(ms + EPS) * ga_ref[...]).astype(BF16)
        mix_ref[sub, :, ATTN_WIDTH:] = in_order(sub, 4).astype(BF16)

    def prologue(sub):
        return ([lambda: regroup(o2_ref, step, sub, 0), lambda: regroup(l2_ref, step, sub, 1),
                 lambda: regroup(o3_ref, CHUNK, sub, 2), lambda: regroup(l3_ref, CHUNK, sub, 3),
                 lambda: regroup(ssm_ref, CHUNK, sub, 4)]
                + [lambda s=s: merge_slab(sub, s) for s in range(n_slabs)] + [lambda: merge_finish(sub)])

    def main(sub):
        part = slice(sub * rows, (sub + 1) * rows)
        st = {}

        def out_proj():
            x1 = x_ref[part, :] + jnp.dot(mix_ref[sub], wout_ref[...], preferred_element_type=F32)
            ms2 = jnp.mean(x1 * x1, axis=-1, keepdims=True)
            st["xn"] = (x1 * lax.rsqrt(ms2 + EPS) * g2_ref[...]).astype(BF16)
            st["acc"] = x1

        def up(c):
            st[c] = jnp.dot(st["xn"], wup_ref[:, c * FF_CHUNK:(c + 1) * FF_CHUNK], preferred_element_type=F32)

        def down(c):
            h = jnp.square(jnp.maximum(st.pop(c), 0.0)).astype(BF16)
            st["acc"] = st["acc"] + jnp.dot(h, wdn_ref[c * FF_CHUNK:(c + 1) * FF_CHUNK, :],
                                            preferred_element_type=F32)

        def finish():
            out_ref[part, :] = st["acc"]

        stages = [out_proj, lambda: up(0)]
        for c in range(n_ff):
            if c + 1 < n_ff:
                stages.append(lambda c=c: up(c + 1))
            stages.append(lambda c=c: down(c))
        return stages + [finish]

    tail = 3
    program = prologue(0)
    all_stages = [main(sub) for sub in range(n_sub)]
    for sub in range(n_sub):
        stages = all_stages[sub]
        head = stages if sub == 0 else stages[1:]
        if sub + 1 < n_sub:
            side = prologue(sub + 1)
            body = []
            for stage in head[:-tail]:
                body.append(stage)
                if side:
                    body.append(side.pop(0))
            program += body + side + [all_stages[sub + 1][0]] + head[-tail:]
        else:
            program += head
    for piece in program:
        piece()


def _mlp(x, os_, ls_, ssm, ga, w_out, g2, w_up, w_dn):
    s = x.shape[0]
    rows = MLP_ROWS
    xt = pl.BlockSpec((rows, D_MODEL), lambda i: (i, 0))
    ht = [pl.BlockSpec((N_HEAD_TILES, d, rows // d, LANE_TILE), lambda i: (0, 0, i, 0)) for d in DILATIONS]
    return pl.pallas_call(
        _mlp_kernel,
        grid=(s // rows,),
        in_specs=[xt, *ht, *ht, pl.BlockSpec((rows // CHUNK, CHUNK * SSM_WIDTH), lambda i: (i, 0)),
                  _const_spec(ga.shape), _const_spec(w_out.shape), _const_spec(g2.shape),
                  _const_spec(w_up.shape), _const_spec(w_dn.shape)],
        out_specs=xt,
        out_shape=jax.ShapeDtypeStruct((s, D_MODEL), F32),
        scratch_shapes=[pltpu.VMEM((rows // PROJ_SUB, 5, ATTN_WIDTH // SLAB, PROJ_SUB, SLAB), F32),
                        pltpu.VMEM((rows // PROJ_SUB, 3, ATTN_WIDTH // SLAB, DILATIONS[1],
                                    PROJ_SUB // DILATIONS[1], SLAB), F32),
                        pltpu.VMEM((rows // PROJ_SUB, PROJ_SUB, D_MODEL), BF16)],
        compiler_params=_params(dimension_semantics=("arbitrary",)),
        name="mlp",
    )(x, *os_, *ls_, ssm, ga, w_out, g2, w_up, w_dn)


def _layer(x, norm1_g, w_in, q_norm_g, k_norm_g, ssm_a_re, ssm_a_im, ssm_log_dt, ssm_b_re, ssm_b_im,
           ssm_c_re, ssm_c_im, ssm_d, glu_w, glu_b, attn_out_norm_g, ssm_out_norm_g, w_out, norm2_g,
           w_mlp_up, w_mlp_down):
    row = lambda g: g.astype(F32).reshape(1, -1)
    heads = ATTN_WIDTH // HEAD_DIM
    head_id = jnp.arange(LANE_TILE) // HEAD_DIM
    hmat = jnp.where(head_id[:, None] == head_id[None, :], 1.0 / HEAD_DIM, 0.0).astype(BF16)
    *qkv, u = _proj(x, row(norm1_g), w_in.astype(F32), hmat,
                    row(jnp.tile(q_norm_g, heads)), row(jnp.tile(k_norm_g, heads)))
    outs, lses = [], []
    mlp_weights = (w_out.astype(F32), w_mlp_up.astype(F32), w_mlp_down.astype(F32))
    for i in range(len(DILATIONS)):
        o, lse, cast = _attn_pattern(*qkv[3 * i:3 * i + 3], cast=mlp_weights if i == 0 else ())
        outs.append(o)
        lses.append(lse)
        if i == 0:
            w_out_b, w_up_b, w_dn_b = cast
    w = _s5_weights(ssm_a_re, ssm_a_im, ssm_log_dt, ssm_b_re, ssm_b_im, ssm_c_re, ssm_c_im, ssm_d)
    ssm = _s5(u, w, glu_w.astype(BF16), row(glu_b), row(ssm_out_norm_g))
    return _mlp(x, outs, lses, ssm, row(attn_out_norm_g), w_out_b, row(norm2_g), w_up_b, w_dn_b)


def kernel(x, norm1_g, w_in, q_norm_g, k_norm_g, ssm_a_re, ssm_a_im, ssm_log_dt, ssm_b_re, ssm_b_im,
           ssm_c_re, ssm_c_im, ssm_d, glu_w, glu_b, attn_out_norm_g, ssm_out_norm_g, w_out, norm2_g,
           w_mlp_up, w_mlp_down):
    params = (norm1_g, w_in, q_norm_g, k_norm_g, ssm_a_re, ssm_a_im, ssm_log_dt, ssm_b_re, ssm_b_im,
              ssm_c_re, ssm_c_im, ssm_d, glu_w, glu_b, attn_out_norm_g, ssm_out_norm_g, w_out, norm2_g,
              w_mlp_up, w_mlp_down)
    batch = x.shape[0]
    outs = []
    for bi in range(batch):
        h = x[bi]
        for layer in range(norm1_g.shape[0]):
            h = _layer(h, *[p[layer] for p in params])
        outs.append(h)
    return jnp.stack(outs)
```

```python
import functools
import math

import jax
import jax.numpy as jnp
from jax import lax
from jax.experimental import pallas as pl
from jax.experimental.pallas import tpu as pltpu

F32 = jnp.float32
BF16 = jnp.bfloat16

D_MODEL = 1024
ATTN_WIDTH = 512
HEAD_DIM = 64
SSM_WIDTH = 512
SSM_GROUP = 16
SSM_GROUPS = 32
SSM_STATE = 64
D_FF = 4096
EPS = 1e-6
NEG_INF = -1e30
DILATIONS = (1, 4, 16)
BLOCK = 128

SLAB = 128
LANE_TILE = 256
HEADS_PER_TILE = LANE_TILE // HEAD_DIM
N_HEAD_TILES = ATTN_WIDTH // LANE_TILE
CHUNK = 16
SUB = 2
PAIRS = 2
SUB_STEPS = SUB * PAIRS
N_SUB = CHUNK // SUB_STEPS
OCT = 8
N_OCT = SSM_GROUPS // OCT
SCAN_GROUP = 8
STATE_COLS = SSM_GROUPS * SSM_STATE
VMEM_LIMIT = 56 * 1024 * 1024

LOG2E = math.log2(math.e)
Q_SCALE = HEAD_DIM ** -0.5 * LOG2E

PROJ_ROWS = 1024
PROJ_SUB = 256
SLAB_BUFFERS = 2
PROJ_AHEAD = 2
ATTN_ROWS = 4096
S5_ROWS = 256
MLP_ROWS = 512
FF_CHUNK = 512


def _const_spec(shape):
    nd = len(shape)
    return pl.BlockSpec(shape, lambda *_: (0,) * nd, pipeline_mode=pl.Buffered(1))


def _params(**kw):
    return pltpu.CompilerParams(vmem_limit_bytes=VMEM_LIMIT, **kw)


def _proj_kernel(x_ref, g1_ref, w_ref, hmat_ref, gq_ref, gk_ref,
                 q1_ref, k1_ref, v1_ref, q4_ref, k4_ref, v4_ref, q16_ref, k16_ref, v16_ref, u_ref,
                 slab_ref, slab4_ref, wb_ref):
    @pl.when(pl.program_id(0) == 0)
    def _():
        wb_ref[...] = w_ref[...].astype(BF16)

    outs = ((q1_ref, q4_ref, q16_ref), (k1_ref, k4_ref, k16_ref), (v1_ref, v4_ref, v16_ref))
    slabs_per_tensor = ATTN_WIDTH // SLAB
    step = DILATIONS[1]
    rows = PROJ_SUB
    n_sub = x_ref.shape[0] // rows
    n_chunks = w_ref.shape[1] // LANE_TILE
    chunks_per_tensor = ATTN_WIDTH // LANE_TILE
    gains = (gq_ref, gk_ref)
    xn = [None] * n_sub

    def prologue(sub):
        x = x_ref[sub * rows:(sub + 1) * rows, :]
        ms = jnp.mean(x * x, axis=-1, keepdims=True)
        xn[sub] = (x * lax.rsqrt(ms + EPS) * g1_ref[...]).astype(BF16)

    def matmul(sub, c):
        return jnp.dot(xn[sub], wb_ref[:, c * LANE_TILE:(c + 1) * LANE_TILE], preferred_element_type=F32)

    def epilogue(sub, c, val):
        ti, t = divmod(c, chunks_per_tensor)
        lane_tile = slice(t * LANE_TILE, (t + 1) * LANE_TILE)
        if ti < 2:
            ms_h = jnp.dot((val * val).astype(BF16), hmat_ref[...], preferred_element_type=F32)
            val = val * lax.rsqrt(ms_h + EPS) * gains[ti][:, lane_tile]
            if ti == 0:
                val = val * Q_SCALE
        if ti < 3:
            outs[ti][0][t, 0, sub * rows:(sub + 1) * rows, :] = val.astype(BF16)
        r4_rows = slice(sub * (rows // step), (sub + 1) * (rows // step))
        r16_rows = slice(sub * (rows // CHUNK), (sub + 1) * (rows // CHUNK))
        for half in range(LANE_TILE // SLAB):
            s = t * (LANE_TILE // SLAB) + half
            lanes = slice(half * SLAB, (half + 1) * SLAB)
            slab = ti * slabs_per_tensor + s
            buf = sub % SLAB_BUFFERS
            slab_ref[buf, slab] = val[:, lanes]
            for r4 in range(step):
                part = slab_ref[buf, slab, pl.ds(r4, rows // step, stride=step), :]
                if ti < 3:
                    outs[ti][1][t, r4, r4_rows, lanes] = part.astype(BF16)
                slab4_ref[buf, slab, r4] = part
            for r4 in range(step):
                for j4 in range(step):
                    r16 = step * j4 + r4
                    part = slab4_ref[buf, slab, r4, pl.ds(j4, rows // CHUNK, stride=step), :].astype(BF16)
                    if ti < 3:
                        outs[ti][2][t, r16, r16_rows, lanes] = part
                    else:
                        u_ref[r16_rows, r16 * SSM_WIDTH + s * SLAB:r16 * SSM_WIDTH + (s + 1) * SLAB] = part

    work = [(sub, c) for sub in range(n_sub) for c in range(n_chunks)]
    prologue(0)
    pending = []
    for sub, c in work:
        pending.append((sub, c, matmul(sub, c)))
        if len(pending) > PROJ_AHEAD:
            epilogue(*pending.pop(0))
        if c == n_chunks // 2 and sub + 1 < n_sub:
            prologue(sub + 1)
    for item in pending:
        epilogue(*item)


def _slab_home(s):
    t, half = divmod(s, LANE_TILE // SLAB)
    return t, slice(half * SLAB, (half + 1) * SLAB)


def _proj(x, g1, w_in, hmat, gq, gk):
    s = x.shape[0]
    rows = PROJ_ROWS
    out_specs, out_shape = [], []
    for d in DILATIONS:
        for _ in range(3):
            out_specs.append(pl.BlockSpec((N_HEAD_TILES, d, rows // d, LANE_TILE), lambda i: (0, 0, i, 0)))
            out_shape.append(jax.ShapeDtypeStruct((N_HEAD_TILES, d, s // d, LANE_TILE), BF16))
    out_specs.append(pl.BlockSpec((rows // CHUNK, CHUNK * SSM_WIDTH), lambda i: (i, 0)))
    out_shape.append(jax.ShapeDtypeStruct((s // CHUNK, CHUNK * SSM_WIDTH), BF16))
    return pl.pallas_call(
        _proj_kernel,
        grid=(s // rows,),
        in_specs=[
            pl.BlockSpec((rows, D_MODEL), lambda i: (i, 0)),
            _const_spec((1, D_MODEL)),
            _const_spec(w_in.shape),
            _const_spec(hmat.shape),
            _const_spec((1, ATTN_WIDTH)),
            _const_spec((1, ATTN_WIDTH)),
        ],
        out_specs=out_specs,
        out_shape=out_shape,
        scratch_shapes=[pltpu.VMEM((SLAB_BUFFERS, 4 * ATTN_WIDTH // SLAB, PROJ_SUB, SLAB), F32),
                        pltpu.VMEM((SLAB_BUFFERS, 4 * ATTN_WIDTH // SLAB, DILATIONS[1],
                                    PROJ_SUB // DILATIONS[1], SLAB), F32),
                        pltpu.VMEM(w_in.shape, BF16)],
        compiler_params=_params(dimension_semantics=("arbitrary",)),
        name="proj",
    )(x, g1, w_in, hmat, gq, gk)


def _attn_kernel(q_ref, kc_ref, kp_ref, vc_ref, vp_ref, *rest):
    n_cast = (len(rest) - 4) // 2
    cast_in, (o_ref, lse_ref), cast_out = rest[:n_cast], rest[n_cast:n_cast + 2], rest[n_cast + 2:2 * n_cast + 2]
    kbuf, vbuf = rest[2 * n_cast + 2:]
    for src, dst in zip(cast_in, cast_out):
        dst[...] = src[...].astype(BF16)
    n_res, rows = q_ref.shape[0], q_ref.shape[1]
    kbuf[:, 0:BLOCK, :] = kp_ref[...]
    kbuf[:, BLOCK:, :] = kc_ref[...]
    vbuf[:, 0:BLOCK, :] = vp_ref[...]
    vbuf[:, BLOCK:, :] = vc_ref[...]

    lane = lax.broadcasted_iota(jnp.int32, (BLOCK, LANE_TILE), 1)
    head_masks = [(lane >= h * HEAD_DIM) & (lane < (h + 1) * HEAD_DIM) for h in range(HEADS_PER_TILE)]
    qi = lax.broadcasted_iota(jnp.int32, (BLOCK, 2 * BLOCK), 0)
    ki = lax.broadcasted_iota(jnp.int32, (BLOCK, 2 * BLOCK), 1)
    band = (ki >= qi) & (ki <= qi + BLOCK)
    bias_band = jnp.where(band, 0.0, NEG_INF).astype(F32)
    no_prev = pl.program_id(2) == 0
    bias_first = jnp.where(band & ((ki >= BLOCK) | jnp.logical_not(no_prev)), 0.0, NEG_INF).astype(F32)

    for r, b in [(r, b) for r in range(n_res) for b in range(rows // BLOCK)]:
        qb = q_ref[r, b * BLOCK:(b + 1) * BLOCK, :]
        zero = jnp.zeros_like(qb)
        q_stack = jnp.concatenate([jnp.where(mk, qb, zero) for mk in head_masks], axis=0)
        kw = kbuf[r, b * BLOCK:(b + 2) * BLOCK, :]
        vw = vbuf[r, b * BLOCK:(b + 2) * BLOCK, :]
        s = lax.dot_general(q_stack, kw, (((1,), (1,)), ((), ())), preferred_element_type=F32)
        bias = bias_first if b == 0 else bias_band
        ps, ms, ls = [], [], []
        for h in range(HEADS_PER_TILE):
            sh = s[h * BLOCK:(h + 1) * BLOCK, :] + bias
            m = jnp.max(sh, axis=-1, keepdims=True)
            p = jnp.exp2(sh - m)
            ls.append(jnp.sum(p, axis=-1, keepdims=True))
            ms.append(m)
            ps.append(p.astype(BF16))
        pv = jnp.dot(jnp.concatenate(ps, axis=0), vw, preferred_element_type=F32)
        o = jnp.zeros((BLOCK, LANE_TILE), F32)
        lse = jnp.zeros((BLOCK, LANE_TILE), F32)
        for h in range(HEADS_PER_TILE):
            o = jnp.where(head_masks[h], pv[h * BLOCK:(h + 1) * BLOCK, :] / ls[h], o)
            lse = jnp.where(head_masks[h], ms[h] + jnp.log2(ls[h]), lse)
        o_ref[r, b * BLOCK:(b + 1) * BLOCK, :] = o.astype(BF16)
        lse_ref[r, b * BLOCK:(b + 1) * BLOCK, :] = lse


def _attn_pattern(q, k, v, cast=()):
    nt, dilation, n, _ = q.shape
    rows = min(ATTN_ROWS, n)
    n_res = min(ATTN_ROWS // rows, dilation)
    blocks_per_tile = rows // BLOCK
    grid = (nt, dilation // n_res, n // rows)
    n_steps = grid[0] * grid[1] * grid[2]
    cur = pl.BlockSpec((None, n_res, rows, LANE_TILE), lambda t, r, j: (t, r, j, 0))
    prev = pl.BlockSpec((None, n_res, BLOCK, LANE_TILE),
                        lambda t, r, j: (t, r, jnp.maximum(j * blocks_per_tile - 1, 0), 0))
    step_id = lambda t, r, j: ((t * grid[1] + r) * grid[2] + j, 0)
    cast_specs = [pl.BlockSpec((w.shape[0] // n_steps, w.shape[1]), step_id) for w in cast]
    outs = pl.pallas_call(
        _attn_kernel,
        grid=grid,
        in_specs=[cur, cur, prev, cur, prev] + cast_specs,
        out_specs=[cur, cur] + cast_specs,
        out_shape=[jax.ShapeDtypeStruct(q.shape, BF16), jax.ShapeDtypeStruct(q.shape, F32)]
        + [jax.ShapeDtypeStruct(w.shape, BF16) for w in cast],
        scratch_shapes=[pltpu.VMEM((n_res, rows + BLOCK, LANE_TILE), BF16),
                        pltpu.VMEM((n_res, rows + BLOCK, LANE_TILE), BF16)],
        compiler_params=_params(dimension_semantics=("arbitrary", "arbitrary", "arbitrary")),
        name=f"attn_d{dilation}",
    )(q, k, k, v, v, *cast)
    return outs[0], outs[1], outs[2:]


def _s5_weights(a_re, a_im, log_dt, b_re, b_im, c_re, c_im, d_skip):
    p, c = SSM_STATE, SSM_GROUP
    lr, li = a_re.astype(F32), a_im.astype(F32)
    dt = jnp.exp(log_dt.astype(F32))[:, None]

    def apow(j):
        mag = jnp.exp(lr * dt * j)
        return mag * jnp.cos(li * dt * j), mag * jnp.sin(li * dt * j)

    ab_r, ab_i = apow(1.0)
    den = lr * lr + li * li
    nr, ni = ab_r - 1.0, ab_i
    cr = (nr * lr + ni * li) / den
    ci = (ni * lr - nr * li) / den
    br, bi = b_re.astype(F32), b_im.astype(F32)
    bb_r = cr[..., None] * br - ci[..., None] * bi
    bb_i = cr[..., None] * bi + ci[..., None] * br
    cre, cim = c_re.astype(F32), c_im.astype(F32)

    def apow_many(js):
        j = jnp.asarray(js, F32)[:, None, None]
        mag = jnp.exp(lr * dt * j)
        return mag * jnp.cos(li * dt * j), mag * jnp.sin(li * dt * j)

    pr, pi = apow_many(range(SUB_STEPS + 1))
    wp_r = jnp.stack([pr[SUB_STEPS - 1 - s] for s in range(SUB_STEPS)])
    wp_i = jnp.stack([pi[SUB_STEPS - 1 - s] for s in range(SUB_STEPS)])
    win_r = wp_r[..., None] * bb_r[None] - wp_i[..., None] * bb_i[None]
    win_i = wp_r[..., None] * bb_i[None] + wp_i[..., None] * bb_r[None]
    pad_lanes = lambda w: jnp.pad(w, [(0, 0)] * (w.ndim - 1) + [(0, SLAB - w.shape[-1])])
    to_in = lambda w: pad_lanes(w.reshape(PAIRS, SUB, N_OCT, OCT, p, c).transpose(0, 2, 1, 3, 5, 4)
                                .reshape(PAIRS, N_OCT, SUB * OCT * c, p))
    out_r = cre[None] * pr[:, :, None, :] - cim[None] * pi[:, :, None, :]
    out_i = -(cre[None] * pi[:, :, None, :] + cim[None] * pr[:, :, None, :])
    to_out = lambda w: pad_lanes(w.reshape(SUB_STEPS + 1, N_OCT, OCT, c, p).transpose(0, 1, 2, 4, 3)
                                 .reshape(SUB_STEPS + 1, N_OCT, OCT * p, c))

    flat = lambda re, im: jnp.stack([re.reshape(-1), im.reshape(-1)])
    top_r, top_i = apow_many([CHUNK])
    lam = flat(top_r[0], top_i[0])
    levels = [lam]
    for _ in range(SCAN_GROUP - 1):
        lr_, li_ = levels[-1][0], levels[-1][1]
        levels.append(jnp.stack([lr_ * lam[0] - li_ * lam[1], lr_ * lam[1] + li_ * lam[0]]))
    m_t, w_r, w_i, e_r, e_i = _s5_expand(to_in(win_r), to_in(win_i), to_out(out_r), to_out(out_i),
                                         d_skip.astype(F32).reshape(N_OCT, 1, OCT * c))
    return dict(m_t=m_t, w_r=w_r, w_i=w_i, e_r=e_r, e_i=e_i,
                a_sub=flat(pr[SUB_STEPS], pi[SUB_STEPS]),
                levels=jnp.stack(levels, axis=1))


def _s5_expand_kernel(wr_ref, wi_ref, or_ref, oi_ref, d_ref, mt_ref, wfr_ref, wfi_ref, efr_ref, efi_ref):
    p, c = SSM_STATE, SSM_GROUP
    n_state, n_lane = OCT * p, OCT * c

    def rep(inner, total):
        r = lax.broadcasted_iota(jnp.int32, (SLAB, total), 0)
        col = lax.broadcasted_iota(jnp.int32, (SLAB, total), 1)
        return jnp.where((col & (inner - 1)) == r, 1.0, 0.0).astype(BF16)

    shift = lambda n: int(math.log2(n))
    rep_in, rep_out = rep(p, n_state), rep(c, n_lane)
    in_row = lax.broadcasted_iota(jnp.int32, (SUB * n_lane, n_state), 0)
    in_col = lax.broadcasted_iota(jnp.int32, (SUB * n_lane, n_state), 1)
    mask_in = ((in_row >> shift(c)) & (OCT - 1)) == (in_col >> shift(p))
    out_row = lax.broadcasted_iota(jnp.int32, (n_state, n_lane), 0)
    out_col = lax.broadcasted_iota(jnp.int32, (n_state, n_lane), 1)
    mask_out = (out_row >> shift(p)) == (out_col >> shift(c))
    diag = (lax.broadcasted_iota(jnp.int32, (n_lane, n_lane), 0)
            == lax.broadcasted_iota(jnp.int32, (n_lane, n_lane), 1))

    def split(x):
        hi = x.astype(BF16)
        return hi, (x - hi.astype(F32)).astype(BF16)

    def expand(x_b, rep_m, mask):
        full = jnp.dot(x_b, rep_m, preferred_element_type=F32)
        return jnp.where(mask, full, 0.0).astype(BF16)

    last = slice((SUB - 1) * n_lane, SUB * n_lane)
    for o in range(N_OCT):
        in0 = []
        for src, dst in ((wr_ref, wfr_ref), (wi_ref, wfi_ref)):
            for q in range(PAIRS):
                hi, lo = split(src[q, o])
                full = expand(hi, rep_in, mask_in)
                dst[o, q * SUB * n_lane:(q + 1) * SUB * n_lane, :] = full
                if q == PAIRS - 1:
                    in0.append((full[last, :], expand(lo, rep_in, mask_in)[last, :]))
        lags = [None] * SUB_STEPS
        for (src, dst), (in_hi, in_lo) in zip(((or_ref, efr_ref), (oi_ref, efi_ref)), in0):
            outs_hi = []
            for j in range(SUB_STEPS + 1):
                hi, lo = split(src[j, o])
                out_hi = expand(hi, rep_out, mask_out)
                outs_hi.append(out_hi)
                if j < SUB_STEPS:
                    out_lo = expand(lo, rep_out, mask_out)
                    part = (jnp.dot(in_hi, out_hi, preferred_element_type=F32)
                            + jnp.dot(in_hi, out_lo, preferred_element_type=F32)
                            + jnp.dot(in_lo, out_hi, preferred_element_type=F32))
                    lags[j] = part if lags[j] is None else lags[j] + part
            dst[o] = jnp.concatenate(outs_hi[1:], axis=1)
        lags[0] = lags[0] + jnp.where(diag, d_ref[o], 0.0)
        zero = jnp.zeros_like(lags[0])
        for dist in range(PAIRS):
            lag = lambda t_in, t_out: SUB * dist + t_out - t_in
            mt_ref[dist, o] = jnp.concatenate(
                [jnp.concatenate([lags[lag(t_in, t_out)] if lag(t_in, t_out) >= 0 else zero
                                  for t_out in range(SUB)], axis=1)
                 for t_in in range(SUB)], axis=0).astype(BF16)


def _s5_expand(wc_r, wc_i, oc_r, oc_i, d_row):
    n_in, n_state = wc_r.shape[2], oc_r.shape[2]
    shapes = [jax.ShapeDtypeStruct((PAIRS, N_OCT, n_in, n_in), BF16),
              jax.ShapeDtypeStruct((N_OCT, PAIRS * n_in, n_state), BF16),
              jax.ShapeDtypeStruct((N_OCT, PAIRS * n_in, n_state), BF16),
              jax.ShapeDtypeStruct((N_OCT, n_state, PAIRS * n_in), BF16),
              jax.ShapeDtypeStruct((N_OCT, n_state, PAIRS * n_in), BF16)]
    return pl.pallas_call(
        _s5_expand_kernel,
        out_shape=shapes,
        compiler_params=_params(),
        name="s5_expand",
    )(wc_r, wc_i, oc_r, oc_i, d_row)


def _cmul(ar, ai, br, bi):
    return ar * br - ai * bi, ar * bi + ai * br


def _s5_kernel(u_ref, mt_ref, wr_ref, wi_ref, er_ref, ei_ref, asub_ref, lvl_ref,
               gluw_ref, glub_ref, gout_ref, out_ref, carry_ref, zr_ref, zi_ref, locr_ref, loci_ref):
    rows = u_ref.shape[0]
    width = SSM_WIDTH
    oct_lanes = OCT * SSM_GROUP
    oct_state = OCT * SSM_STATE

    @pl.when(pl.program_id(0) == 0)
    def _():
        carry_ref[...] = jnp.zeros_like(carry_ref)

    def u_tile(a, q, o):
        first = a * SUB_STEPS + q * SUB
        return jnp.concatenate(
            [u_ref[:, (first + t) * width + o * oct_lanes:(first + t) * width + (o + 1) * oct_lanes]
             for t in range(SUB)], axis=1)

    a_r, a_i = asub_ref[0:1, :], asub_ref[1:2, :]

    for a in range(N_SUB):
        for o in range(N_OCT):
            cols = slice(o * oct_state, (o + 1) * oct_state)
            ut = jnp.concatenate([u_tile(a, q, o) for q in range(PAIRS)], axis=1)
            pr = jnp.dot(ut, wr_ref[o], preferred_element_type=F32)
            pi = jnp.dot(ut, wi_ref[o], preferred_element_type=F32)
            if a > 0:
                hr, hi = _cmul(zr_ref[:, cols], zi_ref[:, cols], a_r[:, cols], a_i[:, cols])
                pr, pi = hr + pr, hi + pi
            zr_ref[:, cols] = pr
            zi_ref[:, cols] = pi
            if a < N_SUB - 1:
                locr_ref[a, :, cols] = pr.astype(BF16)
                loci_ref[a, :, cols] = pi.astype(BF16)

    row = lax.broadcasted_iota(jnp.int32, (rows, 1), 0)
    in_group = row & (SCAN_GROUP - 1)
    sr, si = zr_ref[...], zi_ref[...]
    sh = 1
    while sh < SCAN_GROUP:
        keep = in_group >= sh
        pr_, pi_ = lvl_ref[0, sh - 1:sh, :], lvl_ref[1, sh - 1:sh, :]
        tr = jnp.where(keep, pltpu.roll(sr, sh, axis=0), 0.0)
        ti = jnp.where(keep, pltpu.roll(si, sh, axis=0), 0.0)
        mr, mi = _cmul(tr, ti, pr_, pi_)
        sr, si = sr + mr, si + mi
        sh *= 2
    c_r, c_i = carry_ref[0:1, :], carry_ref[1:2, :]
    pw_r, pw_i = lvl_ref[0], lvl_ref[1]
    last_r, last_i = c_r, c_i
    for g in range(rows // SCAN_GROUP):
        grp = slice(g * SCAN_GROUP, (g + 1) * SCAN_GROUP)
        ar, ai = _cmul(jnp.broadcast_to(last_r, pw_r.shape), jnp.broadcast_to(last_i, pw_i.shape), pw_r, pw_i)
        gr, gi = sr[grp, :] + ar, si[grp, :] + ai
        zr_ref[grp, :] = gr
        zi_ref[grp, :] = gi
        last_r, last_i = gr[SCAN_GROUP - 1:SCAN_GROUP, :], gi[SCAN_GROUP - 1:SCAN_GROUP, :]
    carry_ref[0:1, :] = last_r
    carry_ref[1:2, :] = last_i
    xr = jnp.where(row == 0, c_r, pltpu.roll(zr_ref[...], 1, axis=0))
    xi = jnp.where(row == 0, c_i, pltpu.roll(zi_ref[...], 1, axis=0))
    zr_ref[...] = xr
    zi_ref[...] = xi

    for a in range(N_SUB):
        if a > 0:
            nr, ni = _cmul(zr_ref[...], zi_ref[...], a_r, a_i)
            zr_ref[...] = nr
            zi_ref[...] = ni
        if a == 0:
            xr_b = zr_ref[...].astype(BF16)
            xi_b = zi_ref[...].astype(BF16)
        else:
            xr_b = (zr_ref[...] + locr_ref[a - 1].astype(F32)).astype(BF16)
            xi_b = (zi_ref[...] + loci_ref[a - 1].astype(F32)).astype(BF16)
        tiles = []
        for o in range(N_OCT):
            cols = slice(o * oct_state, (o + 1) * oct_state)
            from_state = (jnp.dot(xr_b[:, cols], er_ref[o], preferred_element_type=F32)
                          + jnp.dot(xi_b[:, cols], ei_ref[o], preferred_element_type=F32))
            parts = []
            for q in range(PAIRS):
                acc = from_state[:, q * SUB * oct_lanes:(q + 1) * SUB * oct_lanes]
                for q_in in range(q + 1):
                    acc = acc + jnp.dot(u_tile(a, q_in, o), mt_ref[q - q_in, o], preferred_element_type=F32)
                parts.append(acc)
            tiles.append(jnp.concatenate(parts, axis=1))
        for b in range(SUB_STEPS):
            t = a * SUB_STEPS + b
            y = jnp.concatenate([tl[:, b * oct_lanes:(b + 1) * oct_lanes] for tl in tiles], axis=1)
            z = 0.5 * y * (1.0 + jnp.tanh(math.sqrt(2.0 / math.pi) * (y + 0.044715 * (y * y * y))))
            gate = jnp.dot(z.astype(BF16), gluw_ref[...], preferred_element_type=F32) + glub_ref[...]
            o = z * (1.0 / (1.0 + jnp.exp(-gate)))
            ms = jnp.mean(o * o, axis=-1, keepdims=True)
            out_ref[:, t * width:(t + 1) * width] = (o * lax.rsqrt(ms + EPS) * gout_ref[...]).astype(BF16)


def _s5(uv, w, glu_w, glu_b, g_out):
    n = uv.shape[0]
    rows = S5_ROWS
    tile = pl.BlockSpec((rows, CHUNK * SSM_WIDTH), lambda i: (i, 0))
    return pl.pallas_call(
        _s5_kernel,
        grid=(n // rows,),
        in_specs=[tile] + [_const_spec(a.shape) for a in
                           (w["m_t"], w["w_r"], w["w_i"], w["e_r"], w["e_i"], w["a_sub"], w["levels"],
                            glu_w, glu_b, g_out)],
        out_specs=tile,
        out_shape=jax.ShapeDtypeStruct((n, CHUNK * SSM_WIDTH), BF16),
        scratch_shapes=[pltpu.VMEM((2, STATE_COLS), F32),
                        pltpu.VMEM((rows, STATE_COLS), F32),
                        pltpu.VMEM((rows, STATE_COLS), F32),
                        pltpu.VMEM((N_SUB - 1, rows, STATE_COLS), BF16),
                        pltpu.VMEM((N_SUB - 1, rows, STATE_COLS), BF16)],
        compiler_params=_params(dimension_semantics=("arbitrary",)),
        name="s5",
    )(uv, w["m_t"], w["w_r"], w["w_i"], w["e_r"], w["e_i"], w["a_sub"], w["levels"], glu_w, glu_b, g_out)


def _mlp_kernel(x_ref, o1_ref, o2_ref, o3_ref, l1_ref, l2_ref, l3_ref, ssm_ref, ga_ref, wout_ref,
                g2_ref, wup_ref, wdn_ref, out_ref, slab_ref, slab4_ref, mix_ref):
    rows = PROJ_SUB
    n_sub = x_ref.shape[0] // rows
    n_slabs = ATTN_WIDTH // SLAB
    step = DILATIONS[1]
    n_ff = D_FF // FF_CHUNK

    def regroup(ref, d, sub, buf):
        part = slice(sub * (rows // d), (sub + 1) * (rows // d))
        for s in range(n_slabs):
            t, lanes = _slab_home(s)
            if ref is ssm_ref:
                piece = lambda r: ref[part, r * SSM_WIDTH + s * SLAB:r * SSM_WIDTH + (s + 1) * SLAB]
            else:
                piece = lambda r: ref[t, r, part, lanes]
            if d == step:
                for r in range(d):
                    slab_ref[sub, buf, s, pl.ds(r, rows // d, stride=d), :] = piece(r).astype(F32)
            else:
                for r4 in range(step):
                    for j4 in range(step):
                        slab4_ref[sub, buf - 2, s, r4, pl.ds(j4, rows // d, stride=step), :] = (
                            piece(step * j4 + r4).astype(F32))
                for r4 in range(step):
                    slab_ref[sub, buf, s, pl.ds(r4, rows // step, stride=step), :] = slab4_ref[sub, buf - 2, s, r4]

    def in_order(sub, buf):
        return jnp.concatenate([slab_ref[sub, buf, s] for s in range(n_slabs)], axis=1)

    def merge_slab(sub, s):
        t, lanes = _slab_home(s)
        part = slice(sub * rows, (sub + 1) * rows)
        l1, l2, l3 = l1_ref[t, 0, part, lanes], slab_ref[sub, 1, s], slab_ref[sub, 3, s]
        m = jnp.maximum(jnp.maximum(l1, l2), l3)
        w1, w2, w3 = jnp.exp2(l1 - m), jnp.exp2(l2 - m), jnp.exp2(l3 - m)
        num = w1 * o1_ref[t, 0, part, lanes].astype(F32) + w2 * slab_ref[sub, 0, s] + w3 * slab_ref[sub, 2, s]
        slab_ref[sub, 0, s] = num / (w1 + w2 + w3)

    def merge_finish(sub):
        attn = in_order(sub, 0)
        ms = jnp.mean(attn * attn, axis=-1, keepdims=True)
        mix_ref[sub, :, 0:ATTN_WIDTH] = (attn * lax.rsqrt(ms + EPS) * ga_ref[...]).astype(BF16)
        mix_ref[sub, :, ATTN_WIDTH:] = in_order(sub, 4).astype(BF16)

    def prologue(sub):
        return ([lambda: regroup(o2_ref, step, sub, 0), lambda: regroup(l2_ref, step, sub, 1),
                 lambda: regroup(o3_ref, CHUNK, sub, 2), lambda: regroup(l3_ref, CHUNK, sub, 3),
                 lambda: regroup(ssm_ref, CHUNK, sub, 4)]
                + [lambda s=s: merge_slab(sub, s) for s in range(n_slabs)] + [lambda: merge_finish(sub)])

    def main(sub):
        part = slice(sub * rows, (sub + 1) * rows)
        st = {}

        def out_proj():
            x1 = x_ref[part, :] + jnp.dot(mix_ref[sub], wout_ref[...], preferred_element_type=F32)
            ms2 = jnp.mean(x1 * x1, axis=-1, keepdims=True)
            st["xn"] = (x1 * lax.rsqrt(ms2 + EPS) * g2_ref[...]).astype(BF16)
            st["acc"] = x1

        def up(c):
            st[c] = jnp.dot(st["xn"], wup_ref[:, c * FF_CHUNK:(c + 1) * FF_CHUNK], preferred_element_type=F32)

        def down(c):
            h = jnp.square(jnp.maximum(st.pop(c), 0.0)).astype(BF16)
            st["acc"] = st["acc"] + jnp.dot(h, wdn_ref[c * FF_CHUNK:(c + 1) * FF_CHUNK, :],
                                            preferred_element_type=F32)

        def finish():
            out_ref[part, :] = st["acc"]

        stages = [out_proj, lambda: up(0)]
        for c in range(n_ff):
            if c + 1 < n_ff:
                stages.append(lambda c=c: up(c + 1))
            stages.append(lambda c=c: down(c))
        return stages + [finish]

    tail = 3
    program = prologue(0)
    all_stages = [main(sub) for sub in range(n_sub)]
    for sub in range(n_sub):
        stages = all_stages[sub]
        head = stages if sub == 0 else stages[1:]
        if sub + 1 < n_sub:
            side = prologue(sub + 1)
            body = []
            for stage in head[:-tail]:
                body.append(stage)
                if side:
                    body.append(side.pop(0))
            program += body + side + [all_stages[sub + 1][0]] + head[-tail:]
        else:
            program += head
    for piece in program:
        piece()


def _mlp(x, os_, ls_, ssm, ga, w_out, g2, w_up, w_dn):
    s = x.shape[0]
    rows = MLP_ROWS
    xt = pl.BlockSpec((rows, D_MODEL), lambda i: (i, 0))
    ht = [pl.BlockSpec((N_HEAD_TILES, d, rows // d, LANE_TILE), lambda i: (0, 0, i, 0)) for d in DILATIONS]
    return pl.pallas_call(
        _mlp_kernel,
        grid=(s // rows,),
        in_specs=[xt, *ht, *ht, pl.BlockSpec((rows // CHUNK, CHUNK * SSM_WIDTH), lambda i: (i, 0)),
                  _const_spec(ga.shape), _const_spec(w_out.shape), _const_spec(g2.shape),
                  _const_spec(w_up.shape), _const_spec(w_dn.shape)],
        out_specs=xt,
        out_shape=jax.ShapeDtypeStruct((s, D_MODEL), F32),
        scratch_shapes=[pltpu.VMEM((rows // PROJ_SUB, 5, ATTN_WIDTH // SLAB, PROJ_SUB, SLAB), F32),
                        pltpu.VMEM((rows // PROJ_SUB, 3, ATTN_WIDTH // SLAB, DILATIONS[1],
                                    PROJ_SUB // DILATIONS[1], SLAB), F32),
                        pltpu.VMEM((rows // PROJ_SUB, PROJ_SUB, D_MODEL), BF16)],
        compiler_params=_params(dimension_semantics=("arbitrary",)),
        name="mlp",
    )(x, *os_, *ls_, ssm, ga, w_out, g2, w_up, w_dn)


def _layer(x, norm1_g, w_in, q_norm_g, k_norm_g, ssm_a_re, ssm_a_im, ssm_log_dt, ssm_b_re, ssm_b_im,
           ssm_c_re, ssm_c_im, ssm_d, glu_w, glu_b, attn_out_norm_g, ssm_out_norm_g, w_out, norm2_g,
           w_mlp_up, w_mlp_down):
    row = lambda g: g.astype(F32).reshape(1, -1)
    heads = ATTN_WIDTH // HEAD_DIM
    head_id = jnp.arange(LANE_TILE) // HEAD_DIM
    hmat = jnp.where(head_id[:, None] == head_id[None, :], 1.0 / HEAD_DIM, 0.0).astype(BF16)
    *qkv, u = _proj(x, row(norm1_g), w_in.astype(F32), hmat,
                    row(jnp.tile(q_norm_g, heads)), row(jnp.tile(k_norm_g, heads)))
    outs, lses = [], []
    mlp_weights = (w_out.astype(F32), w_mlp_up.astype(F32), w_mlp_down.astype(F32))
    for i in range(len(DILATIONS)):
        o, lse, cast = _attn_pattern(*qkv[3 * i:3 * i + 3], cast=mlp_weights if i == 0 else ())
        outs.append(o)
        lses.append(lse)
        if i == 0:
            w_out_b, w_up_b, w_dn_b = cast
    w = _s5_weights(ssm_a_re, ssm_a_im, ssm_log_dt, ssm_b_re, ssm_b_im, ssm_c_re, ssm_c_im, ssm_d)
    ssm = _s5(u, w, glu_w.astype(BF16), row(glu_b), row(ssm_out_norm_g))
    return _mlp(x, outs, lses, ssm, row(attn_out_norm_g), w_out_b, row(norm2_g), w_up_b, w_dn_b)


def kernel(x, norm1_g, w_in, q_norm_g, k_norm_g, ssm_a_re, ssm_a_im, ssm_log_dt, ssm_b_re, ssm_b_im,
           ssm_c_re, ssm_c_im, ssm_d, glu_w, glu_b, attn_out_norm_g, ssm_out_norm_g, w_out, norm2_g,
           w_mlp_up, w_mlp_down):
    params = (norm1_g, w_in, q_norm_g, k_norm_g, ssm_a_re, ssm_a_im, ssm_log_dt, ssm_b_re, ssm_b_im,
              ssm_c_re, ssm_c_im, ssm_d, glu_w, glu_b, attn_out_norm_g, ssm_out_norm_g, w_out, norm2_g,
              w_mlp_up, w_mlp_down)
    batch = x.shape[0]
    outs = []
    for bi in range(batch):
        h = x[bi]
        for layer in range(norm1_g.shape[0]):
            h = _layer(h, *[p[layer] for p in params])
        outs.append(h)
    return jnp.stack(outs)
```

```python
import functools
import math

import jax
import jax.numpy as jnp
from jax import lax
from jax.experimental import pallas as pl
from jax.experimental.pallas import tpu as pltpu

F32 = jnp.float32
BF16 = jnp.bfloat16

D_MODEL = 1024
ATTN_WIDTH = 512
HEAD_DIM = 64
SSM_WIDTH = 512
SSM_GROUP = 16
SSM_GROUPS = 32
SSM_STATE = 64
D_FF = 4096
EPS = 1e-6
NEG_INF = -1e30
DILATIONS = (1, 4, 16)
BLOCK = 128

SLAB = 128
LANE_TILE = 256
HEADS_PER_TILE = LANE_TILE // HEAD_DIM
N_HEAD_TILES = ATTN_WIDTH // LANE_TILE
CHUNK = 16
SUB = 2
PAIRS = 2
SUB_STEPS = SUB * PAIRS
N_SUB = CHUNK // SUB_STEPS
OCT = 8
N_OCT = SSM_GROUPS // OCT
SCAN_GROUP = 8
STATE_COLS = SSM_GROUPS * SSM_STATE
VMEM_LIMIT = 56 * 1024 * 1024

LOG2E = math.log2(math.e)
Q_SCALE = HEAD_DIM ** -0.5 * LOG2E

PROJ_ROWS = 1024
PROJ_SUB = 256
SLAB_BUFFERS = 2
PROJ_AHEAD = 2
ATTN_ROWS = 4096
S5_ROWS = 256
MLP_ROWS = 512
FF_CHUNK = 512


def _const_spec(shape):
    nd = len(shape)
    return pl.BlockSpec(shape, lambda *_: (0,) * nd, pipeline_mode=pl.Buffered(1))


def _params(**kw):
    return pltpu.CompilerParams(vmem_limit_bytes=VMEM_LIMIT, **kw)


def _proj_kernel(x_ref, g1_ref, w_ref, hmat_ref, gq_ref, gk_ref,
                 q1_ref, k1_ref, v1_ref, q4_ref, k4_ref, v4_ref, q16_ref, k16_ref, v16_ref, u_ref,
                 slab_ref, slab4_ref, wb_ref):
    @pl.when(pl.program_id(0) == 0)
    def _():
        wb_ref[...] = w_ref[...].astype(BF16)

    outs = ((q1_ref, q4_ref, q16_ref), (k1_ref, k4_ref, k16_ref), (v1_ref, v4_ref, v16_ref))
    slabs_per_tensor = ATTN_WIDTH // SLAB
    step = DILATIONS[1]
    rows = PROJ_SUB
    n_sub = x_ref.shape[0] // rows
    n_chunks = w_ref.shape[1] // LANE_TILE
    chunks_per_tensor = ATTN_WIDTH // LANE_TILE
    gains = (gq_ref, gk_ref)
    xn = [None] * n_sub

    def prologue(sub):
        x = x_ref[sub * rows:(sub + 1) * rows, :]
        ms = jnp.mean(x * x, axis=-1, keepdims=True)
        xn[sub] = (x * lax.rsqrt(ms + EPS) * g1_ref[...]).astype(BF16)

    def matmul(sub, c):
        return jnp.dot(xn[sub], wb_ref[:, c * LANE_TILE:(c + 1) * LANE_TILE], preferred_element_type=F32)

    def epilogue(sub, c, val):
        ti, t = divmod(c, chunks_per_tensor)
        lane_tile = slice(t * LANE_TILE, (t + 1) * LANE_TILE)
        if ti < 2:
            ms_h = jnp.dot((val * val).astype(BF16), hmat_ref[...], preferred_element_type=F32)
            val = val * lax.rsqrt(ms_h + EPS) * gains[ti][:, lane_tile]
            if ti == 0:
                val = val * Q_SCALE
        if ti < 3:
            outs[ti][0][t, 0, sub * rows:(sub + 1) * rows, :] = val.astype(BF16)
        r4_rows = slice(sub * (rows // step), (sub + 1) * (rows // step))
        r16_rows = slice(sub * (rows // CHUNK), (sub + 1) * (rows // CHUNK))
        for half in range(LANE_TILE // SLAB):
            s = t * (LANE_TILE // SLAB) + half
            lanes = slice(half * SLAB, (half + 1) * SLAB)
            slab = ti * slabs_per_tensor + s
            buf = sub % SLAB_BUFFERS
            slab_ref[buf, slab] = val[:, lanes]
            for r4 in range(step):
                part = slab_ref[buf, slab, pl.ds(r4, rows // step, stride=step), :]
                if ti < 3:
                    outs[ti][1][t, r4, r4_rows, lanes] = part.astype(BF16)
                slab4_ref[buf, slab, r4] = part
            for r4 in range(step):
                for j4 in range(step):
                    r16 = step * j4 + r4
                    part = slab4_ref[buf, slab, r4, pl.ds(j4, rows // CHUNK, stride=step), :].astype(BF16)
                    if ti < 3:
                        outs[ti][2][t, r16, r16_rows, lanes] = part
                    else:
                        u_ref[r16_rows, r16 * SSM_WIDTH + s * SLAB:r16 * SSM_WIDTH + (s + 1) * SLAB] = part

    work = [(sub, c) for sub in range(n_sub) for c in range(n_chunks)]
    prologue(0)
    pending = []
    for sub, c in work:
        pending.append((sub, c, matmul(sub, c)))
        if len(pending) > PROJ_AHEAD:
            epilogue(*pending.pop(0))
        if c == n_chunks // 2 and sub + 1 < n_sub:
            prologue(sub + 1)
    for item in pending:
        epilogue(*item)


def _slab_home(s):
    t, half = divmod(s, LANE_TILE // SLAB)
    return t, slice(half * SLAB, (half + 1) * SLAB)


def _proj(x, g1, w_in, hmat, gq, gk):
    s = x.shape[0]
    rows = PROJ_ROWS
    out_specs, out_shape = [], []
    for d in DILATIONS:
        for _ in range(3):
            out_specs.append(pl.BlockSpec((N_HEAD_TILES, d, rows // d, LANE_TILE), lambda i: (0, 0, i, 0)))
            out_shape.append(jax.ShapeDtypeStruct((N_HEAD_TILES, d, s // d, LANE_TILE), BF16))
    out_specs.append(pl.BlockSpec((rows // CHUNK, CHUNK * SSM_WIDTH), lambda i: (i, 0)))
    out_shape.append(jax.ShapeDtypeStruct((s // CHUNK, CHUNK * SSM_WIDTH), BF16))
    return pl.pallas_call(
        _proj_kernel,
        grid=(s // rows,),
        in_specs=[
            pl.BlockSpec((rows, D_MODEL), lambda i: (i, 0)),
            _const_spec((1, D_MODEL)),
            _const_spec(w_in.shape),
            _const_spec(hmat.shape),
            _const_spec((1, ATTN_WIDTH)),
            _const_spec((1, ATTN_WIDTH)),
        ],
        out_specs=out_specs,
        out_shape=out_shape,
        scratch_shapes=[pltpu.VMEM((SLAB_BUFFERS, 4 * ATTN_WIDTH // SLAB, PROJ_SUB, SLAB), F32),
                        pltpu.VMEM((SLAB_BUFFERS, 4 * ATTN_WIDTH // SLAB, DILATIONS[1],
                                    PROJ_SUB // DILATIONS[1], SLAB), F32),
                        pltpu.VMEM(w_in.shape, BF16)],
        compiler_params=_params(dimension_semantics=("arbitrary",)),
        name="proj",
    )(x, g1, w_in, hmat, gq, gk)


def _attn_kernel(q_ref, kc_ref, kp_ref, vc_ref, vp_ref, *rest):
    n_cast = (len(rest) - 4) // 2
    cast_in, (o_ref, lse_ref), cast_out = rest[:n_cast], rest[n_cast:n_cast + 2], rest[n_cast + 2:2 * n_cast + 2]
    kbuf, vbuf = rest[2 * n_cast + 2:]
    for src, dst in zip(cast_in, cast_out):
        dst[...] = src[...].astype(BF16)
    n_res, rows = q_ref.shape[0], q_ref.shape[1]
    kbuf[:, 0:BLOCK, :] = kp_ref[...]
    kbuf[:, BLOCK:, :] = kc_ref[...]
    vbuf[:, 0:BLOCK, :] = vp_ref[...]
    vbuf[:, BLOCK:, :] = vc_ref[...]

    lane = lax.broadcasted_iota(jnp.int32, (BLOCK, LANE_TILE), 1)
    head_masks = [(lane >= h * HEAD_DIM) & (lane < (h + 1) * HEAD_DIM) for h in range(HEADS_PER_TILE)]
    qi = lax.broadcasted_iota(jnp.int32, (BLOCK, 2 * BLOCK), 0)
    ki = lax.broadcasted_iota(jnp.int32, (BLOCK, 2 * BLOCK), 1)
    band = (ki >= qi) & (ki <= qi + BLOCK)
    bias_band = jnp.where(band, 0.0, NEG_INF).astype(F32)
    no_prev = pl.program_id(2) == 0
    bias_first = jnp.where(band & ((ki >= BLOCK) | jnp.logical_not(no_prev)), 0.0, NEG_INF).astype(F32)

    for r, b in [(r, b) for r in range(n_res) for b in range(rows // BLOCK)]:
        qb = q_ref[r, b * BLOCK:(b + 1) * BLOCK, :]
        zero = jnp.zeros_like(qb)
        q_stack = jnp.concatenate([jnp.where(mk, qb, zero) for mk in head_masks], axis=0)
        kw = kbuf[r, b * BLOCK:(b + 2) * BLOCK, :]
        vw = vbuf[r, b * BLOCK:(b + 2) * BLOCK, :]
        s = lax.dot_general(q_stack, kw, (((1,), (1,)), ((), ())), preferred_element_type=F32)
        bias = bias_first if b == 0 else bias_band
        ps, ms, ls = [], [], []
        for h in range(HEADS_PER_TILE):
            sh = s[h * BLOCK:(h + 1) * BLOCK, :] + bias
            m = jnp.max(sh, axis=-1, keepdims=True)
            p = jnp.exp2(sh - m)
            ls.append(jnp.sum(p, axis=-1, keepdims=True))
            ms.append(m)
            ps.append(p.astype(BF16))
        pv = jnp.dot(jnp.concatenate(ps, axis=0), vw, preferred_element_type=F32)
        o = jnp.zeros((BLOCK, LANE_TILE), F32)
        lse = jnp.zeros((BLOCK, LANE_TILE), F32)
        for h in range(HEADS_PER_TILE):
            o = jnp.where(head_masks[h], pv[h * BLOCK:(h + 1) * BLOCK, :] / ls[h], o)
            lse = jnp.where(head_masks[h], ms[h] + jnp.log2(ls[h]), lse)
        o_ref[r, b * BLOCK:(b + 1) * BLOCK, :] = o.astype(BF16)
        lse_ref[r, b * BLOCK:(b + 1) * BLOCK, :] = lse


def _attn_pattern(q, k, v, cast=()):
    nt, dilation, n, _ = q.shape
    rows = min(ATTN_ROWS, n)
    n_res = min(ATTN_ROWS // rows, dilation)
    blocks_per_tile = rows // BLOCK
    grid = (nt, dilation // n_res, n // rows)
    n_steps = grid[0] * grid[1] * grid[2]
    cur = pl.BlockSpec((None, n_res, rows, LANE_TILE), lambda t, r, j: (t, r, j, 0))
    prev = pl.BlockSpec((None, n_res, BLOCK, LANE_TILE),
                        lambda t, r, j: (t, r, jnp.maximum(j * blocks_per_tile - 1, 0), 0))
    step_id = lambda t, r, j: ((t * grid[1] + r) * grid[2] + j, 0)
    cast_specs = [pl.BlockSpec((w.shape[0] // n_steps, w.shape[1]), step_id) for w in cast]
    outs = pl.pallas_call(
        _attn_kernel,
        grid=grid,
        in_specs=[cur, cur, prev, cur, prev] + cast_specs,
        out_specs=[cur, cur] + cast_specs,
        out_shape=[jax.ShapeDtypeStruct(q.shape, BF16), jax.ShapeDtypeStruct(q.shape, F32)]
        + [jax.ShapeDtypeStruct(w.shape, BF16) for w in cast],
        scratch_shapes=[pltpu.VMEM((n_res, rows + BLOCK, LANE_TILE), BF16),
                        pltpu.VMEM((n_res, rows + BLOCK, LANE_TILE), BF16)],
        compiler_params=_params(dimension_semantics=("arbitrary", "arbitrary", "arbitrary")),
        name=f"attn_d{dilation}",
    )(q, k, k, v, v, *cast)
    return outs[0], outs[1], outs[2:]


def _s5_weights(a_re, a_im, log_dt, b_re, b_im, c_re, c_im, d_skip):
    p, c = SSM_STATE, SSM_GROUP
    lr, li = a_re.astype(F32), a_im.astype(F32)
    dt = jnp.exp(log_dt.astype(F32))[:, None]

    def apow(j):
        mag = jnp.exp(lr * dt * j)
        return mag * jnp.cos(li * dt * j), mag * jnp.sin(li * dt * j)

    ab_r, ab_i = apow(1.0)
    den = lr * lr + li * li
    nr, ni = ab_r - 1.0, ab_i
    cr = (nr * lr + ni * li) / den
    ci = (ni * lr - nr * li) / den
    br, bi = b_re.astype(F32), b_im.astype(F32)
    bb_r = cr[..., None] * br - ci[..., None] * bi
    bb_i = cr[..., None] * bi + ci[..., None] * br
    cre, cim = c_re.astype(F32), c_im.astype(F32)

    def apow_many(js):
        j = jnp.asarray(js, F32)[:, None, None]
        mag = jnp.exp(lr * dt * j)
        return mag * jnp.cos(li * dt * j), mag * jnp.sin(li * dt * j)

    pr, pi = apow_many(range(SUB_STEPS + 1))
    wp_r = jnp.stack([pr[SUB_STEPS - 1 - s] for s in range(SUB_STEPS)])
    wp_i = jnp.stack([pi[SUB_STEPS - 1 - s] for s in range(SUB_STEPS)])
    win_r = wp_r[..., None] * bb_r[None] - wp_i[..., None] * bb_i[None]
    win_i = wp_r[..., None] * bb_i[None] + wp_i[..., None] * bb_r[None]
    pad_lanes = lambda w: jnp.pad(w, [(0, 0)] * (w.ndim - 1) + [(0, SLAB - w.shape[-1])])
    to_in = lambda w: pad_lanes(w.reshape(PAIRS, SUB, N_OCT, OCT, p, c).transpose(0, 2, 1, 3, 5, 4)
                                .reshape(PAIRS, N_OCT, SUB * OCT * c, p))
    out_r = cre[None] * pr[:, :, None, :] - cim[None] * pi[:, :, None, :]
    out_i = -(cre[None] * pi[:, :, None, :] + cim[None] * pr[:, :, None, :])
    to_out = lambda w: pad_lanes(w.reshape(SUB_STEPS + 1, N_OCT, OCT, c, p).transpose(0, 1, 2, 4, 3)
                                 .reshape(SUB_STEPS + 1, N_OCT, OCT * p, c))

    flat = lambda re, im: jnp.stack([re.reshape(-1), im.reshape(-1)])
    top_r, top_i = apow_many([CHUNK])
    lam = flat(top_r[0], top_i[0])
    levels = [lam]
    for _ in range(SCAN_GROUP - 1):
        lr_, li_ = levels[-1][0], levels[-1][1]
        levels.append(jnp.stack([lr_ * lam[0] - li_ * lam[1], lr_ * lam[1] + li_ * lam[0]]))
    m_t, w_r, w_i, e_r, e_i = _s5_expand(to_in(win_r), to_in(win_i), to_out(out_r), to_out(out_i),
                                         d_skip.astype(F32).reshape(N_OCT, 1, OCT * c))
    return dict(m_t=m_t, w_r=w_r, w_i=w_i, e_r=e_r, e_i=e_i,
                a_sub=flat(pr[SUB_STEPS], pi[SUB_STEPS]),
                levels=jnp.stack(levels, axis=1))


def _s5_expand_kernel(wr_ref, wi_ref, or_ref, oi_ref, d_ref, mt_ref, wfr_ref, wfi_ref, efr_ref, efi_ref):
    p, c = SSM_STATE, SSM_GROUP
    n_state, n_lane = OCT * p, OCT * c

    def rep(inner, total):
        r = lax.broadcasted_iota(jnp.int32, (SLAB, total), 0)
        col = lax.broadcasted_iota(jnp.int32, (SLAB, total), 1)
        return jnp.where((col & (inner - 1)) == r, 1.0, 0.0).astype(BF16)

    shift = lambda n: int(math.log2(n))
    rep_in, rep_out = rep(p, n_state), rep(c, n_lane)
    in_row = lax.broadcasted_iota(jnp.int32, (SUB * n_lane, n_state), 0)
    in_col = lax.broadcasted_iota(jnp.int32, (SUB * n_lane, n_state), 1)
    mask_in = ((in_row >> shift(c)) & (OCT - 1)) == (in_col >> shift(p))
    out_row = lax.broadcasted_iota(jnp.int32, (n_state, n_lane), 0)
    out_col = lax.broadcasted_iota(jnp.int32, (n_state, n_lane), 1)
    mask_out = (out_row >> shift(p)) == (out_col >> shift(c))
    diag = (lax.broadcasted_iota(jnp.int32, (n_lane, n_lane), 0)
            == lax.broadcasted_iota(jnp.int32, (n_lane, n_lane), 1))

    def split(x):
        hi = x.astype(BF16)
        return hi, (x - hi.astype(F32)).astype(BF16)

    def expand(x_b, rep_m, mask):
        full = jnp.dot(x_b, rep_m, preferred_element_type=F32)
        return jnp.where(mask, full, 0.0).astype(BF16)

    last = slice((SUB - 1) * n_lane, SUB * n_lane)
    for o in range(wfr_ref.shape[0]):
        in0 = []
        for src, dst in ((wr_ref, wfr_ref), (wi_ref, wfi_ref)):
            for q in range(PAIRS):
                hi, lo = split(src[q, o])
                full = expand(hi, rep_in, mask_in)
                dst[o, q * SUB * n_lane:(q + 1) * SUB * n_lane, :] = full
                if q == PAIRS - 1:
                    in0.append((full[last, :], expand(lo, rep_in, mask_in)[last, :]))
        lags = [None] * SUB_STEPS
        for (src, dst), (in_hi, in_lo) in zip(((or_ref, efr_ref), (oi_ref, efi_ref)), in0):
            outs_hi = []
            for j in range(SUB_STEPS + 1):
                hi, lo = split(src[j, o])
                out_hi = expand(hi, rep_out, mask_out)
                outs_hi.append(out_hi)
                if j < SUB_STEPS:
                    out_lo = expand(lo, rep_out, mask_out)
                    part = (jnp.dot(in_hi, out_hi, preferred_element_type=F32)
                            + jnp.dot(in_hi, out_lo, preferred_element_type=F32)
                            + jnp.dot(in_lo, out_hi, preferred_element_type=F32))
                    lags[j] = part if lags[j] is None else lags[j] + part
            dst[o] = jnp.concatenate(outs_hi[1:], axis=1)
        lags[0] = lags[0] + jnp.where(diag, d_ref[o], 0.0)
        zero = jnp.zeros_like(lags[0])
        for dist in range(PAIRS):
            lag = lambda t_in, t_out: SUB * dist + t_out - t_in
            mt_ref[dist, o] = jnp.concatenate(
                [jnp.concatenate([lags[lag(t_in, t_out)] if lag(t_in, t_out) >= 0 else zero
                                  for t_out in range(SUB)], axis=1)
                 for t_in in range(SUB)], axis=0).astype(BF16)


def _s5_expand(wc_r, wc_i, oc_r, oc_i, d_row):
    n_in, n_state = wc_r.shape[2], oc_r.shape[2]
    shapes = [jax.ShapeDtypeStruct((PAIRS, N_OCT, n_in, n_in), BF16),
              jax.ShapeDtypeStruct((N_OCT, PAIRS * n_in, n_state), BF16),
              jax.ShapeDtypeStruct((N_OCT, PAIRS * n_in, n_state), BF16),
              jax.ShapeDtypeStruct((N_OCT, n_state, PAIRS * n_in), BF16),
              jax.ShapeDtypeStruct((N_OCT, n_state, PAIRS * n_in), BF16)]
    def per_tile(shape, axis):
        block = tuple(1 if i == axis else n for i, n in enumerate(shape))
        return pl.BlockSpec(block, lambda o: tuple(o if i == axis else 0 for i in range(len(shape))))

    operands = (wc_r, wc_i, oc_r, oc_i, d_row)
    tile_axis = (1, 1, 1, 1, 0)
    return pl.pallas_call(
        _s5_expand_kernel,
        grid=(N_OCT,),
        in_specs=[per_tile(a.shape, ax) for a, ax in zip(operands, tile_axis)],
        out_specs=[per_tile(s.shape, ax) for s, ax in zip(shapes, (1, 0, 0, 0, 0))],
        out_shape=shapes,
        compiler_params=_params(dimension_semantics=("arbitrary",)),
        name="s5_expand",
    )(*operands)


def _cmul(ar, ai, br, bi):
    return ar * br - ai * bi, ar * bi + ai * br


def _s5_kernel(u_ref, mt_ref, wr_ref, wi_ref, er_ref, ei_ref, asub_ref, lvl_ref,
               gluw_ref, glub_ref, gout_ref, out_ref, carry_ref, zr_ref, zi_ref, locr_ref, loci_ref):
    rows = u_ref.shape[0]
    width = SSM_WIDTH
    oct_lanes = OCT * SSM_GROUP
    oct_state = OCT * SSM_STATE

    @pl.when(pl.program_id(0) == 0)
    def _():
        carry_ref[...] = jnp.zeros_like(carry_ref)

    def u_tile(a, q, o):
        first = a * SUB_STEPS + q * SUB
        return jnp.concatenate(
            [u_ref[:, (first + t) * width + o * oct_lanes:(first + t) * width + (o + 1) * oct_lanes]
             for t in range(SUB)], axis=1)

    a_r, a_i = asub_ref[0:1, :], asub_ref[1:2, :]

    for a in range(N_SUB):
        for o in range(N_OCT):
            cols = slice(o * oct_state, (o + 1) * oct_state)
            ut = jnp.concatenate([u_tile(a, q, o) for q in range(PAIRS)], axis=1)
            pr = jnp.dot(ut, wr_ref[o], preferred_element_type=F32)
            pi = jnp.dot(ut, wi_ref[o], preferred_element_type=F32)
            if a > 0:
                hr, hi = _cmul(zr_ref[:, cols], zi_ref[:, cols], a_r[:, cols], a_i[:, cols])
                pr, pi = hr + pr, hi + pi
            zr_ref[:, cols] = pr
            zi_ref[:, cols] = pi
            if a < N_SUB - 1:
                locr_ref[a, :, cols] = pr.astype(BF16)
                loci_ref[a, :, cols] = pi.astype(BF16)

    row = lax.broadcasted_iota(jnp.int32, (rows, 1), 0)
    in_group = row & (SCAN_GROUP - 1)
    sr, si = zr_ref[...], zi_ref[...]
    sh = 1
    while sh < SCAN_GROUP:
        keep = in_group >= sh
        pr_, pi_ = lvl_ref[0, sh - 1:sh, :], lvl_ref[1, sh - 1:sh, :]
        tr = jnp.where(keep, pltpu.roll(sr, sh, axis=0), 0.0)
        ti = jnp.where(keep, pltpu.roll(si, sh, axis=0), 0.0)
        mr, mi = _cmul(tr, ti, pr_, pi_)
        sr, si = sr + mr, si + mi
        sh *= 2
    c_r, c_i = carry_ref[0:1, :], carry_ref[1:2, :]
    pw_r, pw_i = lvl_ref[0], lvl_ref[1]
    last_r, last_i = c_r, c_i
    for g in range(rows // SCAN_GROUP):
        grp = slice(g * SCAN_GROUP, (g + 1) * SCAN_GROUP)
        ar, ai = _cmul(jnp.broadcast_to(last_r, pw_r.shape), jnp.broadcast_to(last_i, pw_i.shape), pw_r, pw_i)
        gr, gi = sr[grp, :] + ar, si[grp, :] + ai
        zr_ref[grp, :] = gr
        zi_ref[grp, :] = gi
        last_r, last_i = gr[SCAN_GROUP - 1:SCAN_GROUP, :], gi[SCAN_GROUP - 1:SCAN_GROUP, :]
    carry_ref[0:1, :] = last_r
    carry_ref[1:2, :] = last_i
    xr = jnp.where(row == 0, c_r, pltpu.roll(zr_ref[...], 1, axis=0))
    xi = jnp.where(row == 0, c_i, pltpu.roll(zi_ref[...], 1, axis=0))
    zr_ref[...] = xr
    zi_ref[...] = xi

    for a in range(N_SUB):
        if a > 0:
            nr, ni = _cmul(zr_ref[...], zi_ref[...], a_r, a_i)
            zr_ref[...] = nr
            zi_ref[...] = ni
        if a == 0:
            xr_b = zr_ref[...].astype(BF16)
            xi_b = zi_ref[...].astype(BF16)
        else:
            xr_b = (zr_ref[...] + locr_ref[a - 1].astype(F32)).astype(BF16)
            xi_b = (zi_ref[...] + loci_ref[a - 1].astype(F32)).astype(BF16)
        tiles = []
        for o in range(N_OCT):
            cols = slice(o * oct_state, (o + 1) * oct_state)
            from_state = (jnp.dot(xr_b[:, cols], er_ref[o], preferred_element_type=F32)
                          + jnp.dot(xi_b[:, cols], ei_ref[o], preferred_element_type=F32))
            parts = []
            for q in range(PAIRS):
                acc = from_state[:, q * SUB * oct_lanes:(q + 1) * SUB * oct_lanes]
                for q_in in range(q + 1):
                    acc = acc + jnp.dot(u_tile(a, q_in, o), mt_ref[q - q_in, o], preferred_element_type=F32)
                parts.append(acc)
            tiles.append(jnp.concatenate(parts, axis=1))
        for b in range(SUB_STEPS):
            t = a * SUB_STEPS + b
            y = jnp.concatenate([tl[:, b * oct_lanes:(b + 1) * oct_lanes] for tl in tiles], axis=1)
            z = 0.5 * y * (1.0 + jnp.tanh(math.sqrt(2.0 / math.pi) * (y + 0.044715 * (y * y * y))))
            gate = jnp.dot(z.astype(BF16), gluw_ref[...], preferred_element_type=F32) + glub_ref[...]
            o = z * (1.0 / (1.0 + jnp.exp(-gate)))
            ms = jnp.mean(o * o, axis=-1, keepdims=True)
            out_ref[:, t * width:(t + 1) * width] = (o * lax.rsqrt(ms + EPS) * gout_ref[...]).astype(BF16)


def _s5(uv, w, glu_w, glu_b, g_out):
    n = uv.shape[0]
    rows = S5_ROWS
    tile = pl.BlockSpec((rows, CHUNK * SSM_WIDTH), lambda i: (i, 0))
    return pl.pallas_call(
        _s5_kernel,
        grid=(n // rows,),
        in_specs=[tile] + [_const_spec(a.shape) for a in
                           (w["m_t"], w["w_r"], w["w_i"], w["e_r"], w["e_i"], w["a_sub"], w["levels"],
                            glu_w, glu_b, g_out)],
        out_specs=tile,
        out_shape=jax.ShapeDtypeStruct((n, CHUNK * SSM_WIDTH), BF16),
        scratch_shapes=[pltpu.VMEM((2, STATE_COLS), F32),
                        pltpu.VMEM((rows, STATE_COLS), F32),
                        pltpu.VMEM((rows, STATE_COLS), F32),
                        pltpu.VMEM((N_SUB - 1, rows, STATE_COLS), BF16),
                        pltpu.VMEM((N_SUB - 1, rows, STATE_COLS), BF16)],
        compiler_params=_params(dimension_semantics=("arbitrary",)),
        name="s5",
    )(uv, w["m_t"], w["w_r"], w["w_i"], w["e_r"], w["e_i"], w["a_sub"], w["levels"], glu_w, glu_b, g_out)


def _mlp_kernel(x_ref, o1_ref, o2_ref, o3_ref, l1_ref, l2_ref, l3_ref, ssm_ref, ga_ref, wout_ref,
                g2_ref, wup_ref, wdn_ref, out_ref, slab_ref, slab4_ref, mix_ref):
    rows = PROJ_SUB
    n_sub = x_ref.shape[0] // rows
    n_slabs = ATTN_WIDTH // SLAB
    step = DILATIONS[1]
    n_ff = D_FF // FF_CHUNK

    def regroup(ref, d, sub, buf):
        part = slice(sub * (rows // d), (sub + 1) * (rows // d))
        for s in range(n_slabs):
            t, lanes = _slab_home(s)
            if ref is ssm_ref:
                piece = lambda r: ref[part, r * SSM_WIDTH + s * SLAB:r * SSM_WIDTH + (s + 1) * SLAB]
            else:
                piece = lambda r: ref[t, r, part, lanes]
            if d == step:
                for r in range(d):
                    slab_ref[sub, buf, s, pl.ds(r, rows // d, stride=d), :] = piece(r).astype(F32)
            else:
                for r4 in range(step):
                    for j4 in range(step):
                        slab4_ref[sub, buf - 2, s, r4, pl.ds(j4, rows // d, stride=step), :] = (
                            piece(step * j4 + r4).astype(F32))
                for r4 in range(step):
                    slab_ref[sub, buf, s, pl.ds(r4, rows // step, stride=step), :] = slab4_ref[sub, buf - 2, s, r4]

    def in_order(sub, buf):
        return jnp.concatenate([slab_ref[sub, buf, s] for s in range(n_slabs)], axis=1)

    def merge_slab(sub, s):
        t, lanes = _slab_home(s)
        part = slice(sub * rows, (sub + 1) * rows)
        l1, l2, l3 = l1_ref[t, 0, part, lanes], slab_ref[sub, 1, s], slab_ref[sub, 3, s]
        m = jnp.maximum(jnp.maximum(l1, l2), l3)
        w1, w2, w3 = jnp.exp2(l1 - m), jnp.exp2(l2 - m), jnp.exp2(l3 - m)
        num = w1 * o1_ref[t, 0, part, lanes].astype(F32) + w2 * slab_ref[sub, 0, s] + w3 * slab_ref[sub, 2, s]
        slab_ref[sub, 0, s] = num / (w1 + w2 + w3)

    def merge_finish(sub):
        attn = in_order(sub, 0)
        ms = jnp.mean(attn * attn, axis=-1, keepdims=True)
        mix_ref[sub, :, 0:ATTN_WIDTH] = (attn * lax.rsqrt(ms + EPS) * ga_ref[...]).astype(BF16)
        mix_ref[sub, :, ATTN_WIDTH:] = in_order(sub, 4).astype(BF16)

    def prologue(sub):
        return ([lambda: regroup(o2_ref, step, sub, 0), lambda: regroup(l2_ref, step, sub, 1),
                 lambda: regroup(o3_ref, CHUNK, sub, 2), lambda: regroup(l3_ref, CHUNK, sub, 3),
                 lambda: regroup(ssm_ref, CHUNK, sub, 4)]
                + [lambda s=s: merge_slab(sub, s) for s in range(n_slabs)] + [lambda: merge_finish(sub)])

    def main(sub):
        part = slice(sub * rows, (sub + 1) * rows)
        st = {}

        def out_proj():
            x1 = x_ref[part, :] + jnp.dot(mix_ref[sub], wout_ref[...], preferred_element_type=F32)
            ms2 = jnp.mean(x1 * x1, axis=-1, keepdims=True)
            st["xn"] = (x1 * lax.rsqrt(ms2 + EPS) * g2_ref[...]).astype(BF16)
            st["acc"] = x1

        def up(c):
            st[c] = jnp.dot(st["xn"], wup_ref[:, c * FF_CHUNK:(c + 1) * FF_CHUNK], preferred_element_type=F32)

        def down(c):
            h = jnp.square(jnp.maximum(st.pop(c), 0.0)).astype(BF16)
            st["acc"] = st["acc"] + jnp.dot(h, wdn_ref[c * FF_CHUNK:(c + 1) * FF_CHUNK, :],
                                            preferred_element_type=F32)

        def finish():
            out_ref[part, :] = st["acc"]

        stages = [out_proj, lambda: up(0)]
        for c in range(n_ff):
            if c + 1 < n_ff:
                stages.append(lambda c=c: up(c + 1))
            stages.append(lambda c=c: down(c))
        return stages + [finish]

    tail = 3
    program = prologue(0)
    all_stages = [main(sub) for sub in range(n_sub)]
    for sub in range(n_sub):
        stages = all_stages[sub]
        head = stages if sub == 0 else stages[1:]
        if sub + 1 < n_sub:
            side = prologue(sub + 1)
            body = []
            for stage in head[:-tail]:
                body.append(stage)
                if side:
                    body.append(side.pop(0))
            program += body + side + [all_stages[sub + 1][0]] + head[-tail:]
        else:
            program += head
    for piece in program:
        piece()


def _mlp(x, os_, ls_, ssm, ga, w_out, g2, w_up, w_dn):
    s = x.shape[0]
    rows = MLP_ROWS
    xt = pl.BlockSpec((rows, D_MODEL), lambda i: (i, 0))
    ht = [pl.BlockSpec((N_HEAD_TILES, d, rows // d, LANE_TILE), lambda i: (0, 0, i, 0)) for d in DILATIONS]
    return pl.pallas_call(
        _mlp_kernel,
        grid=(s // rows,),
        in_specs=[xt, *ht, *ht, pl.BlockSpec((rows // CHUNK, CHUNK * SSM_WIDTH), lambda i: (i, 0)),
                  _const_spec(ga.shape), _const_spec(w_out.shape), _const_spec(g2.shape),
                  _const_spec(w_up.shape), _const_spec(w_dn.shape)],
        out_specs=xt,
        out_shape=jax.ShapeDtypeStruct((s, D_MODEL), F32),
        scratch_shapes=[pltpu.VMEM((rows // PROJ_SUB, 5, ATTN_WIDTH // SLAB, PROJ_SUB, SLAB), F32),
                        pltpu.VMEM((rows // PROJ_SUB, 3, ATTN_WIDTH // SLAB, DILATIONS[1],
                                    PROJ_SUB // DILATIONS[1], SLAB), F32),
                        pltpu.VMEM((rows // PROJ_SUB, PROJ_SUB, D_MODEL), BF16)],
        compiler_params=_params(dimension_semantics=("arbitrary",)),
        name="mlp",
    )(x, *os_, *ls_, ssm, ga, w_out, g2, w_up, w_dn)


def _layer(x, norm1_g, w_in, q_norm_g, k_norm_g, ssm_a_re, ssm_a_im, ssm_log_dt, ssm_b_re, ssm_b_im,
           ssm_c_re, ssm_c_im, ssm_d, glu_w, glu_b, attn_out_norm_g, ssm_out_norm_g, w_out, norm2_g,
           w_mlp_up, w_mlp_down):
    row = lambda g: g.astype(F32).reshape(1, -1)
    heads = ATTN_WIDTH // HEAD_DIM
    head_id = jnp.arange(LANE_TILE) // HEAD_DIM
    hmat = jnp.where(head_id[:, None] == head_id[None, :], 1.0 / HEAD_DIM, 0.0).astype(BF16)
    *qkv, u = _proj(x, row(norm1_g), w_in.astype(F32), hmat,
                    row(jnp.tile(q_norm_g, heads)), row(jnp.tile(k_norm_g, heads)))
    outs, lses = [], []
    mlp_weights = (w_out.astype(F32), w_mlp_up.astype(F32), w_mlp_down.astype(F32))
    for i in range(len(DILATIONS)):
        o, lse, cast = _attn_pattern(*qkv[3 * i:3 * i + 3], cast=mlp_weights if i == 0 else ())
        outs.append(o)
        lses.append(lse)
        if i == 0:
            w_out_b, w_up_b, w_dn_b = cast
    w = _s5_weights(ssm_a_re, ssm_a_im, ssm_log_dt, ssm_b_re, ssm_b_im, ssm_c_re, ssm_c_im, ssm_d)
    ssm = _s5(u, w, glu_w.astype(BF16), row(glu_b), row(ssm_out_norm_g))
    return _mlp(x, outs, lses, ssm, row(attn_out_norm_g), w_out_b, row(norm2_g), w_up_b, w_dn_b)


def kernel(x, norm1_g, w_in, q_norm_g, k_norm_g, ssm_a_re, ssm_a_im, ssm_log_dt, ssm_b_re, ssm_b_im,
           ssm_c_re, ssm_c_im, ssm_d, glu_w, glu_b, attn_out_norm_g, ssm_out_norm_g, w_out, norm2_g,
           w_mlp_up, w_mlp_down):
    params = (norm1_g, w_in, q_norm_g, k_norm_g, ssm_a_re, ssm_a_im, ssm_log_dt, ssm_b_re, ssm_b_im,
              ssm_c_re, ssm_c_im, ssm_d, glu_w, glu_b, attn_out_norm_g, ssm_out_norm_g, w_out, norm2_g,
              w_mlp_up, w_mlp_down)
    batch = x.shape[0]
    outs = []
    for bi in range(batch):
        h = x[bi]
        for layer in range(norm1_g.shape[0]):
            h = _layer(h, *[p[layer] for p in params])
        outs.append(h)
    return jnp.stack(outs)
```

```python
import functools
import math

import jax
import jax.numpy as jnp
from jax import lax
from jax.experimental import pallas as pl
from jax.experimental.pallas import tpu as pltpu

F32 = jnp.float32
BF16 = jnp.bfloat16

D_MODEL = 1024
ATTN_WIDTH = 512
HEAD_DIM = 64
SSM_WIDTH = 512
SSM_GROUP = 16
SSM_GROUPS = 32
SSM_STATE = 64
D_FF = 4096
EPS = 1e-6
NEG_INF = -1e30
DILATIONS = (1, 4, 16)
BLOCK = 128

SLAB = 128
LANE_TILE = 256
HEADS_PER_TILE = LANE_TILE // HEAD_DIM
N_HEAD_TILES = ATTN_WIDTH // LANE_TILE
CHUNK = 16
SUB = 2
PAIRS = 2
SUB_STEPS = SUB * PAIRS
N_SUB = CHUNK // SUB_STEPS
OCT = 8
N_OCT = SSM_GROUPS // OCT
SCAN_GROUP = 8
STATE_COLS = SSM_GROUPS * SSM_STATE
VMEM_LIMIT = 56 * 1024 * 1024

LOG2E = math.log2(math.e)
Q_SCALE = HEAD_DIM ** -0.5 * LOG2E

PROJ_ROWS = 1024
PROJ_SUB = 256
SLAB_BUFFERS = 2
PROJ_AHEAD = 2
ATTN_ROWS = 4096
S5_ROWS = 256
MLP_ROWS = 512
FF_CHUNK = 512


def _const_spec(shape):
    nd = len(shape)
    return pl.BlockSpec(shape, lambda *_: (0,) * nd, pipeline_mode=pl.Buffered(1))


def _params(**kw):
    return pltpu.CompilerParams(vmem_limit_bytes=VMEM_LIMIT, **kw)


def _proj_kernel(x_ref, g1_ref, w_ref, hmat_ref, gq_ref, gk_ref,
                 q1_ref, k1_ref, v1_ref, q4_ref, k4_ref, v4_ref, q16_ref, k16_ref, v16_ref, u_ref,
                 slab_ref, slab4_ref, wb_ref):
    @pl.when(pl.program_id(0) == 0)
    def _():
        wb_ref[...] = w_ref[...].astype(BF16)

    outs = ((q1_ref, q4_ref, q16_ref), (k1_ref, k4_ref, k16_ref), (v1_ref, v4_ref, v16_ref))
    slabs_per_tensor = ATTN_WIDTH // SLAB
    step = DILATIONS[1]
    rows = PROJ_SUB
    n_sub = x_ref.shape[0] // rows
    n_chunks = w_ref.shape[1] // LANE_TILE
    chunks_per_tensor = ATTN_WIDTH // LANE_TILE
    gains = (gq_ref, gk_ref)
    xn = [None] * n_sub

    def prologue(sub):
        x = x_ref[sub * rows:(sub + 1) * rows, :]
        ms = jnp.mean(x * x, axis=-1, keepdims=True)
        xn[sub] = (x * lax.rsqrt(ms + EPS) * g1_ref[...]).astype(BF16)

    def matmul(sub, c):
        return jnp.dot(xn[sub], wb_ref[:, c * LANE_TILE:(c + 1) * LANE_TILE], preferred_element_type=F32)

    def epilogue(sub, c, val):
        ti, t = divmod(c, chunks_per_tensor)
        lane_tile = slice(t * LANE_TILE, (t + 1) * LANE_TILE)
        if ti < 2:
            ms_h = jnp.dot((val * val).astype(BF16), hmat_ref[...], preferred_element_type=F32)
            val = val * lax.rsqrt(ms_h + EPS) * gains[ti][:, lane_tile]
            if ti == 0:
                val = val * Q_SCALE
        if ti < 3:
            outs[ti][0][t, 0, sub * rows:(sub + 1) * rows, :] = val.astype(BF16)
        r4_rows = slice(sub * (rows // step), (sub + 1) * (rows // step))
        r16_rows = slice(sub * (rows // CHUNK), (sub + 1) * (rows // CHUNK))
        for half in range(LANE_TILE // SLAB):
            s = t * (LANE_TILE // SLAB) + half
            lanes = slice(half * SLAB, (half + 1) * SLAB)
            slab = ti * slabs_per_tensor + s
            buf = sub % SLAB_BUFFERS
            slab_ref[buf, slab] = val[:, lanes]
            for r4 in range(step):
                part = slab_ref[buf, slab, pl.ds(r4, rows // step, stride=step), :]
                if ti < 3:
                    outs[ti][1][t, r4, r4_rows, lanes] = part.astype(BF16)
                slab4_ref[buf, slab, r4] = part
            for r4 in range(step):
                for j4 in range(step):
                    r16 = step * j4 + r4
                    part = slab4_ref[buf, slab, r4, pl.ds(j4, rows // CHUNK, stride=step), :].astype(BF16)
                    if ti < 3:
                        outs[ti][2][t, r16, r16_rows, lanes] = part
                    else:
                        u_ref[r16_rows, r16 * SSM_WIDTH + s * SLAB:r16 * SSM_WIDTH + (s + 1) * SLAB] = part

    work = [(sub, c) for sub in range(n_sub) for c in range(n_chunks)]
    prologue(0)
    pending = []
    for sub, c in work:
        pending.append((sub, c, matmul(sub, c)))
        if len(pending) > PROJ_AHEAD:
            epilogue(*pending.pop(0))
        if c == n_chunks // 2 and sub + 1 < n_sub:
            prologue(sub + 1)
    for item in pending:
        epilogue(*item)


def _slab_home(s):
    t, half = divmod(s, LANE_TILE // SLAB)
    return t, slice(half * SLAB, (half + 1) * SLAB)


def _proj(x, g1, w_in, hmat, gq, gk):
    s = x.shape[0]
    rows = PROJ_ROWS
    out_specs, out_shape = [], []
    for d in DILATIONS:
        for _ in range(3):
            out_specs.append(pl.BlockSpec((N_HEAD_TILES, d, rows // d, LANE_TILE), lambda i: (0, 0, i, 0)))
            out_shape.append(jax.ShapeDtypeStruct((N_HEAD_TILES, d, s // d, LANE_TILE), BF16))
    out_specs.append(pl.BlockSpec((rows // CHUNK, CHUNK * SSM_WIDTH), lambda i: (i, 0)))
    out_shape.append(jax.ShapeDtypeStruct((s // CHUNK, CHUNK * SSM_WIDTH), BF16))
    return pl.pallas_call(
        _proj_kernel,
        grid=(s // rows,),
        in_specs=[
            pl.BlockSpec((rows, D_MODEL), lambda i: (i, 0)),
            _const_spec((1, D_MODEL)),
            _const_spec(w_in.shape),
            _const_spec(hmat.shape),
            _const_spec((1, ATTN_WIDTH)),
            _const_spec((1, ATTN_WIDTH)),
        ],
        out_specs=out_specs,
        out_shape=out_shape,
        scratch_shapes=[pltpu.VMEM((SLAB_BUFFERS, 4 * ATTN_WIDTH // SLAB, PROJ_SUB, SLAB), F32),
                        pltpu.VMEM((SLAB_BUFFERS, 4 * ATTN_WIDTH // SLAB, DILATIONS[1],
                                    PROJ_SUB // DILATIONS[1], SLAB), F32),
                        pltpu.VMEM(w_in.shape, BF16)],
        compiler_params=_params(dimension_semantics=("arbitrary",)),
        name="proj",
    )(x, g1, w_in, hmat, gq, gk)


def _attn_kernel(q_ref, kc_ref, kp_ref, vc_ref, vp_ref, *rest):
    n_cast = (len(rest) - 4) // 2
    cast_in, (o_ref, lse_ref), cast_out = rest[:n_cast], rest[n_cast:n_cast + 2], rest[n_cast + 2:2 * n_cast + 2]
    kbuf, vbuf = rest[2 * n_cast + 2:]
    for src, dst in zip(cast_in, cast_out):
        dst[...] = src[...].astype(BF16)
    n_res, rows = q_ref.shape[0], q_ref.shape[1]
    kbuf[:, 0:BLOCK, :] = kp_ref[...]
    kbuf[:, BLOCK:, :] = kc_ref[...]
    vbuf[:, 0:BLOCK, :] = vp_ref[...]
    vbuf[:, BLOCK:, :] = vc_ref[...]

    lane = lax.broadcasted_iota(jnp.int32, (BLOCK, LANE_TILE), 1)
    head_masks = [(lane >= h * HEAD_DIM) & (lane < (h + 1) * HEAD_DIM) for h in range(HEADS_PER_TILE)]
    qi = lax.broadcasted_iota(jnp.int32, (BLOCK, 2 * BLOCK), 0)
    ki = lax.broadcasted_iota(jnp.int32, (BLOCK, 2 * BLOCK), 1)
    band = (ki >= qi) & (ki <= qi + BLOCK)
    bias_band = jnp.where(band, 0.0, NEG_INF).astype(F32)
    no_prev = pl.program_id(2) == 0
    bias_first = jnp.where(band & ((ki >= BLOCK) | jnp.logical_not(no_prev)), 0.0, NEG_INF).astype(F32)

    for r, b in [(r, b) for r in range(n_res) for b in range(rows // BLOCK)]:
        qb = q_ref[r, b * BLOCK:(b + 1) * BLOCK, :]
        zero = jnp.zeros_like(qb)
        q_stack = jnp.concatenate([jnp.where(mk, qb, zero) for mk in head_masks], axis=0)
        kw = kbuf[r, b * BLOCK:(b + 2) * BLOCK, :]
        vw = vbuf[r, b * BLOCK:(b + 2) * BLOCK, :]
        s = lax.dot_general(q_stack, kw, (((1,), (1,)), ((), ())), preferred_element_type=F32)
        bias = bias_first if b == 0 else bias_band
        ps, ms, ls = [], [], []
        for h in range(HEADS_PER_TILE):
            sh = s[h * BLOCK:(h + 1) * BLOCK, :] + bias
            m = jnp.max(sh, axis=-1, keepdims=True)
            p = jnp.exp2(sh - m)
            ls.append(jnp.sum(p, axis=-1, keepdims=True))
            ms.append(m)
            ps.append(p.astype(BF16))
        pv = jnp.dot(jnp.concatenate(ps, axis=0), vw, preferred_element_type=F32)
        o = jnp.zeros((BLOCK, LANE_TILE), F32)
        lse = jnp.zeros((BLOCK, LANE_TILE), F32)
        for h in range(HEADS_PER_TILE):
            o = jnp.where(head_masks[h], pv[h * BLOCK:(h + 1) * BLOCK, :] / ls[h], o)
            lse = jnp.where(head_masks[h], ms[h] + jnp.log2(ls[h]), lse)
        o_ref[r, b * BLOCK:(b + 1) * BLOCK, :] = o.astype(BF16)
        lse_ref[r, b * BLOCK:(b + 1) * BLOCK, :] = lse


def _attn_pattern(q, k, v, cast=()):
    nt, dilation, n, _ = q.shape
    rows = min(ATTN_ROWS, n)
    n_res = min(ATTN_ROWS // rows, dilation)
    blocks_per_tile = rows // BLOCK
    grid = (nt, dilation // n_res, n // rows)
    n_steps = grid[0] * grid[1] * grid[2]
    cur = pl.BlockSpec((None, n_res, rows, LANE_TILE), lambda t, r, j: (t, r, j, 0))
    prev = pl.BlockSpec((None, n_res, BLOCK, LANE_TILE),
                        lambda t, r, j: (t, r, jnp.maximum(j * blocks_per_tile - 1, 0), 0))
    step_id = lambda t, r, j: ((t * grid[1] + r) * grid[2] + j, 0)
    cast_specs = [pl.BlockSpec((w.shape[0] // n_steps, w.shape[1]), step_id) for w in cast]
    outs = pl.pallas_call(
        _attn_kernel,
        grid=grid,
        in_specs=[cur, cur, prev, cur, prev] + cast_specs,
        out_specs=[cur, cur] + cast_specs,
        out_shape=[jax.ShapeDtypeStruct(q.shape, BF16), jax.ShapeDtypeStruct(q.shape, F32)]
        + [jax.ShapeDtypeStruct(w.shape, BF16) for w in cast],
        scratch_shapes=[pltpu.VMEM((n_res, rows + BLOCK, LANE_TILE), BF16),
                        pltpu.VMEM((n_res, rows + BLOCK, LANE_TILE), BF16)],
        compiler_params=_params(dimension_semantics=("arbitrary", "arbitrary", "arbitrary")),
        name=f"attn_d{dilation}",
    )(q, k, k, v, v, *cast)
    return outs[0], outs[1], outs[2:]


def _s5_weights(a_re, a_im, log_dt, b_re, b_im, c_re, c_im, d_skip):
    p, c = SSM_STATE, SSM_GROUP
    lr, li = a_re.astype(F32), a_im.astype(F32)
    dt = jnp.exp(log_dt.astype(F32))[:, None]

    def apow(j):
        mag = jnp.exp(lr * dt * j)
        return mag * jnp.cos(li * dt * j), mag * jnp.sin(li * dt * j)

    ab_r, ab_i = apow(1.0)
    den = lr * lr + li * li
    nr, ni = ab_r - 1.0, ab_i
    cr = (nr * lr + ni * li) / den
    ci = (ni * lr - nr * li) / den
    br, bi = b_re.astype(F32), b_im.astype(F32)
    bb_r = cr[..., None] * br - ci[..., None] * bi
    bb_i = cr[..., None] * bi + ci[..., None] * br
    cre, cim = c_re.astype(F32), c_im.astype(F32)

    def apow_many(js):
        j = jnp.asarray(js, F32)[:, None, None]
        mag = jnp.exp(lr * dt * j)
        return mag * jnp.cos(li * dt * j), mag * jnp.sin(li * dt * j)

    pr, pi = apow_many(range(SUB_STEPS + 1))
    pad_lanes = lambda w: jnp.pad(w, [(0, 0)] * (w.ndim - 1) + [(0, SLAB - w.shape[-1])])
    wp_r = pad_lanes(jnp.stack([pr[SUB_STEPS - 1 - s] for s in range(SUB_STEPS)]))
    wp_i = pad_lanes(jnp.stack([pi[SUB_STEPS - 1 - s] for s in range(SUB_STEPS)]))
    bbt_r, bbt_i = pad_lanes(bb_r.transpose(0, 2, 1)), pad_lanes(bb_i.transpose(0, 2, 1))
    win_r = wp_r[:, :, None, :] * bbt_r[None] - wp_i[:, :, None, :] * bbt_i[None]
    win_i = wp_r[:, :, None, :] * bbt_i[None] + wp_i[:, :, None, :] * bbt_r[None]
    to_in = lambda w: (w.reshape(PAIRS, SUB, N_OCT, OCT, c, SLAB).transpose(0, 2, 1, 3, 4, 5)
                       .reshape(PAIRS, N_OCT, SUB * OCT * c, SLAB))
    ct_r, ct_i = pad_lanes(cre.transpose(0, 2, 1)), pad_lanes(cim.transpose(0, 2, 1))
    out_r = ct_r[None] * pr[..., None] - ct_i[None] * pi[..., None]
    out_i = -(ct_r[None] * pi[..., None] + ct_i[None] * pr[..., None])
    to_out = lambda w: w.reshape(SUB_STEPS + 1, N_OCT, OCT * p, SLAB)

    flat = lambda re, im: jnp.stack([re.reshape(-1), im.reshape(-1)])
    top_r, top_i = apow_many([CHUNK])
    lam = flat(top_r[0], top_i[0])
    levels = [lam]
    for _ in range(SCAN_GROUP - 1):
        lr_, li_ = levels[-1][0], levels[-1][1]
        levels.append(jnp.stack([lr_ * lam[0] - li_ * lam[1], lr_ * lam[1] + li_ * lam[0]]))
    m_t, w_r, w_i, e_r, e_i = _s5_expand(to_in(win_r), to_in(win_i), to_out(out_r), to_out(out_i),
                                         d_skip.astype(F32).reshape(N_OCT, 1, OCT * c))
    return dict(m_t=m_t, w_r=w_r, w_i=w_i, e_r=e_r, e_i=e_i,
                a_sub=flat(pr[SUB_STEPS], pi[SUB_STEPS]),
                levels=jnp.stack(levels, axis=1))


def _s5_expand_kernel(wr_ref, wi_ref, or_ref, oi_ref, d_ref, mt_ref, wfr_ref, wfi_ref, efr_ref, efi_ref):
    p, c = SSM_STATE, SSM_GROUP
    n_state, n_lane = OCT * p, OCT * c

    def rep(inner, total):
        r = lax.broadcasted_iota(jnp.int32, (SLAB, total), 0)
        col = lax.broadcasted_iota(jnp.int32, (SLAB, total), 1)
        return jnp.where((col & (inner - 1)) == r, 1.0, 0.0).astype(BF16)

    shift = lambda n: int(math.log2(n))
    rep_in, rep_out = rep(p, n_state), rep(c, n_lane)
    in_row = lax.broadcasted_iota(jnp.int32, (SUB * n_lane, n_state), 0)
    in_col = lax.broadcasted_iota(jnp.int32, (SUB * n_lane, n_state), 1)
    mask_in = ((in_row >> shift(c)) & (OCT - 1)) == (in_col >> shift(p))
    out_row = lax.broadcasted_iota(jnp.int32, (n_state, n_lane), 0)
    out_col = lax.broadcasted_iota(jnp.int32, (n_state, n_lane), 1)
    mask_out = (out_row >> shift(p)) == (out_col >> shift(c))
    diag = (lax.broadcasted_iota(jnp.int32, (n_lane, n_lane), 0)
            == lax.broadcasted_iota(jnp.int32, (n_lane, n_lane), 1))

    def split(x):
        hi = x.astype(BF16)
        return hi, (x - hi.astype(F32)).astype(BF16)

    def expand(x_b, rep_m, mask):
        full = jnp.dot(x_b, rep_m, preferred_element_type=F32)
        return jnp.where(mask, full, 0.0).astype(BF16)

    last = slice((SUB - 1) * n_lane, SUB * n_lane)
    for o in range(wfr_ref.shape[0]):
        in0 = []
        for src, dst in ((wr_ref, wfr_ref), (wi_ref, wfi_ref)):
            for q in range(PAIRS):
                hi, lo = split(src[q, o])
                full = expand(hi, rep_in, mask_in)
                dst[o, q * SUB * n_lane:(q + 1) * SUB * n_lane, :] = full
                if q == PAIRS - 1:
                    in0.append((full[last, :], expand(lo, rep_in, mask_in)[last, :]))
        lags = [None] * SUB_STEPS
        for (src, dst), (in_hi, in_lo) in zip(((or_ref, efr_ref), (oi_ref, efi_ref)), in0):
            outs_hi = []
            for j in range(SUB_STEPS + 1):
                hi, lo = split(src[j, o])
                out_hi = expand(hi, rep_out, mask_out)
                outs_hi.append(out_hi)
                if j < SUB_STEPS:
                    out_lo = expand(lo, rep_out, mask_out)
                    part = (jnp.dot(in_hi, out_hi, preferred_element_type=F32)
                            + jnp.dot(in_hi, out_lo, preferred_element_type=F32)
                            + jnp.dot(in_lo, out_hi, preferred_element_type=F32))
                    lags[j] = part if lags[j] is None else lags[j] + part
            dst[o] = jnp.concatenate(outs_hi[1:], axis=1)
        lags[0] = lags[0] + jnp.where(diag, d_ref[o], 0.0)
        zero = jnp.zeros_like(lags[0])
        for dist in range(PAIRS):
            lag = lambda t_in, t_out: SUB * dist + t_out - t_in
            mt_ref[dist, o] = jnp.concatenate(
                [jnp.concatenate([lags[lag(t_in, t_out)] if lag(t_in, t_out) >= 0 else zero
                                  for t_out in range(SUB)], axis=1)
                 for t_in in range(SUB)], axis=0).astype(BF16)


def _s5_expand(wc_r, wc_i, oc_r, oc_i, d_row):
    n_in, n_state = wc_r.shape[2], oc_r.shape[2]
    shapes = [jax.ShapeDtypeStruct((PAIRS, N_OCT, n_in, n_in), BF16),
              jax.ShapeDtypeStruct((N_OCT, PAIRS * n_in, n_state), BF16),
              jax.ShapeDtypeStruct((N_OCT, PAIRS * n_in, n_state), BF16),
              jax.ShapeDtypeStruct((N_OCT, n_state, PAIRS * n_in), BF16),
              jax.ShapeDtypeStruct((N_OCT, n_state, PAIRS * n_in), BF16)]
    def per_tile(shape, axis):
        block = tuple(1 if i == axis else n for i, n in enumerate(shape))
        return pl.BlockSpec(block, lambda o: tuple(o if i == axis else 0 for i in range(len(shape))))

    operands = (wc_r, wc_i, oc_r, oc_i, d_row)
    tile_axis = (1, 1, 1, 1, 0)
    return pl.pallas_call(
        _s5_expand_kernel,
        grid=(N_OCT,),
        in_specs=[per_tile(a.shape, ax) for a, ax in zip(operands, tile_axis)],
        out_specs=[per_tile(s.shape, ax) for s, ax in zip(shapes, (1, 0, 0, 0, 0))],
        out_shape=shapes,
        compiler_params=_params(dimension_semantics=("arbitrary",)),
        name="s5_expand",
    )(*operands)


def _cmul(ar, ai, br, bi):
    return ar * br - ai * bi, ar * bi + ai * br


def _s5_kernel(u_ref, mt_ref, wr_ref, wi_ref, er_ref, ei_ref, asub_ref, lvl_ref,
               gluw_ref, glub_ref, gout_ref, out_ref, carry_ref, zr_ref, zi_ref, locr_ref, loci_ref):
    rows = u_ref.shape[0]
    width = SSM_WIDTH
    oct_lanes = OCT * SSM_GROUP
    oct_state = OCT * SSM_STATE

    @pl.when(pl.program_id(0) == 0)
    def _():
        carry_ref[...] = jnp.zeros_like(carry_ref)

    def u_tile(a, q, o):
        first = a * SUB_STEPS + q * SUB
        return jnp.concatenate(
            [u_ref[:, (first + t) * width + o * oct_lanes:(first + t) * width + (o + 1) * oct_lanes]
             for t in range(SUB)], axis=1)

    a_r, a_i = asub_ref[0:1, :], asub_ref[1:2, :]

    for a in range(N_SUB):
        for o in range(N_OCT):
            cols = slice(o * oct_state, (o + 1) * oct_state)
            ut = jnp.concatenate([u_tile(a, q, o) for q in range(PAIRS)], axis=1)
            pr = jnp.dot(ut, wr_ref[o], preferred_element_type=F32)
            pi = jnp.dot(ut, wi_ref[o], preferred_element_type=F32)
            if a > 0:
                hr, hi = _cmul(zr_ref[:, cols], zi_ref[:, cols], a_r[:, cols], a_i[:, cols])
                pr, pi = hr + pr, hi + pi
            zr_ref[:, cols] = pr
            zi_ref[:, cols] = pi
            if a < N_SUB - 1:
                locr_ref[a, :, cols] = pr.astype(BF16)
                loci_ref[a, :, cols] = pi.astype(BF16)

    row = lax.broadcasted_iota(jnp.int32, (rows, 1), 0)
    in_group = row & (SCAN_GROUP - 1)
    sr, si = zr_ref[...], zi_ref[...]
    sh = 1
    while sh < SCAN_GROUP:
        keep = in_group >= sh
        pr_, pi_ = lvl_ref[0, sh - 1:sh, :], lvl_ref[1, sh - 1:sh, :]
        tr = jnp.where(keep, pltpu.roll(sr, sh, axis=0), 0.0)
        ti = jnp.where(keep, pltpu.roll(si, sh, axis=0), 0.0)
        mr, mi = _cmul(tr, ti, pr_, pi_)
        sr, si = sr + mr, si + mi
        sh *= 2
    c_r, c_i = carry_ref[0:1, :], carry_ref[1:2, :]
    pw_r, pw_i = lvl_ref[0], lvl_ref[1]
    last_r, last_i = c_r, c_i
    for g in range(rows // SCAN_GROUP):
        grp = slice(g * SCAN_GROUP, (g + 1) * SCAN_GROUP)
        ar, ai = _cmul(jnp.broadcast_to(last_r, pw_r.shape), jnp.broadcast_to(last_i, pw_i.shape), pw_r, pw_i)
        gr, gi = sr[grp, :] + ar, si[grp, :] + ai
        zr_ref[grp, :] = gr
        zi_ref[grp, :] = gi
        last_r, last_i = gr[SCAN_GROUP - 1:SCAN_GROUP, :], gi[SCAN_GROUP - 1:SCAN_GROUP, :]
    carry_ref[0:1, :] = last_r
    carry_ref[1:2, :] = last_i
    xr = jnp.where(row == 0, c_r, pltpu.roll(zr_ref[...], 1, axis=0))
    xi = jnp.where(row == 0, c_i, pltpu.roll(zi_ref[...], 1, axis=0))
    zr_ref[...] = xr
    zi_ref[...] = xi

    for a in range(N_SUB):
        if a > 0:
            nr, ni = _cmul(zr_ref[...], zi_ref[...], a_r, a_i)
            zr_ref[...] = nr
            zi_ref[...] = ni
        if a == 0:
            xr_b = zr_ref[...].astype(BF16)
            xi_b = zi_ref[...].astype(BF16)
        else:
            xr_b = (zr_ref[...] + locr_ref[a - 1].astype(F32)).astype(BF16)
            xi_b = (zi_ref[...] + loci_ref[a - 1].astype(F32)).astype(BF16)
        tiles = []
        for o in range(N_OCT):
            cols = slice(o * oct_state, (o + 1) * oct_state)
            from_state = (jnp.dot(xr_b[:, cols], er_ref[o], preferred_element_type=F32)
                          + jnp.dot(xi_b[:, cols], ei_ref[o], preferred_element_type=F32))
            parts = []
            for q in range(PAIRS):
                acc = from_state[:, q * SUB * oct_lanes:(q + 1) * SUB * oct_lanes]
                for q_in in range(q + 1):
                    acc = acc + jnp.dot(u_tile(a, q_in, o), mt_ref[q - q_in, o], preferred_element_type=F32)
                parts.append(acc)
            tiles.append(jnp.concatenate(parts, axis=1))
        for b in range(SUB_STEPS):
            t = a * SUB_STEPS + b
            y = jnp.concatenate([tl[:, b * oct_lanes:(b + 1) * oct_lanes] for tl in tiles], axis=1)
            z = 0.5 * y * (1.0 + jnp.tanh(math.sqrt(2.0 / math.pi) * (y + 0.044715 * (y * y * y))))
            gate = jnp.dot(z.astype(BF16), gluw_ref[...], preferred_element_type=F32) + glub_ref[...]
            o = z * (1.0 / (1.0 + jnp.exp(-gate)))
            ms = jnp.mean(o * o, axis=-1, keepdims=True)
            out_ref[:, t * width:(t + 1) * width] = (o * lax.rsqrt(ms + EPS) * gout_ref[...]).astype(BF16)


def _s5(uv, w, glu_w, glu_b, g_out):
    n = uv.shape[0]
    rows = S5_ROWS
    tile = pl.BlockSpec((rows, CHUNK * SSM_WIDTH), lambda i: (i, 0))
    return pl.pallas_call(
        _s5_kernel,
        grid=(n // rows,),
        in_specs=[tile] + [_const_spec(a.shape) for a in
                           (w["m_t"], w["w_r"], w["w_i"], w["e_r"], w["e_i"], w["a_sub"], w["levels"],
                            glu_w, glu_b, g_out)],
        out_specs=tile,
        out_shape=jax.ShapeDtypeStruct((n, CHUNK * SSM_WIDTH), BF16),
        scratch_shapes=[pltpu.VMEM((2, STATE_COLS), F32),
                        pltpu.VMEM((rows, STATE_COLS), F32),
                        pltpu.VMEM((rows, STATE_COLS), F32),
                        pltpu.VMEM((N_SUB - 1, rows, STATE_COLS), BF16),
                        pltpu.VMEM((N_SUB - 1, rows, STATE_COLS), BF16)],
        compiler_params=_params(dimension_semantics=("arbitrary",)),
        name="s5",
    )(uv, w["m_t"], w["w_r"], w["w_i"], w["e_r"], w["e_i"], w["a_sub"], w["levels"], glu_w, glu_b, g_out)


def _mlp_kernel(x_ref, o1_ref, o2_ref, o3_ref, l1_ref, l2_ref, l3_ref, ssm_ref, ga_ref, wout_ref,
                g2_ref, wup_ref, wdn_ref, out_ref, slab_ref, slab4_ref, mix_ref):
    rows = PROJ_SUB
    n_sub = x_ref.shape[0] // rows
    n_slabs = ATTN_WIDTH // SLAB
    step = DILATIONS[1]
    n_ff = D_FF // FF_CHUNK

    def regroup(ref, d, sub, buf):
        part = slice(sub * (rows // d), (sub + 1) * (rows // d))
        for s in range(n_slabs):
            t, lanes = _slab_home(s)
            if ref is ssm_ref:
                piece = lambda r: ref[part, r * SSM_WIDTH + s * SLAB:r * SSM_WIDTH + (s + 1) * SLAB]
            else:
                piece = lambda r: ref[t, r, part, lanes]
            if d == step:
                for r in range(d):
                    slab_ref[sub, buf, s, pl.ds(r, rows // d, stride=d), :] = piece(r).astype(F32)
            else:
                for r4 in range(step):
                    for j4 in range(step):
                        slab4_ref[sub, buf - 2, s, r4, pl.ds(j4, rows // d, stride=step), :] = (
                            piece(step * j4 + r4).astype(F32))
                for r4 in range(step):
                    slab_ref[sub, buf, s, pl.ds(r4, rows // step, stride=step), :] = slab4_ref[sub, buf - 2, s, r4]

    def in_order(sub, buf):
        return jnp.concatenate([slab_ref[sub, buf, s] for s in range(n_slabs)], axis=1)

    def merge_slab(sub, s):
        t, lanes = _slab_home(s)
        part = slice(sub * rows, (sub + 1) * rows)
        l1, l2, l3 = l1_ref[t, 0, part, lanes], slab_ref[sub, 1, s], slab_ref[sub, 3, s]
        m = jnp.maximum(jnp.maximum(l1, l2), l3)
        w1, w2, w3 = jnp.exp2(l1 - m), jnp.exp2(l2 - m), jnp.exp2(l3 - m)
        num = w1 * o1_ref[t, 0, part, lanes].astype(F32) + w2 * slab_ref[sub, 0, s] + w3 * slab_ref[sub, 2, s]
        slab_ref[sub, 0, s] = num / (w1 + w2 + w3)

    def merge_finish(sub):
        attn = in_order(sub, 0)
        ms = jnp.mean(attn * attn, axis=-1, keepdims=True)
        mix_ref[sub, :, 0:ATTN_WIDTH] = (attn * lax.rsqrt(ms + EPS) * ga_ref[...]).astype(BF16)
        mix_ref[sub, :, ATTN_WIDTH:] = in_order(sub, 4).astype(BF16)

    def prologue(sub):
        return ([lambda: regroup(o2_ref, step, sub, 0), lambda: regroup(l2_ref, step, sub, 1),
                 lambda: regroup(o3_ref, CHUNK, sub, 2), lambda: regroup(l3_ref, CHUNK, sub, 3),
                 lambda: regroup(ssm_ref, CHUNK, sub, 4)]
                + [lambda s=s: merge_slab(sub, s) for s in range(n_slabs)] + [lambda: merge_finish(sub)])

    def main(sub):
        part = slice(sub * rows, (sub + 1) * rows)
        st = {}

        def out_proj():
            x1 = x_ref[part, :] + jnp.dot(mix_ref[sub], wout_ref[...], preferred_element_type=F32)
            ms2 = jnp.mean(x1 * x1, axis=-1, keepdims=True)
            st["xn"] = (x1 * lax.rsqrt(ms2 + EPS) * g2_ref[...]).astype(BF16)
            st["acc"] = x1

        def up(c):
            st[c] = jnp.dot(st["xn"], wup_ref[:, c * FF_CHUNK:(c + 1) * FF_CHUNK], preferred_element_type=F32)

        def down(c):
            h = jnp.square(jnp.maximum(st.pop(c), 0.0)).astype(BF16)
            st["acc"] = st["acc"] + jnp.dot(h, wdn_ref[c * FF_CHUNK:(c + 1) * FF_CHUNK, :],
                                            preferred_element_type=F32)

        def finish():
            out_ref[part, :] = st["acc"]

        stages = [out_proj, lambda: up(0)]
        for c in range(n_ff):
            if c + 1 < n_ff:
                stages.append(lambda c=c: up(c + 1))
            stages.append(lambda c=c: down(c))
        return stages + [finish]

    tail = 3
    program = prologue(0)
    all_stages = [main(sub) for sub in range(n_sub)]
    for sub in range(n_sub):
        stages = all_stages[sub]
        head = stages if sub == 0 else stages[1:]
        if sub + 1 < n_sub:
            side = prologue(sub + 1)
            body = []
            for stage in head[:-tail]:
                body.append(stage)
                if side:
                    body.append(side.pop(0))
            program += body + side + [all_stages[sub + 1][0]] + head[-tail:]
        else:
            program += head
    for piece in program:
        piece()


def _mlp(x, os_, ls_, ssm, ga, w_out, g2, w_up, w_dn):
    s = x.shape[0]
    rows = MLP_ROWS
    xt = pl.BlockSpec((rows, D_MODEL), lambda i: (i, 0))
    ht = [pl.BlockSpec((N_HEAD_TILES, d, rows // d, LANE_TILE), lambda i: (0, 0, i, 0)) for d in DILATIONS]
    return pl.pallas_call(
        _mlp_kernel,
        grid=(s // rows,),
        in_specs=[xt, *ht, *ht, pl.BlockSpec((rows // CHUNK, CHUNK * SSM_WIDTH), lambda i: (i, 0)),
                  _const_spec(ga.shape), _const_spec(w_out.shape), _const_spec(g2.shape),
                  _const_spec(w_up.shape), _const_spec(w_dn.shape)],
        out_specs=xt,
        out_shape=jax.ShapeDtypeStruct((s, D_MODEL), F32),
        scratch_shapes=[pltpu.VMEM((rows // PROJ_SUB, 5, ATTN_WIDTH // SLAB, PROJ_SUB, SLAB), F32),
                        pltpu.VMEM((rows // PROJ_SUB, 3, ATTN_WIDTH // SLAB, DILATIONS[1],
                                    PROJ_SUB // DILATIONS[1], SLAB), F32),
                        pltpu.VMEM((rows // PROJ_SUB, PROJ_SUB, D_MODEL), BF16)],
        compiler_params=_params(dimension_semantics=("arbitrary",)),
        name="mlp",
    )(x, *os_, *ls_, ssm, ga, w_out, g2, w_up, w_dn)


def _layer(x, norm1_g, w_in, q_norm_g, k_norm_g, ssm_a_re, ssm_a_im, ssm_log_dt, ssm_b_re, ssm_b_im,
           ssm_c_re, ssm_c_im, ssm_d, glu_w, glu_b, attn_out_norm_g, ssm_out_norm_g, w_out, norm2_g,
           w_mlp_up, w_mlp_down):
    row = lambda g: g.astype(F32).reshape(1, -1)
    heads = ATTN_WIDTH // HEAD_DIM
    head_id = jnp.arange(LANE_TILE) // HEAD_DIM
    hmat = jnp.where(head_id[:, None] == head_id[None, :], 1.0 / HEAD_DIM, 0.0).astype(BF16)
    *qkv, u = _proj(x, row(norm1_g), w_in.astype(F32), hmat,
                    row(jnp.tile(q_norm_g, heads)), row(jnp.tile(k_norm_g, heads)))
    outs, lses = [], []
    mlp_weights = (w_out.astype(F32), w_mlp_up.astype(F32), w_mlp_down.astype(F32))
    for i in range(len(DILATIONS)):
        o, lse, cast = _attn_pattern(*qkv[3 * i:3 * i + 3], cast=mlp_weights if i == 0 else ())
        outs.append(o)
        lses.append(lse)
        if i == 0:
            w_out_b, w_up_b, w_dn_b = cast
    w = _s5_weights(ssm_a_re, ssm_a_im, ssm_log_dt, ssm_b_re, ssm_b_im, ssm_c_re, ssm_c_im, ssm_d)
    ssm = _s5(u, w, glu_w.astype(BF16), row(glu_b), row(ssm_out_norm_g))
    return _mlp(x, outs, lses, ssm, row(attn_out_norm_g), w_out_b, row(norm2_g), w_up_b, w_dn_b)


def kernel(x, norm1_g, w_in, q_norm_g, k_norm_g, ssm_a_re, ssm_a_im, ssm_log_dt, ssm_b_re, ssm_b_im,
           ssm_c_re, ssm_c_im, ssm_d, glu_w, glu_b, attn_out_norm_g, ssm_out_norm_g, w_out, norm2_g,
           w_mlp_up, w_mlp_down):
    params = (norm1_g, w_in, q_norm_g, k_norm_g, ssm_a_re, ssm_a_im, ssm_log_dt, ssm_b_re, ssm_b_im,
              ssm_c_re, ssm_c_im, ssm_d, glu_w, glu_b, attn_out_norm_g, ssm_out_norm_g, w_out, norm2_g,
              w_mlp_up, w_mlp_down)
    batch = x.shape[0]
    outs = []
    for bi in range(batch):
        h = x[bi]
        for layer in range(norm1_g.shape[0]):
            h = _layer(h, *[p[layer] for p in params])
        outs.append(h)
    return jnp.stack(outs)
```

```python
import functools
import math

import jax
import jax.numpy as jnp
from jax import lax
from jax.experimental import pallas as pl
from jax.experimental.pallas import tpu as pltpu

F32 = jnp.float32
BF16 = jnp.bfloat16

D_MODEL = 1024
ATTN_WIDTH = 512
HEAD_DIM = 64
SSM_WIDTH = 512
SSM_GROUP = 16
SSM_GROUPS = 32
SSM_STATE = 64
D_FF = 4096
EPS = 1e-6
NEG_INF = -1e30
DILATIONS = (1, 4, 16)
BLOCK = 128

SLAB = 128
LANE_TILE = 256
HEADS_PER_TILE = LANE_TILE // HEAD_DIM
N_HEAD_TILES = ATTN_WIDTH // LANE_TILE
CHUNK = 16
SUB = 2
PAIRS = 2
SUB_STEPS = SUB * PAIRS
N_SUB = CHUNK // SUB_STEPS
OCT = 8
N_OCT = SSM_GROUPS // OCT
SCAN_GROUP = 8
STATE_COLS = SSM_GROUPS * SSM_STATE
VMEM_LIMIT = 56 * 1024 * 1024

LOG2E = math.log2(math.e)
GELU_C = 2.0 * math.sqrt(2.0 / math.pi)
Q_SCALE = HEAD_DIM ** -0.5 * LOG2E

PROJ_ROWS = 1024
PROJ_SUB = 256
SLAB_BUFFERS = 2
PROJ_AHEAD = 2
ATTN_ROWS = 4096
S5_ROWS = 256
MLP_ROWS = 512
FF_CHUNK = 512


def _const_spec(shape):
    nd = len(shape)
    return pl.BlockSpec(shape, lambda *_: (0,) * nd, pipeline_mode=pl.Buffered(1))


def _params(**kw):
    return pltpu.CompilerParams(vmem_limit_bytes=VMEM_LIMIT, **kw)


def _proj_kernel(x_ref, g1_ref, w_ref, hmat_ref, gq_ref, gk_ref,
                 q1_ref, k1_ref, v1_ref, q4_ref, k4_ref, v4_ref, q16_ref, k16_ref, v16_ref, u_ref,
                 slab_ref, slab4_ref, wb_ref):
    @pl.when(pl.program_id(0) == 0)
    def _():
        wb_ref[...] = w_ref[...].astype(BF16)

    outs = ((q1_ref, q4_ref, q16_ref), (k1_ref, k4_ref, k16_ref), (v1_ref, v4_ref, v16_ref))
    slabs_per_tensor = ATTN_WIDTH // SLAB
    step = DILATIONS[1]
    rows = PROJ_SUB
    n_sub = x_ref.shape[0] // rows
    n_chunks = w_ref.shape[1] // LANE_TILE
    chunks_per_tensor = ATTN_WIDTH // LANE_TILE
    gains = (gq_ref, gk_ref)
    xn = [None] * n_sub

    def prologue(sub):
        x = x_ref[sub * rows:(sub + 1) * rows, :]
        ms = jnp.mean(x * x, axis=-1, keepdims=True)
        xn[sub] = (x * lax.rsqrt(ms + EPS) * g1_ref[...]).astype(BF16)

    def matmul(sub, c):
        return jnp.dot(xn[sub], wb_ref[:, c * LANE_TILE:(c + 1) * LANE_TILE], preferred_element_type=F32)

    def epilogue(sub, c, val):
        ti, t = divmod(c, chunks_per_tensor)
        lane_tile = slice(t * LANE_TILE, (t + 1) * LANE_TILE)
        if ti < 2:
            ms_h = jnp.dot((val * val).astype(BF16), hmat_ref[...], preferred_element_type=F32)
            val = val * lax.rsqrt(ms_h + EPS) * gains[ti][:, lane_tile]
            if ti == 0:
                val = val * Q_SCALE
        if ti < 3:
            outs[ti][0][t, 0, sub * rows:(sub + 1) * rows, :] = val.astype(BF16)
        r4_rows = slice(sub * (rows // step), (sub + 1) * (rows // step))
        r16_rows = slice(sub * (rows // CHUNK), (sub + 1) * (rows // CHUNK))
        for half in range(LANE_TILE // SLAB):
            s = t * (LANE_TILE // SLAB) + half
            lanes = slice(half * SLAB, (half + 1) * SLAB)
            slab = ti * slabs_per_tensor + s
            buf = sub % SLAB_BUFFERS
            slab_ref[buf, slab] = val[:, lanes]
            for r4 in range(step):
                part = slab_ref[buf, slab, pl.ds(r4, rows // step, stride=step), :]
                if ti < 3:
                    outs[ti][1][t, r4, r4_rows, lanes] = part.astype(BF16)
                slab4_ref[buf, slab, r4] = part
            for r4 in range(step):
                for j4 in range(step):
                    r16 = step * j4 + r4
                    part = slab4_ref[buf, slab, r4, pl.ds(j4, rows // CHUNK, stride=step), :].astype(BF16)
                    if ti < 3:
                        outs[ti][2][t, r16, r16_rows, lanes] = part
                    else:
                        u_ref[r16_rows, r16 * SSM_WIDTH + s * SLAB:r16 * SSM_WIDTH + (s + 1) * SLAB] = part

    work = [(sub, c) for sub in range(n_sub) for c in range(n_chunks)]
    prologue(0)
    pending = []
    for sub, c in work:
        pending.append((sub, c, matmul(sub, c)))
        if len(pending) > PROJ_AHEAD:
            epilogue(*pending.pop(0))
        if c == n_chunks // 2 and sub + 1 < n_sub:
            prologue(sub + 1)
    for item in pending:
        epilogue(*item)


def _slab_home(s):
    t, half = divmod(s, LANE_TILE // SLAB)
    return t, slice(half * SLAB, (half + 1) * SLAB)


def _proj(x, g1, w_in, hmat, gq, gk):
    s = x.shape[0]
    rows = PROJ_ROWS
    out_specs, out_shape = [], []
    for d in DILATIONS:
        for _ in range(3):
            out_specs.append(pl.BlockSpec((N_HEAD_TILES, d, rows // d, LANE_TILE), lambda i: (0, 0, i, 0)))
            out_shape.append(jax.ShapeDtypeStruct((N_HEAD_TILES, d, s // d, LANE_TILE), BF16))
    out_specs.append(pl.BlockSpec((rows // CHUNK, CHUNK * SSM_WIDTH), lambda i: (i, 0)))
    out_shape.append(jax.ShapeDtypeStruct((s // CHUNK, CHUNK * SSM_WIDTH), BF16))
    return pl.pallas_call(
        _proj_kernel,
        grid=(s // rows,),
        in_specs=[
            pl.BlockSpec((rows, D_MODEL), lambda i: (i, 0)),
            _const_spec((1, D_MODEL)),
            _const_spec(w_in.shape),
            _const_spec(hmat.shape),
            _const_spec((1, ATTN_WIDTH)),
            _const_spec((1, ATTN_WIDTH)),
        ],
        out_specs=out_specs,
        out_shape=out_shape,
        scratch_shapes=[pltpu.VMEM((SLAB_BUFFERS, 4 * ATTN_WIDTH // SLAB, PROJ_SUB, SLAB), F32),
                        pltpu.VMEM((SLAB_BUFFERS, 4 * ATTN_WIDTH // SLAB, DILATIONS[1],
                                    PROJ_SUB // DILATIONS[1], SLAB), F32),
                        pltpu.VMEM(w_in.shape, BF16)],
        compiler_params=_params(dimension_semantics=("arbitrary",)),
        name="proj",
    )(x, g1, w_in, hmat, gq, gk)


def _attn_kernel(q_ref, kc_ref, kp_ref, vc_ref, vp_ref, *rest):
    n_cast = (len(rest) - 4) // 2
    cast_in, (o_ref, lse_ref), cast_out = rest[:n_cast], rest[n_cast:n_cast + 2], rest[n_cast + 2:2 * n_cast + 2]
    kbuf, vbuf = rest[2 * n_cast + 2:]
    for src, dst in zip(cast_in, cast_out):
        dst[...] = src[...].astype(BF16)
    n_res, rows = q_ref.shape[0], q_ref.shape[1]
    kbuf[:, 0:BLOCK, :] = kp_ref[...]
    kbuf[:, BLOCK:, :] = kc_ref[...]
    vbuf[:, 0:BLOCK, :] = vp_ref[...]
    vbuf[:, BLOCK:, :] = vc_ref[...]

    lane = lax.broadcasted_iota(jnp.int32, (BLOCK, LANE_TILE), 1)
    head_masks = [(lane >= h * HEAD_DIM) & (lane < (h + 1) * HEAD_DIM) for h in range(HEADS_PER_TILE)]
    qi = lax.broadcasted_iota(jnp.int32, (BLOCK, 2 * BLOCK), 0)
    ki = lax.broadcasted_iota(jnp.int32, (BLOCK, 2 * BLOCK), 1)
    band = (ki >= qi) & (ki <= qi + BLOCK)
    bias_band = jnp.where(band, 0.0, NEG_INF).astype(F32)
    no_prev = pl.program_id(2) == 0
    bias_first = jnp.where(band & ((ki >= BLOCK) | jnp.logical_not(no_prev)), 0.0, NEG_INF).astype(F32)

    for r, b in [(r, b) for r in range(n_res) for b in range(rows // BLOCK)]:
        qb = q_ref[r, b * BLOCK:(b + 1) * BLOCK, :]
        zero = jnp.zeros_like(qb)
        q_stack = jnp.concatenate([jnp.where(mk, qb, zero) for mk in head_masks], axis=0)
        kw = kbuf[r, b * BLOCK:(b + 2) * BLOCK, :]
        vw = vbuf[r, b * BLOCK:(b + 2) * BLOCK, :]
        s = lax.dot_general(q_stack, kw, (((1,), (1,)), ((), ())), preferred_element_type=F32)
        bias = bias_first if b == 0 else bias_band
        ps, ms, ls = [], [], []
        for h in range(HEADS_PER_TILE):
            sh = s[h * BLOCK:(h + 1) * BLOCK, :] + bias
            m = jnp.max(sh, axis=-1, keepdims=True)
            p = jnp.exp2(sh - m)
            ls.append(jnp.sum(p, axis=-1, keepdims=True))
            ms.append(m)
            ps.append(p.astype(BF16))
        pv = jnp.dot(jnp.concatenate(ps, axis=0), vw, preferred_element_type=F32)
        o = jnp.zeros((BLOCK, LANE_TILE), F32)
        lse = jnp.zeros((BLOCK, LANE_TILE), F32)
        for h in range(HEADS_PER_TILE):
            o = jnp.where(head_masks[h], pv[h * BLOCK:(h + 1) * BLOCK, :] / ls[h], o)
            lse = jnp.where(head_masks[h], ms[h] + jnp.log2(ls[h]), lse)
        o_ref[r, b * BLOCK:(b + 1) * BLOCK, :] = o.astype(BF16)
        lse_ref[r, b * BLOCK:(b + 1) * BLOCK, :] = lse


def _attn_pattern(q, k, v, cast=()):
    nt, dilation, n, _ = q.shape
    rows = min(ATTN_ROWS, n)
    n_res = min(ATTN_ROWS // rows, dilation)
    blocks_per_tile = rows // BLOCK
    grid = (nt, dilation // n_res, n // rows)
    n_steps = grid[0] * grid[1] * grid[2]
    cur = pl.BlockSpec((None, n_res, rows, LANE_TILE), lambda t, r, j: (t, r, j, 0))
    prev = pl.BlockSpec((None, n_res, BLOCK, LANE_TILE),
                        lambda t, r, j: (t, r, jnp.maximum(j * blocks_per_tile - 1, 0), 0))
    step_id = lambda t, r, j: ((t * grid[1] + r) * grid[2] + j, 0)
    cast_specs = [pl.BlockSpec((w.shape[0] // n_steps, w.shape[1]), step_id) for w in cast]
    outs = pl.pallas_call(
        _attn_kernel,
        grid=grid,
        in_specs=[cur, cur, prev, cur, prev] + cast_specs,
        out_specs=[cur, cur] + cast_specs,
        out_shape=[jax.ShapeDtypeStruct(q.shape, BF16), jax.ShapeDtypeStruct(q.shape, F32)]
        + [jax.ShapeDtypeStruct(w.shape, BF16) for w in cast],
        scratch_shapes=[pltpu.VMEM((n_res, rows + BLOCK, LANE_TILE), BF16),
                        pltpu.VMEM((n_res, rows + BLOCK, LANE_TILE), BF16)],
        compiler_params=_params(dimension_semantics=("arbitrary", "arbitrary", "arbitrary")),
        name=f"attn_d{dilation}",
    )(q, k, k, v, v, *cast)
    return outs[0], outs[1], outs[2:]


def _s5_weights(a_re, a_im, log_dt, b_re, b_im, c_re, c_im, d_skip):
    p, c = SSM_STATE, SSM_GROUP
    lr, li = a_re.astype(F32), a_im.astype(F32)
    dt = jnp.exp(log_dt.astype(F32))[:, None]

    def apow(j):
        mag = jnp.exp(lr * dt * j)
        return mag * jnp.cos(li * dt * j), mag * jnp.sin(li * dt * j)

    ab_r, ab_i = apow(1.0)
    den = lr * lr + li * li
    nr, ni = ab_r - 1.0, ab_i
    cr = (nr * lr + ni * li) / den
    ci = (ni * lr - nr * li) / den
    br, bi = b_re.astype(F32), b_im.astype(F32)
    bb_r = cr[..., None] * br - ci[..., None] * bi
    bb_i = cr[..., None] * bi + ci[..., None] * br
    cre, cim = c_re.astype(F32), c_im.astype(F32)

    def apow_many(js):
        j = jnp.asarray(js, F32)[:, None, None]
        mag = jnp.exp(lr * dt * j)
        return mag * jnp.cos(li * dt * j), mag * jnp.sin(li * dt * j)

    pr, pi = apow_many(range(SUB_STEPS + 1))
    wp_r = jnp.stack([pr[SUB_STEPS - 1 - s] for s in range(SUB_STEPS)])
    wp_i = jnp.stack([pi[SUB_STEPS - 1 - s] for s in range(SUB_STEPS)])
    win_r = wp_r[..., None] * bb_r[None] - wp_i[..., None] * bb_i[None]
    win_i = wp_r[..., None] * bb_i[None] + wp_i[..., None] * bb_r[None]
    pad_lanes = lambda w: jnp.pad(w, [(0, 0)] * (w.ndim - 1) + [(0, SLAB - w.shape[-1])])
    to_in = lambda w: pad_lanes(w.reshape(PAIRS, SUB, N_OCT, OCT, p, c).transpose(0, 2, 1, 3, 5, 4)
                                .reshape(PAIRS, N_OCT, SUB * OCT * c, p))
    out_r = cre[None] * pr[:, :, None, :] - cim[None] * pi[:, :, None, :]
    out_i = -(cre[None] * pi[:, :, None, :] + cim[None] * pr[:, :, None, :])
    to_out = lambda w: pad_lanes(w.reshape(SUB_STEPS + 1, N_OCT, OCT, c, p).transpose(0, 1, 2, 4, 3)
                                 .reshape(SUB_STEPS + 1, N_OCT, OCT * p, c))

    flat = lambda re, im: jnp.stack([re.reshape(-1), im.reshape(-1)])
    top_r, top_i = apow_many([CHUNK])
    lam = flat(top_r[0], top_i[0])
    levels = [lam]
    for _ in range(SCAN_GROUP - 1):
        lr_, li_ = levels[-1][0], levels[-1][1]
        levels.append(jnp.stack([lr_ * lam[0] - li_ * lam[1], lr_ * lam[1] + li_ * lam[0]]))
    g_out, w_r, w_i = _s5_expand(to_in(win_r), to_in(win_i), to_out(out_r), to_out(out_i),
                                 d_skip.astype(F32).reshape(N_OCT, 1, OCT * c))
    return dict(g_out=g_out, w_r=w_r, w_i=w_i,
                a_sub=flat(pr[SUB_STEPS], pi[SUB_STEPS]),
                levels=jnp.stack(levels, axis=1))


def _s5_expand_kernel(wr_ref, wi_ref, or_ref, oi_ref, d_ref, g_ref, wfr_ref, wfi_ref):
    p, c = SSM_STATE, SSM_GROUP
    n_state, n_lane = OCT * p, OCT * c

    def rep(inner, total):
        r = lax.broadcasted_iota(jnp.int32, (SLAB, total), 0)
        col = lax.broadcasted_iota(jnp.int32, (SLAB, total), 1)
        return jnp.where((col & (inner - 1)) == r, 1.0, 0.0).astype(BF16)

    shift = lambda n: int(math.log2(n))
    rep_in, rep_out = rep(p, n_state), rep(c, n_lane)
    in_row = lax.broadcasted_iota(jnp.int32, (SUB * n_lane, n_state), 0)
    in_col = lax.broadcasted_iota(jnp.int32, (SUB * n_lane, n_state), 1)
    mask_in = ((in_row >> shift(c)) & (OCT - 1)) == (in_col >> shift(p))
    out_row = lax.broadcasted_iota(jnp.int32, (n_state, n_lane), 0)
    out_col = lax.broadcasted_iota(jnp.int32, (n_state, n_lane), 1)
    mask_out = (out_row >> shift(p)) == (out_col >> shift(c))
    diag = (lax.broadcasted_iota(jnp.int32, (n_lane, n_lane), 0)
            == lax.broadcasted_iota(jnp.int32, (n_lane, n_lane), 1))

    def split(x):
        hi = x.astype(BF16)
        return hi, (x - hi.astype(F32)).astype(BF16)

    def expand(x_b, rep_m, mask):
        full = jnp.dot(x_b, rep_m, preferred_element_type=F32)
        return jnp.where(mask, full, 0.0).astype(BF16)

    last = slice((SUB - 1) * n_lane, SUB * n_lane)
    for o in range(wfr_ref.shape[0]):
        in0 = []
        for src, dst in ((wr_ref, wfr_ref), (wi_ref, wfi_ref)):
            for q in range(PAIRS):
                hi, lo = split(src[q, o])
                full = expand(hi, rep_in, mask_in)
                dst[o, q * SUB * n_lane:(q + 1) * SUB * n_lane, :] = full
                if q == PAIRS - 1:
                    in0.append((full[last, :], expand(lo, rep_in, mask_in)[last, :]))
        lags = [None] * SUB_STEPS
        for part_no, (src, (in_hi, in_lo)) in enumerate(zip((or_ref, oi_ref), in0)):
            outs_hi = []
            for j in range(SUB_STEPS + 1):
                hi, lo = split(src[j, o])
                out_hi = expand(hi, rep_out, mask_out)
                outs_hi.append(out_hi)
                if j < SUB_STEPS:
                    out_lo = expand(lo, rep_out, mask_out)
                    part = (jnp.dot(in_hi, out_hi, preferred_element_type=F32)
                            + jnp.dot(in_hi, out_lo, preferred_element_type=F32)
                            + jnp.dot(in_lo, out_hi, preferred_element_type=F32))
                    lags[j] = part if lags[j] is None else lags[j] + part
            g_ref[o, part_no * n_state:(part_no + 1) * n_state, :] = jnp.concatenate(outs_hi[1:], axis=1)
        lags[0] = lags[0] + jnp.where(diag, d_ref[o], 0.0)
        zero = jnp.zeros_like(lags[0])
        g_ref[o, 2 * n_state:, :] = jnp.concatenate(
            [jnp.concatenate([lags[s_out - s_in] if s_out >= s_in else zero for s_out in range(SUB_STEPS)], axis=1)
             for s_in in range(SUB_STEPS)], axis=0).astype(BF16)


def _s5_expand(wc_r, wc_i, oc_r, oc_i, d_row):
    n_in, n_state = wc_r.shape[2], oc_r.shape[2]
    shapes = [jax.ShapeDtypeStruct((N_OCT, 2 * n_state + PAIRS * n_in, PAIRS * n_in), BF16),
              jax.ShapeDtypeStruct((N_OCT, PAIRS * n_in, n_state), BF16),
              jax.ShapeDtypeStruct((N_OCT, PAIRS * n_in, n_state), BF16)]
    def per_tile(shape, axis):
        block = tuple(1 if i == axis else n for i, n in enumerate(shape))
        return pl.BlockSpec(block, lambda o: tuple(o if i == axis else 0 for i in range(len(shape))))

    operands = (wc_r, wc_i, oc_r, oc_i, d_row)
    tile_axis = (1, 1, 1, 1, 0)
    return pl.pallas_call(
        _s5_expand_kernel,
        grid=(N_OCT,),
        in_specs=[per_tile(a.shape, ax) for a, ax in zip(operands, tile_axis)],
        out_specs=[per_tile(s.shape, 0) for s in shapes],
        out_shape=shapes,
        compiler_params=_params(dimension_semantics=("arbitrary",)),
        name="s5_expand",
    )(*operands)


def _cmul(ar, ai, br, bi):
    return ar * br - ai * bi, ar * bi + ai * br


def _s5_kernel(u_ref, g_ref, wr_ref, wi_ref, asub_ref, lvl_ref,
               gluw_ref, glub_ref, gout_ref, out_ref, carry_ref, zr_ref, zi_ref, locr_ref, loci_ref):
    rows = u_ref.shape[0]
    width = SSM_WIDTH
    oct_lanes = OCT * SSM_GROUP
    oct_state = OCT * SSM_STATE

    @pl.when(pl.program_id(0) == 0)
    def _():
        carry_ref[...] = jnp.zeros_like(carry_ref)

    def u_tile(a, q, o):
        first = a * SUB_STEPS + q * SUB
        return jnp.concatenate(
            [u_ref[:, (first + t) * width + o * oct_lanes:(first + t) * width + (o + 1) * oct_lanes]
             for t in range(SUB)], axis=1)

    a_r, a_i = asub_ref[0:1, :], asub_ref[1:2, :]

    for a in range(N_SUB):
        for o in range(N_OCT):
            cols = slice(o * oct_state, (o + 1) * oct_state)
            ut = jnp.concatenate([u_tile(a, q, o) for q in range(PAIRS)], axis=1)
            pr = jnp.dot(ut, wr_ref[o], preferred_element_type=F32)
            pi = jnp.dot(ut, wi_ref[o], preferred_element_type=F32)
            if a > 0:
                hr, hi = _cmul(zr_ref[:, cols], zi_ref[:, cols], a_r[:, cols], a_i[:, cols])
                pr, pi = hr + pr, hi + pi
            zr_ref[:, cols] = pr
            zi_ref[:, cols] = pi
            if a < N_SUB - 1:
                locr_ref[a, :, cols] = pr.astype(BF16)
                loci_ref[a, :, cols] = pi.astype(BF16)

    row = lax.broadcasted_iota(jnp.int32, (rows, 1), 0)
    in_group = row & (SCAN_GROUP - 1)
    sr, si = zr_ref[...], zi_ref[...]
    sh = 1
    while sh < SCAN_GROUP:
        keep = in_group >= sh
        pr_, pi_ = lvl_ref[0, sh - 1:sh, :], lvl_ref[1, sh - 1:sh, :]
        tr = jnp.where(keep, pltpu.roll(sr, sh, axis=0), 0.0)
        ti = jnp.where(keep, pltpu.roll(si, sh, axis=0), 0.0)
        mr, mi = _cmul(tr, ti, pr_, pi_)
        sr, si = sr + mr, si + mi
        sh *= 2
    c_r, c_i = carry_ref[0:1, :], carry_ref[1:2, :]
    pw_r, pw_i = lvl_ref[0], lvl_ref[1]
    last_r, last_i = c_r, c_i
    for g in range(rows // SCAN_GROUP):
        grp = slice(g * SCAN_GROUP, (g + 1) * SCAN_GROUP)
        ar, ai = _cmul(jnp.broadcast_to(last_r, pw_r.shape), jnp.broadcast_to(last_i, pw_i.shape), pw_r, pw_i)
        gr, gi = sr[grp, :] + ar, si[grp, :] + ai
        zr_ref[grp, :] = gr
        zi_ref[grp, :] = gi
        last_r, last_i = gr[SCAN_GROUP - 1:SCAN_GROUP, :], gi[SCAN_GROUP - 1:SCAN_GROUP, :]
    carry_ref[0:1, :] = last_r
    carry_ref[1:2, :] = last_i
    xr = jnp.where(row == 0, c_r, pltpu.roll(zr_ref[...], 1, axis=0))
    xi = jnp.where(row == 0, c_i, pltpu.roll(zi_ref[...], 1, axis=0))
    zr_ref[...] = xr
    zi_ref[...] = xi

    for a in range(N_SUB):
        if a > 0:
            nr, ni = _cmul(zr_ref[...], zi_ref[...], a_r, a_i)
            zr_ref[...] = nr
            zi_ref[...] = ni
        if a == 0:
            xr_b = zr_ref[...].astype(BF16)
            xi_b = zi_ref[...].astype(BF16)
        else:
            xr_b = (zr_ref[...] + locr_ref[a - 1].astype(F32)).astype(BF16)
            xi_b = (zi_ref[...] + loci_ref[a - 1].astype(F32)).astype(BF16)
        tiles = []
        for o in range(N_OCT):
            cols = slice(o * oct_state, (o + 1) * oct_state)
            lhs = jnp.concatenate([xr_b[:, cols], xi_b[:, cols]] + [u_tile(a, q, o) for q in range(PAIRS)], axis=1)
            tiles.append(jnp.dot(lhs, g_ref[o], preferred_element_type=F32))
        for b in range(SUB_STEPS):
            t = a * SUB_STEPS + b
            y = jnp.concatenate([tl[:, b * oct_lanes:(b + 1) * oct_lanes] for tl in tiles], axis=1)
            y2 = y * y
            z = y * (1.0 / (1.0 + jnp.exp2(y * (y2 * (-GELU_C * 0.044715 * LOG2E) - GELU_C * LOG2E))))
            gate = jnp.dot(z.astype(BF16), gluw_ref[...], preferred_element_type=F32) + glub_ref[...]
            o = z * (1.0 / (1.0 + jnp.exp2(gate)))
            ms = jnp.mean(o * o, axis=-1, keepdims=True)
            out_ref[:, t * width:(t + 1) * width] = (o * lax.rsqrt(ms + EPS) * gout_ref[...]).astype(BF16)


def _s5(uv, w, glu_w, glu_b, g_out):
    n = uv.shape[0]
    rows = S5_ROWS
    tile = pl.BlockSpec((rows, CHUNK * SSM_WIDTH), lambda i: (i, 0))
    return pl.pallas_call(
        _s5_kernel,
        grid=(n // rows,),
        in_specs=[tile] + [_const_spec(a.shape) for a in
                           (w["g_out"], w["w_r"], w["w_i"], w["a_sub"], w["levels"], glu_w, glu_b, g_out)],
        out_specs=tile,
        out_shape=jax.ShapeDtypeStruct((n, CHUNK * SSM_WIDTH), BF16),
        scratch_shapes=[pltpu.VMEM((2, STATE_COLS), F32),
                        pltpu.VMEM((rows, STATE_COLS), F32),
                        pltpu.VMEM((rows, STATE_COLS), F32),
                        pltpu.VMEM((N_SUB - 1, rows, STATE_COLS), BF16),
                        pltpu.VMEM((N_SUB - 1, rows, STATE_COLS), BF16)],
        compiler_params=_params(dimension_semantics=("arbitrary",)),
        name="s5",
    )(uv, w["g_out"], w["w_r"], w["w_i"], w["a_sub"], w["levels"], glu_w, glu_b, g_out)


def _mlp_kernel(x_ref, o1_ref, o2_ref, o3_ref, l1_ref, l2_ref, l3_ref, ssm_ref, ga_ref, wout_ref,
                g2_ref, wup_ref, wdn_ref, out_ref, slab_ref, slab4_ref, mix_ref):
    rows = PROJ_SUB
    n_sub = x_ref.shape[0] // rows
    n_slabs = ATTN_WIDTH // SLAB
    step = DILATIONS[1]
    n_ff = D_FF // FF_CHUNK

    def regroup(ref, d, sub, buf):
        part = slice(sub * (rows // d), (sub + 1) * (rows // d))
        for s in range(n_slabs):
            t, lanes = _slab_home(s)
            if ref is ssm_ref:
                piece = lambda r: ref[part, r * SSM_WIDTH + s * SLAB:r * SSM_WIDTH + (s + 1) * SLAB]
            else:
                piece = lambda r: ref[t, r, part, lanes]
            if d == step:
                for r in range(d):
                    slab_ref[sub, buf, s, pl.ds(r, rows // d, stride=d), :] = piece(r).astype(F32)
            else:
                for r4 in range(step):
                    for j4 in range(step):
                        slab4_ref[sub, buf - 2, s, r4, pl.ds(j4, rows // d, stride=step), :] = (
                            piece(step * j4 + r4).astype(F32))
                for r4 in range(step):
                    slab_ref[sub, buf, s, pl.ds(r4, rows // step, stride=step), :] = slab4_ref[sub, buf - 2, s, r4]

    def in_order(sub, buf):
        return jnp.concatenate([slab_ref[sub, buf, s] for s in range(n_slabs)], axis=1)

    def merge_slab(sub, s):
        t, lanes = _slab_home(s)
        part = slice(sub * rows, (sub + 1) * rows)
        l1, l2, l3 = l1_ref[t, 0, part, lanes], slab_ref[sub, 1, s], slab_ref[sub, 3, s]
        m = jnp.maximum(jnp.maximum(l1, l2), l3)
        w1, w2, w3 = jnp.exp2(l1 - m), jnp.exp2(l2 - m), jnp.exp2(l3 - m)
        num = w1 * o1_ref[t, 0, part, lanes].astype(F32) + w2 * slab_ref[sub, 0, s] + w3 * slab_ref[sub, 2, s]
        slab_ref[sub, 0, s] = num / (w1 + w2 + w3)

    def merge_finish(sub):
        attn = in_order(sub, 0)
        ms = jnp.mean(attn * attn, axis=-1, keepdims=True)
        mix_ref[sub, :, 0:ATTN_WIDTH] = (attn * lax.rsqrt(ms + EPS) * ga_ref[...]).astype(BF16)
        mix_ref[sub, :, ATTN_WIDTH:] = in_order(sub, 4).astype(BF16)

    def prologue(sub):
        return ([lambda: regroup(o2_ref, step, sub, 0), lambda: regroup(l2_ref, step, sub, 1),
                 lambda: regroup(o3_ref, CHUNK, sub, 2), lambda: regroup(l3_ref, CHUNK, sub, 3),
                 lambda: regroup(ssm_ref, CHUNK, sub, 4)]
                + [lambda s=s: merge_slab(sub, s) for s in range(n_slabs)] + [lambda: merge_finish(sub)])

    def main(sub):
        part = slice(sub * rows, (sub + 1) * rows)
        st = {}

        def out_proj():
            x1 = x_ref[part, :] + jnp.dot(mix_ref[sub], wout_ref[...], preferred_element_type=F32)
            ms2 = jnp.mean(x1 * x1, axis=-1, keepdims=True)
            st["xn"] = (x1 * lax.rsqrt(ms2 + EPS) * g2_ref[...]).astype(BF16)
            st["acc"] = x1

        def up(c):
            st[c] = jnp.dot(st["xn"], wup_ref[:, c * FF_CHUNK:(c + 1) * FF_CHUNK], preferred_element_type=F32)

        def down(c):
            h = jnp.square(jnp.maximum(st.pop(c), 0.0)).astype(BF16)
            st["acc"] = st["acc"] + jnp.dot(h, wdn_ref[c * FF_CHUNK:(c + 1) * FF_CHUNK, :],
                                            preferred_element_type=F32)

        def finish():
            out_ref[part, :] = st["acc"]

        stages = [out_proj, lambda: up(0)]
        for c in range(n_ff):
            if c + 1 < n_ff:
                stages.append(lambda c=c: up(c + 1))
            stages.append(lambda c=c: down(c))
        return stages + [finish]

    tail = 3
    program = prologue(0)
    all_stages = [main(sub) for sub in range(n_sub)]
    for sub in range(n_sub):
        stages = all_stages[sub]
        head = stages if sub == 0 else stages[1:]
        if sub + 1 < n_sub:
            side = prologue(sub + 1)
            body = []
            for stage in head[:-tail]:
                body.append(stage)
                if side:
                    body.append(side.pop(0))
            program += body + side + [all_stages[sub + 1][0]] + head[-tail:]
        else:
            program += head
    for piece in program:
        piece()


def _mlp(x, os_, ls_, ssm, ga, w_out, g2, w_up, w_dn):
    s = x.shape[0]
    rows = MLP_ROWS
    xt = pl.BlockSpec((rows, D_MODEL), lambda i: (i, 0))
    ht = [pl.BlockSpec((N_HEAD_TILES, d, rows // d, LANE_TILE), lambda i: (0, 0, i, 0)) for d in DILATIONS]
    return pl.pallas_call(
        _mlp_kernel,
        grid=(s // rows,),
        in_specs=[xt, *ht, *ht, pl.BlockSpec((rows // CHUNK, CHUNK * SSM_WIDTH), lambda i: (i, 0)),
                  _const_spec(ga.shape), _const_spec(w_out.shape), _const_spec(g2.shape),
                  _const_spec(w_up.shape), _const_spec(w_dn.shape)],
        out_specs=xt,
        out_shape=jax.ShapeDtypeStruct((s, D_MODEL), F32),
        scratch_shapes=[pltpu.VMEM((rows // PROJ_SUB, 5, ATTN_WIDTH // SLAB, PROJ_SUB, SLAB), F32),
                        pltpu.VMEM((rows // PROJ_SUB, 3, ATTN_WIDTH // SLAB, DILATIONS[1],
                                    PROJ_SUB // DILATIONS[1], SLAB), F32),
                        pltpu.VMEM((rows // PROJ_SUB, PROJ_SUB, D_MODEL), BF16)],
        compiler_params=_params(dimension_semantics=("arbitrary",)),
        name="mlp",
    )(x, *os_, *ls_, ssm, ga, w_out, g2, w_up, w_dn)


def _layer(x, norm1_g, w_in, q_norm_g, k_norm_g, ssm_a_re, ssm_a_im, ssm_log_dt, ssm_b_re, ssm_b_im,
           ssm_c_re, ssm_c_im, ssm_d, glu_w, glu_b, attn_out_norm_g, ssm_out_norm_g, w_out, norm2_g,
           w_mlp_up, w_mlp_down):
    row = lambda g: g.astype(F32).reshape(1, -1)
    heads = ATTN_WIDTH // HEAD_DIM
    head_id = jnp.arange(LANE_TILE) // HEAD_DIM
    hmat = jnp.where(head_id[:, None] == head_id[None, :], 1.0 / HEAD_DIM, 0.0).astype(BF16)
    *qkv, u = _proj(x, row(norm1_g), w_in.astype(F32), hmat,
                    row(jnp.tile(q_norm_g, heads)), row(jnp.tile(k_norm_g, heads)))
    outs, lses = [], []
    mlp_weights = (w_out.astype(F32), w_mlp_up.astype(F32), w_mlp_down.astype(F32))
    for i in range(len(DILATIONS)):
        o, lse, cast = _attn_pattern(*qkv[3 * i:3 * i + 3], cast=mlp_weights if i == 0 else ())
        outs.append(o)
        lses.append(lse)
        if i == 0:
            w_out_b, w_up_b, w_dn_b = cast
    w = _s5_weights(ssm_a_re, ssm_a_im, ssm_log_dt, ssm_b_re, ssm_b_im, ssm_c_re, ssm_c_im, ssm_d)
    ssm = _s5(u, w, (glu_w.astype(F32) * -LOG2E).astype(BF16), row(glu_b) * -LOG2E, row(ssm_out_norm_g))
    return _mlp(x, outs, lses, ssm, row(attn_out_norm_g), w_out_b, row(norm2_g), w_up_b, w_dn_b)


def kernel(x, norm1_g, w_in, q_norm_g, k_norm_g, ssm_a_re, ssm_a_im, ssm_log_dt, ssm_b_re, ssm_b_im,
           ssm_c_re, ssm_c_im, ssm_d, glu_w, glu_b, attn_out_norm_g, ssm_out_norm_g, w_out, norm2_g,
           w_mlp_up, w_mlp_down):
    params = (norm1_g, w_in, q_norm_g, k_norm_g, ssm_a_re, ssm_a_im, ssm_log_dt, ssm_b_re, ssm_b_im,
              ssm_c_re, ssm_c_im, ssm_d, glu_w, glu_b, attn_out_norm_g, ssm_out_norm_g, w_out, norm2_g,
              w_mlp_up, w_mlp_down)
    batch = x.shape[0]
    outs = []
    for bi in range(batch):
        h = x[bi]
        for layer in range(norm1_g.shape[0]):
            h = _layer(h, *[p[layer] for p in params])
        outs.append(h)
    return jnp.stack(outs)
```

```python
import functools
import math

import jax
import jax.numpy as jnp
from jax import lax
from jax.experimental import pallas as pl
from jax.experimental.pallas import tpu as pltpu

F32 = jnp.float32
BF16 = jnp.bfloat16

D_MODEL = 1024
ATTN_WIDTH = 512
HEAD_DIM = 64
SSM_WIDTH = 512
SSM_GROUP = 16
SSM_GROUPS = 32
SSM_STATE = 64
D_FF = 4096
EPS = 1e-6
NEG_INF = -1e30
DILATIONS = (1, 4, 16)
BLOCK = 128

SLAB = 128
LANE_TILE = 256
HEADS_PER_TILE = LANE_TILE // HEAD_DIM
N_HEAD_TILES = ATTN_WIDTH // LANE_TILE
CHUNK = 16
SUB = 2
PAIRS = 2
SUB_STEPS = SUB * PAIRS
N_SUB = CHUNK // SUB_STEPS
OCT = 8
N_OCT = SSM_GROUPS // OCT
SCAN_GROUP = 8
STATE_COLS = SSM_GROUPS * SSM_STATE
VMEM_LIMIT = 56 * 1024 * 1024

LOG2E = math.log2(math.e)
GELU_C = 2.0 * math.sqrt(2.0 / math.pi)
Q_SCALE = HEAD_DIM ** -0.5 * LOG2E

PROJ_ROWS = 1024
PROJ_SUB = 256
SLAB_BUFFERS = 2
PROJ_AHEAD = 2
ATTN_ROWS = 4096
S5_ROWS = 256
MLP_ROWS = 512
FF_CHUNK = 512


def _const_spec(shape):
    nd = len(shape)
    return pl.BlockSpec(shape, lambda *_: (0,) * nd, pipeline_mode=pl.Buffered(1))


def _params(**kw):
    return pltpu.CompilerParams(vmem_limit_bytes=VMEM_LIMIT, **kw)


def _proj_kernel(x_ref, g1_ref, w_ref, hmat_ref, gq_ref, gk_ref,
                 q1_ref, k1_ref, v1_ref, q4_ref, k4_ref, v4_ref, q16_ref, k16_ref, v16_ref, u_ref,
                 slab_ref, slab4_ref, wb_ref):
    @pl.when(pl.program_id(0) == 0)
    def _():
        wb_ref[...] = w_ref[...].astype(BF16)

    outs = ((q1_ref, q4_ref, q16_ref), (k1_ref, k4_ref, k16_ref), (v1_ref, v4_ref, v16_ref))
    slabs_per_tensor = ATTN_WIDTH // SLAB
    step = DILATIONS[1]
    rows = PROJ_SUB
    n_sub = x_ref.shape[0] // rows
    n_chunks = w_ref.shape[1] // LANE_TILE
    chunks_per_tensor = ATTN_WIDTH // LANE_TILE
    gains = (gq_ref, gk_ref)
    xn = [None] * n_sub

    def prologue(sub):
        x = x_ref[sub * rows:(sub + 1) * rows, :]
        ms = jnp.mean(x * x, axis=-1, keepdims=True)
        xn[sub] = (x * lax.rsqrt(ms + EPS) * g1_ref[...]).astype(BF16)

    def matmul(sub, c):
        return jnp.dot(xn[sub], wb_ref[:, c * LANE_TILE:(c + 1) * LANE_TILE], preferred_element_type=F32)

    def epilogue(sub, c, val):
        ti, t = divmod(c, chunks_per_tensor)
        lane_tile = slice(t * LANE_TILE, (t + 1) * LANE_TILE)
        if ti < 2:
            ms_h = jnp.dot((val * val).astype(BF16), hmat_ref[...], preferred_element_type=F32)
            val = val * lax.rsqrt(ms_h + EPS) * gains[ti][:, lane_tile]
            if ti == 0:
                val = val * Q_SCALE
        if ti < 3:
            outs[ti][0][t, 0, sub * rows:(sub + 1) * rows, :] = val.astype(BF16)
        r4_rows = slice(sub * (rows // step), (sub + 1) * (rows // step))
        r16_rows = slice(sub * (rows // CHUNK), (sub + 1) * (rows // CHUNK))
        for half in range(LANE_TILE // SLAB):
            s = t * (LANE_TILE // SLAB) + half
            lanes = slice(half * SLAB, (half + 1) * SLAB)
            slab = ti * slabs_per_tensor + s
            buf = sub % SLAB_BUFFERS
            slab_ref[buf, slab] = val[:, lanes]
            for r4 in range(step):
                part = slab_ref[buf, slab, pl.ds(r4, rows // step, stride=step), :]
                if ti < 3:
                    outs[ti][1][t, r4, r4_rows, lanes] = part.astype(BF16)
                slab4_ref[buf, slab, r4] = part
            for r4 in range(step):
                for j4 in range(step):
                    r16 = step * j4 + r4
                    part = slab4_ref[buf, slab, r4, pl.ds(j4, rows // CHUNK, stride=step), :].astype(BF16)
                    if ti < 3:
                        outs[ti][2][t, r16, r16_rows, lanes] = part
                    else:
                        u_ref[r16_rows, r16 * SSM_WIDTH + s * SLAB:r16 * SSM_WIDTH + (s + 1) * SLAB] = part

    work = [(sub, c) for sub in range(n_sub) for c in range(n_chunks)]
    prologue(0)
    pending = []
    for sub, c in work:
        pending.append((sub, c, matmul(sub, c)))
        if len(pending) > PROJ_AHEAD:
            epilogue(*pending.pop(0))
        if c == n_chunks // 2 and sub + 1 < n_sub:
            prologue(sub + 1)
    for item in pending:
        epilogue(*item)


def _slab_home(s):
    t, half = divmod(s, LANE_TILE // SLAB)
    return t, slice(half * SLAB, (half + 1) * SLAB)


def _proj(x, g1, w_in, hmat, gq, gk):
    s = x.shape[0]
    rows = PROJ_ROWS
    out_specs, out_shape = [], []
    for d in DILATIONS:
        for _ in range(3):
            out_specs.append(pl.BlockSpec((N_HEAD_TILES, d, rows // d, LANE_TILE), lambda i: (0, 0, i, 0)))
            out_shape.append(jax.ShapeDtypeStruct((N_HEAD_TILES, d, s // d, LANE_TILE), BF16))
    out_specs.append(pl.BlockSpec((rows // CHUNK, CHUNK * SSM_WIDTH), lambda i: (i, 0)))
    out_shape.append(jax.ShapeDtypeStruct((s // CHUNK, CHUNK * SSM_WIDTH), BF16))
    return pl.pallas_call(
        _proj_kernel,
        grid=(s // rows,),
        in_specs=[
            pl.BlockSpec((rows, D_MODEL), lambda i: (i, 0)),
            _const_spec((1, D_MODEL)),
            _const_spec(w_in.shape),
            _const_spec(hmat.shape),
            _const_spec((1, ATTN_WIDTH)),
            _const_spec((1, ATTN_WIDTH)),
        ],
        out_specs=out_specs,
        out_shape=out_shape,
        scratch_shapes=[pltpu.VMEM((SLAB_BUFFERS, 4 * ATTN_WIDTH // SLAB, PROJ_SUB, SLAB), F32),
                        pltpu.VMEM((SLAB_BUFFERS, 4 * ATTN_WIDTH // SLAB, DILATIONS[1],
                                    PROJ_SUB // DILATIONS[1], SLAB), F32),
                        pltpu.VMEM(w_in.shape, BF16)],
        compiler_params=_params(dimension_semantics=("arbitrary",)),
        name="proj",
    )(x, g1, w_in, hmat, gq, gk)


def _attn_kernel(q_ref, kc_ref, kp_ref, vc_ref, vp_ref, *rest):
    n_cast = (len(rest) - 4) // 2
    cast_in, (o_ref, lse_ref), cast_out = rest[:n_cast], rest[n_cast:n_cast + 2], rest[n_cast + 2:2 * n_cast + 2]
    kbuf, vbuf = rest[2 * n_cast + 2:]
    for src, dst in zip(cast_in, cast_out):
        dst[...] = src[...].astype(BF16)
    n_res, rows = q_ref.shape[0], q_ref.shape[1]
    kbuf[:, 0:BLOCK, :] = kp_ref[...]
    kbuf[:, BLOCK:, :] = kc_ref[...]
    vbuf[:, 0:BLOCK, :] = vp_ref[...]
    vbuf[:, BLOCK:, :] = vc_ref[...]

    heads_per_slab = SLAB // HEAD_DIM
    first_head = lax.broadcasted_iota(jnp.int32, (BLOCK, SLAB), 1) < HEAD_DIM
    qi = lax.broadcasted_iota(jnp.int32, (BLOCK, 2 * BLOCK), 0)
    ki = lax.broadcasted_iota(jnp.int32, (BLOCK, 2 * BLOCK), 1)
    band = (ki >= qi) & (ki <= qi + BLOCK)
    bias_band = jnp.where(band, 0.0, NEG_INF).astype(F32)
    no_prev = pl.program_id(2) == 0
    bias_first = jnp.where(band & ((ki >= BLOCK) | jnp.logical_not(no_prev)), 0.0, NEG_INF).astype(F32)

    for r, b in [(r, b) for r in range(n_res) for b in range(rows // BLOCK)]:
        qb = q_ref[r, b * BLOCK:(b + 1) * BLOCK, :]
        zero = jnp.zeros((BLOCK, SLAB), BF16)
        q_rows = []
        for h in range(HEADS_PER_TILE):
            slab, second = divmod(h, heads_per_slab)
            piece = qb[:, slab * SLAB:(slab + 1) * SLAB]
            piece = jnp.where(first_head, zero, piece) if second else jnp.where(first_head, piece, zero)
            q_rows.append(jnp.concatenate([piece if s == slab else zero for s in range(LANE_TILE // SLAB)], axis=1))
        q_stack = jnp.concatenate(q_rows, axis=0)
        kw = kbuf[r, b * BLOCK:(b + 2) * BLOCK, :]
        vw = vbuf[r, b * BLOCK:(b + 2) * BLOCK, :]
        s = lax.dot_general(q_stack, kw, (((1,), (1,)), ((), ())), preferred_element_type=F32)
        bias = bias_first if b == 0 else bias_band
        ps, ms, ls = [], [], []
        for h in range(HEADS_PER_TILE):
            sh = s[h * BLOCK:(h + 1) * BLOCK, :] + bias
            m = jnp.max(sh, axis=-1, keepdims=True)
            p = jnp.exp2(sh - m)
            ls.append(jnp.sum(p, axis=-1, keepdims=True))
            ms.append(m)
            ps.append(p.astype(BF16))
        pv = jnp.dot(jnp.concatenate(ps, axis=0), vw, preferred_element_type=F32)
        o_slabs, lse_slabs = [], []
        for slab in range(LANE_TILE // SLAB):
            cols = slice(slab * SLAB, (slab + 1) * SLAB)
            h0, h1 = slab * heads_per_slab, slab * heads_per_slab + 1
            o_slabs.append(jnp.where(first_head, pv[h0 * BLOCK:(h0 + 1) * BLOCK, cols] / ls[h0],
                                     pv[h1 * BLOCK:(h1 + 1) * BLOCK, cols] / ls[h1]))
            lse_slabs.append(jnp.where(first_head, ms[h0] + jnp.log2(ls[h0]),
                                       ms[h1] + jnp.log2(ls[h1])))
        o_ref[r, b * BLOCK:(b + 1) * BLOCK, :] = jnp.concatenate(o_slabs, axis=1).astype(BF16)
        lse_ref[r, b * BLOCK:(b + 1) * BLOCK, :] = jnp.concatenate(lse_slabs, axis=1)


def _attn_pattern(q, k, v, cast=()):
    nt, dilation, n, _ = q.shape
    rows = min(ATTN_ROWS, n)
    n_res = min(ATTN_ROWS // rows, dilation)
    blocks_per_tile = rows // BLOCK
    grid = (nt, dilation // n_res, n // rows)
    n_steps = grid[0] * grid[1] * grid[2]
    cur = pl.BlockSpec((None, n_res, rows, LANE_TILE), lambda t, r, j: (t, r, j, 0))
    prev = pl.BlockSpec((None, n_res, BLOCK, LANE_TILE),
                        lambda t, r, j: (t, r, jnp.maximum(j * blocks_per_tile - 1, 0), 0))
    step_id = lambda t, r, j: ((t * grid[1] + r) * grid[2] + j, 0)
    cast_specs = [pl.BlockSpec((w.shape[0] // n_steps, w.shape[1]), step_id) for w in cast]
    outs = pl.pallas_call(
        _attn_kernel,
        grid=grid,
        in_specs=[cur, cur, prev, cur, prev] + cast_specs,
        out_specs=[cur, cur] + cast_specs,
        out_shape=[jax.ShapeDtypeStruct(q.shape, BF16), jax.ShapeDtypeStruct(q.shape, F32)]
        + [jax.ShapeDtypeStruct(w.shape, BF16) for w in cast],
        scratch_shapes=[pltpu.VMEM((n_res, rows + BLOCK, LANE_TILE), BF16),
                        pltpu.VMEM((n_res, rows + BLOCK, LANE_TILE), BF16)],
        compiler_params=_params(dimension_semantics=("arbitrary", "arbitrary", "arbitrary")),
        name=f"attn_d{dilation}",
    )(q, k, k, v, v, *cast)
    return outs[0], outs[1], outs[2:]


def _s5_weights(a_re, a_im, log_dt, b_re, b_im, c_re, c_im, d_skip):
    p, c = SSM_STATE, SSM_GROUP
    lr, li = a_re.astype(F32), a_im.astype(F32)
    dt = jnp.exp(log_dt.astype(F32))[:, None]

    def apow(j):
        mag = jnp.exp(lr * dt * j)
        return mag * jnp.cos(li * dt * j), mag * jnp.sin(li * dt * j)

    ab_r, ab_i = apow(1.0)
    den = lr * lr + li * li
    nr, ni = ab_r - 1.0, ab_i
    cr = (nr * lr + ni * li) / den
    ci = (ni * lr - nr * li) / den
    br, bi = b_re.astype(F32), b_im.astype(F32)
    bb_r = cr[..., None] * br - ci[..., None] * bi
    bb_i = cr[..., None] * bi + ci[..., None] * br
    cre, cim = c_re.astype(F32), c_im.astype(F32)

    def apow_many(js):
        j = jnp.asarray(js, F32)[:, None, None]
        mag = jnp.exp(lr * dt * j)
        return mag * jnp.cos(li * dt * j), mag * jnp.sin(li * dt * j)

    pr, pi = apow_many(range(SUB_STEPS + 1))
    pad_lanes = lambda w: jnp.pad(w, [(0, 0)] * (w.ndim - 1) + [(0, SLAB - w.shape[-1])])
    wp_r = pad_lanes(jnp.stack([pr[SUB_STEPS - 1 - s] for s in range(SUB_STEPS)]))
    wp_i = pad_lanes(jnp.stack([pi[SUB_STEPS - 1 - s] for s in range(SUB_STEPS)]))
    bbt_r, bbt_i = pad_lanes(bb_r.transpose(0, 2, 1)), pad_lanes(bb_i.transpose(0, 2, 1))
    win_r = wp_r[:, :, None, :] * bbt_r[None] - wp_i[:, :, None, :] * bbt_i[None]
    win_i = wp_r[:, :, None, :] * bbt_i[None] + wp_i[:, :, None, :] * bbt_r[None]
    to_in = lambda w: (w.reshape(PAIRS, SUB, N_OCT, OCT, c, SLAB).transpose(0, 2, 1, 3, 4, 5)
                       .reshape(PAIRS, N_OCT, SUB * OCT * c, SLAB))
    ct_r, ct_i = pad_lanes(cre.transpose(0, 2, 1)), pad_lanes(cim.transpose(0, 2, 1))
    out_r = ct_r[None] * pr[..., None] - ct_i[None] * pi[..., None]
    out_i = -(ct_r[None] * pi[..., None] + ct_i[None] * pr[..., None])
    to_out = lambda w: w.reshape(SUB_STEPS + 1, N_OCT, OCT * p, SLAB)

    flat = lambda re, im: jnp.stack([re.reshape(-1), im.reshape(-1)])
    top_r, top_i = apow_many([CHUNK])
    lam = flat(top_r[0], top_i[0])
    levels = [lam]
    for _ in range(SCAN_GROUP - 1):
        lr_, li_ = levels[-1][0], levels[-1][1]
        levels.append(jnp.stack([lr_ * lam[0] - li_ * lam[1], lr_ * lam[1] + li_ * lam[0]]))
    m_t, w_r, w_i, e_r, e_i = _s5_expand(to_in(win_r), to_in(win_i), to_out(out_r), to_out(out_i),
                                         d_skip.astype(F32).reshape(N_OCT, 1, OCT * c))
    return dict(m_t=m_t, w_r=w_r, w_i=w_i, e_r=e_r, e_i=e_i,
                a_sub=flat(pr[SUB_STEPS], pi[SUB_STEPS]),
                levels=jnp.stack(levels, axis=1))


def _s5_expand_kernel(wr_ref, wi_ref, or_ref, oi_ref, d_ref, mt_ref, wfr_ref, wfi_ref, efr_ref, efi_ref):
    p, c = SSM_STATE, SSM_GROUP
    n_state, n_lane = OCT * p, OCT * c

    def rep(inner, total):
        r = lax.broadcasted_iota(jnp.int32, (SLAB, total), 0)
        col = lax.broadcasted_iota(jnp.int32, (SLAB, total), 1)
        return jnp.where((col & (inner - 1)) == r, 1.0, 0.0).astype(BF16)

    shift = lambda n: int(math.log2(n))
    rep_in, rep_out = rep(p, n_state), rep(c, n_lane)
    in_row = lax.broadcasted_iota(jnp.int32, (SUB * n_lane, n_state), 0)
    in_col = lax.broadcasted_iota(jnp.int32, (SUB * n_lane, n_state), 1)
    mask_in = ((in_row >> shift(c)) & (OCT - 1)) == (in_col >> shift(p))
    out_row = lax.broadcasted_iota(jnp.int32, (n_state, n_lane), 0)
    out_col = lax.broadcasted_iota(jnp.int32, (n_state, n_lane), 1)
    mask_out = (out_row >> shift(p)) == (out_col >> shift(c))
    diag = (lax.broadcasted_iota(jnp.int32, (n_lane, n_lane), 0)
            == lax.broadcasted_iota(jnp.int32, (n_lane, n_lane), 1))

    def split(x):
        hi = x.astype(BF16)
        return hi, (x - hi.astype(F32)).astype(BF16)

    def expand(x_b, rep_m, mask):
        full = jnp.dot(x_b, rep_m, preferred_element_type=F32)
        return jnp.where(mask, full, 0.0).astype(BF16)

    last = slice((SUB - 1) * n_lane, SUB * n_lane)
    for o in range(wfr_ref.shape[0]):
        in0 = []
        for src, dst in ((wr_ref, wfr_ref), (wi_ref, wfi_ref)):
            for q in range(PAIRS):
                hi, lo = split(src[q, o])
                full = expand(hi, rep_in, mask_in)
                dst[o, q * SUB * n_lane:(q + 1) * SUB * n_lane, :] = full
                if q == PAIRS - 1:
                    in0.append((full[last, :], expand(lo, rep_in, mask_in)[last, :]))
        lags = [None] * SUB_STEPS
        for (src, dst), (in_hi, in_lo) in zip(((or_ref, efr_ref), (oi_ref, efi_ref)), in0):
            outs_hi = []
            for j in range(SUB_STEPS + 1):
                hi, lo = split(src[j, o])
                out_hi = expand(hi, rep_out, mask_out)
                outs_hi.append(out_hi)
                if j < SUB_STEPS:
                    out_lo = expand(lo, rep_out, mask_out)
                    part = (jnp.dot(in_hi, out_hi, preferred_element_type=F32)
                            + jnp.dot(in_hi, out_lo, preferred_element_type=F32)
                            + jnp.dot(in_lo, out_hi, preferred_element_type=F32))
                    lags[j] = part if lags[j] is None else lags[j] + part
            dst[o] = jnp.concatenate(outs_hi[1:], axis=1)
        lags[0] = lags[0] + jnp.where(diag, d_ref[o], 0.0)
        zero = jnp.zeros_like(lags[0])
        for dist in range(PAIRS):
            lag = lambda t_in, t_out: SUB * dist + t_out - t_in
            mt_ref[dist, o] = jnp.concatenate(
                [jnp.concatenate([lags[lag(t_in, t_out)] if lag(t_in, t_out) >= 0 else zero
                                  for t_out in range(SUB)], axis=1)
                 for t_in in range(SUB)], axis=0).astype(BF16)


def _s5_expand(wc_r, wc_i, oc_r, oc_i, d_row):
    n_in, n_state = wc_r.shape[2], oc_r.shape[2]
    shapes = [jax.ShapeDtypeStruct((PAIRS, N_OCT, n_in, n_in), BF16),
              jax.ShapeDtypeStruct((N_OCT, PAIRS * n_in, n_state), BF16),
              jax.ShapeDtypeStruct((N_OCT, PAIRS * n_in, n_state), BF16),
              jax.ShapeDtypeStruct((N_OCT, n_state, PAIRS * n_in), BF16),
              jax.ShapeDtypeStruct((N_OCT, n_state, PAIRS * n_in), BF16)]
    def per_tile(shape, axis):
        block = tuple(1 if i == axis else n for i, n in enumerate(shape))
        return pl.BlockSpec(block, lambda o: tuple(o if i == axis else 0 for i in range(len(shape))))

    operands = (wc_r, wc_i, oc_r, oc_i, d_row)
    tile_axis = (1, 1, 1, 1, 0)
    return pl.pallas_call(
        _s5_expand_kernel,
        grid=(N_OCT,),
        in_specs=[per_tile(a.shape, ax) for a, ax in zip(operands, tile_axis)],
        out_specs=[per_tile(s.shape, ax) for s, ax in zip(shapes, (1, 0, 0, 0, 0))],
        out_shape=shapes,
        compiler_params=_params(dimension_semantics=("arbitrary",)),
        name="s5_expand",
    )(*operands)


def _cmul(ar, ai, br, bi):
    return ar * br - ai * bi, ar * bi + ai * br


def _s5_kernel(u_ref, mt_ref, wr_ref, wi_ref, er_ref, ei_ref, asub_ref, lvl_ref,
               gluw_ref, glub_ref, gout_ref, out_ref, carry_ref, zr_ref, zi_ref, locr_ref, loci_ref):
    rows = u_ref.shape[0]
    width = SSM_WIDTH
    oct_lanes = OCT * SSM_GROUP
    oct_state = OCT * SSM_STATE

    @pl.when(pl.program_id(0) == 0)
    def _():
        carry_ref[...] = jnp.zeros_like(carry_ref)

    def u_tile(a, q, o):
        first = a * SUB_STEPS + q * SUB
        return jnp.concatenate(
            [u_ref[:, (first + t) * width + o * oct_lanes:(first + t) * width + (o + 1) * oct_lanes]
             for t in range(SUB)], axis=1)

    a_r, a_i = asub_ref[0:1, :], asub_ref[1:2, :]

    for a in range(N_SUB):
        for o in range(N_OCT):
            cols = slice(o * oct_state, (o + 1) * oct_state)
            ut = jnp.concatenate([u_tile(a, q, o) for q in range(PAIRS)], axis=1)
            pr = jnp.dot(ut, wr_ref[o], preferred_element_type=F32)
            pi = jnp.dot(ut, wi_ref[o], preferred_element_type=F32)
            if a > 0:
                hr, hi = _cmul(zr_ref[:, cols], zi_ref[:, cols], a_r[:, cols], a_i[:, cols])
                pr, pi = hr + pr, hi + pi
            zr_ref[:, cols] = pr
            zi_ref[:, cols] = pi
            if a < N_SUB - 1:
                locr_ref[a, :, cols] = pr.astype(BF16)
                loci_ref[a, :, cols] = pi.astype(BF16)

    row = lax.broadcasted_iota(jnp.int32, (rows, 1), 0)
    in_group = row & (SCAN_GROUP - 1)
    sr, si = zr_ref[...], zi_ref[...]
    sh = 1
    while sh < SCAN_GROUP:
        keep = in_group >= sh
        pr_, pi_ = lvl_ref[0, sh - 1:sh, :], lvl_ref[1, sh - 1:sh, :]
        tr = jnp.where(keep, pltpu.roll(sr, sh, axis=0), 0.0)
        ti = jnp.where(keep, pltpu.roll(si, sh, axis=0), 0.0)
        mr, mi = _cmul(tr, ti, pr_, pi_)
        sr, si = sr + mr, si + mi
        sh *= 2
    c_r, c_i = carry_ref[0:1, :], carry_ref[1:2, :]
    pw_r, pw_i = lvl_ref[0], lvl_ref[1]
    last_r, last_i = c_r, c_i
    for g in range(rows // SCAN_GROUP):
        grp = slice(g * SCAN_GROUP, (g + 1) * SCAN_GROUP)
        ar, ai = _cmul(jnp.broadcast_to(last_r, pw_r.shape), jnp.broadcast_to(last_i, pw_i.shape), pw_r, pw_i)
        gr, gi = sr[grp, :] + ar, si[grp, :] + ai
        zr_ref[grp, :] = gr
        zi_ref[grp, :] = gi
        last_r, last_i = gr[SCAN_GROUP - 1:SCAN_GROUP, :], gi[SCAN_GROUP - 1:SCAN_GROUP, :]
    carry_ref[0:1, :] = last_r
    carry_ref[1:2, :] = last_i
    xr = jnp.where(row == 0, c_r, pltpu.roll(zr_ref[...], 1, axis=0))
    xi = jnp.where(row == 0, c_i, pltpu.roll(zi_ref[...], 1, axis=0))
    zr_ref[...] = xr
    zi_ref[...] = xi

    for a in range(N_SUB):
        if a > 0:
            nr, ni = _cmul(zr_ref[...], zi_ref[...], a_r, a_i)
            zr_ref[...] = nr
            zi_ref[...] = ni
        if a == 0:
            xr_b = zr_ref[...].astype(BF16)
            xi_b = zi_ref[...].astype(BF16)
        else:
            xr_b = (zr_ref[...] + locr_ref[a - 1].astype(F32)).astype(BF16)
            xi_b = (zi_ref[...] + loci_ref[a - 1].astype(F32)).astype(BF16)
        tiles = []
        for o in range(N_OCT):
            cols = slice(o * oct_state, (o + 1) * oct_state)
            from_state = (jnp.dot(xr_b[:, cols], er_ref[o], preferred_element_type=F32)
                          + jnp.dot(xi_b[:, cols], ei_ref[o], preferred_element_type=F32))
            parts = []
            for q in range(PAIRS):
                acc = from_state[:, q * SUB * oct_lanes:(q + 1) * SUB * oct_lanes]
                for q_in in range(q + 1):
                    acc = acc + jnp.dot(u_tile(a, q_in, o), mt_ref[q - q_in, o], preferred_element_type=F32)
                parts.append(acc)
            tiles.append(jnp.concatenate(parts, axis=1))
        for b in range(SUB_STEPS):
            t = a * SUB_STEPS + b
            y = jnp.concatenate([tl[:, b * oct_lanes:(b + 1) * oct_lanes] for tl in tiles], axis=1)
            y2 = y * y
            z = y * (1.0 / (1.0 + jnp.exp2(y * (y2 * (-GELU_C * 0.044715 * LOG2E) - GELU_C * LOG2E))))
            gate = jnp.dot(z.astype(BF16), gluw_ref[...], preferred_element_type=F32) + glub_ref[...]
            o = z * (1.0 / (1.0 + jnp.exp2(gate)))
            ms = jnp.mean(o * o, axis=-1, keepdims=True)
            out_ref[:, t * width:(t + 1) * width] = (o * lax.rsqrt(ms + EPS) * gout_ref[...]).astype(BF16)


def _s5(uv, w, glu_w, glu_b, g_out):
    n = uv.shape[0]
    rows = S5_ROWS
    tile = pl.BlockSpec((rows, CHUNK * SSM_WIDTH), lambda i: (i, 0))
    return pl.pallas_call(
        _s5_kernel,
        grid=(n // rows,),
        in_specs=[tile] + [_const_spec(a.shape) for a in
                           (w["m_t"], w["w_r"], w["w_i"], w["e_r"], w["e_i"], w["a_sub"], w["levels"],
                            glu_w, glu_b, g_out)],
        out_specs=tile,
        out_shape=jax.ShapeDtypeStruct((n, CHUNK * SSM_WIDTH), BF16),
        scratch_shapes=[pltpu.VMEM((2, STATE_COLS), F32),
                        pltpu.VMEM((rows, STATE_COLS), F32),
                        pltpu.VMEM((rows, STATE_COLS), F32),
                        pltpu.VMEM((N_SUB - 1, rows, STATE_COLS), BF16),
                        pltpu.VMEM((N_SUB - 1, rows, STATE_COLS), BF16)],
        compiler_params=_params(dimension_semantics=("arbitrary",)),
        name="s5",
    )(uv, w["m_t"], w["w_r"], w["w_i"], w["e_r"], w["e_i"], w["a_sub"], w["levels"], glu_w, glu_b, g_out)


def _mlp_kernel(x_ref, o1_ref, o2_ref, o3_ref, l1_ref, l2_ref, l3_ref, ssm_ref, ga_ref, wout_ref,
                g2_ref, wup_ref, wdn_ref, out_ref, slab_ref, slab4_ref, mix_ref):
    rows = PROJ_SUB
    n_sub = x_ref.shape[0] // rows
    n_slabs = ATTN_WIDTH // SLAB
    step = DILATIONS[1]
    n_ff = D_FF // FF_CHUNK

    def regroup(ref, d, sub, buf):
        part = slice(sub * (rows // d), (sub + 1) * (rows // d))
        for s in range(n_slabs):
            t, lanes = _slab_home(s)
            if ref is ssm_ref:
                piece = lambda r: ref[part, r * SSM_WIDTH + s * SLAB:r * SSM_WIDTH + (s + 1) * SLAB]
            else:
                piece = lambda r: ref[t, r, part, lanes]
            if d == step:
                for r in range(d):
                    slab_ref[sub, buf, s, pl.ds(r, rows // d, stride=d), :] = piece(r).astype(F32)
            else:
                for r4 in range(step):
                    for j4 in range(step):
                        slab4_ref[sub, buf - 2, s, r4, pl.ds(j4, rows // d, stride=step), :] = (
                            piece(step * j4 + r4).astype(F32))
                for r4 in range(step):
                    slab_ref[sub, buf, s, pl.ds(r4, rows // step, stride=step), :] = slab4_ref[sub, buf - 2, s, r4]

    def in_order(sub, buf):
        return jnp.concatenate([slab_ref[sub, buf, s] for s in range(n_slabs)], axis=1)

    def merge_slab(sub, s):
        t, lanes = _slab_home(s)
        part = slice(sub * rows, (sub + 1) * rows)
        l1, l2, l3 = l1_ref[t, 0, part, lanes], slab_ref[sub, 1, s], slab_ref[sub, 3, s]
        m = jnp.maximum(jnp.maximum(l1, l2), l3)
        w1, w2, w3 = jnp.exp2(l1 - m), jnp.exp2(l2 - m), jnp.exp2(l3 - m)
        num = w1 * o1_ref[t, 0, part, lanes].astype(F32) + w2 * slab_ref[sub, 0, s] + w3 * slab_ref[sub, 2, s]
        slab_ref[sub, 0, s] = num / (w1 + w2 + w3)

    def merge_finish(sub):
        attn = in_order(sub, 0)
        ms = jnp.mean(attn * attn, axis=-1, keepdims=True)
        mix_ref[sub, :, 0:ATTN_WIDTH] = (attn * lax.rsqrt(ms + EPS) * ga_ref[...]).astype(BF16)
        mix_ref[sub, :, ATTN_WIDTH:] = in_order(sub, 4).astype(BF16)

    def prologue(sub):
        return ([lambda: regroup(o2_ref, step, sub, 0), lambda: regroup(l2_ref, step, sub, 1),
                 lambda: regroup(o3_ref, CHUNK, sub, 2), lambda: regroup(l3_ref, CHUNK, sub, 3),
                 lambda: regroup(ssm_ref, CHUNK, sub, 4)]
                + [lambda s=s: merge_slab(sub, s) for s in range(n_slabs)] + [lambda: merge_finish(sub)])

    def main(sub):
        part = slice(sub * rows, (sub + 1) * rows)
        st = {}

        def out_proj():
            x1 = x_ref[part, :] + jnp.dot(mix_ref[sub], wout_ref[...], preferred_element_type=F32)
            ms2 = jnp.mean(x1 * x1, axis=-1, keepdims=True)
            st["xn"] = (x1 * lax.rsqrt(ms2 + EPS) * g2_ref[...]).astype(BF16)
            st["acc"] = x1

        def up(c):
            st[c] = jnp.dot(st["xn"], wup_ref[:, c * FF_CHUNK:(c + 1) * FF_CHUNK], preferred_element_type=F32)

        def down(c):
            h = jnp.square(jnp.maximum(st.pop(c), 0.0)).astype(BF16)
            st["acc"] = st["acc"] + jnp.dot(h, wdn_ref[c * FF_CHUNK:(c + 1) * FF_CHUNK, :],
                                            preferred_element_type=F32)

        def finish():
            out_ref[part, :] = st["acc"]

        stages = [out_proj, lambda: up(0)]
        for c in range(n_ff):
            if c + 1 < n_ff:
                stages.append(lambda c=c: up(c + 1))
            stages.append(lambda c=c: down(c))
        return stages + [finish]

    tail = 3
    program = prologue(0)
    all_stages = [main(sub) for sub in range(n_sub)]
    for sub in range(n_sub):
        stages = all_stages[sub]
        head = stages if sub == 0 else stages[1:]
        if sub + 1 < n_sub:
            side = prologue(sub + 1)
            body = []
            for stage in head[:-tail]:
                body.append(stage)
                if side:
                    body.append(side.pop(0))
            program += body + side + [all_stages[sub + 1][0]] + head[-tail:]
        else:
            program += head
    for piece in program:
        piece()


def _mlp(x, os_, ls_, ssm, ga, w_out, g2, w_up, w_dn):
    s = x.shape[0]
    rows = MLP_ROWS
    xt = pl.BlockSpec((rows, D_MODEL), lambda i: (i, 0))
    ht = [pl.BlockSpec((N_HEAD_TILES, d, rows // d, LANE_TILE), lambda i: (0, 0, i, 0)) for d in DILATIONS]
    return pl.pallas_call(
        _mlp_kernel,
        grid=(s // rows,),
        in_specs=[xt, *ht, *ht, pl.BlockSpec((rows // CHUNK, CHUNK * SSM_WIDTH), lambda i: (i, 0)),
                  _const_spec(ga.shape), _const_spec(w_out.shape), _const_spec(g2.shape),
                  _const_spec(w_up.shape), _const_spec(w_dn.shape)],
        out_specs=xt,
        out_shape=jax.ShapeDtypeStruct((s, D_MODEL), F32),
        scratch_shapes=[pltpu.VMEM((rows // PROJ_SUB, 5, ATTN_WIDTH // SLAB, PROJ_SUB, SLAB), F32),
                        pltpu.VMEM((rows // PROJ_SUB, 3, ATTN_WIDTH // SLAB, DILATIONS[1],
                                    PROJ_SUB // DILATIONS[1], SLAB), F32),
                        pltpu.VMEM((rows // PROJ_SUB, PROJ_SUB, D_MODEL), BF16)],
        compiler_params=_params(dimension_semantics=("arbitrary",)),
        name="mlp",
    )(x, *os_, *ls_, ssm, ga, w_out, g2, w_up, w_dn)


def _layer(x, norm1_g, w_in, q_norm_g, k_norm_g, ssm_a_re, ssm_a_im, ssm_log_dt, ssm_b_re, ssm_b_im,
           ssm_c_re, ssm_c_im, ssm_d, glu_w, glu_b, attn_out_norm_g, ssm_out_norm_g, w_out, norm2_g,
           w_mlp_up, w_mlp_down):
    row = lambda g: g.astype(F32).reshape(1, -1)
    heads = ATTN_WIDTH // HEAD_DIM
    head_id = jnp.arange(LANE_TILE) // HEAD_DIM
    hmat = jnp.where(head_id[:, None] == head_id[None, :], 1.0 / HEAD_DIM, 0.0).astype(BF16)
    *qkv, u = _proj(x, row(norm1_g), w_in.astype(F32), hmat,
                    row(jnp.tile(q_norm_g, heads)), row(jnp.tile(k_norm_g, heads)))
    outs, lses = [], []
    mlp_weights = (w_out.astype(F32), w_mlp_up.astype(F32), w_mlp_down.astype(F32))
    for i in range(len(DILATIONS)):
        o, lse, cast = _attn_pattern(*qkv[3 * i:3 * i + 3], cast=mlp_weights if i == 0 else ())
        outs.append(o)
        lses.append(lse)
        if i == 0:
            w_out_b, w_up_b, w_dn_b = cast
    w = _s5_weights(ssm_a_re, ssm_a_im, ssm_log_dt, ssm_b_re, ssm_b_im, ssm_c_re, ssm_c_im, ssm_d)
    ssm = _s5(u, w, (glu_w.astype(F32) * -LOG2E).astype(BF16), row(glu_b) * -LOG2E, row(ssm_out_norm_g))
    return _mlp(x, outs, lses, ssm, row(attn_out_norm_g), w_out_b, row(norm2_g), w_up_b, w_dn_b)


def kernel(x, norm1_g, w_in, q_norm_g, k_norm_g, ssm_a_re, ssm_a_im, ssm_log_dt, ssm_b_re, ssm_b_im,
           ssm_c_re, ssm_c_im, ssm_d, glu_w, glu_b, attn_out_norm_g, ssm_out_norm_g, w_out, norm2_g,
           w_mlp_up, w_mlp_down):
    params = (norm1_g, w_in, q_norm_g, k_norm_g, ssm_a_re, ssm_a_im, ssm_log_dt, ssm_b_re, ssm_b_im,
              ssm_c_re, ssm_c_im, ssm_d, glu_w, glu_b, attn_out_norm_g, ssm_out_norm_g, w_out, norm2_g,
              w_mlp_up, w_mlp_down)
    batch = x.shape[0]
    outs = []
    for bi in range(batch):
        h = x[bi]
        for layer in range(norm1_g.shape[0]):
            h = _layer(h, *[p[layer] for p in params])
        outs.append(h)
    return jnp.stack(outs)
```

```python
import functools
import math

import jax
import jax.numpy as jnp
from jax import lax
from jax.experimental import pallas as pl
from jax.experimental.pallas import tpu as pltpu

F32 = jnp.float32
BF16 = jnp.bfloat16

D_MODEL = 1024
ATTN_WIDTH = 512
HEAD_DIM = 64
SSM_WIDTH = 512
SSM_GROUP = 16
SSM_GROUPS = 32
SSM_STATE = 64
D_FF = 4096
EPS = 1e-6
NEG_INF = -1e30
DILATIONS = (1, 4, 16)
BLOCK = 128

SLAB = 128
LANE_TILE = 256
HEADS_PER_TILE = LANE_TILE // HEAD_DIM
N_HEAD_TILES = ATTN_WIDTH // LANE_TILE
CHUNK = 16
SUB = 2
PAIRS = 2
SUB_STEPS = SUB * PAIRS
N_SUB = CHUNK // SUB_STEPS
OCT = 8
N_OCT = SSM_GROUPS // OCT
SCAN_GROUP = 8
STATE_COLS = SSM_GROUPS * SSM_STATE
VMEM_LIMIT = 56 * 1024 * 1024

LOG2E = math.log2(math.e)
GELU_C = 2.0 * math.sqrt(2.0 / math.pi)
Q_SCALE = HEAD_DIM ** -0.5 * LOG2E

PROJ_ROWS = 1024
PROJ_SUB = 256
SLAB_BUFFERS = 2
PROJ_AHEAD = 2
ATTN_ROWS = 4096
S5_ROWS = 256
MLP_ROWS = 512
FF_CHUNK = 512


def _const_spec(shape):
    nd = len(shape)
    return pl.BlockSpec(shape, lambda *_: (0,) * nd, pipeline_mode=pl.Buffered(1))


def _params(**kw):
    return pltpu.CompilerParams(vmem_limit_bytes=VMEM_LIMIT, **kw)


def _proj_kernel(x_ref, g1_ref, w_ref, hmat_ref, gq_ref, gk_ref,
                 q1_ref, k1_ref, v1_ref, q4_ref, k4_ref, v4_ref, q16_ref, k16_ref, v16_ref, u_ref,
                 slab_ref, slab4_ref, wb_ref):
    @pl.when(pl.program_id(0) == 0)
    def _():
        wb_ref[...] = w_ref[...].astype(BF16)

    outs = ((q1_ref, q4_ref, q16_ref), (k1_ref, k4_ref, k16_ref), (v1_ref, v4_ref, v16_ref))
    slabs_per_tensor = ATTN_WIDTH // SLAB
    step = DILATIONS[1]
    rows = PROJ_SUB
    n_sub = x_ref.shape[0] // rows
    n_chunks = w_ref.shape[1] // LANE_TILE
    chunks_per_tensor = ATTN_WIDTH // LANE_TILE
    gains = (gq_ref, gk_ref)
    xn = [None] * n_sub

    def prologue(sub):
        x = x_ref[sub * rows:(sub + 1) * rows, :]
        ms = jnp.mean(x * x, axis=-1, keepdims=True)
        xn[sub] = (x * lax.rsqrt(ms + EPS) * g1_ref[...]).astype(BF16)

    def matmul(sub, c):
        return jnp.dot(xn[sub], wb_ref[:, c * LANE_TILE:(c + 1) * LANE_TILE], preferred_element_type=F32)

    def epilogue(sub, c, val):
        ti, t = divmod(c, chunks_per_tensor)
        lane_tile = slice(t * LANE_TILE, (t + 1) * LANE_TILE)
        if ti < 2:
            ms_h = jnp.dot((val * val).astype(BF16), hmat_ref[...], preferred_element_type=F32)
            val = val * lax.rsqrt(ms_h + EPS) * gains[ti][:, lane_tile]
            if ti == 0:
                val = val * Q_SCALE
        if ti < 3:
            outs[ti][0][t, 0, sub * rows:(sub + 1) * rows, :] = val.astype(BF16)
        r4_rows = slice(sub * (rows // step), (sub + 1) * (rows // step))
        r16_rows = slice(sub * (rows // CHUNK), (sub + 1) * (rows // CHUNK))
        for half in range(LANE_TILE // SLAB):
            s = t * (LANE_TILE // SLAB) + half
            lanes = slice(half * SLAB, (half + 1) * SLAB)
            slab = ti * slabs_per_tensor + s
            buf = sub % SLAB_BUFFERS
            slab_ref[buf, slab] = val[:, lanes]
            for r4 in range(step):
                part = slab_ref[buf, slab, pl.ds(r4, rows // step, stride=step), :]
                if ti < 3:
                    outs[ti][1][t, r4, r4_rows, lanes] = part.astype(BF16)
                slab4_ref[buf, slab, r4] = part
            for r4 in range(step):
                for j4 in range(step):
                    r16 = step * j4 + r4
                    part = slab4_ref[buf, slab, r4, pl.ds(j4, rows // CHUNK, stride=step), :].astype(BF16)
                    if ti < 3:
                        outs[ti][2][t, r16, r16_rows, lanes] = part
                    else:
                        u_ref[r16_rows, r16 * SSM_WIDTH + s * SLAB:r16 * SSM_WIDTH + (s + 1) * SLAB] = part

    work = [(sub, c) for sub in range(n_sub) for c in range(n_chunks)]
    prologue(0)
    pending = []
    for sub, c in work:
        pending.append((sub, c, matmul(sub, c)))
        if len(pending) > PROJ_AHEAD:
            epilogue(*pending.pop(0))
        if c == n_chunks // 2 and sub + 1 < n_sub:
            prologue(sub + 1)
    for item in pending:
        epilogue(*item)


def _slab_home(s):
    t, half = divmod(s, LANE_TILE // SLAB)
    return t, slice(half * SLAB, (half + 1) * SLAB)


def _proj(x, g1, w_in, hmat, gq, gk):
    s = x.shape[0]
    rows = PROJ_ROWS
    out_specs, out_shape = [], []
    for d in DILATIONS:
        for _ in range(3):
            out_specs.append(pl.BlockSpec((N_HEAD_TILES, d, rows // d, LANE_TILE), lambda i: (0, 0, i, 0)))
            out_shape.append(jax.ShapeDtypeStruct((N_HEAD_TILES, d, s // d, LANE_TILE), BF16))
    out_specs.append(pl.BlockSpec((rows // CHUNK, CHUNK * SSM_WIDTH), lambda i: (i, 0)))
    out_shape.append(jax.ShapeDtypeStruct((s // CHUNK, CHUNK * SSM_WIDTH), BF16))
    return pl.pallas_call(
        _proj_kernel,
        grid=(s // rows,),
        in_specs=[
            pl.BlockSpec((rows, D_MODEL), lambda i: (i, 0)),
            _const_spec((1, D_MODEL)),
            _const_spec(w_in.shape),
            _const_spec(hmat.shape),
            _const_spec((1, ATTN_WIDTH)),
            _const_spec((1, ATTN_WIDTH)),
        ],
        out_specs=out_specs,
        out_shape=out_shape,
        scratch_shapes=[pltpu.VMEM((SLAB_BUFFERS, 4 * ATTN_WIDTH // SLAB, PROJ_SUB, SLAB), F32),
                        pltpu.VMEM((SLAB_BUFFERS, 4 * ATTN_WIDTH // SLAB, DILATIONS[1],
                                    PROJ_SUB // DILATIONS[1], SLAB), F32),
                        pltpu.VMEM(w_in.shape, BF16)],
        compiler_params=_params(dimension_semantics=("arbitrary",)),
        name="proj",
    )(x, g1, w_in, hmat, gq, gk)


def _attn_kernel(q_ref, kc_ref, kp_ref, vc_ref, vp_ref, *rest):
    n_cast = (len(rest) - 4) // 2
    cast_in, (o_ref, lse_ref), cast_out = rest[:n_cast], rest[n_cast:n_cast + 2], rest[n_cast + 2:2 * n_cast + 2]
    kbuf, vbuf = rest[2 * n_cast + 2:]
    for src, dst in zip(cast_in, cast_out):
        dst[...] = src[...].astype(BF16)
    n_res, rows = q_ref.shape[0], q_ref.shape[1]
    kbuf[:, 0:BLOCK, :] = kp_ref[...]
    kbuf[:, BLOCK:, :] = kc_ref[...]
    vbuf[:, 0:BLOCK, :] = vp_ref[...]
    vbuf[:, BLOCK:, :] = vc_ref[...]

    heads_per_slab = SLAB // HEAD_DIM
    first_head = lax.broadcasted_iota(jnp.int32, (BLOCK, SLAB), 1) < HEAD_DIM
    qi = lax.broadcasted_iota(jnp.int32, (BLOCK, 2 * BLOCK), 0)
    ki = lax.broadcasted_iota(jnp.int32, (BLOCK, 2 * BLOCK), 1)
    band = (ki >= qi) & (ki <= qi + BLOCK)
    bias_band = jnp.where(band, 0.0, NEG_INF).astype(F32)
    no_prev = pl.program_id(2) == 0
    bias_first = jnp.where(band & ((ki >= BLOCK) | jnp.logical_not(no_prev)), 0.0, NEG_INF).astype(F32)

    for r, b in [(r, b) for r in range(n_res) for b in range(rows // BLOCK)]:
        qb = q_ref[r, b * BLOCK:(b + 1) * BLOCK, :]
        zero = jnp.zeros((BLOCK, SLAB), BF16)
        q_rows = []
        for h in range(HEADS_PER_TILE):
            slab, second = divmod(h, heads_per_slab)
            piece = qb[:, slab * SLAB:(slab + 1) * SLAB]
            piece = jnp.where(first_head, zero, piece) if second else jnp.where(first_head, piece, zero)
            q_rows.append(jnp.concatenate([piece if s == slab else zero for s in range(LANE_TILE // SLAB)], axis=1))
        q_stack = jnp.concatenate(q_rows, axis=0)
        kw = kbuf[r, b * BLOCK:(b + 2) * BLOCK, :]
        vw = vbuf[r, b * BLOCK:(b + 2) * BLOCK, :]
        s = lax.dot_general(q_stack, kw, (((1,), (1,)), ((), ())), preferred_element_type=F32)
        bias = bias_first if b == 0 else bias_band
        ps, ms, ls = [], [], []
        for h in range(HEADS_PER_TILE):
            sh = s[h * BLOCK:(h + 1) * BLOCK, :] + bias
            m = jnp.max(sh, axis=-1, keepdims=True)
            p = jnp.exp2(sh - m)
            ls.append(jnp.sum(p, axis=-1, keepdims=True))
            ms.append(m)
            ps.append(p.astype(BF16))
        pv = jnp.dot(jnp.concatenate(ps, axis=0), vw, preferred_element_type=F32)
        o_slabs, lse_slabs = [], []
        for slab in range(LANE_TILE // SLAB):
            cols = slice(slab * SLAB, (slab + 1) * SLAB)
            h0, h1 = slab * heads_per_slab, slab * heads_per_slab + 1
            l_slab = jnp.where(first_head, ls[h0], ls[h1])
            m_slab = jnp.where(first_head, ms[h0], ms[h1])
            pv_slab = jnp.where(first_head, pv[h0 * BLOCK:(h0 + 1) * BLOCK, cols],
                                pv[h1 * BLOCK:(h1 + 1) * BLOCK, cols])
            o_slabs.append(pv_slab / l_slab)
            lse_slabs.append(m_slab + jnp.log2(l_slab))
        o_ref[r, b * BLOCK:(b + 1) * BLOCK, :] = jnp.concatenate(o_slabs, axis=1).astype(BF16)
        lse_ref[r, b * BLOCK:(b + 1) * BLOCK, :] = jnp.concatenate(lse_slabs, axis=1)


def _attn_pattern(q, k, v, cast=()):
    nt, dilation, n, _ = q.shape
    rows = min(ATTN_ROWS, n)
    n_res = min(ATTN_ROWS // rows, dilation)
    blocks_per_tile = rows // BLOCK
    grid = (nt, dilation // n_res, n // rows)
    n_steps = grid[0] * grid[1] * grid[2]
    cur = pl.BlockSpec((None, n_res, rows, LANE_TILE), lambda t, r, j: (t, r, j, 0))
    prev = pl.BlockSpec((None, n_res, BLOCK, LANE_TILE),
                        lambda t, r, j: (t, r, jnp.maximum(j * blocks_per_tile - 1, 0), 0))
    step_id = lambda t, r, j: ((t * grid[1] + r) * grid[2] + j, 0)
    cast_specs = [pl.BlockSpec((w.shape[0] // n_steps, w.shape[1]), step_id) for w in cast]
    outs = pl.pallas_call(
        _attn_kernel,
        grid=grid,
        in_specs=[cur, cur, prev, cur, prev] + cast_specs,
        out_specs=[cur, cur] + cast_specs,
        out_shape=[jax.ShapeDtypeStruct(q.shape, BF16), jax.ShapeDtypeStruct(q.shape, F32)]
        + [jax.ShapeDtypeStruct(w.shape, BF16) for w in cast],
        scratch_shapes=[pltpu.VMEM((n_res, rows + BLOCK, LANE_TILE), BF16),
                        pltpu.VMEM((n_res, rows + BLOCK, LANE_TILE), BF16)],
        compiler_params=_params(dimension_semantics=("arbitrary", "arbitrary", "arbitrary")),
        name=f"attn_d{dilation}",
    )(q, k, k, v, v, *cast)
    return outs[0], outs[1], outs[2:]


def _s5_weights(a_re, a_im, log_dt, b_re, b_im, c_re, c_im, d_skip):
    p, c = SSM_STATE, SSM_GROUP
    lr, li = a_re.astype(F32), a_im.astype(F32)
    dt = jnp.exp(log_dt.astype(F32))[:, None]

    def apow(j):
        mag = jnp.exp(lr * dt * j)
        return mag * jnp.cos(li * dt * j), mag * jnp.sin(li * dt * j)

    ab_r, ab_i = apow(1.0)
    den = lr * lr + li * li
    nr, ni = ab_r - 1.0, ab_i
    cr = (nr * lr + ni * li) / den
    ci = (ni * lr - nr * li) / den
    br, bi = b_re.astype(F32), b_im.astype(F32)
    bb_r = cr[..., None] * br - ci[..., None] * bi
    bb_i = cr[..., None] * bi + ci[..., None] * br
    cre, cim = c_re.astype(F32), c_im.astype(F32)

    def apow_many(js):
        j = jnp.asarray(js, F32)[:, None, None]
        mag = jnp.exp(lr * dt * j)
        return mag * jnp.cos(li * dt * j), mag * jnp.sin(li * dt * j)

    pr, pi = apow_many(range(SUB_STEPS + 1))
    pad_lanes = lambda w: jnp.pad(w, [(0, 0)] * (w.ndim - 1) + [(0, SLAB - w.shape[-1])])
    wp_r = pad_lanes(jnp.stack([pr[SUB_STEPS - 1 - s] for s in range(SUB_STEPS)]))
    wp_i = pad_lanes(jnp.stack([pi[SUB_STEPS - 1 - s] for s in range(SUB_STEPS)]))
    bbt_r, bbt_i = pad_lanes(bb_r.transpose(0, 2, 1)), pad_lanes(bb_i.transpose(0, 2, 1))
    win_r = wp_r[:, :, None, :] * bbt_r[None] - wp_i[:, :, None, :] * bbt_i[None]
    win_i = wp_r[:, :, None, :] * bbt_i[None] + wp_i[:, :, None, :] * bbt_r[None]
    to_in = lambda w: (w.reshape(PAIRS, SUB, N_OCT, OCT, c, SLAB).transpose(0, 2, 1, 3, 4, 5)
                       .reshape(PAIRS, N_OCT, SUB * OCT * c, SLAB))
    ct_r, ct_i = pad_lanes(cre.transpose(0, 2, 1)), pad_lanes(cim.transpose(0, 2, 1))
    out_r = ct_r[None] * pr[..., None] - ct_i[None] * pi[..., None]
    out_i = -(ct_r[None] * pi[..., None] + ct_i[None] * pr[..., None])
    to_out = lambda w: w.reshape(SUB_STEPS + 1, N_OCT, OCT * p, SLAB)

    flat = lambda re, im: jnp.stack([re.reshape(-1), im.reshape(-1)])
    top_r, top_i = apow_many([CHUNK])
    lam = flat(top_r[0], top_i[0])
    levels = [lam]
    for _ in range(SCAN_GROUP - 1):
        lr_, li_ = levels[-1][0], levels[-1][1]
        levels.append(jnp.stack([lr_ * lam[0] - li_ * lam[1], lr_ * lam[1] + li_ * lam[0]]))
    m_t, w_r, w_i, e_r, e_i = _s5_expand(to_in(win_r), to_in(win_i), to_out(out_r), to_out(out_i),
                                         d_skip.astype(F32).reshape(N_OCT, 1, OCT * c))
    return dict(m_t=m_t, w_r=w_r, w_i=w_i, e_r=e_r, e_i=e_i,
                a_sub=flat(pr[SUB_STEPS], pi[SUB_STEPS]),
                levels=jnp.stack(levels, axis=1))


def _s5_expand_kernel(wr_ref, wi_ref, or_ref, oi_ref, d_ref, mt_ref, wfr_ref, wfi_ref, efr_ref, efi_ref):
    p, c = SSM_STATE, SSM_GROUP
    n_state, n_lane = OCT * p, OCT * c

    def rep(inner, total):
        r = lax.broadcasted_iota(jnp.int32, (SLAB, total), 0)
        col = lax.broadcasted_iota(jnp.int32, (SLAB, total), 1)
        return jnp.where((col & (inner - 1)) == r, 1.0, 0.0).astype(BF16)

    shift = lambda n: int(math.log2(n))
    rep_in, rep_out = rep(p, n_state), rep(c, n_lane)
    in_row = lax.broadcasted_iota(jnp.int32, (SUB * n_lane, n_state), 0)
    in_col = lax.broadcasted_iota(jnp.int32, (SUB * n_lane, n_state), 1)
    mask_in = ((in_row >> shift(c)) & (OCT - 1)) == (in_col >> shift(p))
    out_row = lax.broadcasted_iota(jnp.int32, (n_state, n_lane), 0)
    out_col = lax.broadcasted_iota(jnp.int32, (n_state, n_lane), 1)
    mask_out = (out_row >> shift(p)) == (out_col >> shift(c))
    diag = (lax.broadcasted_iota(jnp.int32, (n_lane, n_lane), 0)
            == lax.broadcasted_iota(jnp.int32, (n_lane, n_lane), 1))

    def split(x):
        hi = x.astype(BF16)
        return hi, (x - hi.astype(F32)).astype(BF16)

    def expand(x_b, rep_m, mask):
        full = jnp.dot(x_b, rep_m, preferred_element_type=F32)
        return jnp.where(mask, full, 0.0).astype(BF16)

    last = slice((SUB - 1) * n_lane, SUB * n_lane)
    for o in range(wfr_ref.shape[0]):
        in0 = []
        for src, dst in ((wr_ref, wfr_ref), (wi_ref, wfi_ref)):
            for q in range(PAIRS):
                hi, lo = split(src[q, o])
                full = expand(hi, rep_in, mask_in)
                dst[o, q * SUB * n_lane:(q + 1) * SUB * n_lane, :] = full
                if q == PAIRS - 1:
                    in0.append((full[last, :], expand(lo, rep_in, mask_in)[last, :]))
        lags = [None] * SUB_STEPS
        for (src, dst), (in_hi, in_lo) in zip(((or_ref, efr_ref), (oi_ref, efi_ref)), in0):
            outs_hi = []
            for j in range(SUB_STEPS + 1):
                hi, lo = split(src[j, o])
                out_hi = expand(hi, rep_out, mask_out)
                outs_hi.append(out_hi)
                if j < SUB_STEPS:
                    out_lo = expand(lo, rep_out, mask_out)
                    part = (jnp.dot(in_hi, out_hi, preferred_element_type=F32)
                            + jnp.dot(in_hi, out_lo, preferred_element_type=F32)
                            + jnp.dot(in_lo, out_hi, preferred_element_type=F32))
                    lags[j] = part if lags[j] is None else lags[j] + part
            dst[o] = jnp.concatenate(outs_hi[1:], axis=1)
        lags[0] = lags[0] + jnp.where(diag, d_ref[o], 0.0)
        zero = jnp.zeros_like(lags[0])
        for dist in range(PAIRS):
            lag = lambda t_in, t_out: SUB * dist + t_out - t_in
            mt_ref[dist, o] = jnp.concatenate(
                [jnp.concatenate([lags[lag(t_in, t_out)] if lag(t_in, t_out) >= 0 else zero
                                  for t_out in range(SUB)], axis=1)
                 for t_in in range(SUB)], axis=0).astype(BF16)


def _s5_expand(wc_r, wc_i, oc_r, oc_i, d_row):
    n_in, n_state = wc_r.shape[2], oc_r.shape[2]
    shapes = [jax.ShapeDtypeStruct((PAIRS, N_OCT, n_in, n_in), BF16),
              jax.ShapeDtypeStruct((N_OCT, PAIRS * n_in, n_state), BF16),
              jax.ShapeDtypeStruct((N_OCT, PAIRS * n_in, n_state), BF16),
              jax.ShapeDtypeStruct((N_OCT, n_state, PAIRS * n_in), BF16),
              jax.ShapeDtypeStruct((N_OCT, n_state, PAIRS * n_in), BF16)]
    def per_tile(shape, axis):
        block = tuple(1 if i == axis else n for i, n in enumerate(shape))
        return pl.BlockSpec(block, lambda o: tuple(o if i == axis else 0 for i in range(len(shape))))

    operands = (wc_r, wc_i, oc_r, oc_i, d_row)
    tile_axis = (1, 1, 1, 1, 0)
    return pl.pallas_call(
        _s5_expand_kernel,
        grid=(N_OCT,),
        in_specs=[per_tile(a.shape, ax) for a, ax in zip(operands, tile_axis)],
        out_specs=[per_tile(s.shape, ax) for s, ax in zip(shapes, (1, 0, 0, 0, 0))],
        out_shape=shapes,
        compiler_params=_params(dimension_semantics=("arbitrary",)),
        name="s5_expand",
    )(*operands)


def _cmul(ar, ai, br, bi):
    return ar * br - ai * bi, ar * bi + ai * br


def _s5_kernel(u_ref, mt_ref, wr_ref, wi_ref, er_ref, ei_ref, asub_ref, lvl_ref,
               gluw_ref, glub_ref, gout_ref, out_ref, carry_ref, zr_ref, zi_ref, locr_ref, loci_ref):
    rows = u_ref.shape[0]
    width = SSM_WIDTH
    oct_lanes = OCT * SSM_GROUP
    oct_state = OCT * SSM_STATE

    @pl.when(pl.program_id(0) == 0)
    def _():
        carry_ref[...] = jnp.zeros_like(carry_ref)

    def u_tile(a, q, o):
        first = a * SUB_STEPS + q * SUB
        return jnp.concatenate(
            [u_ref[:, (first + t) * width + o * oct_lanes:(first + t) * width + (o + 1) * oct_lanes]
             for t in range(SUB)], axis=1)

    a_r, a_i = asub_ref[0:1, :], asub_ref[1:2, :]

    for a in range(N_SUB):
        for o in range(N_OCT):
            cols = slice(o * oct_state, (o + 1) * oct_state)
            ut = jnp.concatenate([u_tile(a, q, o) for q in range(PAIRS)], axis=1)
            pr = jnp.dot(ut, wr_ref[o], preferred_element_type=F32)
            pi = jnp.dot(ut, wi_ref[o], preferred_element_type=F32)
            if a > 0:
                hr, hi = _cmul(zr_ref[:, cols], zi_ref[:, cols], a_r[:, cols], a_i[:, cols])
                pr, pi = hr + pr, hi + pi
            zr_ref[:, cols] = pr
            zi_ref[:, cols] = pi
            if a < N_SUB - 1:
                locr_ref[a, :, cols] = pr.astype(BF16)
                loci_ref[a, :, cols] = pi.astype(BF16)

    row = lax.broadcasted_iota(jnp.int32, (rows, 1), 0)
    in_group = row & (SCAN_GROUP - 1)
    sr, si = zr_ref[...], zi_ref[...]
    sh = 1
    while sh < SCAN_GROUP:
        keep = in_group >= sh
        pr_, pi_ = lvl_ref[0, sh - 1:sh, :], lvl_ref[1, sh - 1:sh, :]
        tr = jnp.where(keep, pltpu.roll(sr, sh, axis=0), 0.0)
        ti = jnp.where(keep, pltpu.roll(si, sh, axis=0), 0.0)
        mr, mi = _cmul(tr, ti, pr_, pi_)
        sr, si = sr + mr, si + mi
        sh *= 2
    c_r, c_i = carry_ref[0:1, :], carry_ref[1:2, :]
    pw_r, pw_i = lvl_ref[0], lvl_ref[1]
    last_r, last_i = c_r, c_i
    for g in range(rows // SCAN_GROUP):
        grp = slice(g * SCAN_GROUP, (g + 1) * SCAN_GROUP)
        ar, ai = _cmul(jnp.broadcast_to(last_r, pw_r.shape), jnp.broadcast_to(last_i, pw_i.shape), pw_r, pw_i)
        gr, gi = sr[grp, :] + ar, si[grp, :] + ai
        zr_ref[grp, :] = gr
        zi_ref[grp, :] = gi
        last_r, last_i = gr[SCAN_GROUP - 1:SCAN_GROUP, :], gi[SCAN_GROUP - 1:SCAN_GROUP, :]
    carry_ref[0:1, :] = last_r
    carry_ref[1:2, :] = last_i
    xr = jnp.where(row == 0, c_r, pltpu.roll(zr_ref[...], 1, axis=0))
    xi = jnp.where(row == 0, c_i, pltpu.roll(zi_ref[...], 1, axis=0))
    zr_ref[...] = xr
    zi_ref[...] = xi

    for a in range(N_SUB):
        if a > 0:
            nr, ni = _cmul(zr_ref[...], zi_ref[...], a_r, a_i)
            zr_ref[...] = nr
            zi_ref[...] = ni
        if a == 0:
            xr_b = zr_ref[...].astype(BF16)
            xi_b = zi_ref[...].astype(BF16)
        else:
            xr_b = (zr_ref[...] + locr_ref[a - 1].astype(F32)).astype(BF16)
            xi_b = (zi_ref[...] + loci_ref[a - 1].astype(F32)).astype(BF16)
        tiles = []
        for o in range(N_OCT):
            cols = slice(o * oct_state, (o + 1) * oct_state)
            from_state = (jnp.dot(xr_b[:, cols], er_ref[o], preferred_element_type=F32)
                          + jnp.dot(xi_b[:, cols], ei_ref[o], preferred_element_type=F32))
            parts = []
            for q in range(PAIRS):
                acc = from_state[:, q * SUB * oct_lanes:(q + 1) * SUB * oct_lanes]
                for q_in in range(q + 1):
                    acc = acc + jnp.dot(u_tile(a, q_in, o), mt_ref[q - q_in, o], preferred_element_type=F32)
                parts.append(acc)
            tiles.append(jnp.concatenate(parts, axis=1))
        for b in range(SUB_STEPS):
            t = a * SUB_STEPS + b
            y = jnp.concatenate([tl[:, b * oct_lanes:(b + 1) * oct_lanes] for tl in tiles], axis=1)
            y2 = y * y
            z = y * (1.0 / (1.0 + jnp.exp2(y * (y2 * (-GELU_C * 0.044715 * LOG2E) - GELU_C * LOG2E))))
            gate = jnp.dot(z.astype(BF16), gluw_ref[...], preferred_element_type=F32) + glub_ref[...]
            o = z * (1.0 / (1.0 + jnp.exp2(gate)))
            ms = jnp.mean(o * o, axis=-1, keepdims=True)
            out_ref[:, t * width:(t + 1) * width] = (o * lax.rsqrt(ms + EPS) * gout_ref[...]).astype(BF16)


def _s5(uv, w, glu_w, glu_b, g_out):
    n = uv.shape[0]
    rows = S5_ROWS
    tile = pl.BlockSpec((rows, CHUNK * SSM_WIDTH), lambda i: (i, 0))
    return pl.pallas_call(
        _s5_kernel,
        grid=(n // rows,),
        in_specs=[tile] + [_const_spec(a.shape) for a in
                           (w["m_t"], w["w_r"], w["w_i"], w["e_r"], w["e_i"], w["a_sub"], w["levels"],
                            glu_w, glu_b, g_out)],
        out_specs=tile,
        out_shape=jax.ShapeDtypeStruct((n, CHUNK * SSM_WIDTH), BF16),
        scratch_shapes=[pltpu.VMEM((2, STATE_COLS), F32),
                        pltpu.VMEM((rows, STATE_COLS), F32),
                        pltpu.VMEM((rows, STATE_COLS), F32),
                        pltpu.VMEM((N_SUB - 1, rows, STATE_COLS), BF16),
                        pltpu.VMEM((N_SUB - 1, rows, STATE_COLS), BF16)],
        compiler_params=_params(dimension_semantics=("arbitrary",)),
        name="s5",
    )(uv, w["m_t"], w["w_r"], w["w_i"], w["e_r"], w["e_i"], w["a_sub"], w["levels"], glu_w, glu_b, g_out)


def _mlp_kernel(x_ref, o1_ref, o2_ref, o3_ref, l1_ref, l2_ref, l3_ref, ssm_ref, ga_ref, wout_ref,
                g2_ref, wup_ref, wdn_ref, out_ref, slab_ref, slab4_ref, mix_ref):
    rows = PROJ_SUB
    n_sub = x_ref.shape[0] // rows
    n_slabs = ATTN_WIDTH // SLAB
    step = DILATIONS[1]
    n_ff = D_FF // FF_CHUNK

    def regroup(ref, d, sub, buf):
        part = slice(sub * (rows // d), (sub + 1) * (rows // d))
        for s in range(n_slabs):
            t, lanes = _slab_home(s)
            if ref is ssm_ref:
                piece = lambda r: ref[part, r * SSM_WIDTH + s * SLAB:r * SSM_WIDTH + (s + 1) * SLAB]
            else:
                piece = lambda r: ref[t, r, part, lanes]
            if d == step:
                for r in range(d):
                    slab_ref[sub, buf, s, pl.ds(r, rows // d, stride=d), :] = piece(r).astype(F32)
            else:
                for r4 in range(step):
                    for j4 in range(step):
                        slab4_ref[sub, buf - 2, s, r4, pl.ds(j4, rows // d, stride=step), :] = (
                            piece(step * j4 + r4).astype(F32))
                for r4 in range(step):
                    slab_ref[sub, buf, s, pl.ds(r4, rows // step, stride=step), :] = slab4_ref[sub, buf - 2, s, r4]

    def in_order(sub, buf):
        return jnp.concatenate([slab_ref[sub, buf, s] for s in range(n_slabs)], axis=1)

    def merge_slab(sub, s):
        t, lanes = _slab_home(s)
        part = slice(sub * rows, (sub + 1) * rows)
        l1, l2, l3 = l1_ref[t, 0, part, lanes], slab_ref[sub, 1, s], slab_ref[sub, 3, s]
        m = jnp.maximum(jnp.maximum(l1, l2), l3)
        w1, w2, w3 = jnp.exp2(l1 - m), jnp.exp2(l2 - m), jnp.exp2(l3 - m)
        num = w1 * o1_ref[t, 0, part, lanes].astype(F32) + w2 * slab_ref[sub, 0, s] + w3 * slab_ref[sub, 2, s]
        slab_ref[sub, 0, s] = num / (w1 + w2 + w3)

    def merge_finish(sub):
        attn = in_order(sub, 0)
        ms = jnp.mean(attn * attn, axis=-1, keepdims=True)
        mix_ref[sub, :, 0:ATTN_WIDTH] = (attn * lax.rsqrt(ms + EPS) * ga_ref[...]).astype(BF16)
        mix_ref[sub, :, ATTN_WIDTH:] = in_order(sub, 4).astype(BF16)

    def prologue(sub):
        return ([lambda: regroup(o2_ref, step, sub, 0), lambda: regroup(l2_ref, step, sub, 1),
                 lambda: regroup(o3_ref, CHUNK, sub, 2), lambda: regroup(l3_ref, CHUNK, sub, 3),
                 lambda: regroup(ssm_ref, CHUNK, sub, 4)]
                + [lambda s=s: merge_slab(sub, s) for s in range(n_slabs)] + [lambda: merge_finish(sub)])

    def main(sub):
        part = slice(sub * rows, (sub + 1) * rows)
        st = {}

        def out_proj():
            x1 = x_ref[part, :] + jnp.dot(mix_ref[sub], wout_ref[...], preferred_element_type=F32)
            ms2 = jnp.mean(x1 * x1, axis=-1, keepdims=True)
            st["xn"] = (x1 * lax.rsqrt(ms2 + EPS) * g2_ref[...]).astype(BF16)
            st["acc"] = x1

        def up(c):
            st[c] = jnp.dot(st["xn"], wup_ref[:, c * FF_CHUNK:(c + 1) * FF_CHUNK], preferred_element_type=F32)

        def down(c):
            h = jnp.square(jnp.maximum(st.pop(c), 0.0)).astype(BF16)
            st["acc"] = st["acc"] + jnp.dot(h, wdn_ref[c * FF_CHUNK:(c + 1) * FF_CHUNK, :],
                                            preferred_element_type=F32)

        def finish():
            out_ref[part, :] = st["acc"]

        stages = [out_proj, lambda: up(0)]
        for c in range(n_ff):
            if c + 1 < n_ff:
                stages.append(lambda c=c: up(c + 1))
            stages.append(lambda c=c: down(c))
        return stages + [finish]

    tail = 3
    program = prologue(0)
    all_stages = [main(sub) for sub in range(n_sub)]
    for sub in range(n_sub):
        stages = all_stages[sub]
        head = stages if sub == 0 else stages[1:]
        if sub + 1 < n_sub:
            side = prologue(sub + 1)
            body = []
            for stage in head[:-tail]:
                body.append(stage)
                if side:
                    body.append(side.pop(0))
            program += body + side + [all_stages[sub + 1][0]] + head[-tail:]
        else:
            program += head
    for piece in program:
        piece()


def _mlp(x, os_, ls_, ssm, ga, w_out, g2, w_up, w_dn):
    s = x.shape[0]
    rows = MLP_ROWS
    xt = pl.BlockSpec((rows, D_MODEL), lambda i: (i, 0))
    ht = [pl.BlockSpec((N_HEAD_TILES, d, rows // d, LANE_TILE), lambda i: (0, 0, i, 0)) for d in DILATIONS]
    return pl.pallas_call(
        _mlp_kernel,
        grid=(s // rows,),
        in_specs=[xt, *ht, *ht, pl.BlockSpec((rows // CHUNK, CHUNK * SSM_WIDTH), lambda i: (i, 0)),
                  _const_spec(ga.shape), _const_spec(w_out.shape), _const_spec(g2.shape),
                  _const_spec(w_up.shape), _const_spec(w_dn.shape)],
        out_specs=xt,
        out_shape=jax.ShapeDtypeStruct((s, D_MODEL), F32),
        scratch_shapes=[pltpu.VMEM((rows // PROJ_SUB, 5, ATTN_WIDTH // SLAB, PROJ_SUB, SLAB), F32),
                        pltpu.VMEM((rows // PROJ_SUB, 3, ATTN_WIDTH // SLAB, DILATIONS[1],
                                    PROJ_SUB // DILATIONS[1], SLAB), F32),
                        pltpu.VMEM((rows // PROJ_SUB, PROJ_SUB, D_MODEL), BF16)],
        compiler_params=_params(dimension_semantics=("arbitrary",)),
        name="mlp",
    )(x, *os_, *ls_, ssm, ga, w_out, g2, w_up, w_dn)


def _layer(x, norm1_g, w_in, q_norm_g, k_norm_g, ssm_a_re, ssm_a_im, ssm_log_dt, ssm_b_re, ssm_b_im,
           ssm_c_re, ssm_c_im, ssm_d, glu_w, glu_b, attn_out_norm_g, ssm_out_norm_g, w_out, norm2_g,
           w_mlp_up, w_mlp_down):
    row = lambda g: g.astype(F32).reshape(1, -1)
    heads = ATTN_WIDTH // HEAD_DIM
    head_id = jnp.arange(LANE_TILE) // HEAD_DIM
    hmat = jnp.where(head_id[:, None] == head_id[None, :], 1.0 / HEAD_DIM, 0.0).astype(BF16)
    *qkv, u = _proj(x, row(norm1_g), w_in.astype(F32), hmat,
                    row(jnp.tile(q_norm_g, heads)), row(jnp.tile(k_norm_g, heads)))
    outs, lses = [], []
    mlp_weights = (w_out.astype(F32), w_mlp_up.astype(F32), w_mlp_down.astype(F32))
    for i in range(len(DILATIONS)):
        o, lse, cast = _attn_pattern(*qkv[3 * i:3 * i + 3], cast=mlp_weights if i == 0 else ())
        outs.append(o)
        lses.append(lse)
        if i == 0:
            w_out_b, w_up_b, w_dn_b = cast
    w = _s5_weights(ssm_a_re, ssm_a_im, ssm_log_dt, ssm_b_re, ssm_b_im, ssm_c_re, ssm_c_im, ssm_d)
    ssm = _s5(u, w, (glu_w.astype(F32) * -LOG2E).astype(BF16), row(glu_b) * -LOG2E, row(ssm_out_norm_g))
    return _mlp(x, outs, lses, ssm, row(attn_out_norm_g), w_out_b, row(norm2_g), w_up_b, w_dn_b)


def kernel(x, norm1_g, w_in, q_norm_g, k_norm_g, ssm_a_re, ssm_a_im, ssm_log_dt, ssm_b_re, ssm_b_im,
           ssm_c_re, ssm_c_im, ssm_d, glu_w, glu_b, attn_out_norm_g, ssm_out_norm_g, w_out, norm2_g,
           w_mlp_up, w_mlp_down):
    params = (norm1_g, w_in, q_norm_g, k_norm_g, ssm_a_re, ssm_a_im, ssm_log_dt, ssm_b_re, ssm_b_im,
              ssm_c_re, ssm_c_im, ssm_d, glu_w, glu_b, attn_out_norm_g, ssm_out_norm_g, w_out, norm2_g,
              w_mlp_up, w_mlp_down)
    batch = x.shape[0]
    outs = []
    for bi in range(batch):
        h = x[bi]
        for layer in range(norm1_g.shape[0]):
            h = _layer(h, *[p[layer] for p in params])
        outs.append(h)
    return jnp.stack(outs)
```

```python
import functools
import math

import jax
import jax.numpy as jnp
from jax import lax
from jax.experimental import pallas as pl
from jax.experimental.pallas import tpu as pltpu

F32 = jnp.float32
BF16 = jnp.bfloat16

D_MODEL = 1024
ATTN_WIDTH = 512
HEAD_DIM = 64
SSM_WIDTH = 512
SSM_GROUP = 16
SSM_GROUPS = 32
SSM_STATE = 64
D_FF = 4096
EPS = 1e-6
NEG_INF = -1e30
DILATIONS = (1, 4, 16)
BLOCK = 128

SLAB = 128
LANE_TILE = 256
HEADS_PER_TILE = LANE_TILE // HEAD_DIM
N_HEAD_TILES = ATTN_WIDTH // LANE_TILE
CHUNK = 16
SUB = 2
PAIRS = 2
SUB_STEPS = SUB * PAIRS
N_SUB = CHUNK // SUB_STEPS
OCT = 8
N_OCT = SSM_GROUPS // OCT
SCAN_GROUP = 8
STATE_COLS = SSM_GROUPS * SSM_STATE
VMEM_LIMIT = 56 * 1024 * 1024

LOG2E = math.log2(math.e)
GELU_C = 2.0 * math.sqrt(2.0 / math.pi)
Q_SCALE = HEAD_DIM ** -0.5 * LOG2E

PROJ_ROWS = 1024
PROJ_SUB = 256
SLAB_BUFFERS = 2
X_SLOTS = 3
PROJ_AHEAD = 2
ATTN_ROWS = 4096
S5_ROWS = 256
MLP_ROWS = 512
FF_CHUNK = 512


def _const_spec(shape):
    nd = len(shape)
    return pl.BlockSpec(shape, lambda *_: (0,) * nd, pipeline_mode=pl.Buffered(1))


def _params(**kw):
    return pltpu.CompilerParams(vmem_limit_bytes=VMEM_LIMIT, **kw)


def _proj_kernel(x_hbm, g1_ref, w_ref, hmat_ref, gq_ref, gk_ref,
                 q1_ref, k1_ref, v1_ref, q4_ref, k4_ref, v4_ref, q16_ref, k16_ref, v16_ref, u_ref,
                 slab_ref, slab4_ref, wb_ref, xbuf_ref, xsem):
    i = pl.program_id(0)
    n_steps = pl.num_programs(0)

    def x_copy(tile):
        slot = tile % X_SLOTS
        return pltpu.make_async_copy(x_hbm.at[pl.ds(tile * PROJ_ROWS, PROJ_ROWS), :],
                                     xbuf_ref.at[slot], xsem.at[slot])

    @pl.when(i == 0)
    def _():
        for tile in range(X_SLOTS - 1):
            x_copy(tile).start()
        wb_ref[...] = w_ref[...].astype(BF16)

    @pl.when(i + (X_SLOTS - 1) < n_steps)
    def _():
        x_copy(i + (X_SLOTS - 1)).start()

    x_copy(i).wait()
    x_ref = xbuf_ref.at[i % X_SLOTS]

    outs = ((q1_ref, q4_ref, q16_ref), (k1_ref, k4_ref, k16_ref), (v1_ref, v4_ref, v16_ref))
    slabs_per_tensor = ATTN_WIDTH // SLAB
    step = DILATIONS[1]
    rows = PROJ_SUB
    n_sub = x_ref.shape[0] // rows
    n_chunks = w_ref.shape[1] // LANE_TILE
    chunks_per_tensor = ATTN_WIDTH // LANE_TILE
    gains = (gq_ref, gk_ref)
    xn = [None] * n_sub

    def prologue(sub):
        x = x_ref[sub * rows:(sub + 1) * rows, :]
        ms = jnp.mean(x * x, axis=-1, keepdims=True)
        xn[sub] = (x * lax.rsqrt(ms + EPS) * g1_ref[...]).astype(BF16)

    def matmul(sub, c):
        return jnp.dot(xn[sub], wb_ref[:, c * LANE_TILE:(c + 1) * LANE_TILE], preferred_element_type=F32)

    def epilogue(sub, c, val):
        ti, t = divmod(c, chunks_per_tensor)
        lane_tile = slice(t * LANE_TILE, (t + 1) * LANE_TILE)
        if ti < 2:
            ms_h = jnp.dot((val * val).astype(BF16), hmat_ref[...], preferred_element_type=F32)
            val = val * lax.rsqrt(ms_h + EPS) * gains[ti][:, lane_tile]
            if ti == 0:
                val = val * Q_SCALE
        if ti < 3:
            outs[ti][0][t, 0, sub * rows:(sub + 1) * rows, :] = val.astype(BF16)
        r4_rows = slice(sub * (rows // step), (sub + 1) * (rows // step))
        r16_rows = slice(sub * (rows // CHUNK), (sub + 1) * (rows // CHUNK))
        for half in range(LANE_TILE // SLAB):
            s = t * (LANE_TILE // SLAB) + half
            lanes = slice(half * SLAB, (half + 1) * SLAB)
            slab = ti * slabs_per_tensor + s
            buf = sub % SLAB_BUFFERS
            slab_ref[buf, slab] = val[:, lanes]
            for r4 in range(step):
                part = slab_ref[buf, slab, pl.ds(r4, rows // step, stride=step), :]
                if ti < 3:
                    outs[ti][1][t, r4, r4_rows, lanes] = part.astype(BF16)
                slab4_ref[buf, slab, r4] = part
            for r4 in range(step):
                for j4 in range(step):
                    r16 = step * j4 + r4
                    part = slab4_ref[buf, slab, r4, pl.ds(j4, rows // CHUNK, stride=step), :].astype(BF16)
                    if ti < 3:
                        outs[ti][2][t, r16, r16_rows, lanes] = part
                    else:
                        u_ref[r16_rows, r16 * SSM_WIDTH + s * SLAB:r16 * SSM_WIDTH + (s + 1) * SLAB] = part

    work = [(sub, c) for sub in range(n_sub) for c in range(n_chunks)]
    prologue(0)
    pending = []
    for sub, c in work:
        pending.append((sub, c, matmul(sub, c)))
        if len(pending) > PROJ_AHEAD:
            epilogue(*pending.pop(0))
        if c == n_chunks // 2 and sub + 1 < n_sub:
            prologue(sub + 1)
    for item in pending:
        epilogue(*item)


def _slab_home(s):
    t, half = divmod(s, LANE_TILE // SLAB)
    return t, slice(half * SLAB, (half + 1) * SLAB)


def _proj(x, g1, w_in, hmat, gq, gk):
    s = x.shape[0]
    rows = PROJ_ROWS
    out_specs, out_shape = [], []
    for d in DILATIONS:
        for _ in range(3):
            out_specs.append(pl.BlockSpec((N_HEAD_TILES, d, rows // d, LANE_TILE), lambda i: (0, 0, i, 0)))
            out_shape.append(jax.ShapeDtypeStruct((N_HEAD_TILES, d, s // d, LANE_TILE), BF16))
    out_specs.append(pl.BlockSpec((rows // CHUNK, CHUNK * SSM_WIDTH), lambda i: (i, 0)))
    out_shape.append(jax.ShapeDtypeStruct((s // CHUNK, CHUNK * SSM_WIDTH), BF16))
    return pl.pallas_call(
        _proj_kernel,
        grid=(s // rows,),
        in_specs=[
            pl.BlockSpec(memory_space=pl.ANY),
            _const_spec((1, D_MODEL)),
            _const_spec(w_in.shape),
            _const_spec(hmat.shape),
            _const_spec((1, ATTN_WIDTH)),
            _const_spec((1, ATTN_WIDTH)),
        ],
        out_specs=out_specs,
        out_shape=out_shape,
        scratch_shapes=[pltpu.VMEM((SLAB_BUFFERS, 4 * ATTN_WIDTH // SLAB, PROJ_SUB, SLAB), F32),
                        pltpu.VMEM((SLAB_BUFFERS, 4 * ATTN_WIDTH // SLAB, DILATIONS[1],
                                    PROJ_SUB // DILATIONS[1], SLAB), F32),
                        pltpu.VMEM(w_in.shape, BF16),
                        pltpu.VMEM((X_SLOTS, rows, D_MODEL), F32),
                        pltpu.SemaphoreType.DMA((X_SLOTS,))],
        compiler_params=_params(dimension_semantics=("arbitrary",)),
        name="proj",
    )(x, g1, w_in, hmat, gq, gk)


def _attn_kernel(q_ref, kc_ref, kp_ref, vc_ref, vp_ref, *rest):
    n_cast = (len(rest) - 4) // 2
    cast_in, (o_ref, lse_ref), cast_out = rest[:n_cast], rest[n_cast:n_cast + 2], rest[n_cast + 2:2 * n_cast + 2]
    kbuf, vbuf = rest[2 * n_cast + 2:]
    for src, dst in zip(cast_in, cast_out):
        dst[...] = src[...].astype(BF16)
    n_res, rows = q_ref.shape[0], q_ref.shape[1]
    kbuf[:, 0:BLOCK, :] = kp_ref[...]
    kbuf[:, BLOCK:, :] = kc_ref[...]
    vbuf[:, 0:BLOCK, :] = vp_ref[...]
    vbuf[:, BLOCK:, :] = vc_ref[...]

    heads_per_slab = SLAB // HEAD_DIM
    first_head = lax.broadcasted_iota(jnp.int32, (BLOCK, SLAB), 1) < HEAD_DIM
    qi = lax.broadcasted_iota(jnp.int32, (BLOCK, 2 * BLOCK), 0)
    ki = lax.broadcasted_iota(jnp.int32, (BLOCK, 2 * BLOCK), 1)
    band = (ki >= qi) & (ki <= qi + BLOCK)
    bias_band = jnp.where(band, 0.0, NEG_INF).astype(F32)
    no_prev = pl.program_id(2) == 0
    bias_first = jnp.where(band & ((ki >= BLOCK) | jnp.logical_not(no_prev)), 0.0, NEG_INF).astype(F32)

    for r, b in [(r, b) for r in range(n_res) for b in range(rows // BLOCK)]:
        qb = q_ref[r, b * BLOCK:(b + 1) * BLOCK, :]
        zero = jnp.zeros((BLOCK, SLAB), BF16)
        q_rows = []
        for h in range(HEADS_PER_TILE):
            slab, second = divmod(h, heads_per_slab)
            piece = qb[:, slab * SLAB:(slab + 1) * SLAB]
            piece = jnp.where(first_head, zero, piece) if second else jnp.where(first_head, piece, zero)
            q_rows.append(jnp.concatenate([piece if s == slab else zero for s in range(LANE_TILE // SLAB)], axis=1))
        q_stack = jnp.concatenate(q_rows, axis=0)
        kw = kbuf[r, b * BLOCK:(b + 2) * BLOCK, :]
        vw = vbuf[r, b * BLOCK:(b + 2) * BLOCK, :]
        s = lax.dot_general(q_stack, kw, (((1,), (1,)), ((), ())), preferred_element_type=F32)
        bias = bias_first if b == 0 else bias_band
        ps, ms, ls = [], [], []
        for h in range(HEADS_PER_TILE):
            sh = s[h * BLOCK:(h + 1) * BLOCK, :] + bias
            m = jnp.max(sh, axis=-1, keepdims=True)
            p = jnp.exp2(sh - m)
            ls.append(jnp.sum(p, axis=-1, keepdims=True))
            ms.append(m)
            ps.append(p.astype(BF16))
        pv = jnp.dot(jnp.concatenate(ps, axis=0), vw, preferred_element_type=F32)
        o_slabs, lse_slabs = [], []
        for slab in range(LANE_TILE // SLAB):
            cols = slice(slab * SLAB, (slab + 1) * SLAB)
            h0, h1 = slab * heads_per_slab, slab * heads_per_slab + 1
            l_slab = jnp.where(first_head, ls[h0], ls[h1])
            m_slab = jnp.where(first_head, ms[h0], ms[h1])
            pv_slab = jnp.where(first_head, pv[h0 * BLOCK:(h0 + 1) * BLOCK, cols],
                                pv[h1 * BLOCK:(h1 + 1) * BLOCK, cols])
            o_slabs.append(pv_slab / l_slab)
            lse_slabs.append(m_slab + jnp.log2(l_slab))
        o_ref[r, b * BLOCK:(b + 1) * BLOCK, :] = jnp.concatenate(o_slabs, axis=1).astype(BF16)
        lse_ref[r, b * BLOCK:(b + 1) * BLOCK, :] = jnp.concatenate(lse_slabs, axis=1)


def _attn_pattern(q, k, v, cast=()):
    nt, dilation, n, _ = q.shape
    rows = min(ATTN_ROWS, n)
    n_res = min(ATTN_ROWS // rows, dilation)
    blocks_per_tile = rows // BLOCK
    grid = (nt, dilation // n_res, n // rows)
    n_steps = grid[0] * grid[1] * grid[2]
    cur = pl.BlockSpec((None, n_res, rows, LANE_TILE), lambda t, r, j: (t, r, j, 0))
    prev = pl.BlockSpec((None, n_res, BLOCK, LANE_TILE),
                        lambda t, r, j: (t, r, jnp.maximum(j * blocks_per_tile - 1, 0), 0))
    step_id = lambda t, r, j: ((t * grid[1] + r) * grid[2] + j, 0)
    cast_specs = [pl.BlockSpec((w.shape[0] // n_steps, w.shape[1]), step_id) for w in cast]
    outs = pl.pallas_call(
        _attn_kernel,
        grid=grid,
        in_specs=[cur, cur, prev, cur, prev] + cast_specs,
        out_specs=[cur, cur] + cast_specs,
        out_shape=[jax.ShapeDtypeStruct(q.shape, BF16), jax.ShapeDtypeStruct(q.shape, F32)]
        + [jax.ShapeDtypeStruct(w.shape, BF16) for w in cast],
        scratch_shapes=[pltpu.VMEM((n_res, rows + BLOCK, LANE_TILE), BF16),
                        pltpu.VMEM((n_res, rows + BLOCK, LANE_TILE), BF16)],
        compiler_params=_params(dimension_semantics=("arbitrary", "arbitrary", "arbitrary")),
        name=f"attn_d{dilation}",
    )(q, k, k, v, v, *cast)
    return outs[0], outs[1], outs[2:]


def _s5_weights(a_re, a_im, log_dt, b_re, b_im, c_re, c_im, d_skip):
    p, c = SSM_STATE, SSM_GROUP
    lr, li = a_re.astype(F32), a_im.astype(F32)
    dt = jnp.exp(log_dt.astype(F32))[:, None]

    def apow(j):
        mag = jnp.exp(lr * dt * j)
        return mag * jnp.cos(li * dt * j), mag * jnp.sin(li * dt * j)

    ab_r, ab_i = apow(1.0)
    den = lr * lr + li * li
    nr, ni = ab_r - 1.0, ab_i
    cr = (nr * lr + ni * li) / den
    ci = (ni * lr - nr * li) / den
    br, bi = b_re.astype(F32), b_im.astype(F32)
    bb_r = cr[..., None] * br - ci[..., None] * bi
    bb_i = cr[..., None] * bi + ci[..., None] * br
    cre, cim = c_re.astype(F32), c_im.astype(F32)

    def apow_many(js):
        j = jnp.asarray(js, F32)[:, None, None]
        mag = jnp.exp(lr * dt * j)
        return mag * jnp.cos(li * dt * j), mag * jnp.sin(li * dt * j)

    pr, pi = apow_many(range(SUB_STEPS + 1))
    pad_lanes = lambda w: jnp.pad(w, [(0, 0)] * (w.ndim - 1) + [(0, SLAB - w.shape[-1])])
    wp_r = pad_lanes(jnp.stack([pr[SUB_STEPS - 1 - s] for s in range(SUB_STEPS)]))
    wp_i = pad_lanes(jnp.stack([pi[SUB_STEPS - 1 - s] for s in range(SUB_STEPS)]))
    bbt_r, bbt_i = pad_lanes(bb_r.transpose(0, 2, 1)), pad_lanes(bb_i.transpose(0, 2, 1))
    win_r = wp_r[:, :, None, :] * bbt_r[None] - wp_i[:, :, None, :] * bbt_i[None]
    win_i = wp_r[:, :, None, :] * bbt_i[None] + wp_i[:, :, None, :] * bbt_r[None]
    to_in = lambda w: (w.reshape(PAIRS, SUB, N_OCT, OCT, c, SLAB).transpose(0, 2, 1, 3, 4, 5)
                       .reshape(PAIRS, N_OCT, SUB * OCT * c, SLAB))
    ct_r, ct_i = pad_lanes(cre.transpose(0, 2, 1)), pad_lanes(cim.transpose(0, 2, 1))
    out_r = ct_r[None] * pr[..., None] - ct_i[None] * pi[..., None]
    out_i = -(ct_r[None] * pi[..., None] + ct_i[None] * pr[..., None])
    to_out = lambda w: w.reshape(SUB_STEPS + 1, N_OCT, OCT * p, SLAB)

    flat = lambda re, im: jnp.stack([re.reshape(-1), im.reshape(-1)])
    top_r, top_i = apow_many([CHUNK])
    lam = flat(top_r[0], top_i[0])
    levels = [lam]
    for _ in range(SCAN_GROUP - 1):
        lr_, li_ = levels[-1][0], levels[-1][1]
        levels.append(jnp.stack([lr_ * lam[0] - li_ * lam[1], lr_ * lam[1] + li_ * lam[0]]))
    m_t, w_r, w_i, e_r, e_i = _s5_expand(to_in(win_r), to_in(win_i), to_out(out_r), to_out(out_i),
                                         d_skip.astype(F32).reshape(N_OCT, 1, OCT * c))
    return dict(m_t=m_t, w_r=w_r, w_i=w_i, e_r=e_r, e_i=e_i,
                a_sub=flat(pr[SUB_STEPS], pi[SUB_STEPS]),
                levels=jnp.stack(levels, axis=1))


def _s5_expand_kernel(wr_ref, wi_ref, or_ref, oi_ref, d_ref, mt_ref, wfr_ref, wfi_ref, efr_ref, efi_ref):
    p, c = SSM_STATE, SSM_GROUP
    n_state, n_lane = OCT * p, OCT * c

    def rep(inner, total):
        r = lax.broadcasted_iota(jnp.int32, (SLAB, total), 0)
        col = lax.broadcasted_iota(jnp.int32, (SLAB, total), 1)
        return jnp.where((col & (inner - 1)) == r, 1.0, 0.0).astype(BF16)

    shift = lambda n: int(math.log2(n))
    rep_in, rep_out = rep(p, n_state), rep(c, n_lane)
    in_row = lax.broadcasted_iota(jnp.int32, (SUB * n_lane, n_state), 0)
    in_col = lax.broadcasted_iota(jnp.int32, (SUB * n_lane, n_state), 1)
    mask_in = ((in_row >> shift(c)) & (OCT - 1)) == (in_col >> shift(p))
    out_row = lax.broadcasted_iota(jnp.int32, (n_state, n_lane), 0)
    out_col = lax.broadcasted_iota(jnp.int32, (n_state, n_lane), 1)
    mask_out = (out_row >> shift(p)) == (out_col >> shift(c))
    diag = (lax.broadcasted_iota(jnp.int32, (n_lane, n_lane), 0)
            == lax.broadcasted_iota(jnp.int32, (n_lane, n_lane), 1))

    def split(x):
        hi = x.astype(BF16)
        return hi, (x - hi.astype(F32)).astype(BF16)

    def expand(x_b, rep_m, mask):
        full = jnp.dot(x_b, rep_m, preferred_element_type=F32)
        return jnp.where(mask, full, 0.0).astype(BF16)

    last = slice((SUB - 1) * n_lane, SUB * n_lane)
    for o in range(wfr_ref.shape[0]):
        in0 = []
        for src, dst in ((wr_ref, wfr_ref), (wi_ref, wfi_ref)):
            for q in range(PAIRS):
                hi, lo = split(src[q, o])
                full = expand(hi, rep_in, mask_in)
                dst[o, q * SUB * n_lane:(q + 1) * SUB * n_lane, :] = full
                if q == PAIRS - 1:
                    in0.append((full[last, :], expand(lo, rep_in, mask_in)[last, :]))
        lags = [None] * SUB_STEPS
        for (src, dst), (in_hi, in_lo) in zip(((or_ref, efr_ref), (oi_ref, efi_ref)), in0):
            outs_hi = []
            for j in range(SUB_STEPS + 1):
                hi, lo = split(src[j, o])
                out_hi = expand(hi, rep_out, mask_out)
                outs_hi.append(out_hi)
                if j < SUB_STEPS:
                    out_lo = expand(lo, rep_out, mask_out)
                    part = (jnp.dot(in_hi, out_hi, preferred_element_type=F32)
                            + jnp.dot(in_hi, out_lo, preferred_element_type=F32)
                            + jnp.dot(in_lo, out_hi, preferred_element_type=F32))
                    lags[j] = part if lags[j] is None else lags[j] + part
            dst[o] = jnp.concatenate(outs_hi[1:], axis=1)
        lags[0] = lags[0] + jnp.where(diag, d_ref[o], 0.0)
        zero = jnp.zeros_like(lags[0])
        for dist in range(PAIRS):
            lag = lambda t_in, t_out: SUB * dist + t_out - t_in
            mt_ref[dist, o] = jnp.concatenate(
                [jnp.concatenate([lags[lag(t_in, t_out)] if lag(t_in, t_out) >= 0 else zero
                                  for t_out in range(SUB)], axis=1)
                 for t_in in range(SUB)], axis=0).astype(BF16)


def _s5_expand(wc_r, wc_i, oc_r, oc_i, d_row):
    n_in, n_state = wc_r.shape[2], oc_r.shape[2]
    shapes = [jax.ShapeDtypeStruct((PAIRS, N_OCT, n_in, n_in), BF16),
              jax.ShapeDtypeStruct((N_OCT, PAIRS * n_in, n_state), BF16),
              jax.ShapeDtypeStruct((N_OCT, PAIRS * n_in, n_state), BF16),
              jax.ShapeDtypeStruct((N_OCT, n_state, PAIRS * n_in), BF16),
              jax.ShapeDtypeStruct((N_OCT, n_state, PAIRS * n_in), BF16)]
    def per_tile(shape, axis):
        block = tuple(1 if i == axis else n for i, n in enumerate(shape))
        return pl.BlockSpec(block, lambda o: tuple(o if i == axis else 0 for i in range(len(shape))))

    operands = (wc_r, wc_i, oc_r, oc_i, d_row)
    tile_axis = (1, 1, 1, 1, 0)
    return pl.pallas_call(
        _s5_expand_kernel,
        grid=(N_OCT,),
        in_specs=[per_tile(a.shape, ax) for a, ax in zip(operands, tile_axis)],
        out_specs=[per_tile(s.shape, ax) for s, ax in zip(shapes, (1, 0, 0, 0, 0))],
        out_shape=shapes,
        compiler_params=_params(dimension_semantics=("arbitrary",)),
        name="s5_expand",
    )(*operands)


def _cmul(ar, ai, br, bi):
    return ar * br - ai * bi, ar * bi + ai * br


def _s5_kernel(u_ref, mt_ref, wr_ref, wi_ref, er_ref, ei_ref, asub_ref, lvl_ref,
               gluw_ref, glub_ref, gout_ref, out_ref, carry_ref, zr_ref, zi_ref, locr_ref, loci_ref):
    rows = u_ref.shape[0]
    width = SSM_WIDTH
    oct_lanes = OCT * SSM_GROUP
    oct_state = OCT * SSM_STATE

    @pl.when(pl.program_id(0) == 0)
    def _():
        carry_ref[...] = jnp.zeros_like(carry_ref)

    def u_tile(a, q, o):
        first = a * SUB_STEPS + q * SUB
        return jnp.concatenate(
            [u_ref[:, (first + t) * width + o * oct_lanes:(first + t) * width + (o + 1) * oct_lanes]
             for t in range(SUB)], axis=1)

    a_r, a_i = asub_ref[0:1, :], asub_ref[1:2, :]

    for a in range(N_SUB):
        for o in range(N_OCT):
            cols = slice(o * oct_state, (o + 1) * oct_state)
            ut = jnp.concatenate([u_tile(a, q, o) for q in range(PAIRS)], axis=1)
            pr = jnp.dot(ut, wr_ref[o], preferred_element_type=F32)
            pi = jnp.dot(ut, wi_ref[o], preferred_element_type=F32)
            if a > 0:
                hr, hi = _cmul(zr_ref[:, cols], zi_ref[:, cols], a_r[:, cols], a_i[:, cols])
                pr, pi = hr + pr, hi + pi
            zr_ref[:, cols] = pr
            zi_ref[:, cols] = pi
            if a < N_SUB - 1:
                locr_ref[a, :, cols] = pr.astype(BF16)
                loci_ref[a, :, cols] = pi.astype(BF16)

    row = lax.broadcasted_iota(jnp.int32, (rows, 1), 0)
    in_group = row & (SCAN_GROUP - 1)
    sr, si = zr_ref[...], zi_ref[...]
    sh = 1
    while sh < SCAN_GROUP:
        keep = in_group >= sh
        pr_, pi_ = lvl_ref[0, sh - 1:sh, :], lvl_ref[1, sh - 1:sh, :]
        tr = jnp.where(keep, pltpu.roll(sr, sh, axis=0), 0.0)
        ti = jnp.where(keep, pltpu.roll(si, sh, axis=0), 0.0)
        mr, mi = _cmul(tr, ti, pr_, pi_)
        sr, si = sr + mr, si + mi
        sh *= 2
    c_r, c_i = carry_ref[0:1, :], carry_ref[1:2, :]
    pw_r, pw_i = lvl_ref[0], lvl_ref[1]
    last_r, last_i = c_r, c_i
    for g in range(rows // SCAN_GROUP):
        grp = slice(g * SCAN_GROUP, (g + 1) * SCAN_GROUP)
        ar, ai = _cmul(jnp.broadcast_to(last_r, pw_r.shape), jnp.broadcast_to(last_i, pw_i.shape), pw_r, pw_i)
        gr, gi = sr[grp, :] + ar, si[grp, :] + ai
        zr_ref[grp, :] = gr
        zi_ref[grp, :] = gi
        last_r, last_i = gr[SCAN_GROUP - 1:SCAN_GROUP, :], gi[SCAN_GROUP - 1:SCAN_GROUP, :]
    carry_ref[0:1, :] = last_r
    carry_ref[1:2, :] = last_i
    xr = jnp.where(row == 0, c_r, pltpu.roll(zr_ref[...], 1, axis=0))
    xi = jnp.where(row == 0, c_i, pltpu.roll(zi_ref[...], 1, axis=0))
    zr_ref[...] = xr
    zi_ref[...] = xi

    for a in range(N_SUB):
        if a > 0:
            nr, ni = _cmul(zr_ref[...], zi_ref[...], a_r, a_i)
            zr_ref[...] = nr
            zi_ref[...] = ni
        if a == 0:
            xr_b = zr_ref[...].astype(BF16)
            xi_b = zi_ref[...].astype(BF16)
        else:
            xr_b = (zr_ref[...] + locr_ref[a - 1].astype(F32)).astype(BF16)
            xi_b = (zi_ref[...] + loci_ref[a - 1].astype(F32)).astype(BF16)
        tiles = []
        for o in range(N_OCT):
            cols = slice(o * oct_state, (o + 1) * oct_state)
            from_state = (jnp.dot(xr_b[:, cols], er_ref[o], preferred_element_type=F32)
                          + jnp.dot(xi_b[:, cols], ei_ref[o], preferred_element_type=F32))
            parts = []
            for q in range(PAIRS):
                acc = from_state[:, q * SUB * oct_lanes:(q + 1) * SUB * oct_lanes]
                for q_in in range(q + 1):
                    acc = acc + jnp.dot(u_tile(a, q_in, o), mt_ref[q - q_in, o], preferred_element_type=F32)
                parts.append(acc)
            tiles.append(jnp.concatenate(parts, axis=1))
        for b in range(SUB_STEPS):
            t = a * SUB_STEPS + b
            y = jnp.concatenate([tl[:, b * oct_lanes:(b + 1) * oct_lanes] for tl in tiles], axis=1)
            y2 = y * y
            z = y * (1.0 / (1.0 + jnp.exp2(y * (y2 * (-GELU_C * 0.044715 * LOG2E) - GELU_C * LOG2E))))
            gate = jnp.dot(z.astype(BF16), gluw_ref[...], preferred_element_type=F32) + glub_ref[...]
            o = z * (1.0 / (1.0 + jnp.exp2(gate)))
            ms = jnp.mean(o * o, axis=-1, keepdims=True)
            out_ref[:, t * width:(t + 1) * width] = (o * lax.rsqrt(ms + EPS) * gout_ref[...]).astype(BF16)


def _s5(uv, w, glu_w, glu_b, g_out):
    n = uv.shape[0]
    rows = S5_ROWS
    tile = pl.BlockSpec((rows, CHUNK * SSM_WIDTH), lambda i: (i, 0))
    return pl.pallas_call(
        _s5_kernel,
        grid=(n // rows,),
        in_specs=[tile] + [_const_spec(a.shape) for a in
                           (w["m_t"], w["w_r"], w["w_i"], w["e_r"], w["e_i"], w["a_sub"], w["levels"],
                            glu_w, glu_b, g_out)],
        out_specs=tile,
        out_shape=jax.ShapeDtypeStruct((n, CHUNK * SSM_WIDTH), BF16),
        scratch_shapes=[pltpu.VMEM((2, STATE_COLS), F32),
                        pltpu.VMEM((rows, STATE_COLS), F32),
                        pltpu.VMEM((rows, STATE_COLS), F32),
                        pltpu.VMEM((N_SUB - 1, rows, STATE_COLS), BF16),
                        pltpu.VMEM((N_SUB - 1, rows, STATE_COLS), BF16)],
        compiler_params=_params(dimension_semantics=("arbitrary",)),
        name="s5",
    )(uv, w["m_t"], w["w_r"], w["w_i"], w["e_r"], w["e_i"], w["a_sub"], w["levels"], glu_w, glu_b, g_out)


def _mlp_kernel(x_ref, o1_ref, o2_ref, o3_ref, l1_ref, l2_ref, l3_ref, ssm_ref, ga_ref, wout_ref,
                g2_ref, wup_ref, wdn_ref, out_ref, slab_ref, slab4_ref, mix_ref):
    rows = PROJ_SUB
    n_sub = x_ref.shape[0] // rows
    n_slabs = ATTN_WIDTH // SLAB
    step = DILATIONS[1]
    n_ff = D_FF // FF_CHUNK

    def regroup(ref, d, sub, buf):
        part = slice(sub * (rows // d), (sub + 1) * (rows // d))
        for s in range(n_slabs):
            t, lanes = _slab_home(s)
            if ref is ssm_ref:
                piece = lambda r: ref[part, r * SSM_WIDTH + s * SLAB:r * SSM_WIDTH + (s + 1) * SLAB]
            else:
                piece = lambda r: ref[t, r, part, lanes]
            if d == step:
                for r in range(d):
                    slab_ref[sub, buf, s, pl.ds(r, rows // d, stride=d), :] = piece(r).astype(F32)
            else:
                for r4 in range(step):
                    for j4 in range(step):
                        slab4_ref[sub, buf - 2, s, r4, pl.ds(j4, rows // d, stride=step), :] = (
                            piece(step * j4 + r4).astype(F32))
                for r4 in range(step):
                    slab_ref[sub, buf, s, pl.ds(r4, rows // step, stride=step), :] = slab4_ref[sub, buf - 2, s, r4]

    def in_order(sub, buf):
        return jnp.concatenate([slab_ref[sub, buf, s] for s in range(n_slabs)], axis=1)

    def merge_slab(sub, s):
        t, lanes = _slab_home(s)
        part = slice(sub * rows, (sub + 1) * rows)
        l1, l2, l3 = l1_ref[t, 0, part, lanes], slab_ref[sub, 1, s], slab_ref[sub, 3, s]
        m = jnp.maximum(jnp.maximum(l1, l2), l3)
        w1, w2, w3 = jnp.exp2(l1 - m), jnp.exp2(l2 - m), jnp.exp2(l3 - m)
        num = w1 * o1_ref[t, 0, part, lanes].astype(F32) + w2 * slab_ref[sub, 0, s] + w3 * slab_ref[sub, 2, s]
        slab_ref[sub, 0, s] = num / (w1 + w2 + w3)

    def merge_finish(sub):
        attn = in_order(sub, 0)
        ms = jnp.mean(attn * attn, axis=-1, keepdims=True)
        mix_ref[sub, :, 0:ATTN_WIDTH] = (attn * lax.rsqrt(ms + EPS) * ga_ref[...]).astype(BF16)
        mix_ref[sub, :, ATTN_WIDTH:] = in_order(sub, 4).astype(BF16)

    def prologue(sub):
        return ([lambda: regroup(o2_ref, step, sub, 0), lambda: regroup(l2_ref, step, sub, 1),
                 lambda: regroup(o3_ref, CHUNK, sub, 2), lambda: regroup(l3_ref, CHUNK, sub, 3),
                 lambda: regroup(ssm_ref, CHUNK, sub, 4)]
                + [lambda s=s: merge_slab(sub, s) for s in range(n_slabs)] + [lambda: merge_finish(sub)])

    def main(sub):
        part = slice(sub * rows, (sub + 1) * rows)
        st = {}

        def out_proj():
            x1 = x_ref[part, :] + jnp.dot(mix_ref[sub], wout_ref[...], preferred_element_type=F32)
            ms2 = jnp.mean(x1 * x1, axis=-1, keepdims=True)
            st["xn"] = (x1 * lax.rsqrt(ms2 + EPS) * g2_ref[...]).astype(BF16)
            st["acc"] = x1

        def up(c):
            st[c] = jnp.dot(st["xn"], wup_ref[:, c * FF_CHUNK:(c + 1) * FF_CHUNK], preferred_element_type=F32)

        def down(c):
            h = jnp.square(jnp.maximum(st.pop(c), 0.0)).astype(BF16)
            st["acc"] = st["acc"] + jnp.dot(h, wdn_ref[c * FF_CHUNK:(c + 1) * FF_CHUNK, :],
                                            preferred_element_type=F32)

        def finish():
            out_ref[part, :] = st["acc"]

        stages = [out_proj, lambda: up(0)]
        for c in range(n_ff):
            if c + 1 < n_ff:
                stages.append(lambda c=c: up(c + 1))
            stages.append(lambda c=c: down(c))
        return stages + [finish]

    tail = 3
    program = prologue(0)
    all_stages = [main(sub) for sub in range(n_sub)]
    for sub in range(n_sub):
        stages = all_stages[sub]
        head = stages if sub == 0 else stages[1:]
        if sub + 1 < n_sub:
            side = prologue(sub + 1)
            body = []
            for stage in head[:-tail]:
                body.append(stage)
                if side:
                    body.append(side.pop(0))
            program += body + side + [all_stages[sub + 1][0]] + head[-tail:]
        else:
            program += head
    for piece in program:
        piece()


def _mlp(x, os_, ls_, ssm, ga, w_out, g2, w_up, w_dn):
    s = x.shape[0]
    rows = MLP_ROWS
    xt = pl.BlockSpec((rows, D_MODEL), lambda i: (i, 0))
    ht = [pl.BlockSpec((N_HEAD_TILES, d, rows // d, LANE_TILE), lambda i: (0, 0, i, 0)) for d in DILATIONS]
    return pl.pallas_call(
        _mlp_kernel,
        grid=(s // rows,),
        in_specs=[xt, *ht, *ht, pl.BlockSpec((rows // CHUNK, CHUNK * SSM_WIDTH), lambda i: (i, 0)),
                  _const_spec(ga.shape), _const_spec(w_out.shape), _const_spec(g2.shape),
                  _const_spec(w_up.shape), _const_spec(w_dn.shape)],
        out_specs=xt,
        out_shape=jax.ShapeDtypeStruct((s, D_MODEL), F32),
        scratch_shapes=[pltpu.VMEM((rows // PROJ_SUB, 5, ATTN_WIDTH // SLAB, PROJ_SUB, SLAB), F32),
                        pltpu.VMEM((rows // PROJ_SUB, 3, ATTN_WIDTH // SLAB, DILATIONS[1],
                                    PROJ_SUB // DILATIONS[1], SLAB), F32),
                        pltpu.VMEM((rows // PROJ_SUB, PROJ_SUB, D_MODEL), BF16)],
        compiler_params=_params(dimension_semantics=("arbitrary",)),
        name="mlp",
    )(x, *os_, *ls_, ssm, ga, w_out, g2, w_up, w_dn)


def _layer(x, norm1_g, w_in, q_norm_g, k_norm_g, ssm_a_re, ssm_a_im, ssm_log_dt, ssm_b_re, ssm_b_im,
           ssm_c_re, ssm_c_im, ssm_d, glu_w, glu_b, attn_out_norm_g, ssm_out_norm_g, w_out, norm2_g,
           w_mlp_up, w_mlp_down):
    row = lambda g: g.astype(F32).reshape(1, -1)
    heads = ATTN_WIDTH // HEAD_DIM
    head_id = jnp.arange(LANE_TILE) // HEAD_DIM
    hmat = jnp.where(head_id[:, None] == head_id[None, :], 1.0 / HEAD_DIM, 0.0).astype(BF16)
    *qkv, u = _proj(x, row(norm1_g), w_in.astype(F32), hmat,
                    row(jnp.tile(q_norm_g, heads)), row(jnp.tile(k_norm_g, heads)))
    outs, lses = [], []
    mlp_weights = (w_out.astype(F32), w_mlp_up.astype(F32), w_mlp_down.astype(F32))
    for i in range(len(DILATIONS)):
        o, lse, cast = _attn_pattern(*qkv[3 * i:3 * i + 3], cast=mlp_weights if i == 0 else ())
        outs.append(o)
        lses.append(lse)
        if i == 0:
            w_out_b, w_up_b, w_dn_b = cast
    w = _s5_weights(ssm_a_re, ssm_a_im, ssm_log_dt, ssm_b_re, ssm_b_im, ssm_c_re, ssm_c_im, ssm_d)
    ssm = _s5(u, w, (glu_w.astype(F32) * -LOG2E).astype(BF16), row(glu_b) * -LOG2E, row(ssm_out_norm_g))
    return _mlp(x, outs, lses, ssm, row(attn_out_norm_g), w_out_b, row(norm2_g), w_up_b, w_dn_b)


def kernel(x, norm1_g, w_in, q_norm_g, k_norm_g, ssm_a_re, ssm_a_im, ssm_log_dt, ssm_b_re, ssm_b_im,
           ssm_c_re, ssm_c_im, ssm_d, glu_w, glu_b, attn_out_norm_g, ssm_out_norm_g, w_out, norm2_g,
           w_mlp_up, w_mlp_down):
    params = (norm1_g, w_in, q_norm_g, k_norm_g, ssm_a_re, ssm_a_im, ssm_log_dt, ssm_b_re, ssm_b_im,
              ssm_c_re, ssm_c_im, ssm_d, glu_w, glu_b, attn_out_norm_g, ssm_out_norm_g, w_out, norm2_g,
              w_mlp_up, w_mlp_down)
    batch = x.shape[0]
    outs = []
    for bi in range(batch):
        h = x[bi]
        for layer in range(norm1_g.shape[0]):
            h = _layer(h, *[p[layer] for p in params])
        outs.append(h)
    return jnp.stack(outs)
```
